```python
import jax, jax.numpy as jnp
from jax import lax
import numpy as np

D_MODEL = 2048
BATCH = 8
SEQ = 8192
DEPTH = 1

SB_HEADS = 16
SB_HEAD_DIM = 128
SB_BLOCK = 128
GDN_HEADS = 16
GDN_K_DIM = 128
GDN_V_DIM = 128
GDN_CONV = 4
GDN_CHUNK = 64
D_FF = -(-8 * D_MODEL // (3 * 256)) * 256
EPS = 1e-6

D_SB = SB_HEADS * SB_HEAD_DIM
D_GDN_K = GDN_HEADS * GDN_K_DIM
D_GDN_V = GDN_HEADS * GDN_V_DIM
D_GDN_QKV = 2 * D_GDN_K + D_GDN_V
IN_SPLITS = (D_SB, D_SB, D_SB, D_GDN_QKV, D_GDN_V, GDN_HEADS, GDN_HEADS, D_MODEL, D_MODEL)
IN_OFFSETS = tuple(int(o) for o in np.cumsum(IN_SPLITS)[:-1])
D_IN = int(sum(IN_SPLITS))

kernel_name = 'hybrid_stickbreak_gdn_adaln_block'


def _rms(x, w):
    xf = x.astype(jnp.float32)
    return xf * lax.rsqrt(jnp.mean(xf * xf, axis=-1, keepdims=True) + EPS) * w.astype(jnp.float32)


def _l2norm(x):
    return x * lax.rsqrt(jnp.sum(x * x, axis=-1, keepdims=True) + EPS)


def _heads(t, n, d):
    b, s, _ = t.shape
    return t.reshape(b, s, n, d).transpose(0, 2, 1, 3)


def _merge_heads(t):
    b, n, s, d = t.shape
    return t.transpose(0, 2, 1, 3).reshape(b, s, n * d)


def _causal_depthwise_conv(x, w):
    return lax.conv_general_dilated(
        x, w[:, None, :].astype(x.dtype), window_strides=(1,), padding=[(GDN_CONV - 1, 0)],
        dimension_numbers=('NWC', 'WIO', 'NWC'), feature_group_count=x.shape[-1])


def _stick_breaking(q, k, v):
    s_len, d = q.shape[2], q.shape[3]
    scale = d ** -0.5
    outs = []
    for i in range(s_len // SB_BLOCK):
        q0 = i * SB_BLOCK
        kend = q0 + SB_BLOCK
        z = jnp.einsum('bhqd,bhkd->bhqk', q[:, :, q0:kend], k[:, :, :kend]) * scale
        causal = jnp.arange(kend)[None, :] < (q0 + jnp.arange(SB_BLOCK))[:, None]
        log_1m = jnp.where(causal, jax.nn.log_sigmoid(-z), 0.0)
        after = lax.cumsum(log_1m, axis=3, reverse=True) - log_1m
        w = jnp.where(causal, jnp.exp(jax.nn.log_sigmoid(z) + after), 0.0)
        outs.append(jnp.einsum('bhqk,bhkd->bhqd', w, v[:, :, :kend]))
    return jnp.concatenate(outs, axis=2)


def _gated_delta_rule(q, k, v, g, beta):
    b, h, s_len, dk = q.shape
    dv = v.shape[-1]
    n, c = s_len // GDN_CHUNK, GDN_CHUNK
    q = q.reshape(b, h, n, c, dk)
    k = k.reshape(b, h, n, c, dk)
    v = v.reshape(b, h, n, c, dv)
    g = g.reshape(b, h, n, c)
    beta = beta.reshape(b, h, n, c)
    gc = jnp.cumsum(g, axis=-1)
    tril = jnp.tril(jnp.ones((c, c), dtype=bool))
    stril = jnp.tril(jnp.ones((c, c), dtype=bool), k=-1)
    diff = gc[..., :, None] - gc[..., None, :]
    decay = jnp.where(tril, jnp.exp(jnp.where(tril, diff, 0.0)), 0.0)
    kk = jnp.einsum('bhnrd,bhnid->bhnri', k, k)
    m = jnp.eye(c, dtype=q.dtype) + jnp.where(stril, beta[..., :, None] * kk * decay, 0.0)
    solve = lambda rhs: lax.linalg.triangular_solve(m, rhs, left_side=True, lower=True, unit_diagonal=True)
    w_v = solve(beta[..., None] * v)
    w_k = solve((beta * jnp.exp(gc))[..., None] * k)
    attn = jnp.einsum('bhnrd,bhnid->bhnri', q, k) * decay
    q_g = q * jnp.exp(gc)[..., None]
    k_dec = k * jnp.exp(gc[..., -1:] - gc)[..., None]
    g_last = jnp.exp(gc[..., -1])

    def step(state, xs):
        w_v_c, w_k_c, q_g_c, attn_c, k_dec_c, g_last_c = xs
        u = w_v_c - jnp.einsum('bhcd,bhde->bhce', w_k_c, state)
        o = jnp.einsum('bhcd,bhde->bhce', q_g_c, state) + jnp.einsum('bhcj,bhje->bhce', attn_c, u)
        state = g_last_c[..., None, None] * state + jnp.einsum('bhcd,bhce->bhde', k_dec_c, u)
        return state, o

    mv = lambda t: jnp.moveaxis(t, 2, 0)
    state0 = jnp.zeros((b, h, dk, dv), dtype=q.dtype)
    _, o = lax.scan(step, state0, (mv(w_v), mv(w_k), mv(q_g), mv(attn), mv(k_dec), mv(g_last)))
    return jnp.moveaxis(o, 0, 2).reshape(b, h, s_len, dv)


def _fwd_setup_inputs(seed: int = 0) -> dict:
    key = jax.random.key(seed)
    ks = jax.random.split(key, 20)
    f32 = jnp.float32
    nrm = lambda k, shape, s: jax.random.normal(k, shape, f32) * s
    gain = lambda k, shape: 1.0 + 0.02 * jax.random.normal(k, shape, f32)
    dt = jnp.exp(jax.random.uniform(ks[10], (DEPTH, GDN_HEADS), f32, np.log(1e-3), np.log(1e-1)))
    return {
        'x': nrm(ks[0], (BATCH, SEQ, D_MODEL), 1.0),
        'c': nrm(ks[1], (BATCH, D_MODEL), 1.0),
        'w_mod': nrm(ks[2], (DEPTH, D_MODEL, 6 * D_MODEL), 0.5 * D_MODEL ** -0.5),
        'b_mod': nrm(ks[3], (DEPTH, 6 * D_MODEL), 0.02),
        'norm1_w': gain(ks[4], (DEPTH, D_MODEL)),
        'w_in': nrm(ks[5], (DEPTH, D_MODEL, D_IN), D_MODEL ** -0.5),
        'q_norm_w': gain(ks[6], (DEPTH, SB_HEAD_DIM)),
        'k_norm_w': gain(ks[7], (DEPTH, SB_HEAD_DIM)),
        'conv_w': nrm(ks[8], (DEPTH, GDN_CONV, D_GDN_QKV), GDN_CONV ** -0.5),
        'a_log': jnp.log(jax.random.uniform(ks[9], (DEPTH, GDN_HEADS), f32, 1.0, 16.0)),
        'dt_bias': dt + jnp.log(-jnp.expm1(-dt)),
        'o_norm_w': gain(ks[11], (DEPTH, GDN_V_DIM)),
        'p_a': nrm(ks[12], (DEPTH, D_SB, D_MODEL), D_SB ** -0.5),
        'p_b': nrm(ks[13], (DEPTH, D_GDN_V, D_MODEL), D_GDN_V ** -0.5),
        'w_out': nrm(ks[14], (DEPTH, D_MODEL, D_MODEL), D_MODEL ** -0.5),
        'norm2_w': gain(ks[15], (DEPTH, D_MODEL)),
        'w_gate': nrm(ks[16], (DEPTH, D_MODEL, D_FF), D_MODEL ** -0.5),
        'w_up': nrm(ks[17], (DEPTH, D_MODEL, D_FF), D_MODEL ** -0.5),
        'w_down': nrm(ks[18], (DEPTH, D_FF, D_MODEL), D_FF ** -0.5),
    }


def _fwd_reference(x, c, w_mod, b_mod, norm1_w, w_in, q_norm_w, k_norm_w, conv_w, a_log, dt_bias,
              o_norm_w, p_a, p_b, w_out, norm2_w, w_gate, w_up, w_down):
    f32 = jnp.float32
    bsz, s_len, _ = x.shape
    h = x.astype(f32)
    c_act = jax.nn.silu(c.astype(f32))
    for l in range(DEPTH):
        mod = c_act @ w_mod[l].astype(f32) + b_mod[l].astype(f32)
        shift1, scale1, gate1, shift2, scale2, gate2 = [t[:, None, :] for t in jnp.split(mod, 6, axis=-1)]

        u = _rms(h, norm1_w[l]) * (1.0 + scale1) + shift1
        proj = u @ w_in[l].astype(f32)
        qa, ka, va, qkv_b, z_b, b_b, a_b, gate_a, gate_b = jnp.split(proj, IN_OFFSETS, axis=-1)

        qa = _rms(_heads(qa, SB_HEADS, SB_HEAD_DIM), q_norm_w[l])
        ka = _rms(_heads(ka, SB_HEADS, SB_HEAD_DIM), k_norm_w[l])
        va = _heads(va, SB_HEADS, SB_HEAD_DIM)
        o_a = _merge_heads(_stick_breaking(qa, ka, va))

        qkv_b = jax.nn.silu(_causal_depthwise_conv(qkv_b, conv_w[l]))
        qb, kb, vb = jnp.split(qkv_b, (D_GDN_K, 2 * D_GDN_K), axis=-1)
        qb = _l2norm(_heads(qb, GDN_HEADS, GDN_K_DIM)) * (GDN_K_DIM ** -0.5)
        kb = _l2norm(_heads(kb, GDN_HEADS, GDN_K_DIM))
        vb = _heads(vb, GDN_HEADS, GDN_V_DIM)
        beta = jax.nn.sigmoid(b_b).transpose(0, 2, 1)
        g = (-jnp.exp(a_log[l].astype(f32)) * jax.nn.softplus(a_b + dt_bias[l].astype(f32))).transpose(0, 2, 1)
        o_b = _gated_delta_rule(qb, kb, vb, g, beta)
        o_b = _rms(o_b, o_norm_w[l]) * jax.nn.silu(_heads(z_b, GDN_HEADS, GDN_V_DIM))
        o_b = _merge_heads(o_b)

        merged = (jax.nn.sigmoid(gate_a) * (o_a @ p_a[l].astype(f32))
                  + jax.nn.sigmoid(gate_b) * (o_b @ p_b[l].astype(f32)))
        h = h + gate1 * (merged @ w_out[l].astype(f32))

        u = _rms(h, norm2_w[l]) * (1.0 + scale2) + shift2
        ff = jax.nn.silu(u @ w_gate[l].astype(f32)) * (u @ w_up[l].astype(f32))
        h = h + gate2 * (ff @ w_down[l].astype(f32))
    return h.astype(x.dtype)


import jax as _jax
import jax.numpy as _jnp

TWIN_FORMAT = 'train_step'
FWD_PARAMS = ['x', 'c', 'w_mod', 'b_mod', 'norm1_w', 'w_in', 'q_norm_w', 'k_norm_w', 'conv_w', 'a_log', 'dt_bias', 'o_norm_w', 'p_a', 'p_b', 'w_out', 'norm2_w', 'w_gate', 'w_up', 'w_down']
TWIN_WEIGHTS = ['w_mod', 'b_mod', 'norm1_w', 'w_in', 'q_norm_w', 'k_norm_w', 'conv_w', 'a_log', 'dt_bias', 'o_norm_w', 'p_a', 'p_b', 'w_out', 'norm2_w', 'w_gate', 'w_up', 'w_down']
TWIN_DIFF_INPUT = 'x'
TWIN_INPUTS = ['x', 'c', 'w_mod', 'b_mod', 'norm1_w', 'w_in', 'q_norm_w', 'k_norm_w', 'conv_w', 'a_log', 'dt_bias', 'o_norm_w', 'p_a', 'p_b', 'w_out', 'norm2_w', 'w_gate', 'w_up', 'w_down', 'loss_target', 'm_w_mod', 'm_b_mod', 'm_norm1_w', 'm_w_in', 'm_q_norm_w', 'm_k_norm_w', 'm_conv_w', 'm_a_log', 'm_dt_bias', 'm_o_norm_w', 'm_p_a', 'm_p_b', 'm_w_out', 'm_norm2_w', 'm_w_gate', 'm_w_up', 'm_w_down', 'v_w_mod', 'v_b_mod', 'v_norm1_w', 'v_w_in', 'v_q_norm_w', 'v_k_norm_w', 'v_conv_w', 'v_a_log', 'v_dt_bias', 'v_o_norm_w', 'v_p_a', 'v_p_b', 'v_w_out', 'v_norm2_w', 'v_w_gate', 'v_w_up', 'v_w_down']
TWIN_OUTPUTS = ['loss', 'grad_x', 'grad_w_mod', 'grad_b_mod', 'grad_norm1_w', 'grad_w_in', 'grad_q_norm_w', 'grad_k_norm_w', 'grad_conv_w', 'grad_a_log', 'grad_dt_bias', 'grad_o_norm_w', 'grad_p_a', 'grad_p_b', 'grad_w_out', 'grad_norm2_w', 'grad_w_gate', 'grad_w_up', 'grad_w_down', 'delta_w_mod', 'delta_b_mod', 'delta_norm1_w', 'delta_w_in', 'delta_q_norm_w', 'delta_k_norm_w', 'delta_conv_w', 'delta_a_log', 'delta_dt_bias', 'delta_o_norm_w', 'delta_p_a', 'delta_p_b', 'delta_w_out', 'delta_norm2_w', 'delta_w_gate', 'delta_w_up', 'delta_w_down', 'new_m_w_mod', 'new_m_b_mod', 'new_m_norm1_w', 'new_m_w_in', 'new_m_q_norm_w', 'new_m_k_norm_w', 'new_m_conv_w', 'new_m_a_log', 'new_m_dt_bias', 'new_m_o_norm_w', 'new_m_p_a', 'new_m_p_b', 'new_m_w_out', 'new_m_norm2_w', 'new_m_w_gate', 'new_m_w_up', 'new_m_w_down', 'new_v_w_mod', 'new_v_b_mod', 'new_v_norm1_w', 'new_v_w_in', 'new_v_q_norm_w', 'new_v_k_norm_w', 'new_v_conv_w', 'new_v_a_log', 'new_v_dt_bias', 'new_v_o_norm_w', 'new_v_p_a', 'new_v_p_b', 'new_v_w_out', 'new_v_norm2_w', 'new_v_w_gate', 'new_v_w_up', 'new_v_w_down']
TWIN_LEAF_KINDS = {'loss': 'loss', 'grad_x': 'grad_x', 'grad_w_mod': 'grad_w', 'grad_b_mod': 'grad_w', 'grad_norm1_w': 'grad_w', 'grad_w_in': 'grad_w', 'grad_q_norm_w': 'grad_w', 'grad_k_norm_w': 'grad_w', 'grad_conv_w': 'grad_w', 'grad_a_log': 'grad_w', 'grad_dt_bias': 'grad_w', 'grad_o_norm_w': 'grad_w', 'grad_p_a': 'grad_w', 'grad_p_b': 'grad_w', 'grad_w_out': 'grad_w', 'grad_norm2_w': 'grad_w', 'grad_w_gate': 'grad_w', 'grad_w_up': 'grad_w', 'grad_w_down': 'grad_w', 'delta_w_mod': 'delta_w', 'delta_b_mod': 'delta_w', 'delta_norm1_w': 'delta_w', 'delta_w_in': 'delta_w', 'delta_q_norm_w': 'delta_w', 'delta_k_norm_w': 'delta_w', 'delta_conv_w': 'delta_w', 'delta_a_log': 'delta_w', 'delta_dt_bias': 'delta_w', 'delta_o_norm_w': 'delta_w', 'delta_p_a': 'delta_w', 'delta_p_b': 'delta_w', 'delta_w_out': 'delta_w', 'delta_norm2_w': 'delta_w', 'delta_w_gate': 'delta_w', 'delta_w_up': 'delta_w', 'delta_w_down': 'delta_w', 'new_m_w_mod': 'new_m', 'new_m_b_mod': 'new_m', 'new_m_norm1_w': 'new_m', 'new_m_w_in': 'new_m', 'new_m_q_norm_w': 'new_m', 'new_m_k_norm_w': 'new_m', 'new_m_conv_w': 'new_m', 'new_m_a_log': 'new_m', 'new_m_dt_bias': 'new_m', 'new_m_o_norm_w': 'new_m', 'new_m_p_a': 'new_m', 'new_m_p_b': 'new_m', 'new_m_w_out': 'new_m', 'new_m_norm2_w': 'new_m', 'new_m_w_gate': 'new_m', 'new_m_w_up': 'new_m', 'new_m_w_down': 'new_m', 'new_v_w_mod': 'new_v', 'new_v_b_mod': 'new_v', 'new_v_norm1_w': 'new_v', 'new_v_w_in': 'new_v', 'new_v_q_norm_w': 'new_v', 'new_v_k_norm_w': 'new_v', 'new_v_conv_w': 'new_v', 'new_v_a_log': 'new_v', 'new_v_dt_bias': 'new_v', 'new_v_o_norm_w': 'new_v', 'new_v_p_a': 'new_v', 'new_v_p_b': 'new_v', 'new_v_w_out': 'new_v', 'new_v_norm2_w': 'new_v', 'new_v_w_gate': 'new_v', 'new_v_w_up': 'new_v', 'new_v_w_down': 'new_v'}


def _forward(args):
    return _fwd_reference(*[args[k] for k in FWD_PARAMS])


def _output_shape():
    def fwd():
        inp = _fwd_setup_inputs(0)
        return _fwd_reference(*[inp[k] for k in FWD_PARAMS])
    out = _jax.eval_shape(fwd)
    return out.shape, out.dtype

N_MICROBATCH = 1
ADAM_LR = 0.001
ADAM_B1 = 0.9
ADAM_B2 = 0.999
ADAM_EPS = 1e-08
ADAM_WD = 0.01
ADAM_STEP = 10
PER_EXAMPLE_BATCH_AXIS = {'x': 0, 'c': 0, 'loss_target': 0}
SHARED_INPUTS = []
_WEIGHT_DTYPES = {'w_mod': _jnp.float32, 'b_mod': _jnp.float32, 'norm1_w': _jnp.float32, 'w_in': _jnp.float32, 'q_norm_w': _jnp.float32, 'k_norm_w': _jnp.float32, 'conv_w': _jnp.float32, 'a_log': _jnp.float32, 'dt_bias': _jnp.float32, 'o_norm_w': _jnp.float32, 'p_a': _jnp.float32, 'p_b': _jnp.float32, 'w_out': _jnp.float32, 'norm2_w': _jnp.float32, 'w_gate': _jnp.float32, 'w_up': _jnp.float32, 'w_down': _jnp.float32}
MOMENT_SCALE = {'w_mod': 6.057519e-01, 'b_mod': 1.653108e+00, 'norm1_w': 9.140474e-01, 'w_in': 2.870101e-02, 'q_norm_w': 8.809260e-01, 'k_norm_w': 8.852029e-01, 'conv_w': 3.086432e-02, 'a_log': 9.419950e-01, 'dt_bias': 9.037173e-01, 'o_norm_w': 5.963577e+00, 'p_a': 4.724338e-02, 'p_b': 3.716435e-02, 'w_out': 5.372441e-02, 'norm2_w': 3.134177e+00, 'w_gate': 4.744468e-02, 'w_up': 3.040203e-02, 'w_down': 4.691809e-02}


def _to_microbatches(a, axis):
    t = _jnp.moveaxis(a, axis, 0)
    t = t.reshape((N_MICROBATCH, t.shape[0] // N_MICROBATCH) + t.shape[1:])
    return _jnp.moveaxis(t, 1, axis + 1)


def setup_inputs(seed: int = 0) -> dict:
    inp = _fwd_setup_inputs(seed)
    key = _jax.random.fold_in(_jax.random.key(seed), 7919)
    shape, _ = _output_shape()
    out = dict(inp)
    out["loss_target"] = _jax.random.normal(_jax.random.fold_in(key, 0), shape, _jnp.float32)
    for i, name in enumerate(TWIN_WEIGHTS):
        w = inp[name].astype(_jnp.float32)
        if MOMENT_SCALE is None:
            s = _jnp.sqrt(_jnp.mean(_jnp.square(w)) + 1e-30)
        else:
            s = MOMENT_SCALE[name]
        km, kv = _jax.random.split(_jax.random.fold_in(key, i + 1))
        out[name] = w
        out["m_" + name] = s * _jax.random.normal(km, w.shape, _jnp.float32)
        out["v_" + name] = (s * s) * _jax.random.uniform(kv, w.shape, _jnp.float32, 0.5, 1.5)
    if N_MICROBATCH > 1:
        for name, axis in PER_EXAMPLE_BATCH_AXIS.items():
            out[name] = _to_microbatches(out[name], axis)
    return {'x': out['x'], 'c': out['c'], 'w_mod': out['w_mod'], 'b_mod': out['b_mod'], 'norm1_w': out['norm1_w'], 'w_in': out['w_in'], 'q_norm_w': out['q_norm_w'], 'k_norm_w': out['k_norm_w'], 'conv_w': out['conv_w'], 'a_log': out['a_log'], 'dt_bias': out['dt_bias'], 'o_norm_w': out['o_norm_w'], 'p_a': out['p_a'], 'p_b': out['p_b'], 'w_out': out['w_out'], 'norm2_w': out['norm2_w'], 'w_gate': out['w_gate'], 'w_up': out['w_up'], 'w_down': out['w_down'], 'loss_target': out['loss_target'], 'm_w_mod': out['m_w_mod'], 'm_b_mod': out['m_b_mod'], 'm_norm1_w': out['m_norm1_w'], 'm_w_in': out['m_w_in'], 'm_q_norm_w': out['m_q_norm_w'], 'm_k_norm_w': out['m_k_norm_w'], 'm_conv_w': out['m_conv_w'], 'm_a_log': out['m_a_log'], 'm_dt_bias': out['m_dt_bias'], 'm_o_norm_w': out['m_o_norm_w'], 'm_p_a': out['m_p_a'], 'm_p_b': out['m_p_b'], 'm_w_out': out['m_w_out'], 'm_norm2_w': out['m_norm2_w'], 'm_w_gate': out['m_w_gate'], 'm_w_up': out['m_w_up'], 'm_w_down': out['m_w_down'], 'v_w_mod': out['v_w_mod'], 'v_b_mod': out['v_b_mod'], 'v_norm1_w': out['v_norm1_w'], 'v_w_in': out['v_w_in'], 'v_q_norm_w': out['v_q_norm_w'], 'v_k_norm_w': out['v_k_norm_w'], 'v_conv_w': out['v_conv_w'], 'v_a_log': out['v_a_log'], 'v_dt_bias': out['v_dt_bias'], 'v_o_norm_w': out['v_o_norm_w'], 'v_p_a': out['v_p_a'], 'v_p_b': out['v_p_b'], 'v_w_out': out['v_w_out'], 'v_norm2_w': out['v_norm2_w'], 'v_w_gate': out['v_w_gate'], 'v_w_up': out['v_w_up'], 'v_w_down': out['v_w_down']}


def _loss(weights, diff, rest, loss_target):
    with _jax.named_scope("forward"):
        args = {**rest, TWIN_DIFF_INPUT: diff, **{k: w.astype(_WEIGHT_DTYPES[k]) for k, w in weights.items()}}
        y = _forward(args)
    with _jax.named_scope("loss_head"):
        err = _jnp.square(y.astype(_jnp.float32) - loss_target)
        return 0.5 * _jnp.sum(_jnp.mean(err, axis=-1)) if err.ndim else 0.5 * err


def _adamw(w, g, m, v):
    m = ADAM_B1 * m + (1.0 - ADAM_B1) * g
    v = ADAM_B2 * v + (1.0 - ADAM_B2) * _jnp.square(g)
    m_hat = m / (1.0 - ADAM_B1 ** ADAM_STEP)
    v_hat = v / (1.0 - ADAM_B2 ** ADAM_STEP)
    delta = -ADAM_LR * (m_hat / (_jnp.sqrt(v_hat) + ADAM_EPS) + ADAM_WD * w)
    return delta, m, v


def reference(x, c, w_mod, b_mod, norm1_w, w_in, q_norm_w, k_norm_w, conv_w, a_log, dt_bias, o_norm_w, p_a, p_b, w_out, norm2_w, w_gate, w_up, w_down, loss_target, m_w_mod, m_b_mod, m_norm1_w, m_w_in, m_q_norm_w, m_k_norm_w, m_conv_w, m_a_log, m_dt_bias, m_o_norm_w, m_p_a, m_p_b, m_w_out, m_norm2_w, m_w_gate, m_w_up, m_w_down, v_w_mod, v_b_mod, v_norm1_w, v_w_in, v_q_norm_w, v_k_norm_w, v_conv_w, v_a_log, v_dt_bias, v_o_norm_w, v_p_a, v_p_b, v_w_out, v_norm2_w, v_w_gate, v_w_up, v_w_down):
    given = dict(x=x, c=c, w_mod=w_mod, b_mod=b_mod, norm1_w=norm1_w, w_in=w_in, q_norm_w=q_norm_w, k_norm_w=k_norm_w, conv_w=conv_w, a_log=a_log, dt_bias=dt_bias, o_norm_w=o_norm_w, p_a=p_a, p_b=p_b, w_out=w_out, norm2_w=norm2_w, w_gate=w_gate, w_up=w_up, w_down=w_down, loss_target=loss_target, m_w_mod=m_w_mod, m_b_mod=m_b_mod, m_norm1_w=m_norm1_w, m_w_in=m_w_in, m_q_norm_w=m_q_norm_w, m_k_norm_w=m_k_norm_w, m_conv_w=m_conv_w, m_a_log=m_a_log, m_dt_bias=m_dt_bias, m_o_norm_w=m_o_norm_w, m_p_a=m_p_a, m_p_b=m_p_b, m_w_out=m_w_out, m_norm2_w=m_norm2_w, m_w_gate=m_w_gate, m_w_up=m_w_up, m_w_down=m_w_down, v_w_mod=v_w_mod, v_b_mod=v_b_mod, v_norm1_w=v_norm1_w, v_w_in=v_w_in, v_q_norm_w=v_q_norm_w, v_k_norm_w=v_k_norm_w, v_conv_w=v_conv_w, v_a_log=v_a_log, v_dt_bias=v_dt_bias, v_o_norm_w=v_o_norm_w, v_p_a=v_p_a, v_p_b=v_p_b, v_w_out=v_w_out, v_norm2_w=v_norm2_w, v_w_gate=v_w_gate, v_w_up=v_w_up, v_w_down=v_w_down)
    weights = {n: given[n] for n in TWIN_WEIGHTS}
    shared = {n: given[n] for n in SHARED_INPUTS}
    per_example = {n: given[n] for n in ['x', 'c']}
    grad_fn = _jax.value_and_grad(_loss, argnums=(0, 1))

    def one_microbatch(ex, loss_target):
        ex = dict(ex)
        diff = ex.pop(TWIN_DIFF_INPUT)
        return grad_fn(weights, diff, {**shared, **ex}, loss_target)

    if N_MICROBATCH == 1:
        loss, (grad_w, grad_x) = one_microbatch(per_example, given["loss_target"])
    else:
        def body(carry, xs):
            loss_sum, grad_sum = carry
            l_k, (gw_k, gx_k) = one_microbatch(xs[0], xs[1])
            with _jax.named_scope("update"):
                return (loss_sum + l_k, _jax.tree.map(_jnp.add, grad_sum, gw_k)), gx_k

        init = (_jnp.zeros((), _jnp.float32), _jax.tree.map(_jnp.zeros_like, weights))
        (loss, grad_w), grad_x = _jax.lax.scan(body, init, (per_example, given["loss_target"]))
    with _jax.named_scope("update"):
        delta_w, new_m, new_v = {}, {}, {}
        for n in TWIN_WEIGHTS:
            delta_w[n], new_m[n], new_v[n] = _adamw(weights[n], grad_w[n], given["m_" + n], given["v_" + n])
    return (loss, grad_x, *[grad_w[n] for n in TWIN_WEIGHTS], *[delta_w[n] for n in TWIN_WEIGHTS],
            *[new_m[n] for n in TWIN_WEIGHTS], *[new_v[n] for n in TWIN_WEIGHTS])
```

```python
import functools

import jax
import jax.numpy as jnp
from jax import lax
from jax.experimental import pallas as pl
from jax.experimental.pallas import tpu as pltpu

F32 = jnp.float32
BF16 = jnp.bfloat16
HIGHEST = lax.Precision.HIGHEST
MESH = pl.DeviceIdType.MESH

HEAD_DIM = 128
GDN_CHUNK = 64
GDN_CONV = 4
EPS = 1e-6
LANES = 128
SUBLANES = 8
VMEM_LIMIT = 56 * 1024 * 1024

ADAM_LR = 0.001
ADAM_B1 = 0.9
ADAM_B2 = 0.999
ADAM_EPS = 1e-08
ADAM_WD = 0.01
ADAM_STEP = 10


def _pcall(body, **kw):
    return pl.pallas_call(body, **kw)


def _params(sem=None):
    if sem is None:
        return pltpu.CompilerParams(vmem_limit_bytes=VMEM_LIMIT)
    return pltpu.CompilerParams(dimension_semantics=sem, vmem_limit_bytes=VMEM_LIMIT)


def _pick(dim, target):
    if dim <= target:
        return dim
    best = None
    for t in range(LANES, target + 1, LANES):
        if dim % t == 0:
            best = t
    assert best is not None, (dim, target)
    return best


def _rows_tile(rows, target):
    t = min(rows, target)
    while rows % t:
        t //= 2
    assert t >= SUBLANES or t == rows, (rows, target)
    return t


def _dot(a, b, hi=False):
    return jnp.dot(a, b, preferred_element_type=F32, precision=HIGHEST if hi else None)


def _dot_nt(a, b, hi=False):
    return lax.dot_general(a, b, (((1,), (1,)), ((), ())), preferred_element_type=F32,
                           precision=HIGHEST if hi else None)


def _dot_tn(a, b, hi=False):
    return lax.dot_general(a, b, (((0,), (0,)), ((), ())), preferred_element_type=F32,
                           precision=HIGHEST if hi else None)


def _sigmoid(x):
    return 1.0 / (1.0 + jnp.exp(-x))


def _softplus(x):
    return jnp.maximum(x, 0.0) + jnp.log(1.0 + jnp.exp(-jnp.abs(x)))


_HBM = pl.BlockSpec(memory_space=pltpu.HBM)


def _my_pos():
    return lax.axis_index("x"), lax.axis_index("y"), lax.axis_index("c")


def _allgather8(name, v):
    def body(v_ref, o_ref, ssem, rsem, lsem):
        x, y, c = _my_pos()
        me = 4 * x + 2 * y + c
        loc = pltpu.make_async_copy(v_ref, o_ref.at[me], lsem)
        loc.start()
        sends, recvs = [], []
        for k in range(1, 8):
            px, py, pc = (x + (k >> 2)) % 2, (y + ((k >> 1) & 1)) % 2, (c + (k & 1)) % 2
            cp = pltpu.make_async_remote_copy(
                src_ref=v_ref, dst_ref=o_ref.at[me], send_sem=ssem.at[k - 1], recv_sem=rsem.at[k - 1],
                device_id=(px, py, pc), device_id_type=MESH)
            cp.start()
            sends.append(cp)
            recvs.append(pltpu.make_async_remote_copy(
                src_ref=v_ref, dst_ref=o_ref.at[4 * px + 2 * py + pc], send_sem=ssem.at[k - 1],
                recv_sem=rsem.at[k - 1], device_id=(px, py, pc), device_id_type=MESH))
        for rc in recvs:
            rc.wait_recv()
        for cp in sends:
            cp.wait_send()
        loc.wait()

    return _pcall(
        body, name=name, out_shape=jax.ShapeDtypeStruct((8,) + v.shape, v.dtype),
        in_specs=[_HBM], out_specs=_HBM,
        scratch_shapes=[pltpu.SemaphoreType.DMA((7,)), pltpu.SemaphoreType.DMA((7,)), pltpu.SemaphoreType.DMA],
    )(v)


def _plane_peers(x, y):
    return [((x + (k >> 1)) % 2, (y + (k & 1)) % 2) for k in range(1, 4)]


def _gather4(name, shards):
    n = len(shards)

    def body(*refs):
        ins, outs = refs[:n], refs[n:2 * n]
        ssem, rsem, lsem = refs[2 * n:]
        x, y, c = _my_pos()
        me = 2 * x + y
        peers = _plane_peers(x, y)
        locs, sends, recvs = [], [], []
        for t in range(n):
            loc = pltpu.make_async_copy(ins[t], outs[t].at[me], lsem.at[t])
            loc.start()
            locs.append(loc)
            for k, (px, py) in enumerate(peers):
                cp = pltpu.make_async_remote_copy(
                    src_ref=ins[t], dst_ref=outs[t].at[me], send_sem=ssem.at[3 * t + k],
                    recv_sem=rsem.at[3 * t + k], device_id=(px, py, c), device_id_type=MESH)
                cp.start()
                sends.append(cp)
                recvs.append(pltpu.make_async_remote_copy(
                    src_ref=ins[t], dst_ref=outs[t].at[2 * px + py], send_sem=ssem.at[3 * t + k],
                    recv_sem=rsem.at[3 * t + k], device_id=(px, py, c), device_id_type=MESH))
        for rc in recvs:
            rc.wait_recv()
        for cp in sends:
            cp.wait_send()
        for loc in locs:
            loc.wait()

    return _pcall(
        body, name=name,
        out_shape=[jax.ShapeDtypeStruct((4,) + s.shape, s.dtype) for s in shards],
        in_specs=[_HBM] * n, out_specs=[_HBM] * n,
        scratch_shapes=[pltpu.SemaphoreType.DMA((3 * n,)), pltpu.SemaphoreType.DMA((3 * n,)),
                        pltpu.SemaphoreType.DMA((n,))],
    )(*shards)


def _scatter4(name, partials):
    n = len(partials)

    def body(*refs):
        ins, outs = refs[:n], refs[n:2 * n]
        ssem, rsem, lsem = refs[2 * n:]
        x, y, c = _my_pos()
        me = 2 * x + y
        peers = _plane_peers(x, y)
        locs, sends, recvs = [], [], []
        for t in range(n):
            loc = pltpu.make_async_copy(ins[t].at[me], outs[t].at[me], lsem.at[t])
            loc.start()
            locs.append(loc)
            for k, (px, py) in enumerate(peers):
                peer = 2 * px + py
                cp = pltpu.make_async_remote_copy(
                    src_ref=ins[t].at[peer], dst_ref=outs[t].at[me], send_sem=ssem.at[3 * t + k],
                    recv_sem=rsem.at[3 * t + k], device_id=(px, py, c), device_id_type=MESH)
                cp.start()
                sends.append(cp)
                recvs.append(pltpu.make_async_remote_copy(
                    src_ref=ins[t].at[peer], dst_ref=outs[t].at[peer], send_sem=ssem.at[3 * t + k],
                    recv_sem=rsem.at[3 * t + k], device_id=(px, py, c), device_id_type=MESH))
        for rc in recvs:
            rc.wait_recv()
        for cp in sends:
            cp.wait_send()
        for loc in locs:
            loc.wait()

    return _pcall(
        body, name=name,
        out_shape=[jax.ShapeDtypeStruct(p.shape, p.dtype) for p in partials],
        in_specs=[_HBM] * n, out_specs=[_HBM] * n,
        scratch_shapes=[pltpu.SemaphoreType.DMA((3 * n,)), pltpu.SemaphoreType.DMA((3 * n,)),
                        pltpu.SemaphoreType.DMA((n,))],
    )(*partials)


def _sibling_swap(name, arrays):
    n = len(arrays)

    def body(*refs):
        ins, outs = refs[:n], refs[n:2 * n]
        ssem, rsem = refs[2 * n:]
        x, y, c = _my_pos()
        cps = []
        for t in range(n):
            cp = pltpu.make_async_remote_copy(
                src_ref=ins[t], dst_ref=outs[t], send_sem=ssem.at[t], recv_sem=rsem.at[t],
                device_id=(x, y, 1 - c), device_id_type=MESH)
            cp.start()
            cps.append(cp)
        for cp in cps:
            cp.wait_recv()
        for cp in cps:
            cp.wait_send()

    return _pcall(
        body, name=name,
        out_shape=[jax.ShapeDtypeStruct(a.shape, a.dtype) for a in arrays],
        in_specs=[_HBM] * n, out_specs=[_HBM] * n,
        scratch_shapes=[pltpu.SemaphoreType.DMA((n,)), pltpu.SemaphoreType.DMA((n,))],
    )(*arrays)


def _mm(name, a, b, *, nt=False, out_dtype=F32, add=None, tm=1024, tn=1024, tk=1024):
    m, k = a.shape
    n = b.shape[0] if nt else b.shape[1]
    assert (b.shape[1] if nt else b.shape[0]) == k
    tm, tn, tk = _pick(m, tm), _pick(n, tn), _pick(k, tk)
    nk = k // tk
    has_add = add is not None

    def body(*refs):
        a_ref, b_ref = refs[0], refs[1]
        c_ref = refs[2] if has_add else None
        o_ref, acc = refs[2 + has_add], refs[3 + has_add]
        kk = pl.program_id(2)
        p = (_dot_nt if nt else _dot)(a_ref[...], b_ref[...])

        @pl.when(kk == 0)
        def _():
            acc[...] = p

        @pl.when(kk > 0)
        def _():
            acc[...] += p

        @pl.when(kk == nk - 1)
        def _():
            r = acc[...]
            if has_add:
                r = r + c_ref[...]
            o_ref[...] = r.astype(o_ref.dtype)

    a_spec = pl.BlockSpec((tm, tk), lambda j, i, kk: (i, kk))
    if nt:
        b_spec = pl.BlockSpec((tn, tk), lambda j, i, kk: (j, kk))
    else:
        b_spec = pl.BlockSpec((tk, tn), lambda j, i, kk: (kk, j))
    o_spec = pl.BlockSpec((tm, tn), lambda j, i, kk: (i, j))
    in_specs = [a_spec, b_spec] + ([o_spec] if has_add else [])
    args = (a, b) + ((add,) if has_add else ())
    return _pcall(
        body, name=name, grid=(n // tn, m // tm, nk),
        out_shape=jax.ShapeDtypeStruct((m, n), out_dtype),
        in_specs=in_specs, out_specs=o_spec,
        scratch_shapes=[pltpu.VMEM((tm, tn), F32)],
        compiler_params=_params(("parallel", "parallel", "arbitrary")),
    )(*args)


class Col:
    def __init__(self, arr, w=None, cb=0, lead=None):
        self.arr, self.cb, self.lead = arr, cb, lead
        self.w = arr.shape[-1] if w is None else w
        self.rows = arr.shape[-2]


def _ew(name, fn, *, tr, ins, consts=(), outs=(), accs=(), halo_prev=(), halo_next=()):
    ins = [c if isinstance(c, Col) else Col(c) for c in ins]
    halo_prev = [c if isinstance(c, Col) else Col(c) for c in halo_prev]
    halo_next = [c if isinstance(c, Col) else Col(c) for c in halo_next]
    rows = ins[0].rows
    tr = _rows_tile(rows, tr)
    nt = rows // tr
    n_in, n_hp, n_hn, n_c, n_o, n_a = len(ins), len(halo_prev), len(halo_next), len(consts), len(outs), len(accs)
    groups = tr // SUBLANES

    def spec(col, kind):
        if kind == "cur":
            shape, idx = (tr, col.w), (lambda i, cb=col.cb: (i, cb))
        elif kind == "prev":
            shape, idx = (SUBLANES, col.w), (lambda i, cb=col.cb: (jnp.maximum(i * groups - 1, 0), cb))
        else:
            shape = (SUBLANES, col.w)
            idx = (lambda i, cb=col.cb: (jnp.minimum((i + 1) * groups, rows // SUBLANES - 1), cb))
        if col.lead is None:
            return pl.BlockSpec(shape, idx)
        return pl.BlockSpec((None,) + shape, lambda i, idx=idx, lead=col.lead: (lead,) + idx(i))

    def body(*refs):
        i = pl.program_id(0)
        p = 0
        tiles = [r[...] for r in refs[p:p + n_in]]; p += n_in
        prev8 = [r[...] for r in refs[p:p + n_hp]]; p += n_hp
        next8 = [r[...] for r in refs[p:p + n_hn]]; p += n_hn
        cvals = [r[...] for r in refs[p:p + n_c]]; p += n_c
        out_refs = refs[p:p + n_o]; p += n_o
        acc_refs = refs[p:p + n_a]
        out_v, acc_v = fn(i, nt, tiles, prev8, next8, cvals)
        for r, v in zip(out_refs, out_v):
            r[...] = v.astype(r.dtype)
        if n_a:
            @pl.when(i == 0)
            def _():
                for r, v in zip(acc_refs, acc_v):
                    r[...] = v

            @pl.when(i > 0)
            def _():
                for r, v in zip(acc_refs, acc_v):
                    r[...] += v

    in_specs = ([spec(c, "cur") for c in ins] + [spec(c, "prev") for c in halo_prev]
                + [spec(c, "next") for c in halo_next]
                + [pl.BlockSpec(c.shape, lambda i, nd=c.ndim: (0,) * nd) for c in consts])
    out_specs = ([pl.BlockSpec((tr, w), lambda i: (i, 0)) for w, _ in outs]
                 + [pl.BlockSpec(s, lambda i: (0, 0)) for s in accs])
    out_shape = ([jax.ShapeDtypeStruct((rows, w), dt) for w, dt in outs]
                 + [jax.ShapeDtypeStruct(s, F32) for s in accs])
    args = [c.arr for c in ins] + [c.arr for c in halo_prev] + [c.arr for c in halo_next] + list(consts)
    res = _pcall(body, name=name, grid=(nt,), out_shape=out_shape, in_specs=in_specs, out_specs=out_specs,
                 compiler_params=_params(("arbitrary",)))(*args)
    return res[:n_o], res[n_o:]


def _colsum(v):
    return jnp.sum(v, axis=0, keepdims=True)


def _heads_of(w):
    return w // HEAD_DIM


def _per_head(fn, *arrays):
    nh = _heads_of(arrays[0].shape[1])
    res = [fn(*[a[:, h * HEAD_DIM:(h + 1) * HEAD_DIM] for a in arrays]) for h in range(nh)]
    if isinstance(res[0], tuple):
        return tuple(jnp.concatenate([r[j] for r in res], axis=1) for j in range(len(res[0])))
    return jnp.concatenate(res, axis=1)


def _head_sum(v):
    nh = _heads_of(v.shape[1])
    out = v[:, :HEAD_DIM]
    for h in range(1, nh):
        out = out + v[:, h * HEAD_DIM:(h + 1) * HEAD_DIM]
    return out


def _rms_fwd(x, w):
    r = lax.rsqrt(jnp.mean(x * x, axis=1, keepdims=True) + EPS)
    return x * r * w


def _rms_bwd(x, w, dy):
    r = lax.rsqrt(jnp.mean(x * x, axis=1, keepdims=True) + EPS)
    xh = x * r
    dxh = dy * w
    dx = r * (dxh - xh * jnp.mean(dxh * xh, axis=1, keepdims=True))
    return dx, dy * xh


def _silu(x):
    return x * _sigmoid(x)


def _dsilu(x):
    s = _sigmoid(x)
    return s * (1.0 + x * (1.0 - s))


def _log1m_sigmoid(z):
    return -(jnp.maximum(z, 0.0) + jnp.log(1.0 + jnp.exp(-jnp.abs(z))))


def _split_dot(v, tri):
    hi = v.astype(BF16)
    lo = (v - hi.astype(F32)).astype(BF16)
    return _dot(hi, tri) + _dot(lo, tri)


def _sb_fwd(qn, kn, vb, *, blk=256):
    s_len, hd = qn.shape
    nh = hd // HEAD_DIM
    b = min(blk, s_len)
    nq = s_len // b
    scale = HEAD_DIM ** -0.5

    def body(q_ref, k_ref, v_ref, o_ref, lt_ref):
        i = pl.program_id(1)
        q = q_ref[...]
        row = lax.broadcasted_iota(jnp.int32, (b, b), 0)
        col = lax.broadcasted_iota(jnp.int32, (b, b), 1)
        later = (row > col).astype(BF16)
        causal = col < row

        def tile(j, run, acc, masked):
            off = pl.multiple_of(j * b, b)
            k = k_ref[pl.ds(off, b), :]
            v = v_ref[pl.ds(off, b), :]
            z = _dot_nt(q, k) * scale
            lm = _log1m_sigmoid(z)
            if masked:
                lm = jnp.where(causal, lm, 0.0)
            cum = _split_dot(lm, later)
            w = jnp.exp(lm + z + cum + run)
            if masked:
                w = jnp.where(causal, w, 0.0)
            acc = acc + _dot(w.astype(BF16), v)
            run = run + cum[:, 0:1] + lm[:, 0:1]
            return run, acc

        run, acc = tile(i, jnp.zeros((b, 1), F32), jnp.zeros((b, HEAD_DIM), F32), True)
        run, acc = lax.fori_loop(0, i, lambda t, c: tile(i - 1 - t, c[0], c[1], False), (run, acc))
        o_ref[...] = acc.astype(o_ref.dtype)
        lt_ref[...] = run

    qspec = pl.BlockSpec((b, HEAD_DIM), lambda h, i: (i, h))
    kspec = pl.BlockSpec((s_len, HEAD_DIM), lambda h, i: (0, h))
    return _pcall(
        body, name="sb_fwd", grid=(nh, nq),
        out_shape=[jax.ShapeDtypeStruct((s_len, hd), BF16), jax.ShapeDtypeStruct((nh, s_len, 1), F32)],
        in_specs=[qspec, kspec, kspec],
        out_specs=[qspec, pl.BlockSpec((None, b, 1), lambda h, i: (h, i, 0))],
        compiler_params=_params(("parallel", "arbitrary")),
    )(qn, kn, vb)


def _sb_bwd(qn, kn, vb, do, ltot, *, blk=256):
    s_len, hd = qn.shape
    nh = hd // HEAD_DIM
    b = min(blk, s_len)
    nq = s_len // b
    scale = HEAD_DIM ** -0.5

    def body(q_ref, k_ref, v_ref, do_ref, lt_ref, dq_ref, dk_ref, dv_ref):
        i = pl.program_id(1)

        @pl.when(i == 0)
        def _():
            dk_ref[...] = jnp.zeros_like(dk_ref)
            dv_ref[...] = jnp.zeros_like(dv_ref)

        q = q_ref[...]
        do_t = do_ref[...]
        lt = lt_ref[...]
        row = lax.broadcasted_iota(jnp.int32, (b, b), 0)
        col = lax.broadcasted_iota(jnp.int32, (b, b), 1)
        upto = (row <= col).astype(BF16)
        before = (row < col).astype(BF16)
        causal = col < row

        def tile(j, pre, ecar, dq, masked):
            off = pl.multiple_of(j * b, b)
            k = k_ref[pl.ds(off, b), :]
            v = v_ref[pl.ds(off, b), :]
            z = _dot_nt(q, k) * scale
            lm = _log1m_sigmoid(z)
            if masked:
                lm = jnp.where(causal, lm, 0.0)
            cum = _split_dot(lm, upto)
            w = jnp.exp(lm + z + (lt - (pre + cum)))
            if masked:
                w = jnp.where(causal, w, 0.0)
            e = _dot_nt(do_t, v) * w
            eb = e.astype(BF16)
            ex = _dot(eb, before) + ecar
            dz = (e - jnp.exp(lm + z) * (e + ex)) * scale
            if masked:
                dz = jnp.where(causal, dz, 0.0)
            dzb = dz.astype(BF16)
            dq = dq + _dot(dzb, k)
            dk_ref[pl.ds(off, b), :] += _dot_tn(dzb, q)
            dv_ref[pl.ds(off, b), :] += _dot_tn(w.astype(BF16), do_t)
            pre = pre + cum[:, b - 1:b]
            ecar = ex[:, b - 1:b] + eb[:, b - 1:b].astype(F32)
            return pre, ecar, dq

        init = (jnp.zeros((b, 1), F32), jnp.zeros((b, 1), F32), jnp.zeros((b, HEAD_DIM), F32))
        pre, ecar, dq = lax.fori_loop(0, i, lambda j, c: tile(j, c[0], c[1], c[2], False), init)
        _, _, dq = tile(i, pre, ecar, dq, True)
        dq_ref[...] = dq

    qspec = pl.BlockSpec((b, HEAD_DIM), lambda h, i: (i, h))
    kspec = pl.BlockSpec((s_len, HEAD_DIM), lambda h, i: (0, h))
    return _pcall(
        body, name="sb_bwd", grid=(nh, nq),
        out_shape=[jax.ShapeDtypeStruct((s_len, hd), F32)] * 3,
        in_specs=[qspec, kspec, kspec, qspec, pl.BlockSpec((None, b, 1), lambda h, i: (h, i, 0))],
        out_specs=[qspec, kspec, kspec],
        compiler_params=_params(("parallel", "arbitrary")),
    )(qn, kn, vb, do, ltot)


def _gdn_group(nh):
    return min(4, nh)


def _gdn_chunk_terms(qh, kh, vh, g_r, g_c, b_c):
    c = GDN_CHUNK
    r = lax.broadcasted_iota(jnp.int32, (c, c), 0)
    s = lax.broadcasted_iota(jnp.int32, (c, c), 1)
    tril, stril = r >= s, r > s
    gcc = jnp.sum(jnp.where(tril, g_r, 0.0), axis=1, keepdims=True)
    gcr = jnp.sum(jnp.where(r <= s, g_c, 0.0), axis=0, keepdims=True)
    dm = jnp.where(tril, jnp.exp(jnp.where(tril, gcc - gcr, 0.0)), 0.0)
    kb = kh.astype(BF16)
    kk = _dot_nt(kb, kb)
    qk = _dot_nt(qh.astype(BF16), kb)
    egc = jnp.exp(gcc)
    gcl = gcc[c - 1:c, :]
    t = dict(tril=tril, stril=stril, gcc=gcc, dm=dm, kb=kb, kk=kk, qk=qk, egc=egc,
             ekd=jnp.exp(gcl - gcc), gl=jnp.exp(gcl),
             a=jnp.where(stril, b_c * kk * dm, 0.0),
             bv=b_c * vh, bk=(b_c * egc) * kh, at=jnp.where(tril, qk * dm, 0.0))
    t["qg"] = qh * egc
    t["kd"] = kh * t["ekd"]
    return t


def _unit_lower_inverse(a):
    c = GDN_CHUNK
    r = lax.broadcasted_iota(jnp.int32, (c, c), 0)
    s = lax.broadcasted_iota(jnp.int32, (c, c), 1)
    p = -a
    t = (r == s).astype(F32) + p
    steps = 0
    span = 2
    while span < c:
        p = _dot(p, p, hi=True)
        t = t + _dot(t, p, hi=True)
        span *= 2
        steps += 1
    return t


def _gdn_fwd(q, k, v, g_col, g_row, b_col, b_row):
    s_len, d = q.shape
    nh = d // HEAD_DIM
    c = GDN_CHUNK
    n_chunks = s_len // c
    grp = _gdn_group(nh)

    def body(q_ref, k_ref, v_ref, gc_ref, gr_ref, bc_ref, br_ref, o_ref, ss_ref, ts_ref, st):
        n = pl.program_id(1)

        @pl.when(n == 0)
        def _():
            st[...] = jnp.zeros_like(st)

        for i in range(grp):
            sl = slice(i * HEAD_DIM, (i + 1) * HEAD_DIM)
            t = _gdn_chunk_terms(q_ref[:, sl], k_ref[:, sl], v_ref[:, sl],
                                 gr_ref[i:i + 1, :], gc_ref[:, i:i + 1], bc_ref[:, i:i + 1])
            tinv = _unit_lower_inverse(t["a"])
            wv = _dot(tinv, t["bv"], hi=True)
            wk = _dot(tinv, t["bk"], hi=True)
            state = st[i]
            sb = state.astype(BF16)
            u = wv - _dot(wk.astype(BF16), sb)
            ub = u.astype(BF16)
            o_ref[:, sl] = _dot(t["qg"].astype(BF16), sb) + _dot(t["at"].astype(BF16), ub)
            ss_ref[i] = state
            ts_ref[i] = tinv
            st[i] = t["gl"] * state + _dot_tn(t["kd"].astype(BF16), ub)

    tok = pl.BlockSpec((c, grp * HEAD_DIM), lambda h, n: (n, h))
    colspec = pl.BlockSpec((None, c, grp), lambda h, n: (h, n, 0))
    rowspec = pl.BlockSpec((None, None, grp, c), lambda h, n: (h, n, 0, 0))
    return _pcall(
        body, name="gdn_fwd", grid=(nh // grp, n_chunks),
        out_shape=[jax.ShapeDtypeStruct((s_len, d), F32),
                   jax.ShapeDtypeStruct((n_chunks, nh, HEAD_DIM, HEAD_DIM), F32),
                   jax.ShapeDtypeStruct((n_chunks, nh, c, c), F32)],
        in_specs=[tok, tok, tok, colspec, rowspec, colspec, rowspec],
        out_specs=[tok, pl.BlockSpec((None, grp, HEAD_DIM, HEAD_DIM), lambda h, n: (n, h, 0, 0)),
                   pl.BlockSpec((None, grp, c, c), lambda h, n: (n, h, 0, 0))],
        scratch_shapes=[pltpu.VMEM((grp, HEAD_DIM, HEAD_DIM), F32)],
        compiler_params=_params(("parallel", "arbitrary")),
    )(q, k, v, g_col, g_row, b_col, b_row)


def _gdn_bwd(q, k, v, g_col, g_row, b_col, b_row, states, tinvs, do):
    s_len, d = q.shape
    nh = d // HEAD_DIM
    c = GDN_CHUNK
    n_chunks = s_len // c
    grp = _gdn_group(nh)

    def body(q_ref, k_ref, v_ref, gc_ref, gr_ref, bc_ref, br_ref, ss_ref, ts_ref, do_ref,
             dq_ref, dk_ref, dv_ref, dgb_ref, dst):
        n = pl.program_id(1)

        @pl.when(n == 0)
        def _():
            dst[...] = jnp.zeros_like(dst)

        r = lax.broadcasted_iota(jnp.int32, (c, c), 0)
        s = lax.broadcasted_iota(jnp.int32, (c, c), 1)
        suffix = (r <= s).astype(F32)
        lane = lax.broadcasted_iota(jnp.int32, (c, LANES), 1)
        dgb = jnp.zeros((c, LANES), F32)
        for i in range(grp):
            sl = slice(i * HEAD_DIM, (i + 1) * HEAD_DIM)
            qh, kh, vh = q_ref[:, sl], k_ref[:, sl], v_ref[:, sl]
            b_c = bc_ref[:, i:i + 1]
            t = _gdn_chunk_terms(qh, kh, vh, gr_ref[i:i + 1, :], gc_ref[:, i:i + 1], b_c)
            tril, stril, dm, kb = t["tril"], t["stril"], t["dm"], t["kb"]
            tinv = ts_ref[i]
            state = ss_ref[i]
            sb = state.astype(BF16)
            d_next = dst[i]
            dnb = d_next.astype(BF16)
            dob = do_ref[:, sl].astype(BF16)
            wv = _dot(tinv, t["bv"], hi=True)
            wk = _dot(tinv, t["bk"], hi=True)
            wkb = wk.astype(BF16)
            u = wv - _dot(wkb, sb)
            ub = u.astype(BF16)
            atb = t["at"].astype(BF16)
            qgb = t["qg"].astype(BF16)
            kdb = t["kd"].astype(BF16)
            du = _dot_tn(atb, dob) + _dot(kdb, dnb)
            dub = du.astype(BF16)
            dat = jnp.where(tril, _dot_nt(dob, ub), 0.0)
            dqg = _dot_nt(dob, sb)
            dkd = _dot_nt(ub, dnb)
            dgl = jnp.sum(jnp.sum(d_next * state, axis=1, keepdims=True), axis=0, keepdims=True)
            dwk = -_dot_nt(dub, sb)
            dst[i] = t["gl"] * d_next + _dot_tn(qgb, dob) - _dot_tn(wkb, dub)
            dbv = _dot_tn(tinv, du, hi=True)
            dbk = _dot_tn(tinv, dwk, hi=True)
            dt = _dot_nt(du, t["bv"], hi=True) + _dot_nt(dwk, t["bk"], hi=True)
            da = -jnp.where(stril, _dot_tn(tinv, _dot_nt(dt, tinv, hi=True), hi=True), 0.0)
            dkk = da * b_c * dm
            ddm = da * b_c * t["kk"] + dat * t["qk"]
            dqk = dat * dm
            dqkb, dkkb = dqk.astype(BF16), dkk.astype(BF16)
            egc, ekd = t["egc"], t["ekd"]
            dq_ref[:, sl] = _dot(dqkb, kb) + dqg * egc
            dk_ref[:, sl] = (_dot_tn(dqkb, qh.astype(BF16)) + _dot(dkkb, kb) + _dot_tn(dkkb, kb)
                             + dbk * (b_c * egc) + dkd * ekd)
            dv_ref[:, sl] = dbv * b_c
            rs = lambda m: jnp.sum(m, axis=1, keepdims=True)
            dbk_k = rs(dbk * kh)
            dbeta = rs(da * t["kk"] * dm) + rs(dbv * vh) + dbk_k * egc
            mx = ddm * dm
            ekd_sum = rs(dkd * kh) * ekd
            dgc = rs(mx) + dbk_k * b_c * egc + rs(dqg * qh) * egc - ekd_sum
            tail = jnp.sum(ekd_sum, axis=0, keepdims=True) + dgl * t["gl"]
            dg = (_dot(suffix, jnp.broadcast_to(dgc, (c, LANES)), hi=True)[:, 0:1]
                  - rs(_dot_nt(suffix, mx, hi=True)) + tail)
            dgb = dgb + jnp.where(lane == i, dbeta, 0.0) + jnp.where(lane == grp + i, dg, 0.0)
        dgb_ref[...] = dgb

    last = n_chunks - 1
    tok = pl.BlockSpec((c, grp * HEAD_DIM), lambda h, n: (last - n, h))
    colspec = pl.BlockSpec((None, c, grp), lambda h, n: (h, last - n, 0))
    rowspec = pl.BlockSpec((None, None, grp, c), lambda h, n: (h, last - n, 0, 0))
    return _pcall(
        body, name="gdn_bwd", grid=(nh // grp, n_chunks),
        out_shape=[jax.ShapeDtypeStruct((s_len, d), F32)] * 3
        + [jax.ShapeDtypeStruct((nh // grp, s_len, LANES), F32)],
        in_specs=[tok, tok, tok, colspec, rowspec, colspec, rowspec,
                  pl.BlockSpec((None, grp, HEAD_DIM, HEAD_DIM), lambda h, n: (last - n, h, 0, 0)),
                  pl.BlockSpec((None, grp, c, c), lambda h, n: (last - n, h, 0, 0)), tok],
        out_specs=[tok, tok, tok, pl.BlockSpec((None, c, LANES), lambda h, n: (h, last - n, 0))],
        scratch_shapes=[pltpu.VMEM((grp, HEAD_DIM, HEAD_DIM), F32)],
        compiler_params=_params(("parallel", "arbitrary")),
    )(q, k, v, g_col, g_row, b_col, b_row, states, tinvs, do)


def _shift_down(prev8, cur, k):
    if k == 0:
        return cur
    ext = jnp.concatenate([prev8, cur], axis=0)
    return pltpu.roll(ext, k, 0)[SUBLANES:, :]


def _shift_up(cur, next8, k):
    if k == 0:
        return cur
    ext = jnp.concatenate([cur, next8], axis=0)
    n = ext.shape[0]
    return pltpu.roll(ext, n - k, 0)[:cur.shape[0], :]


def _conv_pre(i, x, prev8, w):
    prev8 = jnp.where(i == 0, 0.0, prev8)
    pre = None
    for j in range(GDN_CONV):
        term = w[j:j + 1, :] * _shift_down(prev8, x, GDN_CONV - 1 - j)
        pre = term if pre is None else pre + term
    return pre, prev8


def _l2_fwd(a, mult):
    return a * (lax.rsqrt(jnp.sum(a * a, axis=1, keepdims=True) + EPS) * mult)


def _l2_bwd(a, dy, mult):
    r = lax.rsqrt(jnp.sum(a * a, axis=1, keepdims=True) + EPS)
    dy = dy * mult
    return r * dy - a * (r * r * r) * jnp.sum(a * dy, axis=1, keepdims=True)


def _conv_fwd(xb, conv_w, group, *, norm, mult, tr=256):
    d = xb.shape[1] // 3

    def fn(i, nt, tiles, prev8, next8, cv):
        pre, _ = _conv_pre(i, tiles[0], prev8[0], cv[0])
        a = _silu(pre)
        if norm:
            a = _per_head(lambda ah: _l2_fwd(ah, mult), a)
        return [a], []

    col = Col(xb, d, group)
    wg = lax.slice_in_dim(conv_w, group * d, (group + 1) * d, axis=1)
    (y,), _ = _ew(f"conv_fwd{group}", fn, tr=tr, ins=[col], halo_prev=[col], consts=[wg], outs=[(d, F32)])
    return y


def _conv_bwd(xb, conv_w, group, dy, *, norm, mult, tr=256):
    d = xb.shape[1] // 3
    col = Col(xb, d, group)
    wg = lax.slice_in_dim(conv_w, group * d, (group + 1) * d, axis=1)

    def fn_pre(i, nt, tiles, prev8, next8, cv):
        x, dyt = tiles
        pre, p8 = _conv_pre(i, x, prev8[0], cv[0])
        if norm:
            da = _per_head(lambda ah, dh: _l2_bwd(ah, dh, mult), _silu(pre), dyt)
        else:
            da = dyt
        dpre = da * _dsilu(pre)
        tap = lax.broadcasted_iota(jnp.int32, (GDN_CONV, d), 0)
        dw = jnp.zeros((GDN_CONV, d), F32)
        for j in range(GDN_CONV):
            dw = dw + jnp.where(tap == j, _colsum(dpre * _shift_down(p8, x, GDN_CONV - 1 - j)), 0.0)
        return [dpre], [dw]

    (dpre,), (dw,) = _ew(f"conv_bwd_pre{group}", fn_pre, tr=tr, ins=[col, dy], halo_prev=[col], consts=[wg],
                         outs=[(d, F32)], accs=[(GDN_CONV, d)])

    def fn_dx(i, nt, tiles, prev8, next8, cv):
        n8 = jnp.where(i == nt - 1, 0.0, next8[0])
        dx = None
        for j in range(GDN_CONV):
            term = cv[0][j:j + 1, :] * _shift_up(tiles[0], n8, GDN_CONV - 1 - j)
            dx = term if dx is None else dx + term
        return [dx], []

    (dx,), _ = _ew(f"conv_bwd_dx{group}", fn_dx, tr=tr, ins=[dpre], halo_next=[dpre], consts=[wg], outs=[(d, BF16)])
    return dx, dw


def _adamw(name, w, m, v, grads, *, tr=64):
    shape = w.shape
    w2, m2, v2 = [a.reshape(-1, shape[-1]) for a in (w, m, v)]
    n_g = len(grads)
    bc1 = 1.0 - ADAM_B1 ** ADAM_STEP
    bc2 = 1.0 - ADAM_B2 ** ADAM_STEP

    def fn(i, nt, tiles, prev8, next8, cv):
        wt, mt, vt = tiles[:3]
        g = tiles[3]
        for extra in tiles[4:]:
            g = g + extra
        mn = ADAM_B1 * mt + (1.0 - ADAM_B1) * g
        vn = ADAM_B2 * vt + (1.0 - ADAM_B2) * (g * g)
        delta = -ADAM_LR * ((mn / bc1) / (jnp.sqrt(vn / bc2) + ADAM_EPS) + ADAM_WD * wt)
        return [g, delta, mn, vn], []

    width = shape[-1]
    outs, _ = _ew(name, fn, tr=tr, ins=[w2, m2, v2] + list(grads), outs=[(width, F32)] * 4)
    assert n_g >= 1
    return tuple(o.reshape(shape) for o in outs)


def _pad_cols(a, width):
    return jnp.pad(a, ((0, 0), (0, width - a.shape[1])))


def _gdn_layouts(gbeta, nh, n_chunks):
    grp = _gdn_group(nh)
    s_len = gbeta.shape[0]

    def lay(a):
        col = a.reshape(s_len, nh // grp, grp).transpose(1, 0, 2)
        row = a.reshape(n_chunks, GDN_CHUNK, nh // grp, grp).transpose(2, 0, 3, 1)
        return col, row

    b_col, b_row = lay(gbeta[:, :nh])
    g_col, g_row = lay(gbeta[:, nh:2 * nh])
    return g_col, g_row, b_col, b_row


def kernel(x, c, w_mod, b_mod, norm1_w, w_in, q_norm_w, k_norm_w, conv_w, a_log, dt_bias, o_norm_w, p_a, p_b, w_out, norm2_w, w_gate, w_up, w_down, loss_target, m_w_mod, m_b_mod, m_norm1_w, m_w_in, m_q_norm_w, m_k_norm_w, m_conv_w, m_a_log, m_dt_bias, m_o_norm_w, m_p_a, m_p_b, m_w_out, m_norm2_w, m_w_gate, m_w_up, m_w_down, v_w_mod, v_b_mod, v_norm1_w, v_w_in, v_q_norm_w, v_k_norm_w, v_conv_w, v_a_log, v_dt_bias, v_o_norm_w, v_p_a, v_p_b, v_w_out, v_norm2_w, v_w_gate, v_w_up, v_w_down):
    s_len, d = x.shape[1], x.shape[2]
    nh = d // HEAD_DIM
    n_chunks = s_len // GDN_CHUNK
    ff = 4 * w_gate.shape[2]
    mx, my, mc = _my_pos()
    chip = 2 * mx + my
    dev = 2 * chip + mc
    x2 = x[0]
    tgt = loss_target[0]

    c_all = _allgather8("ag_c", _pad_cols(c, d).reshape(SUBLANES, d // SUBLANES)).reshape(8, d)
    wm = w_mod[0]
    mod_w = wm.shape[1]
    bm_cols = lax.dynamic_slice_in_dim(b_mod, chip * mod_w, mod_w, axis=1)

    def mod_body(c_ref, w_ref, b_ref, o_ref, ca_ref):
        ca = _silu(c_ref[...])
        ca_ref[...] = ca
        o_ref[...] = _dot(ca, w_ref[...], hi=True) + b_ref[...]

    tn_mod = _pick(mod_w, 512)
    mod8, c_act = _pcall(
        mod_body, name="mod_fwd", grid=(mod_w // tn_mod,),
        out_shape=[jax.ShapeDtypeStruct((8, mod_w), F32), jax.ShapeDtypeStruct((8, d), F32)],
        in_specs=[pl.BlockSpec((8, d), lambda j: (0, 0)), pl.BlockSpec((d, tn_mod), lambda j: (0, j)),
                  pl.BlockSpec((1, tn_mod), lambda j: (0, j))],
        out_specs=[pl.BlockSpec((8, tn_mod), lambda j: (0, j)), pl.BlockSpec((8, d), lambda j: (0, 0))],
        compiler_params=_params(("arbitrary",)),
    )(c_all, wm, bm_cols)
    mod_all = _allgather8("ag_mod", mod8)
    mod_me = mod_all.reshape(4, 2, 8, mod_w)[:, mc, dev, :].reshape(1, 6 * d)
    shift1, scale1, gate1, shift2, scale2, gate2 = [mod_me[:, j * d:(j + 1) * d] for j in range(6)]

    gathered = _gather4("ag_weights", [w_in[0].astype(BF16), p_a[0].astype(BF16), p_b[0].astype(BF16),
                                      w_out[0].astype(BF16), w_gate[0].astype(BF16), w_up[0].astype(BF16),
                                      w_down[0].astype(BF16), conv_w[0]])
    w_in_f = gathered[0].transpose(1, 0, 2).reshape(d, -1)
    wa = w_in_f[:, :3 * d]
    wb = w_in_f[:, 3 * d:6 * d]
    wzg = jnp.concatenate([w_in_f[:, 6 * d:7 * d], w_in_f[:, 7 * d + 2 * nh:]], axis=1)
    wba = _pad_cols(w_in_f[:, 7 * d:7 * d + 2 * nh], LANES)
    p_a_f, p_b_f, w_out_f = [g.reshape(d, d) for g in gathered[1:4]]
    w_gate_f, w_up_f = [g.transpose(1, 0, 2).reshape(d, ff) for g in gathered[4:6]]
    w_down_f = gathered[6].reshape(ff, d)
    conv_f = gathered[7].transpose(1, 0, 2).reshape(GDN_CONV, 3 * d)

    def norm_mod_fn(i, nt, tiles, prev8, next8, cv):
        w, sc, sh = cv
        return [_rms_fwd(tiles[0], w) * (1.0 + sc) + sh], []

    (u1,), _ = _ew("norm_mod1", norm_mod_fn, tr=512, ins=[x2], consts=[norm1_w, scale1, shift1], outs=[(d, BF16)])
    proj_a = _mm("proj_a", u1, wa)
    proj_b = _mm("proj_b", u1, wb)
    proj_zg = _mm("proj_zg", u1, wzg)
    proj_ba = _mm("proj_ba", u1, wba)

    def qknorm_fn(i, nt, tiles, prev8, next8, cv):
        qa, ka, va = tiles
        return [_per_head(lambda h: _rms_fwd(h, cv[0]), qa), _per_head(lambda h: _rms_fwd(h, cv[1]), ka), va], []

    (qn, kn, vb), _ = _ew("qknorm", qknorm_fn, tr=256,
                          ins=[Col(proj_a, d, 0), Col(proj_a, d, 1), Col(proj_a, d, 2)],
                          consts=[q_norm_w, k_norm_w], outs=[(d, BF16)] * 3)
    o_a, ltot = _sb_fwd(qn, kn, vb)

    lane_ids = jnp.arange(LANES)
    is_b = (lane_ids < nh)[None, :]
    is_a = ((lane_ids >= nh) & (lane_ids < 2 * nh))[None, :]
    alog128 = jnp.zeros((1, LANES), F32).at[:, nh:2 * nh].set(a_log)
    dtb128 = jnp.zeros((1, LANES), F32).at[:, nh:2 * nh].set(dt_bias)
    is_b_f, is_a_f = is_b.astype(F32), is_a.astype(F32)

    def gbeta_fn(i, nt, tiles, prev8, next8, cv):
        al, dtb, mb, ma = cv
        ba = tiles[0]
        g = -jnp.exp(al) * _softplus(ba + dtb)
        return [jnp.where(mb > 0.5, _sigmoid(ba), jnp.where(ma > 0.5, g, 0.0))], []

    (gbeta,), _ = _ew("gbeta", gbeta_fn, tr=1024, ins=[proj_ba], consts=[alog128, dtb128, is_b_f, is_a_f],
                      outs=[(LANES, F32)])
    g_col, g_row, b_col, b_row = _gdn_layouts(gbeta, nh, n_chunks)
    qscale = HEAD_DIM ** -0.5
    q_b = _conv_fwd(proj_b, conv_f, 0, norm=True, mult=qscale)
    k_b = _conv_fwd(proj_b, conv_f, 1, norm=True, mult=1.0)
    v_b = _conv_fwd(proj_b, conv_f, 2, norm=False, mult=1.0)
    o_raw, states, tinvs = _gdn_fwd(q_b, k_b, v_b, g_col, g_row, b_col, b_row)

    def gated_norm_fn(i, nt, tiles, prev8, next8, cv):
        o, z = tiles
        return [_per_head(lambda h: _rms_fwd(h, cv[0]), o) * _silu(z)], []

    (o_b,), _ = _ew("gated_norm", gated_norm_fn, tr=256, ins=[o_raw, Col(proj_zg, d, 0)], consts=[o_norm_w],
                    outs=[(d, BF16)])
    y_a = _mm("out_a", o_a, p_a_f)
    y_b = _mm("out_b", o_b, p_b_f)

    def merge_fn(i, nt, tiles, prev8, next8, cv):
        ya, yb, ga, gb = tiles
        return [_sigmoid(ga) * ya + _sigmoid(gb) * yb], []

    (merged,), _ = _ew("merge", merge_fn, tr=256, ins=[y_a, y_b, Col(proj_zg, d, 1), Col(proj_zg, d, 2)],
                       outs=[(d, BF16)])
    y_o = _mm("out_proj", merged, w_out_f)

    def resid_norm_fn(i, nt, tiles, prev8, next8, cv):
        xt, yo = tiles
        g1, w, sc, sh = cv
        h1 = xt + g1 * yo
        return [h1, _rms_fwd(h1, w) * (1.0 + sc) + sh], []

    (h1, u2), _ = _ew("resid_norm2", resid_norm_fn, tr=256, ins=[x2, y_o],
                      consts=[gate1, norm2_w, scale2, shift2], outs=[(d, F32), (d, BF16)])
    gt = _mm("ff_gate", u2, w_gate_f)
    up = _mm("ff_up", u2, w_up_f)

    def swiglu_fn(i, nt, tiles, prev8, next8, cv):
        return [_silu(tiles[0]) * tiles[1]], []

    (act,), _ = _ew("swiglu", swiglu_fn, tr=128, ins=[gt, up], outs=[(ff, BF16)])
    y_d = _mm("ff_down", act, w_down_f)

    def loss_fn(i, nt, tiles, prev8, next8, cv):
        h1t, yd, tg = tiles
        diff = h1t + cv[0] * yd - tg
        dy = diff * (1.0 / d)
        return [dy, dy * cv[0]], [_colsum(0.5 * diff * dy), _colsum(dy * yd)]

    (dy, dyd), (loss_cols, dgate2) = _ew("loss", loss_fn, tr=256, ins=[h1, y_d, tgt], consts=[gate2],
                                         outs=[(d, F32), (d, BF16)], accs=[(1, d), (1, d)])
    loss = lax.psum(jnp.sum(loss_cols), ("x", "y", "c"))

    dact = _mm("d_act", dyd, w_down_f, nt=True)
    g_w_down = _mm("g_w_down", act.T, dyd)

    def swiglu_bwd_fn(i, nt, tiles, prev8, next8, cv):
        da, g, u = tiles
        return [da * u * _dsilu(g), da * _silu(g)], []

    (dgt, dup), _ = _ew("swiglu_bwd", swiglu_bwd_fn, tr=128, ins=[dact, gt, up], outs=[(ff, BF16)] * 2)
    du2 = _mm("d_u2_up", dup, w_up_f, nt=True, add=_mm("d_u2_gate", dgt, w_gate_f, nt=True))
    u2_t = u2.T
    g_w_gate = _mm("g_w_gate", u2_t, dgt)
    g_w_up = _mm("g_w_up", u2_t, dup)

    def norm2_bwd_fn(i, nt, tiles, prev8, next8, cv):
        h1t, du, dres, yo = tiles
        w, sc, g1 = cv
        r = lax.rsqrt(jnp.mean(h1t * h1t, axis=1, keepdims=True) + EPS)
        nrm = h1t * r
        dn = du * w * (1.0 + sc)
        dh = r * (dn - nrm * jnp.mean(dn * nrm, axis=1, keepdims=True)) + dres
        return [dh, dh * g1], [_colsum(du), _colsum(du * nrm * w), _colsum(du * nrm * (1.0 + sc)), _colsum(dh * yo)]

    (dh1, dyo), (dshift2, dscale2, g_norm2, dgate1) = _ew(
        "norm2_bwd", norm2_bwd_fn, tr=256, ins=[h1, du2, dy, y_o], consts=[norm2_w, scale2, gate1],
        outs=[(d, F32), (d, BF16)], accs=[(1, d)] * 4)

    dmerged = _mm("d_merged", dyo, w_out_f, nt=True)
    g_w_out = _mm("g_w_out", merged.T, dyo)

    def merge_bwd_fn(i, nt, tiles, prev8, next8, cv):
        dm, ya, yb, ga, gb = tiles
        sa, sb = _sigmoid(ga), _sigmoid(gb)
        return [dm * sa, dm * sb, dm * ya * sa * (1.0 - sa), dm * yb * sb * (1.0 - sb)], []

    (dya, dyb, dga, dgb_gate), _ = _ew(
        "merge_bwd", merge_bwd_fn, tr=256, ins=[dmerged, y_a, y_b, Col(proj_zg, d, 1), Col(proj_zg, d, 2)],
        outs=[(d, BF16)] * 4)
    do_a = _mm("d_o_a", dya, p_a_f, nt=True, out_dtype=BF16)
    g_p_a = _mm("g_p_a", o_a.T, dya)
    do_b = _mm("d_o_b", dyb, p_b_f, nt=True)
    g_p_b = _mm("g_p_b", o_b.T, dyb)

    def gated_norm_bwd_fn(i, nt, tiles, prev8, next8, cv):
        dob, o, z = tiles
        sz = _silu(z)

        def head(oh, dh):
            return _rms_bwd(oh, cv[0], dh)

        dxo, dwn = _per_head(head, o, dob * sz)
        nrm_w = _per_head(lambda h: _rms_fwd(h, cv[0]), o)
        return [dxo, dob * nrm_w * _dsilu(z)], [_colsum(_head_sum(dwn))]

    (do_raw, dz_b), (g_o_norm,) = _ew(
        "gated_norm_bwd", gated_norm_bwd_fn, tr=256, ins=[do_b, o_raw, Col(proj_zg, d, 0)], consts=[o_norm_w],
        outs=[(d, F32), (d, BF16)], accs=[(1, HEAD_DIM)])
    dq_b, dk_b, dv_b, dgb_grp = _gdn_bwd(q_b, k_b, v_b, g_col, g_row, b_col, b_row, states, tinvs, do_raw)
    grp = _gdn_group(nh)
    dbeta = dgb_grp[:, :, :grp].transpose(1, 0, 2).reshape(s_len, nh)
    dg = dgb_grp[:, :, grp:2 * grp].transpose(1, 0, 2).reshape(s_len, nh)
    dgbeta = _pad_cols(jnp.concatenate([dbeta, dg], axis=1), LANES)

    def gbeta_bwd_fn(i, nt, tiles, prev8, next8, cv):
        al, dtb, mb, ma = cv
        ba, dgb = tiles
        beta = _sigmoid(ba)
        arg = ba + dtb
        da = dgb * (-jnp.exp(al)) * _sigmoid(arg)
        g = -jnp.exp(al) * _softplus(arg)
        dba = jnp.where(mb > 0.5, dgb * beta * (1.0 - beta), jnp.where(ma > 0.5, da, 0.0))
        return [dba], [_colsum(jnp.where(ma > 0.5, dgb * g, 0.0)), _colsum(jnp.where(ma > 0.5, da, 0.0))]

    (dba,), (g_alog128, g_dtb128) = _ew(
        "gbeta_bwd", gbeta_bwd_fn, tr=1024, ins=[proj_ba, dgbeta], consts=[alog128, dtb128, is_b_f, is_a_f],
        outs=[(LANES, BF16)], accs=[(1, LANES)] * 2)
    dxq, g_conv_q = _conv_bwd(proj_b, conv_f, 0, dq_b, norm=True, mult=qscale)
    dxk, g_conv_k = _conv_bwd(proj_b, conv_f, 1, dk_b, norm=True, mult=1.0)
    dxv, g_conv_v = _conv_bwd(proj_b, conv_f, 2, dv_b, norm=False, mult=1.0)
    g_conv = jnp.concatenate([g_conv_q, g_conv_k, g_conv_v], axis=1)

    dqn, dkn, dvb = _sb_bwd(qn, kn, vb, do_a, ltot)

    def qknorm_bwd_fn(i, nt, tiles, prev8, next8, cv):
        qa, ka, dq, dk, dv = tiles
        dxq_, dwq = _per_head(lambda h, g: _rms_bwd(h, cv[0], g), qa, dq)
        dxk_, dwk = _per_head(lambda h, g: _rms_bwd(h, cv[1], g), ka, dk)
        return [dxq_, dxk_, dv], [_colsum(_head_sum(dwq)), _colsum(_head_sum(dwk))]

    (dqa, dka, dva), (g_q_norm, g_k_norm) = _ew(
        "qknorm_bwd", qknorm_bwd_fn, tr=256, ins=[Col(proj_a, d, 0), Col(proj_a, d, 1), dqn, dkn, dvb],
        consts=[q_norm_w, k_norm_w], outs=[(d, BF16)] * 3, accs=[(1, HEAD_DIM)] * 2)

    u1_t = u1.T
    d_a = jnp.concatenate([dqa, dka, dva], axis=1)
    d_b = jnp.concatenate([dxq, dxk, dxv], axis=1)
    d_zg = jnp.concatenate([dz_b, dga, dgb_gate], axis=1)
    du1 = _mm("d_u1_a", d_a, wa, nt=True)
    du1 = _mm("d_u1_b", d_b, wb, nt=True, add=du1)
    du1 = _mm("d_u1_zg", d_zg, wzg, nt=True, add=du1)
    du1 = _mm("d_u1_ba", dba, wba, nt=True, add=du1)
    g_wa = _mm("g_w_in_a", u1_t, d_a)
    g_wb = _mm("g_w_in_b", u1_t, d_b)
    g_wzg = _mm("g_w_in_zg", u1_t, d_zg)
    g_wba = _mm("g_w_in_ba", u1_t, dba)
    g_w_in_f = jnp.concatenate([g_wa, g_wb, g_wzg[:, :d], g_wba[:, :2 * nh], g_wzg[:, d:]], axis=1)

    def norm1_bwd_fn(i, nt, tiles, prev8, next8, cv):
        xt, du, dres = tiles
        w, sc = cv
        r = lax.rsqrt(jnp.mean(xt * xt, axis=1, keepdims=True) + EPS)
        nrm = xt * r
        dn = du * w * (1.0 + sc)
        dxt = r * (dn - nrm * jnp.mean(dn * nrm, axis=1, keepdims=True)) + dres
        return [dxt], [_colsum(du), _colsum(du * nrm * w), _colsum(du * nrm * (1.0 + sc))]

    (grad_x,), (dshift1, dscale1, g_norm1) = _ew(
        "norm1_bwd", norm1_bwd_fn, tr=256, ins=[x2, du1, dh1], consts=[norm1_w, scale1],
        outs=[(d, F32)], accs=[(1, d)] * 3)

    dmod_me = jnp.concatenate([dshift1, dscale1, dgate1, dshift2, dscale2, dgate2], axis=1)
    small = jnp.concatenate(
        [dmod_me, g_norm1, g_norm2, g_q_norm, g_k_norm, g_o_norm, g_alog128[:, nh:2 * nh], g_dtb128[:, nh:2 * nh],
         g_conv.reshape(1, -1)], axis=1)
    n_small = small.shape[1]
    pad_to = -(-n_small // (SUBLANES * LANES)) * (SUBLANES * LANES)
    small_all = _allgather8("ag_small", _pad_cols(small, pad_to).reshape(SUBLANES, pad_to // SUBLANES))
    small_all = small_all.reshape(8, pad_to)

    def sum8_fn(i, nt, tiles, prev8, next8, cv):
        return [], [_colsum(tiles[0])]

    _, (small_sum,) = _ew("sum_small", sum8_fn, tr=8, ins=[small_all], accs=[(1, pad_to)])
    offs = [0]
    for width in (6 * d, d, d, HEAD_DIM, HEAD_DIM, HEAD_DIM, nh, nh, GDN_CONV * 3 * d):
        offs.append(offs[-1] + width)
    pieces = [small_sum[:, offs[j]:offs[j + 1]] for j in range(9)]
    (gs_b_mod, gs_norm1, gs_norm2, gs_q_norm, gs_k_norm, gs_o_norm, gs_a_log, gs_dt_bias, gs_conv) = pieces
    conv_cols = 3 * d // 4
    gs_conv_mine = lax.dynamic_slice_in_dim(gs_conv.reshape(GDN_CONV, 3 * d), chip * conv_cols, conv_cols, axis=1)

    dmod_all = lax.dynamic_slice_in_dim(small_all[:, :6 * d], chip * mod_w, mod_w, axis=1)

    def wmod_grad_body(ct_ref, dm_ref, o_ref):
        o_ref[...] = _dot(ct_ref[...], dm_ref[...], hi=True)

    g_w_mod = _pcall(
        wmod_grad_body, name="g_w_mod", grid=(mod_w // tn_mod,),
        out_shape=jax.ShapeDtypeStruct((d, mod_w), F32),
        in_specs=[pl.BlockSpec((d, 8), lambda j: (0, 0)), pl.BlockSpec((8, tn_mod), lambda j: (0, j))],
        out_specs=pl.BlockSpec((d, tn_mod), lambda j: (0, j)),
        compiler_params=_params(("arbitrary",)),
    )(c_act.T, dmod_all)

    partials = [
        g_w_in_f.reshape(d, 4, -1).transpose(1, 0, 2),
        g_p_a.reshape(4, d // 4, d), g_p_b.reshape(4, d // 4, d), g_w_out.reshape(4, d // 4, d),
        g_w_gate.reshape(d, 4, ff // 4).transpose(1, 0, 2), g_w_up.reshape(d, 4, ff // 4).transpose(1, 0, 2),
        g_w_down.reshape(4, ff // 4, d),
    ]
    landed = _scatter4("rs_grads", partials)
    plane_sums = []
    for t, land in enumerate(landed):
        def sum4_fn(i, nt, tiles, prev8, next8, cv):
            return [(tiles[0] + tiles[1]) + (tiles[2] + tiles[3])], []

        (ps,), _ = _ew(f"plane_sum{t}", sum4_fn, tr=64, ins=[Col(land, lead=s) for s in range(4)],
                       outs=[(land.shape[-1], F32)])
        plane_sums.append(ps)
    others = _sibling_swap("swap_grads", plane_sums)

    big = {}
    names = ["w_in", "p_a", "p_b", "w_out", "w_gate", "w_up", "w_down"]
    big_w = [w_in, p_a, p_b, w_out, w_gate, w_up, w_down]
    big_m = [m_w_in, m_p_a, m_p_b, m_w_out, m_w_gate, m_w_up, m_w_down]
    big_v = [v_w_in, v_p_a, v_p_b, v_w_out, v_w_gate, v_w_up, v_w_down]
    for t, nm in enumerate(names):
        big[nm] = _adamw(f"adamw_{nm}", big_w[t], big_m[t], big_v[t], [plane_sums[t], others[t]])
    big["w_mod"] = _adamw("adamw_w_mod", w_mod, m_w_mod, v_w_mod, [g_w_mod])
    big["conv_w"] = _adamw("adamw_conv_w", conv_w, m_conv_w, v_conv_w, [gs_conv_mine], tr=8)
    small_names = ["b_mod", "norm1_w", "norm2_w", "q_norm_w", "k_norm_w", "o_norm_w", "a_log", "dt_bias"]
    small_w = [b_mod, norm1_w, norm2_w, q_norm_w, k_norm_w, o_norm_w, a_log, dt_bias]
    small_m = [m_b_mod, m_norm1_w, m_norm2_w, m_q_norm_w, m_k_norm_w, m_o_norm_w, m_a_log, m_dt_bias]
    small_v = [v_b_mod, v_norm1_w, v_norm2_w, v_q_norm_w, v_k_norm_w, v_o_norm_w, v_a_log, v_dt_bias]
    small_g = [gs_b_mod, gs_norm1, gs_norm2, gs_q_norm, gs_k_norm, gs_o_norm, gs_a_log, gs_dt_bias]
    rep_w = jnp.concatenate(small_w, axis=1)
    rep_m = jnp.concatenate(small_m, axis=1)
    rep_v = jnp.concatenate(small_v, axis=1)
    rep_g = jnp.concatenate(small_g, axis=1)
    rep = _adamw("adamw_small", rep_w, rep_m, rep_v, [rep_g], tr=1)
    roffs = [0]
    for a in small_w:
        roffs.append(roffs[-1] + a.shape[1])
    for j, nm in enumerate(small_names):
        big[nm] = tuple(r[:, roffs[j]:roffs[j + 1]] for r in rep)

    order = ["w_mod", "b_mod", "norm1_w", "w_in", "q_norm_w", "k_norm_w", "conv_w", "a_log", "dt_bias", "o_norm_w",
             "p_a", "p_b", "w_out", "norm2_w", "w_gate", "w_up", "w_down"]
    grads = [big[nm][0] for nm in order]
    deltas = [big[nm][1] for nm in order]
    new_m = [big[nm][2] for nm in order]
    new_v = [big[nm][3] for nm in order]
    return (loss, grad_x[None], *grads, *deltas, *new_m, *new_v)
```

```python
import jax
import jax.numpy as jnp
from jax import lax
from jax.experimental import pallas as pl
from jax.experimental.pallas import tpu as pltpu

F32 = jnp.float32
BF16 = jnp.bfloat16
HIGHEST = lax.Precision.HIGHEST
HIGH = lax.Precision.HIGH
MESH = pl.DeviceIdType.MESH

HEAD_DIM = 128
GDN_CHUNK = 64
GDN_CONV = 4
EPS = 1e-6
LANES = 128
SUBLANES = 8
VMEM_LIMIT = 56 * 1024 * 1024
MM_VMEM_BUDGET = 40 * 1024 * 1024

ADAM_LR = 0.001
ADAM_B1 = 0.9
ADAM_B2 = 0.999
ADAM_EPS = 1e-08
ADAM_WD = 0.01
ADAM_STEP = 10


def _pcall(body, **kw):
    return pl.pallas_call(body, **kw)


def _params(sem=None):
    if sem is None:
        return pltpu.CompilerParams(vmem_limit_bytes=VMEM_LIMIT)
    return pltpu.CompilerParams(dimension_semantics=sem, vmem_limit_bytes=VMEM_LIMIT)


def _pick(dim, target):
    if dim <= target:
        return dim
    best = None
    for t in range(LANES, target + 1, LANES):
        if dim % t == 0:
            best = t
    assert best is not None, (dim, target)
    return best


def _rows_tile(rows, target):
    t = min(rows, target)
    while rows % t:
        t //= 2
    assert t >= SUBLANES or t == rows, (rows, target)
    return t


def _dot(a, b, hi=None):
    return jnp.dot(a, b, preferred_element_type=F32, precision=hi)


def _dot_nt(a, b, hi=None):
    return lax.dot_general(a, b, (((1,), (1,)), ((), ())), preferred_element_type=F32, precision=hi)


def _dot_tn(a, b, hi=None):
    return lax.dot_general(a, b, (((0,), (0,)), ((), ())), preferred_element_type=F32, precision=hi)


def _sigmoid(x):
    return 1.0 / (1.0 + jnp.exp(-x))


def _softplus(x):
    return jnp.maximum(x, 0.0) + jnp.log(1.0 + jnp.exp(-jnp.abs(x)))


_HBM = pl.BlockSpec(memory_space=pltpu.HBM)


def _my_pos():
    return lax.axis_index("x"), lax.axis_index("y"), lax.axis_index("c")


def _allgather8(name, v):
    def body(v_ref, o_ref, ssem, rsem, lsem):
        x, y, c = _my_pos()
        me = 4 * x + 2 * y + c
        loc = pltpu.make_async_copy(v_ref, o_ref.at[me], lsem)
        loc.start()
        sends, recvs = [], []
        for k in range(1, 8):
            px, py, pc = (x + (k >> 2)) % 2, (y + ((k >> 1) & 1)) % 2, (c + (k & 1)) % 2
            cp = pltpu.make_async_remote_copy(
                src_ref=v_ref, dst_ref=o_ref.at[me], send_sem=ssem.at[k - 1], recv_sem=rsem.at[k - 1],
                device_id=(px, py, pc), device_id_type=MESH)
            cp.start()
            sends.append(cp)
            recvs.append(pltpu.make_async_remote_copy(
                src_ref=v_ref, dst_ref=o_ref.at[4 * px + 2 * py + pc], send_sem=ssem.at[k - 1],
                recv_sem=rsem.at[k - 1], device_id=(px, py, pc), device_id_type=MESH))
        for rc in recvs:
            rc.wait_recv()
        for cp in sends:
            cp.wait_send()
        loc.wait()

    return _pcall(
        body, name=name, out_shape=jax.ShapeDtypeStruct((8,) + v.shape, v.dtype),
        in_specs=[_HBM], out_specs=_HBM,
        scratch_shapes=[pltpu.SemaphoreType.DMA((7,)), pltpu.SemaphoreType.DMA((7,)), pltpu.SemaphoreType.DMA],
    )(v)


def _plane_peers(x, y):
    return [((x + (k >> 1)) % 2, (y + (k & 1)) % 2) for k in range(1, 4)]


def _gather4(name, shards, n_split):
    n = len(shards)

    def body(*refs):
        ins, outs = refs[:n], refs[n:2 * n]
        ssem, rsem, fsem, gsem, lsem = refs[2 * n:]
        x, y, c = _my_pos()
        me = 2 * x + y
        peers = _plane_peers(x, y)
        locs, sends, recvs, fwds, fwd_recvs = [], [], [], [], []
        for t in range(n):
            loc = pltpu.make_async_copy(ins[t], outs[t].at[me], lsem.at[t])
            loc.start()
            locs.append(loc)
            split = t < n_split
            hr = ins[t].shape[0] // 2
            for k, (px, py) in enumerate(peers):
                peer = 2 * px + py
                sem = 3 * t + k
                if split:
                    mine = pl.ds(pl.multiple_of(c * hr, 16), hr)
                    other = pl.ds(pl.multiple_of((1 - c) * hr, 16), hr)
                    src, dst, got = ins[t].at[mine], outs[t].at[me, mine], outs[t].at[peer, mine]
                else:
                    src, dst, got = ins[t], outs[t].at[me], outs[t].at[peer]
                cp = pltpu.make_async_remote_copy(
                    src_ref=src, dst_ref=dst, send_sem=ssem.at[sem], recv_sem=rsem.at[sem],
                    device_id=(px, py, c), device_id_type=MESH)
                cp.start()
                sends.append(cp)
                recvs.append(pltpu.make_async_remote_copy(
                    src_ref=src, dst_ref=got, send_sem=ssem.at[sem], recv_sem=rsem.at[sem],
                    device_id=(px, py, c), device_id_type=MESH))
                if split:
                    fwds.append(pltpu.make_async_remote_copy(
                        src_ref=got, dst_ref=got, send_sem=fsem.at[sem], recv_sem=gsem.at[sem],
                        device_id=(x, y, 1 - c), device_id_type=MESH))
                    fwd_recvs.append(pltpu.make_async_remote_copy(
                        src_ref=got, dst_ref=outs[t].at[peer, other], send_sem=fsem.at[sem], recv_sem=gsem.at[sem],
                        device_id=(x, y, 1 - c), device_id_type=MESH))
                else:
                    fwds.append(None)
        for rc, fw in zip(recvs, fwds):
            rc.wait_recv()
            if fw is not None:
                fw.start()
        for fr in fwd_recvs:
            fr.wait_recv()
        for cp in sends + [fw for fw in fwds if fw is not None]:
            cp.wait_send()
        for loc in locs:
            loc.wait()

    return _pcall(
        body, name=name,
        out_shape=[jax.ShapeDtypeStruct((4,) + s.shape, s.dtype) for s in shards],
        in_specs=[_HBM] * n, out_specs=[_HBM] * n,
        scratch_shapes=[pltpu.SemaphoreType.DMA((3 * n,))] * 4 + [pltpu.SemaphoreType.DMA((n,))],
    )(*shards)


def _scatter4(name, partials):
    n = len(partials)

    def body(*refs):
        ins, outs = refs[:n], refs[n:2 * n]
        ssem, rsem, lsem = refs[2 * n:]
        x, y, c = _my_pos()
        me = 2 * x + y
        peers = _plane_peers(x, y)
        locs, sends, recvs = [], [], []
        for t in range(n):
            loc = pltpu.make_async_copy(ins[t].at[me], outs[t].at[me], lsem.at[t])
            loc.start()
            locs.append(loc)
            for k, (px, py) in enumerate(peers):
                peer = 2 * px + py
                cp = pltpu.make_async_remote_copy(
                    src_ref=ins[t].at[peer], dst_ref=outs[t].at[me], send_sem=ssem.at[3 * t + k],
                    recv_sem=rsem.at[3 * t + k], device_id=(px, py, c), device_id_type=MESH)
                cp.start()
                sends.append(cp)
                recvs.append(pltpu.make_async_remote_copy(
                    src_ref=ins[t].at[peer], dst_ref=outs[t].at[peer], send_sem=ssem.at[3 * t + k],
                    recv_sem=rsem.at[3 * t + k], device_id=(px, py, c), device_id_type=MESH))
        for rc in recvs:
            rc.wait_recv()
        for cp in sends:
            cp.wait_send()
        for loc in locs:
            loc.wait()

    return _pcall(
        body, name=name,
        out_shape=[jax.ShapeDtypeStruct(p.shape, p.dtype) for p in partials],
        in_specs=[_HBM] * n, out_specs=[_HBM] * n,
        scratch_shapes=[pltpu.SemaphoreType.DMA((3 * n,)), pltpu.SemaphoreType.DMA((3 * n,)),
                        pltpu.SemaphoreType.DMA((n,))],
    )(*partials)


def _swap_halves(name, partials):
    n = len(partials)

    def body(*refs):
        ins, mine, theirs = refs[:n], refs[n:2 * n], refs[2 * n:3 * n]
        ssem, rsem, lsem = refs[3 * n:]
        x, y, c = _my_pos()
        cps, locs = [], []
        for t in range(n):
            hr = ins[t].shape[1] // 2
            keep = pl.ds(pl.multiple_of(c * hr, SUBLANES), hr)
            give = pl.ds(pl.multiple_of((1 - c) * hr, SUBLANES), hr)
            loc = pltpu.make_async_copy(ins[t].at[:, keep], mine[t], lsem.at[t])
            loc.start()
            locs.append(loc)
            cp = pltpu.make_async_remote_copy(
                src_ref=ins[t].at[:, give], dst_ref=theirs[t], send_sem=ssem.at[t], recv_sem=rsem.at[t],
                device_id=(x, y, 1 - c), device_id_type=MESH)
            cp.start()
            cps.append(cp)
        for cp in cps:
            cp.wait_recv()
        for cp in cps:
            cp.wait_send()
        for loc in locs:
            loc.wait()

    half = [jax.ShapeDtypeStruct((4, p.shape[1] // 2, p.shape[2]), p.dtype) for p in partials]
    res = _pcall(
        body, name=name, out_shape=half + half, in_specs=[_HBM] * n, out_specs=[_HBM] * (2 * n),
        scratch_shapes=[pltpu.SemaphoreType.DMA((n,)), pltpu.SemaphoreType.DMA((n,)), pltpu.SemaphoreType.DMA((n,))],
    )(*partials)
    return res[:n], res[n:]


def _join_halves(name, halves):
    n = len(halves)

    def body(*refs):
        ins, outs = refs[:n], refs[n:2 * n]
        ssem, rsem, lsem = refs[2 * n:]
        x, y, c = _my_pos()
        cps, recvs, locs = [], [], []
        for t in range(n):
            hr = ins[t].shape[0]
            mine = pl.ds(pl.multiple_of(c * hr, SUBLANES), hr)
            other = pl.ds(pl.multiple_of((1 - c) * hr, SUBLANES), hr)
            loc = pltpu.make_async_copy(ins[t], outs[t].at[mine], lsem.at[t])
            loc.start()
            locs.append(loc)
            cp = pltpu.make_async_remote_copy(
                src_ref=ins[t], dst_ref=outs[t].at[mine], send_sem=ssem.at[t], recv_sem=rsem.at[t],
                device_id=(x, y, 1 - c), device_id_type=MESH)
            cp.start()
            cps.append(cp)
            recvs.append(pltpu.make_async_remote_copy(
                src_ref=ins[t], dst_ref=outs[t].at[other], send_sem=ssem.at[t], recv_sem=rsem.at[t],
                device_id=(x, y, 1 - c), device_id_type=MESH))
        for rc in recvs:
            rc.wait_recv()
        for cp in cps:
            cp.wait_send()
        for loc in locs:
            loc.wait()

    return _pcall(
        body, name=name,
        out_shape=[jax.ShapeDtypeStruct((2 * h.shape[0], h.shape[1]), h.dtype) for h in halves],
        in_specs=[_HBM] * n, out_specs=[_HBM] * n,
        scratch_shapes=[pltpu.SemaphoreType.DMA((n,)), pltpu.SemaphoreType.DMA((n,)), pltpu.SemaphoreType.DMA((n,))],
    )(*halves)


def _mm(name, a, b, *, nt=False, out_dtype=F32, add=None, tm=1024, tn=1024, tk=4096):
    m, k = a.shape
    n = b.shape[0] if nt else b.shape[1]
    assert (b.shape[1] if nt else b.shape[0]) == k
    has_add = add is not None
    tm, tn = _pick(m, tm), _pick(n, tn)
    out_bytes = jnp.dtype(out_dtype).itemsize

    def vmem_bytes(tk_):
        steps = k // tk_
        return (4 * (tm + tn) * tk_ + 2 * tm * tn * out_bytes + (8 * tm * tn if has_add else 0)
                + (4 * tm * tn if steps > 1 else 0))

    tk = _pick(k, tk)
    while vmem_bytes(tk) > MM_VMEM_BUDGET and tk > 512:
        tk = _pick(k, tk - LANES)
    nk = k // tk

    def body(*refs):
        a_ref, b_ref = refs[0], refs[1]
        c_ref = refs[2] if has_add else None
        o_ref = refs[2 + has_add]
        p = (_dot_nt if nt else _dot)(a_ref[...], b_ref[...])
        if nk == 1:
            o_ref[...] = (p + c_ref[...] if has_add else p).astype(o_ref.dtype)
            return
        acc = refs[3 + has_add]
        kk = pl.program_id(2)

        @pl.when(kk == 0)
        def _():
            acc[...] = p

        @pl.when(jnp.logical_and(kk > 0, kk < nk - 1))
        def _():
            acc[...] += p

        @pl.when(kk == nk - 1)
        def _():
            r = acc[...] + p
            if has_add:
                r = r + c_ref[...]
            o_ref[...] = r.astype(o_ref.dtype)

    a_spec = pl.BlockSpec((tm, tk), lambda j, i, kk: (i, kk))
    if nt:
        b_spec = pl.BlockSpec((tn, tk), lambda j, i, kk: (j, kk))
    else:
        b_spec = pl.BlockSpec((tk, tn), lambda j, i, kk: (kk, j))
    o_spec = pl.BlockSpec((tm, tn), lambda j, i, kk: (i, j))
    in_specs = [a_spec, b_spec] + ([o_spec] if has_add else [])
    args = (a, b) + ((add,) if has_add else ())
    return _pcall(
        body, name=name, grid=(n // tn, m // tm, nk),
        out_shape=jax.ShapeDtypeStruct((m, n), out_dtype),
        in_specs=in_specs, out_specs=o_spec,
        scratch_shapes=[pltpu.VMEM((tm, tn), F32)] if nk > 1 else [],
        compiler_params=_params(("parallel", "parallel", "arbitrary")),
    )(*args)


class Col:
    def __init__(self, arr, w=None, cb=0, lead=None):
        self.arr, self.cb, self.lead = arr, cb, lead
        self.w = arr.shape[-1] if w is None else w
        self.rows = arr.shape[-2]


def _ew(name, fn, *, tr, ins, consts=(), outs=(), accs=(), halo_prev=(), halo_next=()):
    ins = [c if isinstance(c, Col) else Col(c) for c in ins]
    halo_prev = [c if isinstance(c, Col) else Col(c) for c in halo_prev]
    halo_next = [c if isinstance(c, Col) else Col(c) for c in halo_next]
    rows = ins[0].rows
    tr = _rows_tile(rows, tr)
    nt = rows // tr
    n_in, n_hp, n_hn, n_c, n_o, n_a = len(ins), len(halo_prev), len(halo_next), len(consts), len(outs), len(accs)
    groups = tr // SUBLANES

    def spec(col, kind):
        if kind == "cur":
            shape, idx = (tr, col.w), (lambda i, cb=col.cb: (i, cb))
        elif kind == "prev":
            shape, idx = (SUBLANES, col.w), (lambda i, cb=col.cb: (jnp.maximum(i * groups - 1, 0), cb))
        else:
            shape = (SUBLANES, col.w)
            idx = (lambda i, cb=col.cb: (jnp.minimum((i + 1) * groups, rows // SUBLANES - 1), cb))
        if col.lead is None:
            return pl.BlockSpec(shape, idx)
        return pl.BlockSpec((None,) + shape, lambda i, idx=idx, lead=col.lead: (lead,) + idx(i))

    def body(*refs):
        i = pl.program_id(0)
        p = 0
        tiles = [r[...] for r in refs[p:p + n_in]]; p += n_in
        prev8 = [r[...] for r in refs[p:p + n_hp]]; p += n_hp
        next8 = [r[...] for r in refs[p:p + n_hn]]; p += n_hn
        cvals = [r[...] for r in refs[p:p + n_c]]; p += n_c
        out_refs = refs[p:p + n_o]; p += n_o
        acc_refs = refs[p:p + n_a]
        out_v, acc_v = fn(i, nt, tiles, prev8, next8, cvals)
        for r, v in zip(out_refs, out_v):
            r[...] = v.astype(r.dtype)
        if n_a:
            @pl.when(i == 0)
            def _():
                for r, v in zip(acc_refs, acc_v):
                    r[...] = v

            @pl.when(i > 0)
            def _():
                for r, v in zip(acc_refs, acc_v):
                    r[...] += v

    in_specs = ([spec(c, "cur") for c in ins] + [spec(c, "prev") for c in halo_prev]
                + [spec(c, "next") for c in halo_next]
                + [pl.BlockSpec(c.shape, lambda i, nd=c.ndim: (0,) * nd) for c in consts])
    out_specs = ([pl.BlockSpec((tr, w), lambda i: (i, 0)) for w, _ in outs]
                 + [pl.BlockSpec(s, lambda i: (0, 0)) for s in accs])
    out_shape = ([jax.ShapeDtypeStruct((rows, w), dt) for w, dt in outs]
                 + [jax.ShapeDtypeStruct(s, F32) for s in accs])
    args = [c.arr for c in ins] + [c.arr for c in halo_prev] + [c.arr for c in halo_next] + list(consts)
    res = _pcall(body, name=name, grid=(nt,), out_shape=out_shape, in_specs=in_specs, out_specs=out_specs,
                 compiler_params=_params(("arbitrary",)))(*args)
    return res[:n_o], res[n_o:]


def _colsum(v):
    return jnp.sum(v, axis=0, keepdims=True)


def _heads_of(w):
    return w // HEAD_DIM


def _per_head(fn, *arrays):
    nh = _heads_of(arrays[0].shape[1])
    res = [fn(*[a[:, h * HEAD_DIM:(h + 1) * HEAD_DIM] for a in arrays]) for h in range(nh)]
    if isinstance(res[0], tuple):
        return tuple(jnp.concatenate([r[j] for r in res], axis=1) for j in range(len(res[0])))
    return jnp.concatenate(res, axis=1)


def _head_sum(v):
    nh = _heads_of(v.shape[1])
    out = v[:, :HEAD_DIM]
    for h in range(1, nh):
        out = out + v[:, h * HEAD_DIM:(h + 1) * HEAD_DIM]
    return out


def _rms_fwd(x, w):
    r = lax.rsqrt(jnp.mean(x * x, axis=1, keepdims=True) + EPS)
    return x * r * w


def _rms_bwd(x, w, dy):
    r = lax.rsqrt(jnp.mean(x * x, axis=1, keepdims=True) + EPS)
    xh = x * r
    dxh = dy * w
    dx = r * (dxh - xh * jnp.mean(dxh * xh, axis=1, keepdims=True))
    return dx, dy * xh


def _silu(x):
    return x * _sigmoid(x)


def _dsilu(x):
    s = _sigmoid(x)
    return s * (1.0 + x * (1.0 - s))


SB_BQ_FWD = 512
SB_BQ_BWD = 1024
SB_PAIR = 2
SB_BK = 256


def _softplus_pos(z):
    return jnp.maximum(z, 0.0) + jnp.log(1.0 + jnp.exp(-jnp.abs(z)))


def _split_dot(v, tri):
    top = lax.bitcast_convert_type(lax.bitcast_convert_type(v, jnp.int32) & jnp.int32(-65536), F32)
    return _dot(top.astype(BF16), tri) + _dot((v - top).astype(BF16), tri)


def _sb_fwd(qn, kn, vb, *, bq=SB_BQ_FWD, bk=SB_BK):
    s_len, hd = qn.shape
    nh = hd // HEAD_DIM
    bk = min(bk, s_len)
    bq = min(bq, s_len)
    ndiag = bq // bk
    scale = HEAD_DIM ** -0.5

    def body(q_ref, k_ref, v_ref, o_ref, lt_ref):
        i = pl.program_id(1)
        krow = lax.broadcasted_iota(jnp.int32, (bk, bk), 0)
        kcol = lax.broadcasted_iota(jnp.int32, (bk, bk), 1)
        later = (krow > kcol).astype(BF16)
        row = lax.broadcasted_iota(jnp.int32, (bq, bk), 0)
        col = lax.broadcasted_iota(jnp.int32, (bq, bk), 1)
        q = q_ref[...]

        def tiles(js, carry, diags):
            run, acc = carry
            ks, vs, zs = [], [], []
            for j in js:
                off = pl.multiple_of(j * bk, bk)
                ks.append(k_ref[pl.ds(off, bk), :])
                vs.append(v_ref[pl.ds(off, bk), :])
                zs.append(_dot_nt(q, ks[-1]) * scale)
            sps, cums, masks = [], [], []
            for z, diag in zip(zs, diags):
                sp = _softplus_pos(z)
                causal = None
                if diag is not None:
                    causal = col + diag * bk < row
                    sp = jnp.where(causal, sp, 0.0)
                sps.append(sp)
                masks.append(causal)
                cums.append(_split_dot(sp, later))
            for z, sp, cum, causal, v in zip(zs, sps, cums, masks, vs):
                w = jnp.exp((z - sp) - (cum + run))
                if causal is not None:
                    w = jnp.where(causal, w, 0.0)
                acc = acc + _dot(w.astype(BF16), v)
                run = run + cum[:, 0:1] + sp[:, 0:1]
            return run, acc

        carry = (jnp.zeros((bq, 1), F32), jnp.zeros((bq, HEAD_DIM), F32))
        for dg in reversed(range(0, ndiag, SB_PAIR)):
            dgs = list(reversed(range(dg, dg + SB_PAIR)))
            carry = tiles([i * ndiag + g for g in dgs], carry, dgs)
        run, acc = lax.fori_loop(
            0, i * ndiag // SB_PAIR,
            lambda t, cr: tiles([i * ndiag - 1 - SB_PAIR * t - u for u in range(SB_PAIR)], cr, [None] * SB_PAIR), carry)
        o_ref[...] = acc.astype(o_ref.dtype)
        lt_ref[...] = run

    qspec = pl.BlockSpec((bq, HEAD_DIM), lambda h, i: (i, h))
    kspec = pl.BlockSpec((s_len, HEAD_DIM), lambda h, i: (0, h))
    return _pcall(
        body, name="sb_fwd", grid=(nh, s_len // bq),
        out_shape=[jax.ShapeDtypeStruct((s_len, hd), BF16), jax.ShapeDtypeStruct((nh, s_len, 1), F32)],
        in_specs=[qspec, kspec, kspec],
        out_specs=[qspec, pl.BlockSpec((None, bq, 1), lambda h, i: (h, i, 0))],
        compiler_params=_params(("parallel", "arbitrary")),
    )(qn, kn, vb)


def _sb_bwd(qn, kn, vb, do, ltot, *, bq=SB_BQ_BWD, bk=SB_BK):
    s_len, hd = qn.shape
    nh = hd // HEAD_DIM
    bk = min(bk, s_len)
    bq = min(bq, s_len)
    ndiag = bq // bk
    scale = HEAD_DIM ** -0.5

    def body(q_ref, k_ref, v_ref, do_ref, lt_ref, dq_ref, dk_ref, dv_ref):
        i = pl.program_id(1)

        @pl.when(i == 0)
        def _():
            dk_ref[...] = jnp.zeros_like(dk_ref)
            dv_ref[...] = jnp.zeros_like(dv_ref)

        krow = lax.broadcasted_iota(jnp.int32, (bk, bk), 0)
        kcol = lax.broadcasted_iota(jnp.int32, (bk, bk), 1)
        upto = (krow <= kcol).astype(BF16)
        before = (krow < kcol).astype(BF16)
        row = lax.broadcasted_iota(jnp.int32, (bq, bk), 0)
        col = lax.broadcasted_iota(jnp.int32, (bq, bk), 1)
        q = q_ref[...]
        do_t = do_ref[...]
        lt = lt_ref[...]

        def tiles(js, carry, diags):
            pre, ecar, dq = carry
            offs, ks, zs, dws = [], [], [], []
            for j in js:
                off = pl.multiple_of(j * bk, bk)
                offs.append(off)
                ks.append(k_ref[pl.ds(off, bk), :])
                zs.append(_dot_nt(q, ks[-1]) * scale)
                dws.append(_dot_nt(do_t, v_ref[pl.ds(off, bk), :]))
            sps, cums, masks = [], [], []
            for z, diag in zip(zs, diags):
                sp = _softplus_pos(z)
                causal = None
                if diag is not None:
                    causal = col + diag * bk < row
                    sp = jnp.where(causal, sp, 0.0)
                sps.append(sp)
                masks.append(causal)
                cums.append(_split_dot(sp, upto))
            es, ebs, exs, sigs = [], [], [], []
            for off, z, sp, cum, dw, causal in zip(offs, zs, sps, cums, dws, masks):
                lb = z - sp
                w = jnp.exp(lb - (lt - (pre + cum)))
                if causal is not None:
                    w = jnp.where(causal, w, 0.0)
                dv_ref[pl.ds(off, bk), :] += _dot_tn(w.astype(BF16), do_t)
                e = dw * w
                eb = e.astype(BF16)
                es.append(e)
                ebs.append(eb)
                exs.append(_dot(eb, before))
                sigs.append(jnp.exp(lb))
                pre = pre + cum[:, bk - 1:bk]
            for off, k, e, eb, exm, sig, causal in zip(offs, ks, es, ebs, exs, sigs, masks):
                ex = exm + ecar
                dz = (e - sig * (e + ex)) * scale
                if causal is not None:
                    dz = jnp.where(causal, dz, 0.0)
                dzb = dz.astype(BF16)
                dk_ref[pl.ds(off, bk), :] += _dot_tn(dzb, q)
                dq = dq + _dot(dzb, k)
                ecar = ex[:, bk - 1:bk] + eb[:, bk - 1:bk].astype(F32)
            return pre, ecar, dq

        init = (jnp.zeros((bq, 1), F32), jnp.zeros((bq, 1), F32), jnp.zeros((bq, HEAD_DIM), F32))
        carry = lax.fori_loop(
            0, i * ndiag // SB_PAIR,
            lambda t, cr: tiles([SB_PAIR * t + u for u in range(SB_PAIR)], cr, [None] * SB_PAIR), init)
        for dg in range(0, ndiag, SB_PAIR):
            dgs = list(range(dg, dg + SB_PAIR))
            carry = tiles([i * ndiag + g for g in dgs], carry, dgs)
        dq_ref[...] = carry[2]

    qspec = pl.BlockSpec((bq, HEAD_DIM), lambda h, i: (i, h))
    kspec = pl.BlockSpec((s_len, HEAD_DIM), lambda h, i: (0, h))
    return _pcall(
        body, name="sb_bwd", grid=(nh, s_len // bq),
        out_shape=[jax.ShapeDtypeStruct((s_len, hd), F32)] * 3,
        in_specs=[qspec, kspec, kspec, qspec, pl.BlockSpec((None, bq, 1), lambda h, i: (h, i, 0))],
        out_specs=[qspec, kspec, kspec],
        compiler_params=_params(("parallel", "arbitrary")),
    )(qn, kn, vb, do, ltot)


GDN_GROUP = 16


def _gdn_group(nh):
    return min(GDN_GROUP, nh)


def _gdn_chunk_terms(qh, kh, vh, g_r, g_c, b_c):
    c = GDN_CHUNK
    r = lax.broadcasted_iota(jnp.int32, (c, c), 0)
    s = lax.broadcasted_iota(jnp.int32, (c, c), 1)
    tril, stril = r >= s, r > s
    gcc = jnp.sum(jnp.where(tril, g_r, 0.0), axis=1, keepdims=True)
    gcr = jnp.sum(jnp.where(r <= s, g_c, 0.0), axis=0, keepdims=True)
    dm = jnp.where(tril, jnp.exp(jnp.where(tril, gcc - gcr, 0.0)), 0.0)
    kb = kh.astype(BF16)
    kk = _dot_nt(kb, kb)
    qk = _dot_nt(qh.astype(BF16), kb)
    egc = jnp.exp(gcc)
    gcl = gcc[c - 1:c, :]
    t = dict(tril=tril, stril=stril, gcc=gcc, dm=dm, kb=kb, kk=kk, qk=qk, egc=egc,
             ekd=jnp.exp(gcl - gcc), gl=jnp.exp(gcl),
             a=jnp.where(stril, b_c * kk * dm, 0.0),
             bv=b_c * vh, bk=(b_c * egc) * kh, at=jnp.where(tril, qk * dm, 0.0))
    t["qg"] = qh * egc
    t["kd"] = kh * t["ekd"]
    return t


def _unit_lower_inverses(mats):
    c = GDN_CHUNK
    r = lax.broadcasted_iota(jnp.int32, (c, c), 0)
    s = lax.broadcasted_iota(jnp.int32, (c, c), 1)
    eye = (r == s).astype(F32)
    ps = [-a for a in mats]
    ts = [eye + p for p in ps]
    span = 2
    while span < c:
        ps = [_dot(p, p, hi=HIGH) for p in ps]
        ts = [t + _dot(t, p, hi=HIGH) for t, p in zip(ts, ps)]
        span *= 2
    return ts


def _gdn_fwd(q, k, v, g_col, g_row, b_col, b_row):
    s_len, d = q.shape
    nh = d // HEAD_DIM
    c = GDN_CHUNK
    n_chunks = s_len // c
    grp = _gdn_group(nh)

    def body(q_ref, k_ref, v_ref, gc_ref, gr_ref, bc_ref, br_ref, o_ref, ss_ref, ts_ref, st):
        n = pl.program_id(1)

        @pl.when(n == 0)
        def _():
            st[...] = jnp.zeros_like(st)

        heads = range(grp)
        sls = [slice(i * HEAD_DIM, (i + 1) * HEAD_DIM) for i in heads]
        terms = [_gdn_chunk_terms(q_ref[:, sls[i]], k_ref[:, sls[i]], v_ref[:, sls[i]],
                                  gr_ref[i:i + 1, :], gc_ref[:, i:i + 1], bc_ref[:, i:i + 1]) for i in heads]
        tinvs = _unit_lower_inverses([t["a"] for t in terms])
        wvs = [_dot(tinv, t["bv"], hi=HIGH) for tinv, t in zip(tinvs, terms)]
        wks = [_dot(tinv, t["bk"], hi=HIGH) for tinv, t in zip(tinvs, terms)]
        states = [st[i] for i in heads]
        sbs = [state.astype(BF16) for state in states]
        ubs = [(wv - _dot(wk.astype(BF16), sb)).astype(BF16) for wv, wk, sb in zip(wvs, wks, sbs)]
        for i in heads:
            t = terms[i]
            o_ref[:, sls[i]] = _dot(t["qg"].astype(BF16), sbs[i]) + _dot(t["at"].astype(BF16), ubs[i])
            ss_ref[i] = states[i]
            ts_ref[i] = tinvs[i]
            st[i] = t["gl"] * states[i] + _dot_tn(t["kd"].astype(BF16), ubs[i])

    tok = pl.BlockSpec((c, grp * HEAD_DIM), lambda h, n: (n, h))
    colspec = pl.BlockSpec((None, c, grp), lambda h, n: (h, n, 0))
    rowspec = pl.BlockSpec((None, None, grp, c), lambda h, n: (h, n, 0, 0))
    return _pcall(
        body, name="gdn_fwd", grid=(nh // grp, n_chunks),
        out_shape=[jax.ShapeDtypeStruct((s_len, d), F32),
                   jax.ShapeDtypeStruct((n_chunks, nh, HEAD_DIM, HEAD_DIM), F32),
                   jax.ShapeDtypeStruct((n_chunks, nh, c, c), F32)],
        in_specs=[tok, tok, tok, colspec, rowspec, colspec, rowspec],
        out_specs=[tok, pl.BlockSpec((None, grp, HEAD_DIM, HEAD_DIM), lambda h, n: (n, h, 0, 0)),
                   pl.BlockSpec((None, grp, c, c), lambda h, n: (n, h, 0, 0))],
        scratch_shapes=[pltpu.VMEM((grp, HEAD_DIM, HEAD_DIM), F32)],
        compiler_params=_params(("parallel", "arbitrary")),
    )(q, k, v, g_col, g_row, b_col, b_row)


def _gdn_bwd(q, k, v, g_col, g_row, b_col, b_row, states, tinvs, do):
    s_len, d = q.shape
    nh = d // HEAD_DIM
    c = GDN_CHUNK
    n_chunks = s_len // c
    grp = _gdn_group(nh)

    def body(q_ref, k_ref, v_ref, gc_ref, gr_ref, bc_ref, br_ref, ss_ref, ts_ref, do_ref,
             dq_ref, dk_ref, dv_ref, dgb_ref, dst):
        n = pl.program_id(1)

        @pl.when(n == 0)
        def _():
            dst[...] = jnp.zeros_like(dst)

        r = lax.broadcasted_iota(jnp.int32, (c, c), 0)
        s = lax.broadcasted_iota(jnp.int32, (c, c), 1)
        suffix = (r <= s).astype(F32)
        lane = lax.broadcasted_iota(jnp.int32, (c, LANES), 1)
        heads = range(grp)
        sls = [slice(i * HEAD_DIM, (i + 1) * HEAD_DIM) for i in heads]
        qs = [q_ref[:, sl] for sl in sls]
        ks = [k_ref[:, sl] for sl in sls]
        vs = [v_ref[:, sl] for sl in sls]
        bcs = [bc_ref[:, i:i + 1] for i in heads]
        ts = [_gdn_chunk_terms(qs[i], ks[i], vs[i], gr_ref[i:i + 1, :], gc_ref[:, i:i + 1], bcs[i]) for i in heads]
        tinv = [ts_ref[i] for i in heads]
        state = [ss_ref[i] for i in heads]
        sb = [x.astype(BF16) for x in state]
        dnext = [dst[i] for i in heads]
        dnb = [x.astype(BF16) for x in dnext]
        dob = [do_ref[:, sl].astype(BF16) for sl in sls]
        wv = [_dot(tinv[i], ts[i]["bv"], hi=HIGH) for i in heads]
        wk = [_dot(tinv[i], ts[i]["bk"], hi=HIGH) for i in heads]
        wkb = [x.astype(BF16) for x in wk]
        ub = [(wv[i] - _dot(wkb[i], sb[i])).astype(BF16) for i in heads]
        du = [_dot_tn(ts[i]["at"].astype(BF16), dob[i]) + _dot(ts[i]["kd"].astype(BF16), dnb[i]) for i in heads]
        dub = [x.astype(BF16) for x in du]
        dat = [jnp.where(ts[i]["tril"], _dot_nt(dob[i], ub[i]), 0.0) for i in heads]
        dqg = [_dot_nt(dob[i], sb[i]) for i in heads]
        dkd = [_dot_nt(ub[i], dnb[i]) for i in heads]
        dwk = [-_dot_nt(dub[i], sb[i]) for i in heads]
        for i in heads:
            dst[i] = (ts[i]["gl"] * dnext[i] + _dot_tn(ts[i]["qg"].astype(BF16), dob[i]) - _dot_tn(wkb[i], dub[i]))
        dbv = [_dot_tn(tinv[i], du[i], hi=HIGH) for i in heads]
        dbk = [_dot_tn(tinv[i], dwk[i], hi=HIGH) for i in heads]
        dtm = [_dot_nt(du[i], ts[i]["bv"], hi=HIGH) + _dot_nt(dwk[i], ts[i]["bk"], hi=HIGH) for i in heads]
        dtt = [_dot_nt(dtm[i], tinv[i], hi=HIGH) for i in heads]
        da = [-jnp.where(ts[i]["stril"], _dot_tn(tinv[i], dtt[i], hi=HIGH), 0.0) for i in heads]
        rs = lambda m: jnp.sum(m, axis=1, keepdims=True)
        dgb = jnp.zeros((c, LANES), F32)
        for i in heads:
            t, b_c, dm, kb = ts[i], bcs[i], ts[i]["dm"], ts[i]["kb"]
            egc, ekd = t["egc"], t["ekd"]
            dkk = da[i] * b_c * dm
            ddm = da[i] * b_c * t["kk"] + dat[i] * t["qk"]
            dqkb, dkkb = (dat[i] * dm).astype(BF16), dkk.astype(BF16)
            dq_ref[:, sls[i]] = _dot(dqkb, kb) + dqg[i] * egc
            dk_ref[:, sls[i]] = (_dot_tn(dqkb, qs[i].astype(BF16)) + _dot(dkkb, kb) + _dot_tn(dkkb, kb)
                                 + dbk[i] * (b_c * egc) + dkd[i] * ekd)
            dv_ref[:, sls[i]] = dbv[i] * b_c
            dbk_k = rs(dbk[i] * ks[i])
            dbeta = rs(da[i] * t["kk"] * dm) + rs(dbv[i] * vs[i]) + dbk_k * egc
            mx = ddm * dm
            ekd_sum = rs(dkd[i] * ks[i]) * ekd
            dgc = rs(mx) + dbk_k * b_c * egc + rs(dqg[i] * qs[i]) * egc - ekd_sum
            dgl = jnp.sum(rs(dnext[i] * state[i]), axis=0, keepdims=True)
            tail = jnp.sum(ekd_sum, axis=0, keepdims=True) + dgl * t["gl"]
            dg = (_dot(suffix, jnp.broadcast_to(dgc, (c, LANES)), hi=HIGH)[:, 0:1]
                  - rs(_dot_nt(suffix, mx, hi=HIGH)) + tail)
            dgb = dgb + jnp.where(lane == i, dbeta, 0.0) + jnp.where(lane == grp + i, dg, 0.0)
        dgb_ref[...] = dgb

    last = n_chunks - 1
    tok = pl.BlockSpec((c, grp * HEAD_DIM), lambda h, n: (last - n, h))
    colspec = pl.BlockSpec((None, c, grp), lambda h, n: (h, last - n, 0))
    rowspec = pl.BlockSpec((None, None, grp, c), lambda h, n: (h, last - n, 0, 0))
    return _pcall(
        body, name="gdn_bwd", grid=(nh // grp, n_chunks),
        out_shape=[jax.ShapeDtypeStruct((s_len, d), F32)] * 3
        + [jax.ShapeDtypeStruct((nh // grp, s_len, LANES), F32)],
        in_specs=[tok, tok, tok, colspec, rowspec, colspec, rowspec,
                  pl.BlockSpec((None, grp, HEAD_DIM, HEAD_DIM), lambda h, n: (last - n, h, 0, 0)),
                  pl.BlockSpec((None, grp, c, c), lambda h, n: (last - n, h, 0, 0)), tok],
        out_specs=[tok, tok, tok, pl.BlockSpec((None, c, LANES), lambda h, n: (h, last - n, 0))],
        scratch_shapes=[pltpu.VMEM((grp, HEAD_DIM, HEAD_DIM), F32)],
        compiler_params=_params(("parallel", "arbitrary")),
    )(q, k, v, g_col, g_row, b_col, b_row, states, tinvs, do)


def _shift_down(prev8, cur, k):
    if k == 0:
        return cur
    ext = jnp.concatenate([prev8, cur], axis=0)
    return pltpu.roll(ext, k, 0)[SUBLANES:, :]


def _shift_up(cur, next8, k):
    if k == 0:
        return cur
    ext = jnp.concatenate([cur, next8], axis=0)
    n = ext.shape[0]
    return pltpu.roll(ext, n - k, 0)[:cur.shape[0], :]


def _conv_pre(i, x, prev8, w):
    prev8 = jnp.where(i == 0, 0.0, prev8)
    pre = None
    for j in range(GDN_CONV):
        term = w[j:j + 1, :] * _shift_down(prev8, x, GDN_CONV - 1 - j)
        pre = term if pre is None else pre + term
    return pre, prev8


def _l2_fwd(a, mult):
    return a * (lax.rsqrt(jnp.sum(a * a, axis=1, keepdims=True) + EPS) * mult)


def _l2_bwd(a, dy, mult):
    r = lax.rsqrt(jnp.sum(a * a, axis=1, keepdims=True) + EPS)
    dy = dy * mult
    return r * dy - a * (r * r * r) * jnp.sum(a * dy, axis=1, keepdims=True)


def _conv_fwd(xb, conv_w, group, *, norm, mult, tr=256):
    d = xb.shape[1] // 3

    def fn(i, nt, tiles, prev8, next8, cv):
        pre, _ = _conv_pre(i, tiles[0], prev8[0], cv[0])
        a = _silu(pre)
        if norm:
            a = _per_head(lambda ah: _l2_fwd(ah, mult), a)
        return [a], []

    col = Col(xb, d, group)
    wg = lax.slice_in_dim(conv_w, group * d, (group + 1) * d, axis=1)
    (y,), _ = _ew(f"conv_fwd{group}", fn, tr=tr, ins=[col], halo_prev=[col], consts=[wg], outs=[(d, F32)])
    return y


def _conv_bwd(xb, conv_w, group, dy, *, norm, mult, tr=256):
    d = xb.shape[1] // 3
    col = Col(xb, d, group)
    wg = lax.slice_in_dim(conv_w, group * d, (group + 1) * d, axis=1)

    def fn_pre(i, nt, tiles, prev8, next8, cv):
        x, dyt = tiles
        pre, p8 = _conv_pre(i, x, prev8[0], cv[0])
        if norm:
            da = _per_head(lambda ah, dh: _l2_bwd(ah, dh, mult), _silu(pre), dyt)
        else:
            da = dyt
        dpre = da * _dsilu(pre)
        tap = lax.broadcasted_iota(jnp.int32, (GDN_CONV, d), 0)
        dw = jnp.zeros((GDN_CONV, d), F32)
        for j in range(GDN_CONV):
            dw = dw + jnp.where(tap == j, _colsum(dpre * _shift_down(p8, x, GDN_CONV - 1 - j)), 0.0)
        return [dpre], [dw]

    (dpre,), (dw,) = _ew(f"conv_bwd_pre{group}", fn_pre, tr=tr, ins=[col, dy], halo_prev=[col], consts=[wg],
                         outs=[(d, F32)], accs=[(GDN_CONV, d)])

    def fn_dx(i, nt, tiles, prev8, next8, cv):
        n8 = jnp.where(i == nt - 1, 0.0, next8[0])
        dx = None
        for j in range(GDN_CONV):
            term = cv[0][j:j + 1, :] * _shift_up(tiles[0], n8, GDN_CONV - 1 - j)
            dx = term if dx is None else dx + term
        return [dx], []

    (dx,), _ = _ew(f"conv_bwd_dx{group}", fn_dx, tr=tr, ins=[dpre], halo_next=[dpre], consts=[wg], outs=[(d, BF16)])
    return dx, dw


def _adamw(name, w, m, v, grads, *, tr=64):
    shape = w.shape
    w2, m2, v2 = [a.reshape(-1, shape[-1]) for a in (w, m, v)]
    n_g = len(grads)
    bc1 = 1.0 - ADAM_B1 ** ADAM_STEP
    bc2 = 1.0 - ADAM_B2 ** ADAM_STEP

    def fn(i, nt, tiles, prev8, next8, cv):
        wt, mt, vt = tiles[:3]
        g = tiles[3]
        for extra in tiles[4:]:
            g = g + extra
        mn = ADAM_B1 * mt + (1.0 - ADAM_B1) * g
        vn = ADAM_B2 * vt + (1.0 - ADAM_B2) * (g * g)
        delta = -ADAM_LR * ((mn / bc1) / (jnp.sqrt(vn / bc2) + ADAM_EPS) + ADAM_WD * wt)
        return [g, delta, mn, vn], []

    width = shape[-1]
    outs, _ = _ew(name, fn, tr=tr, ins=[w2, m2, v2] + list(grads), outs=[(width, F32)] * 4)
    assert n_g >= 1
    return tuple(o.reshape(shape) for o in outs)


def _pad_cols(a, width):
    return jnp.pad(a, ((0, 0), (0, width - a.shape[1])))


def _gdn_layouts(gbeta, nh, n_chunks):
    grp = _gdn_group(nh)
    s_len = gbeta.shape[0]

    def lay(a):
        col = a.reshape(s_len, nh // grp, grp).transpose(1, 0, 2)
        row = a.reshape(n_chunks, GDN_CHUNK, nh // grp, grp).transpose(2, 0, 3, 1)
        return col, row

    b_col, b_row = lay(gbeta[:, :nh])
    g_col, g_row = lay(gbeta[:, nh:2 * nh])
    return g_col, g_row, b_col, b_row


def kernel(x, c, w_mod, b_mod, norm1_w, w_in, q_norm_w, k_norm_w, conv_w, a_log, dt_bias, o_norm_w, p_a, p_b, w_out, norm2_w, w_gate, w_up, w_down, loss_target, m_w_mod, m_b_mod, m_norm1_w, m_w_in, m_q_norm_w, m_k_norm_w, m_conv_w, m_a_log, m_dt_bias, m_o_norm_w, m_p_a, m_p_b, m_w_out, m_norm2_w, m_w_gate, m_w_up, m_w_down, v_w_mod, v_b_mod, v_norm1_w, v_w_in, v_q_norm_w, v_k_norm_w, v_conv_w, v_a_log, v_dt_bias, v_o_norm_w, v_p_a, v_p_b, v_w_out, v_norm2_w, v_w_gate, v_w_up, v_w_down):
    s_len, d = x.shape[1], x.shape[2]
    nh = d // HEAD_DIM
    n_chunks = s_len // GDN_CHUNK
    ff = 4 * w_gate.shape[2]
    mx, my, mc = _my_pos()
    chip = 2 * mx + my
    dev = 2 * chip + mc
    x2 = x[0]
    tgt = loss_target[0]

    c_all = _allgather8("ag_c", _pad_cols(c, d).reshape(SUBLANES, d // SUBLANES)).reshape(8, d)
    wm = w_mod[0]
    mod_w = wm.shape[1]
    bm_cols = lax.dynamic_slice_in_dim(b_mod, chip * mod_w, mod_w, axis=1)

    def mod_body(c_ref, w_ref, b_ref, o_ref, ca_ref):
        ca = _silu(c_ref[...])
        ca_ref[...] = ca
        o_ref[...] = _dot(ca, w_ref[...], hi=HIGHEST) + b_ref[...]

    tn_mod = _pick(mod_w, 512)
    mod8, c_act = _pcall(
        mod_body, name="mod_fwd", grid=(mod_w // tn_mod,),
        out_shape=[jax.ShapeDtypeStruct((8, mod_w), F32), jax.ShapeDtypeStruct((8, d), F32)],
        in_specs=[pl.BlockSpec((8, d), lambda j: (0, 0)), pl.BlockSpec((d, tn_mod), lambda j: (0, j)),
                  pl.BlockSpec((1, tn_mod), lambda j: (0, j))],
        out_specs=[pl.BlockSpec((8, tn_mod), lambda j: (0, j)), pl.BlockSpec((8, d), lambda j: (0, 0))],
        compiler_params=_params(("arbitrary",)),
    )(c_all, wm, bm_cols)
    mod_all = _allgather8("ag_mod", mod8)
    mod_me = mod_all.reshape(4, 2, 8, mod_w)[:, mc, dev, :].reshape(1, 6 * d)
    shift1, scale1, gate1, shift2, scale2, gate2 = [mod_me[:, j * d:(j + 1) * d] for j in range(6)]

    gathered = _gather4("ag_weights", [w_in[0].astype(BF16), p_a[0].astype(BF16), p_b[0].astype(BF16),
                                      w_out[0].astype(BF16), w_gate[0].astype(BF16), w_up[0].astype(BF16),
                                      w_down[0].astype(BF16), conv_w[0]], n_split=7)
    w_in_f = gathered[0].transpose(1, 0, 2).reshape(d, -1)
    wa = w_in_f[:, :3 * d]
    wb = w_in_f[:, 3 * d:6 * d]
    wzg = jnp.concatenate([w_in_f[:, 6 * d:7 * d], w_in_f[:, 7 * d + 2 * nh:]], axis=1)
    wba = _pad_cols(w_in_f[:, 7 * d:7 * d + 2 * nh], LANES)
    p_a_f, p_b_f, w_out_f = [g.reshape(d, d) for g in gathered[1:4]]
    w_gate_f, w_up_f = [g.transpose(1, 0, 2).reshape(d, ff) for g in gathered[4:6]]
    w_down_f = gathered[6].reshape(ff, d)
    conv_f = gathered[7].transpose(1, 0, 2).reshape(GDN_CONV, 3 * d)

    def norm_mod_fn(i, nt, tiles, prev8, next8, cv):
        w, sc, sh = cv
        return [_rms_fwd(tiles[0], w) * (1.0 + sc) + sh], []

    (u1,), _ = _ew("norm_mod1", norm_mod_fn, tr=512, ins=[x2], consts=[norm1_w, scale1, shift1], outs=[(d, BF16)])
    proj_a = _mm("proj_a", u1, wa)
    proj_b = _mm("proj_b", u1, wb)
    proj_zg = _mm("proj_zg", u1, wzg)
    proj_ba = _mm("proj_ba", u1, wba)

    def qknorm_fn(i, nt, tiles, prev8, next8, cv):
        qa, ka, va = tiles
        return [_per_head(lambda h: _rms_fwd(h, cv[0]), qa), _per_head(lambda h: _rms_fwd(h, cv[1]), ka), va], []

    (qn, kn, vb), _ = _ew("qknorm", qknorm_fn, tr=256,
                          ins=[Col(proj_a, d, 0), Col(proj_a, d, 1), Col(proj_a, d, 2)],
                          consts=[q_norm_w, k_norm_w], outs=[(d, BF16)] * 3)
    o_a, ltot = _sb_fwd(qn, kn, vb)

    lane_ids = jnp.arange(LANES)
    is_b = (lane_ids < nh)[None, :]
    is_a = ((lane_ids >= nh) & (lane_ids < 2 * nh))[None, :]
    alog128 = jnp.zeros((1, LANES), F32).at[:, nh:2 * nh].set(a_log)
    dtb128 = jnp.zeros((1, LANES), F32).at[:, nh:2 * nh].set(dt_bias)
    is_b_f, is_a_f = is_b.astype(F32), is_a.astype(F32)

    def gbeta_fn(i, nt, tiles, prev8, next8, cv):
        al, dtb, mb, ma = cv
        ba = tiles[0]
        g = -jnp.exp(al) * _softplus(ba + dtb)
        return [jnp.where(mb > 0.5, _sigmoid(ba), jnp.where(ma > 0.5, g, 0.0))], []

    (gbeta,), _ = _ew("gbeta", gbeta_fn, tr=1024, ins=[proj_ba], consts=[alog128, dtb128, is_b_f, is_a_f],
                      outs=[(LANES, F32)])
    g_col, g_row, b_col, b_row = _gdn_layouts(gbeta, nh, n_chunks)
    qscale = HEAD_DIM ** -0.5
    q_b = _conv_fwd(proj_b, conv_f, 0, norm=True, mult=qscale)
    k_b = _conv_fwd(proj_b, conv_f, 1, norm=True, mult=1.0)
    v_b = _conv_fwd(proj_b, conv_f, 2, norm=False, mult=1.0)
    o_raw, states, tinvs = _gdn_fwd(q_b, k_b, v_b, g_col, g_row, b_col, b_row)

    def gated_norm_fn(i, nt, tiles, prev8, next8, cv):
        o, z = tiles
        return [_per_head(lambda h: _rms_fwd(h, cv[0]), o) * _silu(z)], []

    (o_b,), _ = _ew("gated_norm", gated_norm_fn, tr=256, ins=[o_raw, Col(proj_zg, d, 0)], consts=[o_norm_w],
                    outs=[(d, BF16)])
    y_a = _mm("out_a", o_a, p_a_f)
    y_b = _mm("out_b", o_b, p_b_f)

    def merge_fn(i, nt, tiles, prev8, next8, cv):
        ya, yb, ga, gb = tiles
        return [_sigmoid(ga) * ya + _sigmoid(gb) * yb], []

    (merged,), _ = _ew("merge", merge_fn, tr=256, ins=[y_a, y_b, Col(proj_zg, d, 1), Col(proj_zg, d, 2)],
                       outs=[(d, BF16)])
    y_o = _mm("out_proj", merged, w_out_f)

    def resid_norm_fn(i, nt, tiles, prev8, next8, cv):
        xt, yo = tiles
        g1, w, sc, sh = cv
        h1 = xt + g1 * yo
        return [h1, _rms_fwd(h1, w) * (1.0 + sc) + sh], []

    (h1, u2), _ = _ew("resid_norm2", resid_norm_fn, tr=256, ins=[x2, y_o],
                      consts=[gate1, norm2_w, scale2, shift2], outs=[(d, F32), (d, BF16)])
    gt = _mm("ff_gate", u2, w_gate_f)
    up = _mm("ff_up", u2, w_up_f)

    def swiglu_fn(i, nt, tiles, prev8, next8, cv):
        return [_silu(tiles[0]) * tiles[1]], []

    (act,), _ = _ew("swiglu", swiglu_fn, tr=128, ins=[gt, up], outs=[(ff, BF16)])
    y_d = _mm("ff_down", act, w_down_f)

    def loss_fn(i, nt, tiles, prev8, next8, cv):
        h1t, yd, tg = tiles
        diff = h1t + cv[0] * yd - tg
        dy = diff * (1.0 / d)
        return [dy, dy * cv[0]], [_colsum(0.5 * diff * dy), _colsum(dy * yd)]

    (dy, dyd), (loss_cols, dgate2) = _ew("loss", loss_fn, tr=256, ins=[h1, y_d, tgt], consts=[gate2],
                                         outs=[(d, F32), (d, BF16)], accs=[(1, d), (1, d)])
    loss = lax.psum(jnp.sum(loss_cols), ("x", "y", "c"))

    dact = _mm("d_act", dyd, w_down_f, nt=True)
    g_w_down = _mm("g_w_down", act.T, dyd)

    def swiglu_bwd_fn(i, nt, tiles, prev8, next8, cv):
        da, g, u = tiles
        return [da * u * _dsilu(g), da * _silu(g)], []

    (dgt, dup), _ = _ew("swiglu_bwd", swiglu_bwd_fn, tr=128, ins=[dact, gt, up], outs=[(ff, BF16)] * 2)
    du2 = _mm("d_u2_up", dup, w_up_f, nt=True, add=_mm("d_u2_gate", dgt, w_gate_f, nt=True))
    u2_t = u2.T
    g_w_gate = _mm("g_w_gate", u2_t, dgt)
    g_w_up = _mm("g_w_up", u2_t, dup)

    def norm2_bwd_fn(i, nt, tiles, prev8, next8, cv):
        h1t, du, dres, yo = tiles
        w, sc, g1 = cv
        r = lax.rsqrt(jnp.mean(h1t * h1t, axis=1, keepdims=True) + EPS)
        nrm = h1t * r
        dn = du * w * (1.0 + sc)
        dh = r * (dn - nrm * jnp.mean(dn * nrm, axis=1, keepdims=True)) + dres
        return [dh, dh * g1], [_colsum(du), _colsum(du * nrm * w), _colsum(du * nrm * (1.0 + sc)), _colsum(dh * yo)]

    (dh1, dyo), (dshift2, dscale2, g_norm2, dgate1) = _ew(
        "norm2_bwd", norm2_bwd_fn, tr=256, ins=[h1, du2, dy, y_o], consts=[norm2_w, scale2, gate1],
        outs=[(d, F32), (d, BF16)], accs=[(1, d)] * 4)

    dmerged = _mm("d_merged", dyo, w_out_f, nt=True)
    g_w_out = _mm("g_w_out", merged.T, dyo)

    def merge_bwd_fn(i, nt, tiles, prev8, next8, cv):
        dm, ya, yb, ga, gb = tiles
        sa, sb = _sigmoid(ga), _sigmoid(gb)
        return [dm * sa, dm * sb, dm * ya * sa * (1.0 - sa), dm * yb * sb * (1.0 - sb)], []

    (dya, dyb, dga, dgb_gate), _ = _ew(
        "merge_bwd", merge_bwd_fn, tr=256, ins=[dmerged, y_a, y_b, Col(proj_zg, d, 1), Col(proj_zg, d, 2)],
        outs=[(d, BF16)] * 4)
    do_a = _mm("d_o_a", dya, p_a_f, nt=True, out_dtype=BF16)
    g_p_a = _mm("g_p_a", o_a.T, dya)
    do_b = _mm("d_o_b", dyb, p_b_f, nt=True)
    g_p_b = _mm("g_p_b", o_b.T, dyb)

    def gated_norm_bwd_fn(i, nt, tiles, prev8, next8, cv):
        dob, o, z = tiles
        sz = _silu(z)

        def head(oh, dh):
            return _rms_bwd(oh, cv[0], dh)

        dxo, dwn = _per_head(head, o, dob * sz)
        nrm_w = _per_head(lambda h: _rms_fwd(h, cv[0]), o)
        return [dxo, dob * nrm_w * _dsilu(z)], [_colsum(_head_sum(dwn))]

    (do_raw, dz_b), (g_o_norm,) = _ew(
        "gated_norm_bwd", gated_norm_bwd_fn, tr=256, ins=[do_b, o_raw, Col(proj_zg, d, 0)], consts=[o_norm_w],
        outs=[(d, F32), (d, BF16)], accs=[(1, HEAD_DIM)])
    dq_b, dk_b, dv_b, dgb_grp = _gdn_bwd(q_b, k_b, v_b, g_col, g_row, b_col, b_row, states, tinvs, do_raw)
    grp = _gdn_group(nh)
    dbeta = dgb_grp[:, :, :grp].transpose(1, 0, 2).reshape(s_len, nh)
    dg = dgb_grp[:, :, grp:2 * grp].transpose(1, 0, 2).reshape(s_len, nh)
    dgbeta = _pad_cols(jnp.concatenate([dbeta, dg], axis=1), LANES)

    def gbeta_bwd_fn(i, nt, tiles, prev8, next8, cv):
        al, dtb, mb, ma = cv
        ba, dgb = tiles
        beta = _sigmoid(ba)
        arg = ba + dtb
        da = dgb * (-jnp.exp(al)) * _sigmoid(arg)
        g = -jnp.exp(al) * _softplus(arg)
        dba = jnp.where(mb > 0.5, dgb * beta * (1.0 - beta), jnp.where(ma > 0.5, da, 0.0))
        return [dba], [_colsum(jnp.where(ma > 0.5, dgb * g, 0.0)), _colsum(jnp.where(ma > 0.5, da, 0.0))]

    (dba,), (g_alog128, g_dtb128) = _ew(
        "gbeta_bwd", gbeta_bwd_fn, tr=1024, ins=[proj_ba, dgbeta], consts=[alog128, dtb128, is_b_f, is_a_f],
        outs=[(LANES, BF16)], accs=[(1, LANES)] * 2)
    dxq, g_conv_q = _conv_bwd(proj_b, conv_f, 0, dq_b, norm=True, mult=qscale)
    dxk, g_conv_k = _conv_bwd(proj_b, conv_f, 1, dk_b, norm=True, mult=1.0)
    dxv, g_conv_v = _conv_bwd(proj_b, conv_f, 2, dv_b, norm=False, mult=1.0)
    g_conv = jnp.concatenate([g_conv_q, g_conv_k, g_conv_v], axis=1)

    dqn, dkn, dvb = _sb_bwd(qn, kn, vb, do_a, ltot)

    def qknorm_bwd_fn(i, nt, tiles, prev8, next8, cv):
        qa, ka, dq, dk, dv = tiles
        dxq_, dwq = _per_head(lambda h, g: _rms_bwd(h, cv[0], g), qa, dq)
        dxk_, dwk = _per_head(lambda h, g: _rms_bwd(h, cv[1], g), ka, dk)
        return [dxq_, dxk_, dv], [_colsum(_head_sum(dwq)), _colsum(_head_sum(dwk))]

    (dqa, dka, dva), (g_q_norm, g_k_norm) = _ew(
        "qknorm_bwd", qknorm_bwd_fn, tr=256, ins=[Col(proj_a, d, 0), Col(proj_a, d, 1), dqn, dkn, dvb],
        consts=[q_norm_w, k_norm_w], outs=[(d, BF16)] * 3, accs=[(1, HEAD_DIM)] * 2)

    u1_t = u1.T
    d_a = jnp.concatenate([dqa, dka, dva], axis=1)
    d_b = jnp.concatenate([dxq, dxk, dxv], axis=1)
    d_zg = jnp.concatenate([dz_b, dga, dgb_gate], axis=1)
    du1 = _mm("d_u1_a", d_a, wa, nt=True)
    du1 = _mm("d_u1_b", d_b, wb, nt=True, add=du1)
    du1 = _mm("d_u1_zg", d_zg, wzg, nt=True, add=du1)
    du1 = _mm("d_u1_ba", dba, wba, nt=True, add=du1)
    g_wa = _mm("g_w_in_a", u1_t, d_a)
    g_wb = _mm("g_w_in_b", u1_t, d_b)
    g_wzg = _mm("g_w_in_zg", u1_t, d_zg)
    g_wba = _mm("g_w_in_ba", u1_t, dba)
    g_w_in_f = jnp.concatenate([g_wa, g_wb, g_wzg[:, :d], g_wba[:, :2 * nh], g_wzg[:, d:]], axis=1)

    def norm1_bwd_fn(i, nt, tiles, prev8, next8, cv):
        xt, du, dres = tiles
        w, sc = cv
        r = lax.rsqrt(jnp.mean(xt * xt, axis=1, keepdims=True) + EPS)
        nrm = xt * r
        dn = du * w * (1.0 + sc)
        dxt = r * (dn - nrm * jnp.mean(dn * nrm, axis=1, keepdims=True)) + dres
        return [dxt], [_colsum(du), _colsum(du * nrm * w), _colsum(du * nrm * (1.0 + sc))]

    (grad_x,), (dshift1, dscale1, g_norm1) = _ew(
        "norm1_bwd", norm1_bwd_fn, tr=256, ins=[x2, du1, dh1], consts=[norm1_w, scale1],
        outs=[(d, F32)], accs=[(1, d)] * 3)

    dmod_me = jnp.concatenate([dshift1, dscale1, dgate1, dshift2, dscale2, dgate2], axis=1)
    small = jnp.concatenate(
        [dmod_me, g_norm1, g_norm2, g_q_norm, g_k_norm, g_o_norm, g_alog128[:, nh:2 * nh], g_dtb128[:, nh:2 * nh],
         g_conv.reshape(1, -1)], axis=1)
    n_small = small.shape[1]
    pad_to = -(-n_small // (SUBLANES * LANES)) * (SUBLANES * LANES)
    small_all = _allgather8("ag_small", _pad_cols(small, pad_to).reshape(SUBLANES, pad_to // SUBLANES))
    small_all = small_all.reshape(8, pad_to)

    def sum8_fn(i, nt, tiles, prev8, next8, cv):
        return [], [_colsum(tiles[0])]

    _, (small_sum,) = _ew("sum_small", sum8_fn, tr=8, ins=[small_all], accs=[(1, pad_to)])
    offs = [0]
    for width in (6 * d, d, d, HEAD_DIM, HEAD_DIM, HEAD_DIM, nh, nh, GDN_CONV * 3 * d):
        offs.append(offs[-1] + width)
    pieces = [small_sum[:, offs[j]:offs[j + 1]] for j in range(9)]
    (gs_b_mod, gs_norm1, gs_norm2, gs_q_norm, gs_k_norm, gs_o_norm, gs_a_log, gs_dt_bias, gs_conv) = pieces
    conv_cols = 3 * d // 4
    gs_conv_mine = lax.dynamic_slice_in_dim(gs_conv.reshape(GDN_CONV, 3 * d), chip * conv_cols, conv_cols, axis=1)

    dmod_all = lax.dynamic_slice_in_dim(small_all[:, :6 * d], chip * mod_w, mod_w, axis=1)

    def wmod_grad_body(ct_ref, dm_ref, o_ref):
        o_ref[...] = _dot(ct_ref[...], dm_ref[...], hi=HIGHEST)

    g_w_mod = _pcall(
        wmod_grad_body, name="g_w_mod", grid=(mod_w // tn_mod,),
        out_shape=jax.ShapeDtypeStruct((d, mod_w), F32),
        in_specs=[pl.BlockSpec((d, 8), lambda j: (0, 0)), pl.BlockSpec((8, tn_mod), lambda j: (0, j))],
        out_specs=pl.BlockSpec((d, tn_mod), lambda j: (0, j)),
        compiler_params=_params(("arbitrary",)),
    )(c_act.T, dmod_all)

    partials = [
        g_w_in_f.reshape(d, 4, -1).transpose(1, 0, 2),
        g_p_a.reshape(4, d // 4, d), g_p_b.reshape(4, d // 4, d), g_w_out.reshape(4, d // 4, d),
        g_w_gate.reshape(d, 4, ff // 4).transpose(1, 0, 2), g_w_up.reshape(d, 4, ff // 4).transpose(1, 0, 2),
        g_w_down.reshape(4, ff // 4, d),
    ]
    mine_h, theirs_h = _swap_halves("swap_halves", partials)
    chip_halves = []
    for t in range(len(partials)):
        def pair_fn(i, nt, tiles, prev8, next8, cv):
            return [tiles[0] + tiles[1]], []

        width = partials[t].shape[-1]
        (ch,), _ = _ew(f"pair_sum{t}", pair_fn, tr=64,
                       ins=[mine_h[t].reshape(-1, width), theirs_h[t].reshape(-1, width)], outs=[(width, BF16)])
        chip_halves.append(ch.reshape(mine_h[t].shape))
    landed = _scatter4("rs_grads", chip_halves)
    g_halves = []
    for t, land in enumerate(landed):
        def sum4_fn(i, nt, tiles, prev8, next8, cv):
            f = [tl.astype(F32) for tl in tiles]
            return [(f[0] + f[1]) + (f[2] + f[3])], []

        (gh,), _ = _ew(f"chip_sum{t}", sum4_fn, tr=64, ins=[Col(land, lead=s) for s in range(4)],
                       outs=[(land.shape[-1], F32)])
        g_halves.append(gh)
    g_full = _join_halves("join_grads", g_halves)

    big = {}
    names = ["w_in", "p_a", "p_b", "w_out", "w_gate", "w_up", "w_down"]
    big_w = [w_in, p_a, p_b, w_out, w_gate, w_up, w_down]
    big_m = [m_w_in, m_p_a, m_p_b, m_w_out, m_w_gate, m_w_up, m_w_down]
    big_v = [v_w_in, v_p_a, v_p_b, v_w_out, v_w_gate, v_w_up, v_w_down]
    for t, nm in enumerate(names):
        big[nm] = _adamw(f"adamw_{nm}", big_w[t], big_m[t], big_v[t], [g_full[t]])
    big["w_mod"] = _adamw("adamw_w_mod", w_mod, m_w_mod, v_w_mod, [g_w_mod])
    big["conv_w"] = _adamw("adamw_conv_w", conv_w, m_conv_w, v_conv_w, [gs_conv_mine], tr=8)
    small_names = ["b_mod", "norm1_w", "norm2_w", "q_norm_w", "k_norm_w", "o_norm_w", "a_log", "dt_bias"]
    small_w = [b_mod, norm1_w, norm2_w, q_norm_w, k_norm_w, o_norm_w, a_log, dt_bias]
    small_m = [m_b_mod, m_norm1_w, m_norm2_w, m_q_norm_w, m_k_norm_w, m_o_norm_w, m_a_log, m_dt_bias]
    small_v = [v_b_mod, v_norm1_w, v_norm2_w, v_q_norm_w, v_k_norm_w, v_o_norm_w, v_a_log, v_dt_bias]
    small_g = [gs_b_mod, gs_norm1, gs_norm2, gs_q_norm, gs_k_norm, gs_o_norm, gs_a_log, gs_dt_bias]
    rep_w = jnp.concatenate(small_w, axis=1)
    rep_m = jnp.concatenate(small_m, axis=1)
    rep_v = jnp.concatenate(small_v, axis=1)
    rep_g = jnp.concatenate(small_g, axis=1)
    rep = _adamw("adamw_small", rep_w, rep_m, rep_v, [rep_g], tr=1)
    roffs = [0]
    for a in small_w:
        roffs.append(roffs[-1] + a.shape[1])
    for j, nm in enumerate(small_names):
        big[nm] = tuple(r[:, roffs[j]:roffs[j + 1]] for r in rep)

    order = ["w_mod", "b_mod", "norm1_w", "w_in", "q_norm_w", "k_norm_w", "conv_w", "a_log", "dt_bias", "o_norm_w",
             "p_a", "p_b", "w_out", "norm2_w", "w_gate", "w_up", "w_down"]
    grads = [big[nm][0] for nm in order]
    deltas = [big[nm][1] for nm in order]
    new_m = [big[nm][2] for nm in order]
    new_v = [big[nm][3] for nm in order]
    return (loss, grad_x[None], *grads, *deltas, *new_m, *new_v)
```

```python
import jax
import jax.numpy as jnp
from jax import lax
from jax.experimental import pallas as pl
from jax.experimental.pallas import tpu as pltpu

F32 = jnp.float32
BF16 = jnp.bfloat16
HIGHEST = lax.Precision.HIGHEST
HIGH = lax.Precision.HIGH
MESH = pl.DeviceIdType.MESH

HEAD_DIM = 128
GDN_CHUNK = 64
GDN_CONV = 4
EPS = 1e-6
LANES = 128
SUBLANES = 8
VMEM_LIMIT = 56 * 1024 * 1024
MM_VMEM_BUDGET = 40 * 1024 * 1024

ADAM_LR = 0.001
ADAM_B1 = 0.9
ADAM_B2 = 0.999
ADAM_EPS = 1e-08
ADAM_WD = 0.01
ADAM_STEP = 10


def _pcall(body, **kw):
    return pl.pallas_call(body, **kw)


def _params(sem=None):
    if sem is None:
        return pltpu.CompilerParams(vmem_limit_bytes=VMEM_LIMIT)
    return pltpu.CompilerParams(dimension_semantics=sem, vmem_limit_bytes=VMEM_LIMIT)


def _pick(dim, target):
    if dim <= target:
        return dim
    best = None
    for t in range(LANES, target + 1, LANES):
        if dim % t == 0:
            best = t
    assert best is not None, (dim, target)
    return best


def _rows_tile(rows, target):
    t = min(rows, target)
    while rows % t:
        t //= 2
    assert t >= SUBLANES or t == rows, (rows, target)
    return t


def _dot(a, b, hi=None):
    return jnp.dot(a, b, preferred_element_type=F32, precision=hi)


def _dot_nt(a, b, hi=None):
    return lax.dot_general(a, b, (((1,), (1,)), ((), ())), preferred_element_type=F32, precision=hi)


def _dot_tn(a, b, hi=None):
    return lax.dot_general(a, b, (((0,), (0,)), ((), ())), preferred_element_type=F32, precision=hi)


def _sigmoid(x):
    return 1.0 / (1.0 + jnp.exp(-x))


def _softplus(x):
    return jnp.maximum(x, 0.0) + jnp.log(1.0 + jnp.exp(-jnp.abs(x)))


_HBM = pl.BlockSpec(memory_space=pltpu.HBM)


def _my_pos():
    return lax.axis_index("x"), lax.axis_index("y"), lax.axis_index("c")


def _allgather8(name, v):
    def body(v_ref, o_ref, ssem, rsem, lsem):
        x, y, c = _my_pos()
        me = 4 * x + 2 * y + c
        loc = pltpu.make_async_copy(v_ref, o_ref.at[me], lsem)
        loc.start()
        sends, recvs = [], []
        for k in range(1, 8):
            px, py, pc = (x + (k >> 2)) % 2, (y + ((k >> 1) & 1)) % 2, (c + (k & 1)) % 2
            cp = pltpu.make_async_remote_copy(
                src_ref=v_ref, dst_ref=o_ref.at[me], send_sem=ssem.at[k - 1], recv_sem=rsem.at[k - 1],
                device_id=(px, py, pc), device_id_type=MESH)
            cp.start()
            sends.append(cp)
            recvs.append(pltpu.make_async_remote_copy(
                src_ref=v_ref, dst_ref=o_ref.at[4 * px + 2 * py + pc], send_sem=ssem.at[k - 1],
                recv_sem=rsem.at[k - 1], device_id=(px, py, pc), device_id_type=MESH))
        for rc in recvs:
            rc.wait_recv()
        for cp in sends:
            cp.wait_send()
        loc.wait()

    return _pcall(
        body, name=name, out_shape=jax.ShapeDtypeStruct((8,) + v.shape, v.dtype),
        in_specs=[_HBM], out_specs=_HBM,
        scratch_shapes=[pltpu.SemaphoreType.DMA((7,)), pltpu.SemaphoreType.DMA((7,)), pltpu.SemaphoreType.DMA],
    )(v)


def _plane_peers(x, y):
    return [((x + (k >> 1)) % 2, (y + (k & 1)) % 2) for k in range(1, 4)]


def _gather4(name, shards, n_split):
    n = len(shards)

    def body(*refs):
        ins, outs = refs[:n], refs[n:2 * n]
        ssem, rsem, fsem, gsem = refs[2 * n:]
        x, y, c = _my_pos()
        me = 2 * x + y
        peers = _plane_peers(x, y)
        sends, recvs, fwds, fwd_recvs = [], [], [], []
        for t in range(n):
            split = t < n_split
            hr = ins[t].shape[0] // 2
            for k, (px, py) in enumerate(peers):
                peer = 2 * px + py
                sem = 3 * t + k
                if split:
                    mine = pl.ds(pl.multiple_of(c * hr, 16), hr)
                    other = pl.ds(pl.multiple_of((1 - c) * hr, 16), hr)
                    src, dst, got = ins[t].at[mine], outs[t].at[me, mine], outs[t].at[peer, mine]
                else:
                    src, dst, got = ins[t], outs[t].at[me], outs[t].at[peer]
                cp = pltpu.make_async_remote_copy(
                    src_ref=src, dst_ref=dst, send_sem=ssem.at[sem], recv_sem=rsem.at[sem],
                    device_id=(px, py, c), device_id_type=MESH)
                cp.start()
                sends.append(cp)
                recvs.append(pltpu.make_async_remote_copy(
                    src_ref=src, dst_ref=got, send_sem=ssem.at[sem], recv_sem=rsem.at[sem],
                    device_id=(px, py, c), device_id_type=MESH))
                if split:
                    fwds.append(pltpu.make_async_remote_copy(
                        src_ref=got, dst_ref=got, send_sem=fsem.at[sem], recv_sem=gsem.at[sem],
                        device_id=(x, y, 1 - c), device_id_type=MESH))
                    fwd_recvs.append(pltpu.make_async_remote_copy(
                        src_ref=got, dst_ref=outs[t].at[peer, other], send_sem=fsem.at[sem], recv_sem=gsem.at[sem],
                        device_id=(x, y, 1 - c), device_id_type=MESH))
                else:
                    fwds.append(None)
        for rc, fw in zip(recvs, fwds):
            rc.wait_recv()
            if fw is not None:
                fw.start()
        for fr in fwd_recvs:
            fr.wait_recv()
        for cp in sends + [fw for fw in fwds if fw is not None]:
            cp.wait_send()

    return _pcall(
        body, name=name,
        out_shape=[jax.ShapeDtypeStruct((4,) + s.shape, s.dtype) for s in shards],
        in_specs=[_HBM] * n, out_specs=[_HBM] * n,
        scratch_shapes=[pltpu.SemaphoreType.DMA((3 * n,))] * 4,
    )(*shards)


def _fill_slot(slots, own, slot):
    mask = (jnp.arange(4) == slot).reshape((4,) + (1,) * (slots.ndim - 1))
    return jnp.where(mask, own if own.ndim == slots.ndim else own[None], slots)


def _scatter4(name, partials):
    n = len(partials)

    def body(*refs):
        ins, outs = refs[:n], refs[n:2 * n]
        ssem, rsem = refs[2 * n:]
        x, y, c = _my_pos()
        me = 2 * x + y
        peers = _plane_peers(x, y)
        sends, recvs = [], []
        for t in range(n):
            for k, (px, py) in enumerate(peers):
                peer = 2 * px + py
                cp = pltpu.make_async_remote_copy(
                    src_ref=ins[t].at[peer], dst_ref=outs[t].at[me], send_sem=ssem.at[3 * t + k],
                    recv_sem=rsem.at[3 * t + k], device_id=(px, py, c), device_id_type=MESH)
                cp.start()
                sends.append(cp)
                recvs.append(pltpu.make_async_remote_copy(
                    src_ref=ins[t].at[peer], dst_ref=outs[t].at[peer], send_sem=ssem.at[3 * t + k],
                    recv_sem=rsem.at[3 * t + k], device_id=(px, py, c), device_id_type=MESH))
        for rc in recvs:
            rc.wait_recv()
        for cp in sends:
            cp.wait_send()

    return _pcall(
        body, name=name,
        out_shape=[jax.ShapeDtypeStruct(p.shape, p.dtype) for p in partials],
        in_specs=[_HBM] * n, out_specs=[_HBM] * n,
        scratch_shapes=[pltpu.SemaphoreType.DMA((3 * n,)), pltpu.SemaphoreType.DMA((3 * n,))],
    )(*partials)


def _sibling_send(name, arrays, axis=None):
    n = len(arrays)

    def body(*refs):
        ins, outs = refs[:n], refs[n:2 * n]
        ssem, rsem = refs[2 * n:]
        x, y, c = _my_pos()
        cps = []
        for t in range(n):
            src = ins[t]
            if axis is not None:
                hr = ins[t].shape[axis] // 2
                give = pl.ds(pl.multiple_of((1 - c) * hr, SUBLANES), hr)
                src = ins[t].at[give] if axis == 0 else ins[t].at[:, give]
            cp = pltpu.make_async_remote_copy(
                src_ref=src, dst_ref=outs[t], send_sem=ssem.at[t], recv_sem=rsem.at[t],
                device_id=(x, y, 1 - c), device_id_type=MESH)
            cp.start()
            cps.append(cp)
        for cp in cps:
            cp.wait_recv()
        for cp in cps:
            cp.wait_send()

    def half(a):
        shape = list(a.shape)
        if axis is not None:
            shape[axis] //= 2
        return jax.ShapeDtypeStruct(tuple(shape), a.dtype)

    return _pcall(
        body, name=name, out_shape=[half(a) for a in arrays], in_specs=[_HBM] * n, out_specs=[_HBM] * n,
        scratch_shapes=[pltpu.SemaphoreType.DMA((n,)), pltpu.SemaphoreType.DMA((n,))],
    )(*arrays)


def _mm(name, a, b, *, nt=False, out_dtype=F32, add=None, tm=1024, tn=1024, tk=4096):
    m, k = a.shape
    n = b.shape[0] if nt else b.shape[1]
    assert (b.shape[1] if nt else b.shape[0]) == k
    has_add = add is not None
    tm, tn = _pick(m, tm), _pick(n, tn)
    out_bytes = jnp.dtype(out_dtype).itemsize

    def vmem_bytes(tk_):
        steps = k // tk_
        return (4 * (tm + tn) * tk_ + 2 * tm * tn * out_bytes + (8 * tm * tn if has_add else 0)
                + (4 * tm * tn if steps > 1 else 0))

    tk = _pick(k, tk)
    while vmem_bytes(tk) > MM_VMEM_BUDGET and tk > 512:
        tk = _pick(k, tk - LANES)
    nk = k // tk

    def body(*refs):
        a_ref, b_ref = refs[0], refs[1]
        c_ref = refs[2] if has_add else None
        o_ref = refs[2 + has_add]
        p = (_dot_nt if nt else _dot)(a_ref[...], b_ref[...])
        if nk == 1:
            o_ref[...] = (p + c_ref[...] if has_add else p).astype(o_ref.dtype)
            return
        acc = refs[3 + has_add]
        kk = pl.program_id(2)

        @pl.when(kk == 0)
        def _():
            acc[...] = p

        @pl.when(jnp.logical_and(kk > 0, kk < nk - 1))
        def _():
            acc[...] += p

        @pl.when(kk == nk - 1)
        def _():
            r = acc[...] + p
            if has_add:
                r = r + c_ref[...]
            o_ref[...] = r.astype(o_ref.dtype)

    a_spec = pl.BlockSpec((tm, tk), lambda j, i, kk: (i, kk))
    if nt:
        b_spec = pl.BlockSpec((tn, tk), lambda j, i, kk: (j, kk))
    else:
        b_spec = pl.BlockSpec((tk, tn), lambda j, i, kk: (kk, j))
    o_spec = pl.BlockSpec((tm, tn), lambda j, i, kk: (i, j))
    in_specs = [a_spec, b_spec] + ([o_spec] if has_add else [])
    args = (a, b) + ((add,) if has_add else ())
    return _pcall(
        body, name=name, grid=(n // tn, m // tm, nk),
        out_shape=jax.ShapeDtypeStruct((m, n), out_dtype),
        in_specs=in_specs, out_specs=o_spec,
        scratch_shapes=[pltpu.VMEM((tm, tn), F32)] if nk > 1 else [],
        compiler_params=_params(("parallel", "parallel", "arbitrary")),
    )(*args)


class Col:
    def __init__(self, arr, w=None, cb=0, lead=None):
        self.arr, self.cb, self.lead = arr, cb, lead
        self.w = arr.shape[-1] if w is None else w
        self.rows = arr.shape[-2]


def _ew(name, fn, *, tr, ins, consts=(), outs=(), accs=(), halo_prev=(), halo_next=()):
    ins = [c if isinstance(c, Col) else Col(c) for c in ins]
    halo_prev = [c if isinstance(c, Col) else Col(c) for c in halo_prev]
    halo_next = [c if isinstance(c, Col) else Col(c) for c in halo_next]
    rows = ins[0].rows
    tr = _rows_tile(rows, tr)
    nt = rows // tr
    n_in, n_hp, n_hn, n_c, n_o, n_a = len(ins), len(halo_prev), len(halo_next), len(consts), len(outs), len(accs)
    groups = tr // SUBLANES

    def spec(col, kind):
        if kind == "cur":
            shape, idx = (tr, col.w), (lambda i, cb=col.cb: (i, cb))
        elif kind == "prev":
            shape, idx = (SUBLANES, col.w), (lambda i, cb=col.cb: (jnp.maximum(i * groups - 1, 0), cb))
        else:
            shape = (SUBLANES, col.w)
            idx = (lambda i, cb=col.cb: (jnp.minimum((i + 1) * groups, rows // SUBLANES - 1), cb))
        if col.lead is None:
            return pl.BlockSpec(shape, idx)
        return pl.BlockSpec((None,) + shape, lambda i, idx=idx, lead=col.lead: (lead,) + idx(i))

    def body(*refs):
        i = pl.program_id(0)
        p = 0
        tiles = [r[...] for r in refs[p:p + n_in]]; p += n_in
        prev8 = [r[...] for r in refs[p:p + n_hp]]; p += n_hp
        next8 = [r[...] for r in refs[p:p + n_hn]]; p += n_hn
        cvals = [r[...] for r in refs[p:p + n_c]]; p += n_c
        out_refs = refs[p:p + n_o]; p += n_o
        acc_refs = refs[p:p + n_a]
        out_v, acc_v = fn(i, nt, tiles, prev8, next8, cvals)
        for r, v in zip(out_refs, out_v):
            r[...] = v.astype(r.dtype)
        if n_a:
            @pl.when(i == 0)
            def _():
                for r, v in zip(acc_refs, acc_v):
                    r[...] = v

            @pl.when(i > 0)
            def _():
                for r, v in zip(acc_refs, acc_v):
                    r[...] += v

    in_specs = ([spec(c, "cur") for c in ins] + [spec(c, "prev") for c in halo_prev]
                + [spec(c, "next") for c in halo_next]
                + [pl.BlockSpec(c.shape, lambda i, nd=c.ndim: (0,) * nd) for c in consts])
    out_specs = ([pl.BlockSpec((tr, w), lambda i: (i, 0)) for w, _ in outs]
                 + [pl.BlockSpec(s, lambda i: (0, 0)) for s in accs])
    out_shape = ([jax.ShapeDtypeStruct((rows, w), dt) for w, dt in outs]
                 + [jax.ShapeDtypeStruct(s, F32) for s in accs])
    args = [c.arr for c in ins] + [c.arr for c in halo_prev] + [c.arr for c in halo_next] + list(consts)
    res = _pcall(body, name=name, grid=(nt,), out_shape=out_shape, in_specs=in_specs, out_specs=out_specs,
                 compiler_params=_params(("arbitrary",)))(*args)
    return res[:n_o], res[n_o:]


def _colsum(v):
    return jnp.sum(v, axis=0, keepdims=True)


def _heads_of(w):
    return w // HEAD_DIM


def _per_head(fn, *arrays):
    nh = _heads_of(arrays[0].shape[1])
    res = [fn(*[a[:, h * HEAD_DIM:(h + 1) * HEAD_DIM] for a in arrays]) for h in range(nh)]
    if isinstance(res[0], tuple):
        return tuple(jnp.concatenate([r[j] for r in res], axis=1) for j in range(len(res[0])))
    return jnp.concatenate(res, axis=1)


def _head_sum(v):
    nh = _heads_of(v.shape[1])
    out = v[:, :HEAD_DIM]
    for h in range(1, nh):
        out = out + v[:, h * HEAD_DIM:(h + 1) * HEAD_DIM]
    return out


def _rms_fwd(x, w):
    r = lax.rsqrt(jnp.mean(x * x, axis=1, keepdims=True) + EPS)
    return x * r * w


def _rms_bwd(x, w, dy):
    r = lax.rsqrt(jnp.mean(x * x, axis=1, keepdims=True) + EPS)
    xh = x * r
    dxh = dy * w
    dx = r * (dxh - xh * jnp.mean(dxh * xh, axis=1, keepdims=True))
    return dx, dy * xh


def _silu(x):
    return x * _sigmoid(x)


def _dsilu(x):
    s = _sigmoid(x)
    return s * (1.0 + x * (1.0 - s))


SB_BQ_FWD = 512
SB_BQ_BWD = 1024
SB_PAIR = 2
SB_BK = 256


def _softplus_pos(z):
    return jnp.maximum(z, 0.0) + jnp.log(1.0 + jnp.exp(-jnp.abs(z)))


def _split_dot(v, tri):
    top = lax.bitcast_convert_type(lax.bitcast_convert_type(v, jnp.int32) & jnp.int32(-65536), F32)
    return _dot(top.astype(BF16), tri) + _dot((v - top).astype(BF16), tri)


def _sb_fwd(qn, kn, vb, *, bq=SB_BQ_FWD, bk=SB_BK):
    s_len, hd = qn.shape
    nh = hd // HEAD_DIM
    bk = min(bk, s_len)
    bq = min(bq, s_len)
    ndiag = bq // bk
    scale = HEAD_DIM ** -0.5

    def body(q_ref, k_ref, v_ref, o_ref, lt_ref):
        i = pl.program_id(1)
        krow = lax.broadcasted_iota(jnp.int32, (bk, bk), 0)
        kcol = lax.broadcasted_iota(jnp.int32, (bk, bk), 1)
        later = (krow > kcol).astype(BF16)
        row = lax.broadcasted_iota(jnp.int32, (bq, bk), 0)
        col = lax.broadcasted_iota(jnp.int32, (bq, bk), 1)
        q = q_ref[...]

        def tiles(js, carry, diags):
            run, acc = carry
            ks, vs, zs = [], [], []
            for j in js:
                off = pl.multiple_of(j * bk, bk)
                ks.append(k_ref[pl.ds(off, bk), :])
                vs.append(v_ref[pl.ds(off, bk), :])
                zs.append(_dot_nt(q, ks[-1]) * scale)
            sps, cums, masks = [], [], []
            for z, diag in zip(zs, diags):
                sp = _softplus_pos(z)
                causal = None
                if diag is not None:
                    causal = col + diag * bk < row
                    sp = jnp.where(causal, sp, 0.0)
                sps.append(sp)
                masks.append(causal)
                cums.append(_split_dot(sp, later))
            for z, sp, cum, causal, v in zip(zs, sps, cums, masks, vs):
                w = jnp.exp((z - sp) - (cum + run))
                if causal is not None:
                    w = jnp.where(causal, w, 0.0)
                acc = acc + _dot(w.astype(BF16), v)
                run = run + cum[:, 0:1] + sp[:, 0:1]
            return run, acc

        carry = (jnp.zeros((bq, 1), F32), jnp.zeros((bq, HEAD_DIM), F32))
        for dg in reversed(range(0, ndiag, SB_PAIR)):
            dgs = list(reversed(range(dg, dg + SB_PAIR)))
            carry = tiles([i * ndiag + g for g in dgs], carry, dgs)
        run, acc = lax.fori_loop(
            0, i * ndiag // SB_PAIR,
            lambda t, cr: tiles([i * ndiag - 1 - SB_PAIR * t - u for u in range(SB_PAIR)], cr, [None] * SB_PAIR), carry)
        o_ref[...] = acc.astype(o_ref.dtype)
        lt_ref[...] = run

    qspec = pl.BlockSpec((bq, HEAD_DIM), lambda h, i: (i, h))
    kspec = pl.BlockSpec((s_len, HEAD_DIM), lambda h, i: (0, h))
    return _pcall(
        body, name="sb_fwd", grid=(nh, s_len // bq),
        out_shape=[jax.ShapeDtypeStruct((s_len, hd), BF16), jax.ShapeDtypeStruct((nh, s_len, 1), F32)],
        in_specs=[qspec, kspec, kspec],
        out_specs=[qspec, pl.BlockSpec((None, bq, 1), lambda h, i: (h, i, 0))],
        compiler_params=_params(("parallel", "arbitrary")),
    )(qn, kn, vb)


def _sb_bwd(qn, kn, vb, do, ltot, *, bq=SB_BQ_BWD, bk=SB_BK):
    s_len, hd = qn.shape
    nh = hd // HEAD_DIM
    bk = min(bk, s_len)
    bq = min(bq, s_len)
    ndiag = bq // bk
    scale = HEAD_DIM ** -0.5

    def body(q_ref, k_ref, v_ref, do_ref, lt_ref, dq_ref, dk_ref, dv_ref):
        i = pl.program_id(1)

        @pl.when(i == 0)
        def _():
            dk_ref[...] = jnp.zeros_like(dk_ref)
            dv_ref[...] = jnp.zeros_like(dv_ref)

        krow = lax.broadcasted_iota(jnp.int32, (bk, bk), 0)
        kcol = lax.broadcasted_iota(jnp.int32, (bk, bk), 1)
        upto = (krow <= kcol).astype(BF16)
        before = (krow < kcol).astype(BF16)
        row = lax.broadcasted_iota(jnp.int32, (bq, bk), 0)
        col = lax.broadcasted_iota(jnp.int32, (bq, bk), 1)
        q = q_ref[...]
        do_t = do_ref[...]
        lt = lt_ref[...]

        def tiles(js, carry, diags):
            pre, ecar, dq = carry
            offs, ks, zs, dws = [], [], [], []
            for j in js:
                off = pl.multiple_of(j * bk, bk)
                offs.append(off)
                ks.append(k_ref[pl.ds(off, bk), :])
                zs.append(_dot_nt(q, ks[-1]) * scale)
                dws.append(_dot_nt(do_t, v_ref[pl.ds(off, bk), :]))
            sps, cums, masks = [], [], []
            for z, diag in zip(zs, diags):
                sp = _softplus_pos(z)
                causal = None
                if diag is not None:
                    causal = col + diag * bk < row
                    sp = jnp.where(causal, sp, 0.0)
                sps.append(sp)
                masks.append(causal)
                cums.append(_split_dot(sp, upto))
            es, ebs, exs, sigs = [], [], [], []
            for off, z, sp, cum, dw, causal in zip(offs, zs, sps, cums, dws, masks):
                lb = z - sp
                w = jnp.exp(lb - (lt - (pre + cum)))
                if causal is not None:
                    w = jnp.where(causal, w, 0.0)
                dv_ref[pl.ds(off, bk), :] += _dot_tn(w.astype(BF16), do_t)
                e = dw * w
                eb = e.astype(BF16)
                es.append(e)
                ebs.append(eb)
                exs.append(_dot(eb, before))
                sigs.append(jnp.exp(lb))
                pre = pre + cum[:, bk - 1:bk]
            for off, k, e, eb, exm, sig, causal in zip(offs, ks, es, ebs, exs, sigs, masks):
                ex = exm + ecar
                dz = (e - sig * (e + ex)) * scale
                if causal is not None:
                    dz = jnp.where(causal, dz, 0.0)
                dzb = dz.astype(BF16)
                dk_ref[pl.ds(off, bk), :] += _dot_tn(dzb, q)
                dq = dq + _dot(dzb, k)
                ecar = ex[:, bk - 1:bk] + eb[:, bk - 1:bk].astype(F32)
            return pre, ecar, dq

        init = (jnp.zeros((bq, 1), F32), jnp.zeros((bq, 1), F32), jnp.zeros((bq, HEAD_DIM), F32))
        carry = lax.fori_loop(
            0, i * ndiag // SB_PAIR,
            lambda t, cr: tiles([SB_PAIR * t + u for u in range(SB_PAIR)], cr, [None] * SB_PAIR), init)
        for dg in range(0, ndiag, SB_PAIR):
            dgs = list(range(dg, dg + SB_PAIR))
            carry = tiles([i * ndiag + g for g in dgs], carry, dgs)
        dq_ref[...] = carry[2]

    qspec = pl.BlockSpec((bq, HEAD_DIM), lambda h, i: (i, h))
    kspec = pl.BlockSpec((s_len, HEAD_DIM), lambda h, i: (0, h))
    return _pcall(
        body, name="sb_bwd", grid=(nh, s_len // bq),
        out_shape=[jax.ShapeDtypeStruct((s_len, hd), F32)] * 3,
        in_specs=[qspec, kspec, kspec, qspec, pl.BlockSpec((None, bq, 1), lambda h, i: (h, i, 0))],
        out_specs=[qspec, kspec, kspec],
        compiler_params=_params(("parallel", "arbitrary")),
    )(qn, kn, vb, do, ltot)


GDN_GROUP = 16


def _gdn_group(nh):
    return min(GDN_GROUP, nh)


def _gdn_chunk_terms(qh, kh, vh, g_r, g_c, b_c):
    c = GDN_CHUNK
    r = lax.broadcasted_iota(jnp.int32, (c, c), 0)
    s = lax.broadcasted_iota(jnp.int32, (c, c), 1)
    tril, stril = r >= s, r > s
    gcc = jnp.sum(jnp.where(tril, g_r, 0.0), axis=1, keepdims=True)
    gcr = jnp.sum(jnp.where(r <= s, g_c, 0.0), axis=0, keepdims=True)
    dm = jnp.where(tril, jnp.exp(jnp.where(tril, gcc - gcr, 0.0)), 0.0)
    kb = kh.astype(BF16)
    kk = _dot_nt(kb, kb)
    qk = _dot_nt(qh.astype(BF16), kb)
    egc = jnp.exp(gcc)
    gcl = gcc[c - 1:c, :]
    t = dict(tril=tril, stril=stril, gcc=gcc, dm=dm, kb=kb, kk=kk, qk=qk, egc=egc,
             ekd=jnp.exp(gcl - gcc), gl=jnp.exp(gcl),
             a=jnp.where(stril, b_c * kk * dm, 0.0),
             bv=b_c * vh, bk=(b_c * egc) * kh, at=jnp.where(tril, qk * dm, 0.0))
    t["qg"] = qh * egc
    t["kd"] = kh * t["ekd"]
    return t


def _unit_lower_inverses(mats):
    c = GDN_CHUNK
    r = lax.broadcasted_iota(jnp.int32, (c, c), 0)
    s = lax.broadcasted_iota(jnp.int32, (c, c), 1)
    eye = (r == s).astype(F32)
    ps = [-a for a in mats]
    ts = [eye + p for p in ps]
    span = 2
    while span < c:
        ps = [_dot(p, p, hi=HIGH) for p in ps]
        ts = [t + _dot(t, p, hi=HIGH) for t, p in zip(ts, ps)]
        span *= 2
    return ts


def _gdn_fwd(q, k, v, g_col, g_row, b_col, b_row):
    s_len, d = q.shape
    nh = d // HEAD_DIM
    c = GDN_CHUNK
    n_chunks = s_len // c
    grp = _gdn_group(nh)

    def body(q_ref, k_ref, v_ref, gc_ref, gr_ref, bc_ref, br_ref, o_ref, ss_ref, ts_ref, st):
        n = pl.program_id(1)

        @pl.when(n == 0)
        def _():
            st[...] = jnp.zeros_like(st)

        heads = range(grp)
        sls = [slice(i * HEAD_DIM, (i + 1) * HEAD_DIM) for i in heads]
        terms = [_gdn_chunk_terms(q_ref[:, sls[i]], k_ref[:, sls[i]], v_ref[:, sls[i]],
                                  gr_ref[i:i + 1, :], gc_ref[:, i:i + 1], bc_ref[:, i:i + 1]) for i in heads]
        tinvs = _unit_lower_inverses([t["a"] for t in terms])
        wvs = [_dot(tinv, t["bv"], hi=HIGH) for tinv, t in zip(tinvs, terms)]
        wks = [_dot(tinv, t["bk"], hi=HIGH) for tinv, t in zip(tinvs, terms)]
        states = [st[i] for i in heads]
        sbs = [state.astype(BF16) for state in states]
        ubs = [(wv - _dot(wk.astype(BF16), sb)).astype(BF16) for wv, wk, sb in zip(wvs, wks, sbs)]
        for i in heads:
            t = terms[i]
            o_ref[:, sls[i]] = _dot(t["qg"].astype(BF16), sbs[i]) + _dot(t["at"].astype(BF16), ubs[i])
            ss_ref[i] = states[i]
            ts_ref[i] = tinvs[i]
            st[i] = t["gl"] * states[i] + _dot_tn(t["kd"].astype(BF16), ubs[i])

    tok = pl.BlockSpec((c, grp * HEAD_DIM), lambda h, n: (n, h))
    colspec = pl.BlockSpec((None, c, grp), lambda h, n: (h, n, 0))
    rowspec = pl.BlockSpec((None, None, grp, c), lambda h, n: (h, n, 0, 0))
    return _pcall(
        body, name="gdn_fwd", grid=(nh // grp, n_chunks),
        out_shape=[jax.ShapeDtypeStruct((s_len, d), F32),
                   jax.ShapeDtypeStruct((n_chunks, nh, HEAD_DIM, HEAD_DIM), F32),
                   jax.ShapeDtypeStruct((n_chunks, nh, c, c), F32)],
        in_specs=[tok, tok, tok, colspec, rowspec, colspec, rowspec],
        out_specs=[tok, pl.BlockSpec((None, grp, HEAD_DIM, HEAD_DIM), lambda h, n: (n, h, 0, 0)),
                   pl.BlockSpec((None, grp, c, c), lambda h, n: (n, h, 0, 0))],
        scratch_shapes=[pltpu.VMEM((grp, HEAD_DIM, HEAD_DIM), F32)],
        compiler_params=_params(("parallel", "arbitrary")),
    )(q, k, v, g_col, g_row, b_col, b_row)


def _gdn_bwd(q, k, v, g_col, g_row, b_col, b_row, states, tinvs, do):
    s_len, d = q.shape
    nh = d // HEAD_DIM
    c = GDN_CHUNK
    n_chunks = s_len // c
    grp = _gdn_group(nh)

    def body(q_ref, k_ref, v_ref, gc_ref, gr_ref, bc_ref, br_ref, ss_ref, ts_ref, do_ref,
             dq_ref, dk_ref, dv_ref, dgb_ref, dst):
        n = pl.program_id(1)

        @pl.when(n == 0)
        def _():
            dst[...] = jnp.zeros_like(dst)

        r = lax.broadcasted_iota(jnp.int32, (c, c), 0)
        s = lax.broadcasted_iota(jnp.int32, (c, c), 1)
        suffix = (r <= s).astype(F32)
        lane = lax.broadcasted_iota(jnp.int32, (c, LANES), 1)
        heads = range(grp)
        sls = [slice(i * HEAD_DIM, (i + 1) * HEAD_DIM) for i in heads]
        qs = [q_ref[:, sl] for sl in sls]
        ks = [k_ref[:, sl] for sl in sls]
        vs = [v_ref[:, sl] for sl in sls]
        bcs = [bc_ref[:, i:i + 1] for i in heads]
        ts = [_gdn_chunk_terms(qs[i], ks[i], vs[i], gr_ref[i:i + 1, :], gc_ref[:, i:i + 1], bcs[i]) for i in heads]
        tinv = [ts_ref[i] for i in heads]
        state = [ss_ref[i] for i in heads]
        sb = [x.astype(BF16) for x in state]
        dnext = [dst[i] for i in heads]
        dnb = [x.astype(BF16) for x in dnext]
        dob = [do_ref[:, sl].astype(BF16) for sl in sls]
        wv = [_dot(tinv[i], ts[i]["bv"], hi=HIGH) for i in heads]
        wk = [_dot(tinv[i], ts[i]["bk"], hi=HIGH) for i in heads]
        wkb = [x.astype(BF16) for x in wk]
        ub = [(wv[i] - _dot(wkb[i], sb[i])).astype(BF16) for i in heads]
        du = [_dot_tn(ts[i]["at"].astype(BF16), dob[i]) + _dot(ts[i]["kd"].astype(BF16), dnb[i]) for i in heads]
        dub = [x.astype(BF16) for x in du]
        dat = [jnp.where(ts[i]["tril"], _dot_nt(dob[i], ub[i]), 0.0) for i in heads]
        dqg = [_dot_nt(dob[i], sb[i]) for i in heads]
        dkd = [_dot_nt(ub[i], dnb[i]) for i in heads]
        dwk = [-_dot_nt(dub[i], sb[i]) for i in heads]
        for i in heads:
            dst[i] = (ts[i]["gl"] * dnext[i] + _dot_tn(ts[i]["qg"].astype(BF16), dob[i]) - _dot_tn(wkb[i], dub[i]))
        dbv = [_dot_tn(tinv[i], du[i], hi=HIGH) for i in heads]
        dbk = [_dot_tn(tinv[i], dwk[i], hi=HIGH) for i in heads]
        dtm = [_dot_nt(du[i], ts[i]["bv"], hi=HIGH) + _dot_nt(dwk[i], ts[i]["bk"], hi=HIGH) for i in heads]
        dtt = [_dot_nt(dtm[i], tinv[i], hi=HIGH) for i in heads]
        da = [-jnp.where(ts[i]["stril"], _dot_tn(tinv[i], dtt[i], hi=HIGH), 0.0) for i in heads]
        rs = lambda m: jnp.sum(m, axis=1, keepdims=True)
        dgb = jnp.zeros((c, LANES), F32)
        for i in heads:
            t, b_c, dm, kb = ts[i], bcs[i], ts[i]["dm"], ts[i]["kb"]
            egc, ekd = t["egc"], t["ekd"]
            dkk = da[i] * b_c * dm
            ddm = da[i] * b_c * t["kk"] + dat[i] * t["qk"]
            dqkb, dkkb = (dat[i] * dm).astype(BF16), dkk.astype(BF16)
            dq_ref[:, sls[i]] = _dot(dqkb, kb) + dqg[i] * egc
            dk_ref[:, sls[i]] = (_dot_tn(dqkb, qs[i].astype(BF16)) + _dot(dkkb, kb) + _dot_tn(dkkb, kb)
                                 + dbk[i] * (b_c * egc) + dkd[i] * ekd)
            dv_ref[:, sls[i]] = dbv[i] * b_c
            dbk_k = rs(dbk[i] * ks[i])
            dbeta = rs(da[i] * t["kk"] * dm) + rs(dbv[i] * vs[i]) + dbk_k * egc
            mx = ddm * dm
            ekd_sum = rs(dkd[i] * ks[i]) * ekd
            dgc = rs(mx) + dbk_k * b_c * egc + rs(dqg[i] * qs[i]) * egc - ekd_sum
            dgl = jnp.sum(rs(dnext[i] * state[i]), axis=0, keepdims=True)
            tail = jnp.sum(ekd_sum, axis=0, keepdims=True) + dgl * t["gl"]
            dg = (_dot(suffix, jnp.broadcast_to(dgc, (c, LANES)), hi=HIGH)[:, 0:1]
                  - rs(_dot_nt(suffix, mx, hi=HIGH)) + tail)
            dgb = dgb + jnp.where(lane == i, dbeta, 0.0) + jnp.where(lane == grp + i, dg, 0.0)
        dgb_ref[...] = dgb

    last = n_chunks - 1
    tok = pl.BlockSpec((c, grp * HEAD_DIM), lambda h, n: (last - n, h))
    colspec = pl.BlockSpec((None, c, grp), lambda h, n: (h, last - n, 0))
    rowspec = pl.BlockSpec((None, None, grp, c), lambda h, n: (h, last - n, 0, 0))
    return _pcall(
        body, name="gdn_bwd", grid=(nh // grp, n_chunks),
        out_shape=[jax.ShapeDtypeStruct((s_len, d), F32)] * 3
        + [jax.ShapeDtypeStruct((nh // grp, s_len, LANES), F32)],
        in_specs=[tok, tok, tok, colspec, rowspec, colspec, rowspec,
                  pl.BlockSpec((None, grp, HEAD_DIM, HEAD_DIM), lambda h, n: (last - n, h, 0, 0)),
                  pl.BlockSpec((None, grp, c, c), lambda h, n: (last - n, h, 0, 0)), tok],
        out_specs=[tok, tok, tok, pl.BlockSpec((None, c, LANES), lambda h, n: (h, last - n, 0))],
        scratch_shapes=[pltpu.VMEM((grp, HEAD_DIM, HEAD_DIM), F32)],
        compiler_params=_params(("parallel", "arbitrary")),
    )(q, k, v, g_col, g_row, b_col, b_row, states, tinvs, do)


def _shift_down(prev8, cur, k):
    if k == 0:
        return cur
    ext = jnp.concatenate([prev8, cur], axis=0)
    return pltpu.roll(ext, k, 0)[SUBLANES:, :]


def _shift_up(cur, next8, k):
    if k == 0:
        return cur
    ext = jnp.concatenate([cur, next8], axis=0)
    n = ext.shape[0]
    return pltpu.roll(ext, n - k, 0)[:cur.shape[0], :]


def _conv_pre(i, x, prev8, w):
    prev8 = jnp.where(i == 0, 0.0, prev8)
    pre = None
    for j in range(GDN_CONV):
        term = w[j:j + 1, :] * _shift_down(prev8, x, GDN_CONV - 1 - j)
        pre = term if pre is None else pre + term
    return pre, prev8


def _l2_fwd(a, mult):
    return a * (lax.rsqrt(jnp.sum(a * a, axis=1, keepdims=True) + EPS) * mult)


def _l2_bwd(a, dy, mult):
    r = lax.rsqrt(jnp.sum(a * a, axis=1, keepdims=True) + EPS)
    dy = dy * mult
    return r * dy - a * (r * r * r) * jnp.sum(a * dy, axis=1, keepdims=True)


def _conv_fwd(xb, conv_w, group, *, norm, mult, tr=256):
    d = xb.shape[1] // 3

    def fn(i, nt, tiles, prev8, next8, cv):
        pre, _ = _conv_pre(i, tiles[0], prev8[0], cv[0])
        a = _silu(pre)
        if norm:
            a = _per_head(lambda ah: _l2_fwd(ah, mult), a)
        return [a], []

    col = Col(xb, d, group)
    wg = lax.slice_in_dim(conv_w, group * d, (group + 1) * d, axis=1)
    (y,), _ = _ew(f"conv_fwd{group}", fn, tr=tr, ins=[col], halo_prev=[col], consts=[wg], outs=[(d, F32)])
    return y


def _conv_bwd(xb, conv_w, group, dy, *, norm, mult, tr=256):
    d = xb.shape[1] // 3
    col = Col(xb, d, group)
    wg = lax.slice_in_dim(conv_w, group * d, (group + 1) * d, axis=1)

    def fn_pre(i, nt, tiles, prev8, next8, cv):
        x, dyt = tiles
        pre, p8 = _conv_pre(i, x, prev8[0], cv[0])
        if norm:
            da = _per_head(lambda ah, dh: _l2_bwd(ah, dh, mult), _silu(pre), dyt)
        else:
            da = dyt
        dpre = da * _dsilu(pre)
        tap = lax.broadcasted_iota(jnp.int32, (GDN_CONV, d), 0)
        dw = jnp.zeros((GDN_CONV, d), F32)
        for j in range(GDN_CONV):
            dw = dw + jnp.where(tap == j, _colsum(dpre * _shift_down(p8, x, GDN_CONV - 1 - j)), 0.0)
        return [dpre], [dw]

    (dpre,), (dw,) = _ew(f"conv_bwd_pre{group}", fn_pre, tr=tr, ins=[col, dy], halo_prev=[col], consts=[wg],
                         outs=[(d, F32)], accs=[(GDN_CONV, d)])

    def fn_dx(i, nt, tiles, prev8, next8, cv):
        n8 = jnp.where(i == nt - 1, 0.0, next8[0])
        dx = None
        for j in range(GDN_CONV):
            term = cv[0][j:j + 1, :] * _shift_up(tiles[0], n8, GDN_CONV - 1 - j)
            dx = term if dx is None else dx + term
        return [dx], []

    (dx,), _ = _ew(f"conv_bwd_dx{group}", fn_dx, tr=tr, ins=[dpre], halo_next=[dpre], consts=[wg], outs=[(d, BF16)])
    return dx, dw


def _adamw(name, w, m, v, grads, *, tr=64):
    shape = w.shape
    w2, m2, v2 = [a.reshape(-1, shape[-1]) for a in (w, m, v)]
    n_g = len(grads)
    bc1 = 1.0 - ADAM_B1 ** ADAM_STEP
    bc2 = 1.0 - ADAM_B2 ** ADAM_STEP

    def fn(i, nt, tiles, prev8, next8, cv):
        wt, mt, vt = tiles[:3]
        g = tiles[3]
        for extra in tiles[4:]:
            g = g + extra
        mn = ADAM_B1 * mt + (1.0 - ADAM_B1) * g
        vn = ADAM_B2 * vt + (1.0 - ADAM_B2) * (g * g)
        delta = -ADAM_LR * ((mn / bc1) / (jnp.sqrt(vn / bc2) + ADAM_EPS) + ADAM_WD * wt)
        return [g, delta, mn, vn], []

    width = shape[-1]
    outs, _ = _ew(name, fn, tr=tr, ins=[w2, m2, v2] + list(grads), outs=[(width, F32)] * 4)
    assert n_g >= 1
    return tuple(o.reshape(shape) for o in outs)


def _pad_cols(a, width):
    return jnp.pad(a, ((0, 0), (0, width - a.shape[1])))


def _gdn_layouts(gbeta, nh, n_chunks):
    grp = _gdn_group(nh)
    s_len = gbeta.shape[0]

    def lay(a):
        col = a.reshape(s_len, nh // grp, grp).transpose(1, 0, 2)
        row = a.reshape(n_chunks, GDN_CHUNK, nh // grp, grp).transpose(2, 0, 3, 1)
        return col, row

    b_col, b_row = lay(gbeta[:, :nh])
    g_col, g_row = lay(gbeta[:, nh:2 * nh])
    return g_col, g_row, b_col, b_row


def kernel(x, c, w_mod, b_mod, norm1_w, w_in, q_norm_w, k_norm_w, conv_w, a_log, dt_bias, o_norm_w, p_a, p_b, w_out, norm2_w, w_gate, w_up, w_down, loss_target, m_w_mod, m_b_mod, m_norm1_w, m_w_in, m_q_norm_w, m_k_norm_w, m_conv_w, m_a_log, m_dt_bias, m_o_norm_w, m_p_a, m_p_b, m_w_out, m_norm2_w, m_w_gate, m_w_up, m_w_down, v_w_mod, v_b_mod, v_norm1_w, v_w_in, v_q_norm_w, v_k_norm_w, v_conv_w, v_a_log, v_dt_bias, v_o_norm_w, v_p_a, v_p_b, v_w_out, v_norm2_w, v_w_gate, v_w_up, v_w_down):
    s_len, d = x.shape[1], x.shape[2]
    nh = d // HEAD_DIM
    n_chunks = s_len // GDN_CHUNK
    ff = 4 * w_gate.shape[2]
    mx, my, mc = _my_pos()
    chip = 2 * mx + my
    dev = 2 * chip + mc
    x2 = x[0]
    tgt = loss_target[0]

    c_all = _allgather8("ag_c", _pad_cols(c, d).reshape(SUBLANES, d // SUBLANES)).reshape(8, d)
    wm = w_mod[0]
    mod_w = wm.shape[1]
    bm_cols = lax.dynamic_slice_in_dim(b_mod, chip * mod_w, mod_w, axis=1)

    def mod_body(c_ref, w_ref, b_ref, o_ref, ca_ref):
        ca = _silu(c_ref[...])
        ca_ref[...] = ca
        o_ref[...] = _dot(ca, w_ref[...], hi=HIGHEST) + b_ref[...]

    tn_mod = _pick(mod_w, 512)
    mod8, c_act = _pcall(
        mod_body, name="mod_fwd", grid=(mod_w // tn_mod,),
        out_shape=[jax.ShapeDtypeStruct((8, mod_w), F32), jax.ShapeDtypeStruct((8, d), F32)],
        in_specs=[pl.BlockSpec((8, d), lambda j: (0, 0)), pl.BlockSpec((d, tn_mod), lambda j: (0, j)),
                  pl.BlockSpec((1, tn_mod), lambda j: (0, j))],
        out_specs=[pl.BlockSpec((8, tn_mod), lambda j: (0, j)), pl.BlockSpec((8, d), lambda j: (0, 0))],
        compiler_params=_params(("arbitrary",)),
    )(c_all, wm, bm_cols)
    mod_all = _allgather8("ag_mod", mod8)
    mod_me = mod_all.reshape(4, 2, 8, mod_w)[:, mc, dev, :].reshape(1, 6 * d)
    shift1, scale1, gate1, shift2, scale2, gate2 = [mod_me[:, j * d:(j + 1) * d] for j in range(6)]

    shards = [w_in[0].astype(BF16), p_a[0].astype(BF16), p_b[0].astype(BF16), w_out[0].astype(BF16),
              w_gate[0].astype(BF16), w_up[0].astype(BF16), w_down[0].astype(BF16), conv_w[0]]
    gathered = [_fill_slot(g, sh, chip) for g, sh in zip(_gather4("ag_weights", shards, n_split=7), shards)]
    w_in_f = gathered[0].transpose(1, 0, 2).reshape(d, -1)
    wa = w_in_f[:, :3 * d]
    wb = w_in_f[:, 3 * d:6 * d]
    wzg = jnp.concatenate([w_in_f[:, 6 * d:7 * d], w_in_f[:, 7 * d + 2 * nh:]], axis=1)
    wba = _pad_cols(w_in_f[:, 7 * d:7 * d + 2 * nh], LANES)
    p_a_f, p_b_f, w_out_f = [g.reshape(d, d) for g in gathered[1:4]]
    w_gate_f, w_up_f = [g.transpose(1, 0, 2).reshape(d, ff) for g in gathered[4:6]]
    w_down_f = gathered[6].reshape(ff, d)
    conv_f = gathered[7].transpose(1, 0, 2).reshape(GDN_CONV, 3 * d)

    def norm_mod_fn(i, nt, tiles, prev8, next8, cv):
        w, sc, sh = cv
        return [_rms_fwd(tiles[0], w) * (1.0 + sc) + sh], []

    (u1,), _ = _ew("norm_mod1", norm_mod_fn, tr=512, ins=[x2], consts=[norm1_w, scale1, shift1], outs=[(d, BF16)])
    proj_a = _mm("proj_a", u1, wa)
    proj_b = _mm("proj_b", u1, wb)
    proj_zg = _mm("proj_zg", u1, wzg)
    proj_ba = _mm("proj_ba", u1, wba)

    def qknorm_fn(i, nt, tiles, prev8, next8, cv):
        qa, ka, va = tiles
        return [_per_head(lambda h: _rms_fwd(h, cv[0]), qa), _per_head(lambda h: _rms_fwd(h, cv[1]), ka), va], []

    (qn, kn, vb), _ = _ew("qknorm", qknorm_fn, tr=256,
                          ins=[Col(proj_a, d, 0), Col(proj_a, d, 1), Col(proj_a, d, 2)],
                          consts=[q_norm_w, k_norm_w], outs=[(d, BF16)] * 3)
    o_a, ltot = _sb_fwd(qn, kn, vb)

    lane_ids = jnp.arange(LANES)
    is_b = (lane_ids < nh)[None, :]
    is_a = ((lane_ids >= nh) & (lane_ids < 2 * nh))[None, :]
    alog128 = jnp.zeros((1, LANES), F32).at[:, nh:2 * nh].set(a_log)
    dtb128 = jnp.zeros((1, LANES), F32).at[:, nh:2 * nh].set(dt_bias)
    is_b_f, is_a_f = is_b.astype(F32), is_a.astype(F32)

    def gbeta_fn(i, nt, tiles, prev8, next8, cv):
        al, dtb, mb, ma = cv
        ba = tiles[0]
        g = -jnp.exp(al) * _softplus(ba + dtb)
        return [jnp.where(mb > 0.5, _sigmoid(ba), jnp.where(ma > 0.5, g, 0.0))], []

    (gbeta,), _ = _ew("gbeta", gbeta_fn, tr=1024, ins=[proj_ba], consts=[alog128, dtb128, is_b_f, is_a_f],
                      outs=[(LANES, F32)])
    g_col, g_row, b_col, b_row = _gdn_layouts(gbeta, nh, n_chunks)
    qscale = HEAD_DIM ** -0.5
    q_b = _conv_fwd(proj_b, conv_f, 0, norm=True, mult=qscale)
    k_b = _conv_fwd(proj_b, conv_f, 1, norm=True, mult=1.0)
    v_b = _conv_fwd(proj_b, conv_f, 2, norm=False, mult=1.0)
    o_raw, states, tinvs = _gdn_fwd(q_b, k_b, v_b, g_col, g_row, b_col, b_row)

    def gated_norm_fn(i, nt, tiles, prev8, next8, cv):
        o, z = tiles
        return [_per_head(lambda h: _rms_fwd(h, cv[0]), o) * _silu(z)], []

    (o_b,), _ = _ew("gated_norm", gated_norm_fn, tr=256, ins=[o_raw, Col(proj_zg, d, 0)], consts=[o_norm_w],
                    outs=[(d, BF16)])
    y_a = _mm("out_a", o_a, p_a_f)
    y_b = _mm("out_b", o_b, p_b_f)

    def merge_fn(i, nt, tiles, prev8, next8, cv):
        ya, yb, ga, gb = tiles
        return [_sigmoid(ga) * ya + _sigmoid(gb) * yb], []

    (merged,), _ = _ew("merge", merge_fn, tr=256, ins=[y_a, y_b, Col(proj_zg, d, 1), Col(proj_zg, d, 2)],
                       outs=[(d, BF16)])
    y_o = _mm("out_proj", merged, w_out_f)

    def resid_norm_fn(i, nt, tiles, prev8, next8, cv):
        xt, yo = tiles
        g1, w, sc, sh = cv
        h1 = xt + g1 * yo
        return [h1, _rms_fwd(h1, w) * (1.0 + sc) + sh], []

    (h1, u2), _ = _ew("resid_norm2", resid_norm_fn, tr=256, ins=[x2, y_o],
                      consts=[gate1, norm2_w, scale2, shift2], outs=[(d, F32), (d, BF16)])
    gt = _mm("ff_gate", u2, w_gate_f)
    up = _mm("ff_up", u2, w_up_f)

    def swiglu_fn(i, nt, tiles, prev8, next8, cv):
        return [_silu(tiles[0]) * tiles[1]], []

    (act,), _ = _ew("swiglu", swiglu_fn, tr=128, ins=[gt, up], outs=[(ff, BF16)])
    y_d = _mm("ff_down", act, w_down_f)

    def loss_fn(i, nt, tiles, prev8, next8, cv):
        h1t, yd, tg = tiles
        diff = h1t + cv[0] * yd - tg
        dy = diff * (1.0 / d)
        return [dy, dy * cv[0]], [_colsum(0.5 * diff * dy), _colsum(dy * yd)]

    (dy, dyd), (loss_cols, dgate2) = _ew("loss", loss_fn, tr=256, ins=[h1, y_d, tgt], consts=[gate2],
                                         outs=[(d, F32), (d, BF16)], accs=[(1, d), (1, d)])
    loss = lax.psum(jnp.sum(loss_cols), ("x", "y", "c"))

    dact = _mm("d_act", dyd, w_down_f, nt=True)
    g_w_down = _mm("g_w_down", act.T, dyd)

    def swiglu_bwd_fn(i, nt, tiles, prev8, next8, cv):
        da, g, u = tiles
        return [da * u * _dsilu(g), da * _silu(g)], []

    (dgt, dup), _ = _ew("swiglu_bwd", swiglu_bwd_fn, tr=128, ins=[dact, gt, up], outs=[(ff, BF16)] * 2)
    du2 = _mm("d_u2_up", dup, w_up_f, nt=True, add=_mm("d_u2_gate", dgt, w_gate_f, nt=True))
    u2_t = u2.T
    g_w_gate = _mm("g_w_gate", u2_t, dgt)
    g_w_up = _mm("g_w_up", u2_t, dup)

    def norm2_bwd_fn(i, nt, tiles, prev8, next8, cv):
        h1t, du, dres, yo = tiles
        w, sc, g1 = cv
        r = lax.rsqrt(jnp.mean(h1t * h1t, axis=1, keepdims=True) + EPS)
        nrm = h1t * r
        dn = du * w * (1.0 + sc)
        dh = r * (dn - nrm * jnp.mean(dn * nrm, axis=1, keepdims=True)) + dres
        return [dh, dh * g1], [_colsum(du), _colsum(du * nrm * w), _colsum(du * nrm * (1.0 + sc)), _colsum(dh * yo)]

    (dh1, dyo), (dshift2, dscale2, g_norm2, dgate1) = _ew(
        "norm2_bwd", norm2_bwd_fn, tr=256, ins=[h1, du2, dy, y_o], consts=[norm2_w, scale2, gate1],
        outs=[(d, F32), (d, BF16)], accs=[(1, d)] * 4)

    dmerged = _mm("d_merged", dyo, w_out_f, nt=True)
    g_w_out = _mm("g_w_out", merged.T, dyo)

    def merge_bwd_fn(i, nt, tiles, prev8, next8, cv):
        dm, ya, yb, ga, gb = tiles
        sa, sb = _sigmoid(ga), _sigmoid(gb)
        return [dm * sa, dm * sb, dm * ya * sa * (1.0 - sa), dm * yb * sb * (1.0 - sb)], []

    (dya, dyb, dga, dgb_gate), _ = _ew(
        "merge_bwd", merge_bwd_fn, tr=256, ins=[dmerged, y_a, y_b, Col(proj_zg, d, 1), Col(proj_zg, d, 2)],
        outs=[(d, BF16)] * 4)
    do_a = _mm("d_o_a", dya, p_a_f, nt=True, out_dtype=BF16)
    g_p_a = _mm("g_p_a", o_a.T, dya)
    do_b = _mm("d_o_b", dyb, p_b_f, nt=True)
    g_p_b = _mm("g_p_b", o_b.T, dyb)

    def gated_norm_bwd_fn(i, nt, tiles, prev8, next8, cv):
        dob, o, z = tiles
        sz = _silu(z)

        def head(oh, dh):
            return _rms_bwd(oh, cv[0], dh)

        dxo, dwn = _per_head(head, o, dob * sz)
        nrm_w = _per_head(lambda h: _rms_fwd(h, cv[0]), o)
        return [dxo, dob * nrm_w * _dsilu(z)], [_colsum(_head_sum(dwn))]

    (do_raw, dz_b), (g_o_norm,) = _ew(
        "gated_norm_bwd", gated_norm_bwd_fn, tr=256, ins=[do_b, o_raw, Col(proj_zg, d, 0)], consts=[o_norm_w],
        outs=[(d, F32), (d, BF16)], accs=[(1, HEAD_DIM)])
    dq_b, dk_b, dv_b, dgb_grp = _gdn_bwd(q_b, k_b, v_b, g_col, g_row, b_col, b_row, states, tinvs, do_raw)
    grp = _gdn_group(nh)
    dbeta = dgb_grp[:, :, :grp].transpose(1, 0, 2).reshape(s_len, nh)
    dg = dgb_grp[:, :, grp:2 * grp].transpose(1, 0, 2).reshape(s_len, nh)
    dgbeta = _pad_cols(jnp.concatenate([dbeta, dg], axis=1), LANES)

    def gbeta_bwd_fn(i, nt, tiles, prev8, next8, cv):
        al, dtb, mb, ma = cv
        ba, dgb = tiles
        beta = _sigmoid(ba)
        arg = ba + dtb
        da = dgb * (-jnp.exp(al)) * _sigmoid(arg)
        g = -jnp.exp(al) * _softplus(arg)
        dba = jnp.where(mb > 0.5, dgb * beta * (1.0 - beta), jnp.where(ma > 0.5, da, 0.0))
        return [dba], [_colsum(jnp.where(ma > 0.5, dgb * g, 0.0)), _colsum(jnp.where(ma > 0.5, da, 0.0))]

    (dba,), (g_alog128, g_dtb128) = _ew(
        "gbeta_bwd", gbeta_bwd_fn, tr=1024, ins=[proj_ba, dgbeta], consts=[alog128, dtb128, is_b_f, is_a_f],
        outs=[(LANES, BF16)], accs=[(1, LANES)] * 2)
    dxq, g_conv_q = _conv_bwd(proj_b, conv_f, 0, dq_b, norm=True, mult=qscale)
    dxk, g_conv_k = _conv_bwd(proj_b, conv_f, 1, dk_b, norm=True, mult=1.0)
    dxv, g_conv_v = _conv_bwd(proj_b, conv_f, 2, dv_b, norm=False, mult=1.0)
    g_conv = jnp.concatenate([g_conv_q, g_conv_k, g_conv_v], axis=1)

    dqn, dkn, dvb = _sb_bwd(qn, kn, vb, do_a, ltot)

    def qknorm_bwd_fn(i, nt, tiles, prev8, next8, cv):
        qa, ka, dq, dk, dv = tiles
        dxq_, dwq = _per_head(lambda h, g: _rms_bwd(h, cv[0], g), qa, dq)
        dxk_, dwk = _per_head(lambda h, g: _rms_bwd(h, cv[1], g), ka, dk)
        return [dxq_, dxk_, dv], [_colsum(_head_sum(dwq)), _colsum(_head_sum(dwk))]

    (dqa, dka, dva), (g_q_norm, g_k_norm) = _ew(
        "qknorm_bwd", qknorm_bwd_fn, tr=256, ins=[Col(proj_a, d, 0), Col(proj_a, d, 1), dqn, dkn, dvb],
        consts=[q_norm_w, k_norm_w], outs=[(d, BF16)] * 3, accs=[(1, HEAD_DIM)] * 2)

    u1_t = u1.T
    d_a = jnp.concatenate([dqa, dka, dva], axis=1)
    d_b = jnp.concatenate([dxq, dxk, dxv], axis=1)
    d_zg = jnp.concatenate([dz_b, dga, dgb_gate], axis=1)
    du1 = _mm("d_u1_a", d_a, wa, nt=True)
    du1 = _mm("d_u1_b", d_b, wb, nt=True, add=du1)
    du1 = _mm("d_u1_zg", d_zg, wzg, nt=True, add=du1)
    du1 = _mm("d_u1_ba", dba, wba, nt=True, add=du1)
    g_wa = _mm("g_w_in_a", u1_t, d_a)
    g_wb = _mm("g_w_in_b", u1_t, d_b)
    g_wzg = _mm("g_w_in_zg", u1_t, d_zg)
    g_wba = _mm("g_w_in_ba", u1_t, dba)
    g_w_in_f = jnp.concatenate([g_wa, g_wb, g_wzg[:, :d], g_wba[:, :2 * nh], g_wzg[:, d:]], axis=1)

    def norm1_bwd_fn(i, nt, tiles, prev8, next8, cv):
        xt, du, dres = tiles
        w, sc = cv
        r = lax.rsqrt(jnp.mean(xt * xt, axis=1, keepdims=True) + EPS)
        nrm = xt * r
        dn = du * w * (1.0 + sc)
        dxt = r * (dn - nrm * jnp.mean(dn * nrm, axis=1, keepdims=True)) + dres
        return [dxt], [_colsum(du), _colsum(du * nrm * w), _colsum(du * nrm * (1.0 + sc))]

    (grad_x,), (dshift1, dscale1, g_norm1) = _ew(
        "norm1_bwd", norm1_bwd_fn, tr=256, ins=[x2, du1, dh1], consts=[norm1_w, scale1],
        outs=[(d, F32)], accs=[(1, d)] * 3)

    dmod_me = jnp.concatenate([dshift1, dscale1, dgate1, dshift2, dscale2, dgate2], axis=1)
    small = jnp.concatenate(
        [dmod_me, g_norm1, g_norm2, g_q_norm, g_k_norm, g_o_norm, g_alog128[:, nh:2 * nh], g_dtb128[:, nh:2 * nh],
         g_conv.reshape(1, -1)], axis=1)
    n_small = small.shape[1]
    pad_to = -(-n_small // (SUBLANES * LANES)) * (SUBLANES * LANES)
    small_all = _allgather8("ag_small", _pad_cols(small, pad_to).reshape(SUBLANES, pad_to // SUBLANES))
    small_all = small_all.reshape(8, pad_to)

    def sum8_fn(i, nt, tiles, prev8, next8, cv):
        return [], [_colsum(tiles[0])]

    _, (small_sum,) = _ew("sum_small", sum8_fn, tr=8, ins=[small_all], accs=[(1, pad_to)])
    offs = [0]
    for width in (6 * d, d, d, HEAD_DIM, HEAD_DIM, HEAD_DIM, nh, nh, GDN_CONV * 3 * d):
        offs.append(offs[-1] + width)
    pieces = [small_sum[:, offs[j]:offs[j + 1]] for j in range(9)]
    (gs_b_mod, gs_norm1, gs_norm2, gs_q_norm, gs_k_norm, gs_o_norm, gs_a_log, gs_dt_bias, gs_conv) = pieces
    conv_cols = 3 * d // 4
    gs_conv_mine = lax.dynamic_slice_in_dim(gs_conv.reshape(GDN_CONV, 3 * d), chip * conv_cols, conv_cols, axis=1)

    dmod_all = lax.dynamic_slice_in_dim(small_all[:, :6 * d], chip * mod_w, mod_w, axis=1)

    def wmod_grad_body(ct_ref, dm_ref, o_ref):
        o_ref[...] = _dot(ct_ref[...], dm_ref[...], hi=HIGHEST)

    g_w_mod = _pcall(
        wmod_grad_body, name="g_w_mod", grid=(mod_w // tn_mod,),
        out_shape=jax.ShapeDtypeStruct((d, mod_w), F32),
        in_specs=[pl.BlockSpec((d, 8), lambda j: (0, 0)), pl.BlockSpec((8, tn_mod), lambda j: (0, j))],
        out_specs=pl.BlockSpec((d, tn_mod), lambda j: (0, j)),
        compiler_params=_params(("arbitrary",)),
    )(c_act.T, dmod_all)

    partials = [
        g_w_in_f.reshape(d, 4, -1).transpose(1, 0, 2),
        g_p_a.reshape(4, d // 4, d), g_p_b.reshape(4, d // 4, d), g_w_out.reshape(4, d // 4, d),
        g_w_gate.reshape(d, 4, ff // 4).transpose(1, 0, 2), g_w_up.reshape(d, 4, ff // 4).transpose(1, 0, 2),
        g_w_down.reshape(4, ff // 4, d),
    ]
    theirs_h = _sibling_send("swap_halves", partials, axis=1)
    chip_halves = []
    for t, part in enumerate(partials):
        def pair_fn(i, nt, tiles, prev8, next8, cv):
            return [tiles[0] + tiles[1]], []

        hr, width = part.shape[1] // 2, part.shape[2]
        mine = lax.dynamic_slice_in_dim(part, mc * hr, hr, axis=1)
        (ch,), _ = _ew(f"pair_sum{t}", pair_fn, tr=64,
                       ins=[mine.reshape(-1, width), theirs_h[t].reshape(-1, width)], outs=[(width, BF16)])
        chip_halves.append(ch.reshape(mine.shape))
    landed = [_fill_slot(land, ch, chip) for land, ch in zip(_scatter4("rs_grads", chip_halves), chip_halves)]
    g_mine = []
    for t, land in enumerate(landed):
        def sum4_fn(i, nt, tiles, prev8, next8, cv):
            f = [tl.astype(F32) for tl in tiles]
            return [(f[0] + f[1]) + (f[2] + f[3])], []

        (gh,), _ = _ew(f"chip_sum{t}", sum4_fn, tr=64, ins=[Col(land, lead=s) for s in range(4)],
                       outs=[(land.shape[-1], F32)])
        g_mine.append(gh)
    g_theirs = _sibling_send("join_grads", g_mine)
    g_full = [jnp.concatenate([jnp.where(mc == 0, a, b), jnp.where(mc == 0, b, a)], axis=0)
              for a, b in zip(g_mine, g_theirs)]

    big = {}
    names = ["w_in", "p_a", "p_b", "w_out", "w_gate", "w_up", "w_down"]
    big_w = [w_in, p_a, p_b, w_out, w_gate, w_up, w_down]
    big_m = [m_w_in, m_p_a, m_p_b, m_w_out, m_w_gate, m_w_up, m_w_down]
    big_v = [v_w_in, v_p_a, v_p_b, v_w_out, v_w_gate, v_w_up, v_w_down]
    for t, nm in enumerate(names):
        big[nm] = _adamw(f"adamw_{nm}", big_w[t], big_m[t], big_v[t], [g_full[t]])
    big["w_mod"] = _adamw("adamw_w_mod", w_mod, m_w_mod, v_w_mod, [g_w_mod])
    big["conv_w"] = _adamw("adamw_conv_w", conv_w, m_conv_w, v_conv_w, [gs_conv_mine], tr=8)
    small_names = ["b_mod", "norm1_w", "norm2_w", "q_norm_w", "k_norm_w", "o_norm_w", "a_log", "dt_bias"]
    small_w = [b_mod, norm1_w, norm2_w, q_norm_w, k_norm_w, o_norm_w, a_log, dt_bias]
    small_m = [m_b_mod, m_norm1_w, m_norm2_w, m_q_norm_w, m_k_norm_w, m_o_norm_w, m_a_log, m_dt_bias]
    small_v = [v_b_mod, v_norm1_w, v_norm2_w, v_q_norm_w, v_k_norm_w, v_o_norm_w, v_a_log, v_dt_bias]
    small_g = [gs_b_mod, gs_norm1, gs_norm2, gs_q_norm, gs_k_norm, gs_o_norm, gs_a_log, gs_dt_bias]
    rep_w = jnp.concatenate(small_w, axis=1)
    rep_m = jnp.concatenate(small_m, axis=1)
    rep_v = jnp.concatenate(small_v, axis=1)
    rep_g = jnp.concatenate(small_g, axis=1)
    rep = _adamw("adamw_small", rep_w, rep_m, rep_v, [rep_g], tr=1)
    roffs = [0]
    for a in small_w:
        roffs.append(roffs[-1] + a.shape[1])
    for j, nm in enumerate(small_names):
        big[nm] = tuple(r[:, roffs[j]:roffs[j + 1]] for r in rep)

    order = ["w_mod", "b_mod", "norm1_w", "w_in", "q_norm_w", "k_norm_w", "conv_w", "a_log", "dt_bias", "o_norm_w",
             "p_a", "p_b", "w_out", "norm2_w", "w_gate", "w_up", "w_down"]
    grads = [big[nm][0] for nm in order]
    deltas = [big[nm][1] for nm in order]
    new_m = [big[nm][2] for nm in order]
    new_v = [big[nm][3] for nm in order]
    return (loss, grad_x[None], *grads, *deltas, *new_m, *new_v)
```

```python
import jax
import jax.numpy as jnp
from jax import lax
from jax.experimental import pallas as pl
from jax.experimental.pallas import tpu as pltpu

F32 = jnp.float32
BF16 = jnp.bfloat16
HIGHEST = lax.Precision.HIGHEST
HIGH = lax.Precision.HIGH
MESH = pl.DeviceIdType.MESH

HEAD_DIM = 128
GDN_CHUNK = 64
GDN_CONV = 4
EPS = 1e-6
LANES = 128
SUBLANES = 8
VMEM_LIMIT = 56 * 1024 * 1024
MM_VMEM_BUDGET = 40 * 1024 * 1024

ADAM_LR = 0.001
ADAM_B1 = 0.9
ADAM_B2 = 0.999
ADAM_EPS = 1e-08
ADAM_WD = 0.01
ADAM_STEP = 10


def _pcall(body, **kw):
    return pl.pallas_call(body, **kw)


def _params(sem=None):
    if sem is None:
        return pltpu.CompilerParams(vmem_limit_bytes=VMEM_LIMIT)
    return pltpu.CompilerParams(dimension_semantics=sem, vmem_limit_bytes=VMEM_LIMIT)


def _pick(dim, target):
    if dim <= target:
        return dim
    best = None
    for t in range(LANES, target + 1, LANES):
        if dim % t == 0:
            best = t
    assert best is not None, (dim, target)
    return best


def _rows_tile(rows, target):
    t = min(rows, target)
    while rows % t:
        t //= 2
    assert t >= SUBLANES or t == rows, (rows, target)
    return t


def _dot(a, b, hi=None):
    return jnp.dot(a, b, preferred_element_type=F32, precision=hi)


def _dot_nt(a, b, hi=None):
    return lax.dot_general(a, b, (((1,), (1,)), ((), ())), preferred_element_type=F32, precision=hi)


def _dot_tn(a, b, hi=None):
    return lax.dot_general(a, b, (((0,), (0,)), ((), ())), preferred_element_type=F32, precision=hi)


def _sigmoid(x):
    return 1.0 / (1.0 + jnp.exp(-x))


def _softplus(x):
    return jnp.maximum(x, 0.0) + jnp.log(1.0 + jnp.exp(-jnp.abs(x)))


_HBM = pl.BlockSpec(memory_space=pltpu.HBM)


def _my_pos():
    return lax.axis_index("x"), lax.axis_index("y"), lax.axis_index("c")


def _allgather8(name, v):
    def body(v_ref, o_ref, ssem, rsem, lsem):
        x, y, c = _my_pos()
        me = 4 * x + 2 * y + c
        loc = pltpu.make_async_copy(v_ref, o_ref.at[me], lsem)
        loc.start()
        sends, recvs = [], []
        for k in range(1, 8):
            px, py, pc = (x + (k >> 2)) % 2, (y + ((k >> 1) & 1)) % 2, (c + (k & 1)) % 2
            cp = pltpu.make_async_remote_copy(
                src_ref=v_ref, dst_ref=o_ref.at[me], send_sem=ssem.at[k - 1], recv_sem=rsem.at[k - 1],
                device_id=(px, py, pc), device_id_type=MESH)
            cp.start()
            sends.append(cp)
            recvs.append(pltpu.make_async_remote_copy(
                src_ref=v_ref, dst_ref=o_ref.at[4 * px + 2 * py + pc], send_sem=ssem.at[k - 1],
                recv_sem=rsem.at[k - 1], device_id=(px, py, pc), device_id_type=MESH))
        for rc in recvs:
            rc.wait_recv()
        for cp in sends:
            cp.wait_send()
        loc.wait()

    return _pcall(
        body, name=name, out_shape=jax.ShapeDtypeStruct((8,) + v.shape, v.dtype),
        in_specs=[_HBM], out_specs=_HBM,
        scratch_shapes=[pltpu.SemaphoreType.DMA((7,)), pltpu.SemaphoreType.DMA((7,)), pltpu.SemaphoreType.DMA],
    )(v)


def _plane_peers(x, y):
    return [((x + (k >> 1)) % 2, (y + (k & 1)) % 2) for k in range(1, 4)]


def _gather4(name, shards, n_split):
    n = len(shards)

    def body(*refs):
        ins, outs = refs[:n], refs[n:2 * n]
        ssem, rsem, fsem, gsem = refs[2 * n:]
        x, y, c = _my_pos()
        me = 2 * x + y
        peers = _plane_peers(x, y)
        sends, recvs, fwds, fwd_recvs = [], [], [], []
        for t in range(n):
            split = t < n_split
            hr = ins[t].shape[0] // 2
            for k, (px, py) in enumerate(peers):
                peer = 2 * px + py
                sem = 3 * t + k
                if split:
                    mine = pl.ds(pl.multiple_of(c * hr, 16), hr)
                    other = pl.ds(pl.multiple_of((1 - c) * hr, 16), hr)
                    src, dst, got = ins[t].at[mine], outs[t].at[me, mine], outs[t].at[peer, mine]
                else:
                    src, dst, got = ins[t], outs[t].at[me], outs[t].at[peer]
                cp = pltpu.make_async_remote_copy(
                    src_ref=src, dst_ref=dst, send_sem=ssem.at[sem], recv_sem=rsem.at[sem],
                    device_id=(px, py, c), device_id_type=MESH)
                cp.start()
                sends.append(cp)
                recvs.append(pltpu.make_async_remote_copy(
                    src_ref=src, dst_ref=got, send_sem=ssem.at[sem], recv_sem=rsem.at[sem],
                    device_id=(px, py, c), device_id_type=MESH))
                if split:
                    fwds.append(pltpu.make_async_remote_copy(
                        src_ref=got, dst_ref=got, send_sem=fsem.at[sem], recv_sem=gsem.at[sem],
                        device_id=(x, y, 1 - c), device_id_type=MESH))
                    fwd_recvs.append(pltpu.make_async_remote_copy(
                        src_ref=got, dst_ref=outs[t].at[peer, other], send_sem=fsem.at[sem], recv_sem=gsem.at[sem],
                        device_id=(x, y, 1 - c), device_id_type=MESH))
                else:
                    fwds.append(None)
        for rc, fw in zip(recvs, fwds):
            rc.wait_recv()
            if fw is not None:
                fw.start()
        for fr in fwd_recvs:
            fr.wait_recv()
        for cp in sends + [fw for fw in fwds if fw is not None]:
            cp.wait_send()

    return _pcall(
        body, name=name,
        out_shape=[jax.ShapeDtypeStruct((4,) + s.shape, s.dtype) for s in shards],
        in_specs=[_HBM] * n, out_specs=[_HBM] * n,
        scratch_shapes=[pltpu.SemaphoreType.DMA((3 * n,))] * 4,
    )(*shards)


def _fill_slot(slots, own, slot):
    mask = (jnp.arange(4) == slot).reshape((4,) + (1,) * (slots.ndim - 1))
    return jnp.where(mask, own if own.ndim == slots.ndim else own[None], slots)


def _scatter4(name, partials):
    n = len(partials)

    def body(*refs):
        ins, outs = refs[:n], refs[n:2 * n]
        ssem, rsem = refs[2 * n:]
        x, y, c = _my_pos()
        me = 2 * x + y
        peers = _plane_peers(x, y)
        sends, recvs = [], []
        for t in range(n):
            for k, (px, py) in enumerate(peers):
                peer = 2 * px + py
                cp = pltpu.make_async_remote_copy(
                    src_ref=ins[t].at[peer], dst_ref=outs[t].at[me], send_sem=ssem.at[3 * t + k],
                    recv_sem=rsem.at[3 * t + k], device_id=(px, py, c), device_id_type=MESH)
                cp.start()
                sends.append(cp)
                recvs.append(pltpu.make_async_remote_copy(
                    src_ref=ins[t].at[peer], dst_ref=outs[t].at[peer], send_sem=ssem.at[3 * t + k],
                    recv_sem=rsem.at[3 * t + k], device_id=(px, py, c), device_id_type=MESH))
        for rc in recvs:
            rc.wait_recv()
        for cp in sends:
            cp.wait_send()

    return _pcall(
        body, name=name,
        out_shape=[jax.ShapeDtypeStruct(p.shape, p.dtype) for p in partials],
        in_specs=[_HBM] * n, out_specs=[_HBM] * n,
        scratch_shapes=[pltpu.SemaphoreType.DMA((3 * n,)), pltpu.SemaphoreType.DMA((3 * n,))],
    )(*partials)


def _sibling_send(name, arrays, axes=None):
    n = len(arrays)
    axes = [None] * n if axes is None else axes

    def body(*refs):
        ins, outs = refs[:n], refs[n:2 * n]
        ssem, rsem = refs[2 * n:]
        x, y, c = _my_pos()
        cps = []
        for t in range(n):
            src = ins[t]
            if axes[t] is not None:
                hr = ins[t].shape[axes[t]] // 2
                give = pl.ds(pl.multiple_of((1 - c) * hr, SUBLANES), hr)
                src = ins[t].at[give] if axes[t] == 0 else ins[t].at[:, give]
            cp = pltpu.make_async_remote_copy(
                src_ref=src, dst_ref=outs[t], send_sem=ssem.at[t], recv_sem=rsem.at[t],
                device_id=(x, y, 1 - c), device_id_type=MESH)
            cp.start()
            cps.append(cp)
        for cp in cps:
            cp.wait_recv()
        for cp in cps:
            cp.wait_send()

    def half(a, axis):
        shape = list(a.shape)
        if axis is not None:
            shape[axis] //= 2
        return jax.ShapeDtypeStruct(tuple(shape), a.dtype)

    return _pcall(
        body, name=name, out_shape=[half(a, ax) for a, ax in zip(arrays, axes)], in_specs=[_HBM] * n,
        out_specs=[_HBM] * n,
        scratch_shapes=[pltpu.SemaphoreType.DMA((n,)), pltpu.SemaphoreType.DMA((n,))],
    )(*arrays)


def _mm(name, a, b, *, nt=False, out_dtype=F32, add=None, tm=1024, tn=1024, tk=4096):
    m, k = a.shape
    n = b.shape[0] if nt else b.shape[1]
    assert (b.shape[1] if nt else b.shape[0]) == k
    has_add = add is not None
    tm, tn = _pick(m, tm), _pick(n, tn)
    out_bytes = jnp.dtype(out_dtype).itemsize

    def vmem_bytes(tk_):
        steps = k // tk_
        return (4 * (tm + tn) * tk_ + 2 * tm * tn * out_bytes + (8 * tm * tn if has_add else 0)
                + (4 * tm * tn if steps > 1 else 0))

    tk = _pick(k, tk)
    while vmem_bytes(tk) > MM_VMEM_BUDGET and tk > 512:
        tk = _pick(k, tk - LANES)
    nk = k // tk

    def body(*refs):
        a_ref, b_ref = refs[0], refs[1]
        c_ref = refs[2] if has_add else None
        o_ref = refs[2 + has_add]
        p = (_dot_nt if nt else _dot)(a_ref[...], b_ref[...])
        if nk == 1:
            o_ref[...] = (p + c_ref[...] if has_add else p).astype(o_ref.dtype)
            return
        acc = refs[3 + has_add]
        kk = pl.program_id(2)

        @pl.when(kk == 0)
        def _():
            acc[...] = p

        @pl.when(jnp.logical_and(kk > 0, kk < nk - 1))
        def _():
            acc[...] += p

        @pl.when(kk == nk - 1)
        def _():
            r = acc[...] + p
            if has_add:
                r = r + c_ref[...]
            o_ref[...] = r.astype(o_ref.dtype)

    a_spec = pl.BlockSpec((tm, tk), lambda j, i, kk: (i, kk))
    if nt:
        b_spec = pl.BlockSpec((tn, tk), lambda j, i, kk: (j, kk))
    else:
        b_spec = pl.BlockSpec((tk, tn), lambda j, i, kk: (kk, j))
    o_spec = pl.BlockSpec((tm, tn), lambda j, i, kk: (i, j))
    in_specs = [a_spec, b_spec] + ([o_spec] if has_add else [])
    args = (a, b) + ((add,) if has_add else ())
    return _pcall(
        body, name=name, grid=(n // tn, m // tm, nk),
        out_shape=jax.ShapeDtypeStruct((m, n), out_dtype),
        in_specs=in_specs, out_specs=o_spec,
        scratch_shapes=[pltpu.VMEM((tm, tn), F32)] if nk > 1 else [],
        compiler_params=_params(("parallel", "parallel", "arbitrary")),
    )(*args)


class Col:
    def __init__(self, arr, w=None, cb=0, lead=None):
        self.arr, self.cb, self.lead = arr, cb, lead
        self.w = arr.shape[-1] if w is None else w
        self.rows = arr.shape[-2]


def _ew(name, fn, *, tr, ins, consts=(), outs=(), accs=(), halo_prev=(), halo_next=()):
    ins = [c if isinstance(c, Col) else Col(c) for c in ins]
    halo_prev = [c if isinstance(c, Col) else Col(c) for c in halo_prev]
    halo_next = [c if isinstance(c, Col) else Col(c) for c in halo_next]
    rows = ins[0].rows
    tr = _rows_tile(rows, tr)
    nt = rows // tr
    n_in, n_hp, n_hn, n_c, n_o, n_a = len(ins), len(halo_prev), len(halo_next), len(consts), len(outs), len(accs)
    groups = tr // SUBLANES

    def spec(col, kind):
        if kind == "cur":
            shape, idx = (tr, col.w), (lambda i, cb=col.cb: (i, cb))
        elif kind == "prev":
            shape, idx = (SUBLANES, col.w), (lambda i, cb=col.cb: (jnp.maximum(i * groups - 1, 0), cb))
        else:
            shape = (SUBLANES, col.w)
            idx = (lambda i, cb=col.cb: (jnp.minimum((i + 1) * groups, rows // SUBLANES - 1), cb))
        if col.lead is None:
            return pl.BlockSpec(shape, idx)
        return pl.BlockSpec((None,) + shape, lambda i, idx=idx, lead=col.lead: (lead,) + idx(i))

    def body(*refs):
        i = pl.program_id(0)
        p = 0
        tiles = [r[...] for r in refs[p:p + n_in]]; p += n_in
        prev8 = [r[...] for r in refs[p:p + n_hp]]; p += n_hp
        next8 = [r[...] for r in refs[p:p + n_hn]]; p += n_hn
        cvals = [r[...] for r in refs[p:p + n_c]]; p += n_c
        out_refs = refs[p:p + n_o]; p += n_o
        acc_refs = refs[p:p + n_a]
        out_v, acc_v = fn(i, nt, tiles, prev8, next8, cvals)
        for r, v in zip(out_refs, out_v):
            r[...] = v.astype(r.dtype)
        if n_a:
            @pl.when(i == 0)
            def _():
                for r, v in zip(acc_refs, acc_v):
                    r[...] = v

            @pl.when(i > 0)
            def _():
                for r, v in zip(acc_refs, acc_v):
                    r[...] += v

    in_specs = ([spec(c, "cur") for c in ins] + [spec(c, "prev") for c in halo_prev]
                + [spec(c, "next") for c in halo_next]
                + [pl.BlockSpec(c.shape, lambda i, nd=c.ndim: (0,) * nd) for c in consts])
    out_specs = ([pl.BlockSpec((tr, w), lambda i: (i, 0)) for w, _ in outs]
                 + [pl.BlockSpec(s, lambda i: (0, 0)) for s in accs])
    out_shape = ([jax.ShapeDtypeStruct((rows, w), dt) for w, dt in outs]
                 + [jax.ShapeDtypeStruct(s, F32) for s in accs])
    args = [c.arr for c in ins] + [c.arr for c in halo_prev] + [c.arr for c in halo_next] + list(consts)
    res = _pcall(body, name=name, grid=(nt,), out_shape=out_shape, in_specs=in_specs, out_specs=out_specs,
                 compiler_params=_params(("arbitrary",)))(*args)
    return res[:n_o], res[n_o:]


def _colsum(v):
    return jnp.sum(v, axis=0, keepdims=True)


def _heads_of(w):
    return w // HEAD_DIM


def _per_head(fn, *arrays):
    nh = _heads_of(arrays[0].shape[1])
    res = [fn(*[a[:, h * HEAD_DIM:(h + 1) * HEAD_DIM] for a in arrays]) for h in range(nh)]
    if isinstance(res[0], tuple):
        return tuple(jnp.concatenate([r[j] for r in res], axis=1) for j in range(len(res[0])))
    return jnp.concatenate(res, axis=1)


def _head_sum(v):
    nh = _heads_of(v.shape[1])
    out = v[:, :HEAD_DIM]
    for h in range(1, nh):
        out = out + v[:, h * HEAD_DIM:(h + 1) * HEAD_DIM]
    return out


def _rms_fwd(x, w):
    r = lax.rsqrt(jnp.mean(x * x, axis=1, keepdims=True) + EPS)
    return x * r * w


def _rms_bwd(x, w, dy):
    r = lax.rsqrt(jnp.mean(x * x, axis=1, keepdims=True) + EPS)
    xh = x * r
    dxh = dy * w
    dx = r * (dxh - xh * jnp.mean(dxh * xh, axis=1, keepdims=True))
    return dx, dy * xh


def _silu(x):
    return x * _sigmoid(x)


def _dsilu(x):
    s = _sigmoid(x)
    return s * (1.0 + x * (1.0 - s))


SB_BQ = 512
SB_CUTOFF = 112.0
SB_PAIR = 2
SB_BK = 256


def _softplus_pos(z):
    return jnp.maximum(z, 0.0) + jnp.log(1.0 + jnp.exp(-jnp.abs(z)))


def _split_dot(v, tri):
    top = lax.bitcast_convert_type(lax.bitcast_convert_type(v, jnp.int32) & jnp.int32(-65536), F32)
    return _dot(top.astype(BF16), tri) + _dot((v - top).astype(BF16), tri)


def _sb_fwd(qn, kn, vb, *, bq=SB_BQ, bk=SB_BK):
    s_len, hd = qn.shape
    nh = hd // HEAD_DIM
    bk = min(bk, s_len)
    bq = min(bq, s_len)
    ndiag = bq // bk
    scale = HEAD_DIM ** -0.5

    def body(q_ref, k_ref, v_ref, o_ref):
        i = pl.program_id(1)
        krow = lax.broadcasted_iota(jnp.int32, (bk, bk), 0)
        kcol = lax.broadcasted_iota(jnp.int32, (bk, bk), 1)
        later = (krow > kcol).astype(BF16)
        row = lax.broadcasted_iota(jnp.int32, (bq, bk), 0)
        col = lax.broadcasted_iota(jnp.int32, (bq, bk), 1)
        q = q_ref[...]

        def tiles(js, carry, diags):
            run, acc = carry
            ks, vs, zs = [], [], []
            for j in js:
                off = pl.multiple_of(j * bk, bk)
                ks.append(k_ref[pl.ds(off, bk), :])
                vs.append(v_ref[pl.ds(off, bk), :])
                zs.append(_dot_nt(q, ks[-1]) * scale)
            sps, cums, masks = [], [], []
            for z, diag in zip(zs, diags):
                sp = _softplus_pos(z)
                causal = None
                if diag is not None:
                    causal = col + diag * bk < row
                    sp = jnp.where(causal, sp, 0.0)
                sps.append(sp)
                masks.append(causal)
                cums.append(_split_dot(sp, later))
            for z, sp, cum, causal, v in zip(zs, sps, cums, masks, vs):
                w = jnp.exp((z - sp) - (cum + run))
                if causal is not None:
                    w = jnp.where(causal, w, 0.0)
                acc = acc + _dot(w.astype(BF16), v)
                run = run + cum[:, 0:1] + sp[:, 0:1]
            return run, acc

        carry = (jnp.zeros((bq, 1), F32), jnp.zeros((bq, HEAD_DIM), F32))
        for dg in reversed(range(0, ndiag, SB_PAIR)):
            dgs = list(reversed(range(dg, dg + SB_PAIR)))
            carry = tiles([i * ndiag + g for g in dgs], carry, dgs)
        n_pairs = i * ndiag // SB_PAIR

        def more(st):
            return jnp.logical_and(st[0] < n_pairs, jnp.min(st[1]) < SB_CUTOFF)

        def step(st):
            t, run, acc = st
            run, acc = tiles([i * ndiag - 1 - SB_PAIR * t - u for u in range(SB_PAIR)], (run, acc), [None] * SB_PAIR)
            return t + 1, run, acc

        _, _, acc = lax.while_loop(more, step, (jnp.int32(0),) + carry)
        o_ref[...] = acc.astype(o_ref.dtype)

    qspec = pl.BlockSpec((bq, HEAD_DIM), lambda h, i: (i, h))
    kspec = pl.BlockSpec((s_len, HEAD_DIM), lambda h, i: (0, h))
    return _pcall(
        body, name="sb_fwd", grid=(nh, s_len // bq),
        out_shape=jax.ShapeDtypeStruct((s_len, hd), BF16),
        in_specs=[qspec, kspec, kspec], out_specs=qspec,
        compiler_params=_params(("parallel", "arbitrary")),
    )(qn, kn, vb)


def _sb_bwd(qn, kn, vb, do, *, bq=SB_BQ, bk=SB_BK):
    s_len, hd = qn.shape
    nh = hd // HEAD_DIM
    bk = min(bk, s_len)
    bq = min(bq, s_len)
    ndiag = bq // bk
    scale = HEAD_DIM ** -0.5

    def body(q_ref, k_ref, v_ref, do_ref, dq_ref, dk_ref, dv_ref):
        i = pl.program_id(1)

        @pl.when(i == 0)
        def _():
            dk_ref[...] = jnp.zeros_like(dk_ref)
            dv_ref[...] = jnp.zeros_like(dv_ref)

        krow = lax.broadcasted_iota(jnp.int32, (bk, bk), 0)
        kcol = lax.broadcasted_iota(jnp.int32, (bk, bk), 1)
        upto = (krow <= kcol).astype(BF16)
        before = (krow < kcol).astype(BF16)
        row = lax.broadcasted_iota(jnp.int32, (bq, bk), 0)
        col = lax.broadcasted_iota(jnp.int32, (bq, bk), 1)
        q = q_ref[...]
        do_t = do_ref[...]
        ones = jnp.ones((bk, LANES), BF16)
        n_pairs = i * ndiag // SB_PAIR

        def row_sums(js, diags):
            zs = [_dot_nt(q, k_ref[pl.ds(pl.multiple_of(j * bk, bk), bk), :]) * scale for j in js]
            tot = None
            for z, diag in zip(zs, diags):
                sp = _softplus_pos(z)
                if diag is not None:
                    sp = jnp.where(col + diag * bk < row, sp, 0.0)
                part = _split_dot(sp, ones)[:, 0:1]
                tot = part if tot is None else tot + part
            return tot

        def more(st):
            return jnp.logical_and(st[0] < n_pairs, jnp.min(st[1]) < SB_CUTOFF)

        def widen(st):
            t, run = st
            js = [i * ndiag - 1 - SB_PAIR * t - u for u in range(SB_PAIR)]
            return t + 1, run + row_sums(js, [None] * SB_PAIR)

        diag_all = list(range(ndiag))
        used, lt = lax.while_loop(more, widen, (jnp.int32(0), row_sums([i * ndiag + g for g in diag_all], diag_all)))

        def tiles(js, carry, diags):
            pre, ecar, dq = carry
            offs, ks, zs, dws = [], [], [], []
            for j in js:
                off = pl.multiple_of(j * bk, bk)
                offs.append(off)
                ks.append(k_ref[pl.ds(off, bk), :])
                zs.append(_dot_nt(q, ks[-1]) * scale)
                dws.append(_dot_nt(do_t, v_ref[pl.ds(off, bk), :]))
            sps, cums, masks = [], [], []
            for z, diag in zip(zs, diags):
                sp = _softplus_pos(z)
                causal = None
                if diag is not None:
                    causal = col + diag * bk < row
                    sp = jnp.where(causal, sp, 0.0)
                sps.append(sp)
                masks.append(causal)
                cums.append(_split_dot(sp, upto))
            es, ebs, exs, sigs = [], [], [], []
            for off, z, sp, cum, dw, causal in zip(offs, zs, sps, cums, dws, masks):
                lb = z - sp
                w = jnp.exp(lb - (lt - (pre + cum)))
                if causal is not None:
                    w = jnp.where(causal, w, 0.0)
                dv_ref[pl.ds(off, bk), :] += _dot_tn(w.astype(BF16), do_t)
                e = dw * w
                eb = e.astype(BF16)
                es.append(e)
                ebs.append(eb)
                exs.append(_dot(eb, before))
                sigs.append(jnp.exp(lb))
                pre = pre + cum[:, bk - 1:bk]
            for off, k, e, eb, exm, sig, causal in zip(offs, ks, es, ebs, exs, sigs, masks):
                ex = exm + ecar
                dz = (e - sig * (e + ex)) * scale
                if causal is not None:
                    dz = jnp.where(causal, dz, 0.0)
                dzb = dz.astype(BF16)
                dk_ref[pl.ds(off, bk), :] += _dot_tn(dzb, q)
                dq = dq + _dot(dzb, k)
                ecar = ex[:, bk - 1:bk] + eb[:, bk - 1:bk].astype(F32)
            return pre, ecar, dq

        init = (jnp.zeros((bq, 1), F32), jnp.zeros((bq, 1), F32), jnp.zeros((bq, HEAD_DIM), F32))
        carry = lax.fori_loop(
            n_pairs - used, n_pairs,
            lambda t, cr: tiles([SB_PAIR * t + u for u in range(SB_PAIR)], cr, [None] * SB_PAIR), init)
        for dg in range(0, ndiag, SB_PAIR):
            dgs = list(range(dg, dg + SB_PAIR))
            carry = tiles([i * ndiag + g for g in dgs], carry, dgs)
        dq_ref[...] = carry[2]

    qspec = pl.BlockSpec((bq, HEAD_DIM), lambda h, i: (i, h))
    kspec = pl.BlockSpec((s_len, HEAD_DIM), lambda h, i: (0, h))
    return _pcall(
        body, name="sb_bwd", grid=(nh, s_len // bq),
        out_shape=[jax.ShapeDtypeStruct((s_len, hd), F32)] * 3,
        in_specs=[qspec, kspec, kspec, qspec],
        out_specs=[qspec, kspec, kspec],
        compiler_params=_params(("parallel", "arbitrary")),
    )(qn, kn, vb, do)


GDN_GROUP = 16


def _gdn_group(nh):
    return min(GDN_GROUP, nh)


def _gdn_chunk_terms(qh, kh, vh, g_r, g_c, b_c):
    c = GDN_CHUNK
    r = lax.broadcasted_iota(jnp.int32, (c, c), 0)
    s = lax.broadcasted_iota(jnp.int32, (c, c), 1)
    tril, stril = r >= s, r > s
    gcc = jnp.sum(jnp.where(tril, g_r, 0.0), axis=1, keepdims=True)
    gcr = jnp.sum(jnp.where(r <= s, g_c, 0.0), axis=0, keepdims=True)
    dm = jnp.where(tril, jnp.exp(jnp.where(tril, gcc - gcr, 0.0)), 0.0)
    kb = kh.astype(BF16)
    kk = _dot_nt(kb, kb)
    qk = _dot_nt(qh.astype(BF16), kb)
    egc = jnp.exp(gcc)
    gcl = gcc[c - 1:c, :]
    t = dict(tril=tril, stril=stril, gcc=gcc, dm=dm, kb=kb, kk=kk, qk=qk, egc=egc,
             ekd=jnp.exp(gcl - gcc), gl=jnp.exp(gcl),
             a=jnp.where(stril, b_c * kk * dm, 0.0),
             bv=b_c * vh, bk=(b_c * egc) * kh, at=jnp.where(tril, qk * dm, 0.0))
    t["qg"] = qh * egc
    t["kd"] = kh * t["ekd"]
    return t


def _unit_lower_inverses(mats):
    c = GDN_CHUNK
    r = lax.broadcasted_iota(jnp.int32, (c, c), 0)
    s = lax.broadcasted_iota(jnp.int32, (c, c), 1)
    eye = (r == s).astype(F32)
    ps = [-a for a in mats]
    ts = [eye + p for p in ps]
    span = 2
    while span < c:
        ps = [_dot(p, p, hi=HIGH) for p in ps]
        ts = [t + _dot(t, p, hi=HIGH) for t, p in zip(ts, ps)]
        span *= 2
    return ts


def _gdn_fwd(q, k, v, g_col, g_row, b_col, b_row):
    s_len, d = q.shape
    nh = d // HEAD_DIM
    c = GDN_CHUNK
    n_chunks = s_len // c
    grp = _gdn_group(nh)

    def body(q_ref, k_ref, v_ref, gc_ref, gr_ref, bc_ref, br_ref, o_ref, ss_ref, ts_ref, st):
        n = pl.program_id(1)

        @pl.when(n == 0)
        def _():
            st[...] = jnp.zeros_like(st)

        heads = range(grp)
        sls = [slice(i * HEAD_DIM, (i + 1) * HEAD_DIM) for i in heads]
        terms = [_gdn_chunk_terms(q_ref[:, sls[i]], k_ref[:, sls[i]], v_ref[:, sls[i]],
                                  gr_ref[i:i + 1, :], gc_ref[:, i:i + 1], bc_ref[:, i:i + 1]) for i in heads]
        tinvs = _unit_lower_inverses([t["a"] for t in terms])
        wvs = [_dot(tinv, t["bv"], hi=HIGH) for tinv, t in zip(tinvs, terms)]
        wks = [_dot(tinv, t["bk"], hi=HIGH) for tinv, t in zip(tinvs, terms)]
        states = [st[i] for i in heads]
        sbs = [state.astype(BF16) for state in states]
        ubs = [(wv - _dot(wk.astype(BF16), sb)).astype(BF16) for wv, wk, sb in zip(wvs, wks, sbs)]
        for i in heads:
            t = terms[i]
            o_ref[:, sls[i]] = _dot(t["qg"].astype(BF16), sbs[i]) + _dot(t["at"].astype(BF16), ubs[i])
            ss_ref[i] = states[i]
            ts_ref[i] = tinvs[i]
            st[i] = t["gl"] * states[i] + _dot_tn(t["kd"].astype(BF16), ubs[i])

    tok = pl.BlockSpec((c, grp * HEAD_DIM), lambda h, n: (n, h))
    colspec = pl.BlockSpec((None, c, grp), lambda h, n: (h, n, 0))
    rowspec = pl.BlockSpec((None, None, grp, c), lambda h, n: (h, n, 0, 0))
    return _pcall(
        body, name="gdn_fwd", grid=(nh // grp, n_chunks),
        out_shape=[jax.ShapeDtypeStruct((s_len, d), F32),
                   jax.ShapeDtypeStruct((n_chunks, nh, HEAD_DIM, HEAD_DIM), F32),
                   jax.ShapeDtypeStruct((n_chunks, nh, c, c), F32)],
        in_specs=[tok, tok, tok, colspec, rowspec, colspec, rowspec],
        out_specs=[tok, pl.BlockSpec((None, grp, HEAD_DIM, HEAD_DIM), lambda h, n: (n, h, 0, 0)),
                   pl.BlockSpec((None, grp, c, c), lambda h, n: (n, h, 0, 0))],
        scratch_shapes=[pltpu.VMEM((grp, HEAD_DIM, HEAD_DIM), F32)],
        compiler_params=_params(("parallel", "arbitrary")),
    )(q, k, v, g_col, g_row, b_col, b_row)


def _gdn_bwd(q, k, v, g_col, g_row, b_col, b_row, states, tinvs, do):
    s_len, d = q.shape
    nh = d // HEAD_DIM
    c = GDN_CHUNK
    n_chunks = s_len // c
    grp = _gdn_group(nh)

    def body(q_ref, k_ref, v_ref, gc_ref, gr_ref, bc_ref, br_ref, ss_ref, ts_ref, do_ref,
             dq_ref, dk_ref, dv_ref, dgb_ref, dst):
        n = pl.program_id(1)

        @pl.when(n == 0)
        def _():
            dst[...] = jnp.zeros_like(dst)

        r = lax.broadcasted_iota(jnp.int32, (c, c), 0)
        s = lax.broadcasted_iota(jnp.int32, (c, c), 1)
        suffix = (r <= s).astype(F32)
        lane = lax.broadcasted_iota(jnp.int32, (c, LANES), 1)
        heads = range(grp)
        sls = [slice(i * HEAD_DIM, (i + 1) * HEAD_DIM) for i in heads]
        qs = [q_ref[:, sl] for sl in sls]
        ks = [k_ref[:, sl] for sl in sls]
        vs = [v_ref[:, sl] for sl in sls]
        bcs = [bc_ref[:, i:i + 1] for i in heads]
        ts = [_gdn_chunk_terms(qs[i], ks[i], vs[i], gr_ref[i:i + 1, :], gc_ref[:, i:i + 1], bcs[i]) for i in heads]
        tinv = [ts_ref[i] for i in heads]
        state = [ss_ref[i] for i in heads]
        sb = [x.astype(BF16) for x in state]
        dnext = [dst[i] for i in heads]
        dnb = [x.astype(BF16) for x in dnext]
        dob = [do_ref[:, sl].astype(BF16) for sl in sls]
        wv = [_dot(tinv[i], ts[i]["bv"], hi=HIGH) for i in heads]
        wk = [_dot(tinv[i], ts[i]["bk"], hi=HIGH) for i in heads]
        wkb = [x.astype(BF16) for x in wk]
        ub = [(wv[i] - _dot(wkb[i], sb[i])).astype(BF16) for i in heads]
        du = [_dot_tn(ts[i]["at"].astype(BF16), dob[i]) + _dot(ts[i]["kd"].astype(BF16), dnb[i]) for i in heads]
        dub = [x.astype(BF16) for x in du]
        dat = [jnp.where(ts[i]["tril"], _dot_nt(dob[i], ub[i]), 0.0) for i in heads]
        dqg = [_dot_nt(dob[i], sb[i]) for i in heads]
        dkd = [_dot_nt(ub[i], dnb[i]) for i in heads]
        dwk = [-_dot_nt(dub[i], sb[i]) for i in heads]
        for i in heads:
            dst[i] = (ts[i]["gl"] * dnext[i] + _dot_tn(ts[i]["qg"].astype(BF16), dob[i]) - _dot_tn(wkb[i], dub[i]))
        dbv = [_dot_tn(tinv[i], du[i], hi=HIGH) for i in heads]
        dbk = [_dot_tn(tinv[i], dwk[i], hi=HIGH) for i in heads]
        dtm = [_dot_nt(du[i], ts[i]["bv"], hi=HIGH) + _dot_nt(dwk[i], ts[i]["bk"], hi=HIGH) for i in heads]
        dtt = [_dot_nt(dtm[i], tinv[i], hi=HIGH) for i in heads]
        da = [-jnp.where(ts[i]["stril"], _dot_tn(tinv[i], dtt[i], hi=HIGH), 0.0) for i in heads]
        rs = lambda m: jnp.sum(m, axis=1, keepdims=True)
        dgb = jnp.zeros((c, LANES), F32)
        for i in heads:
            t, b_c, dm, kb = ts[i], bcs[i], ts[i]["dm"], ts[i]["kb"]
            egc, ekd = t["egc"], t["ekd"]
            dkk = da[i] * b_c * dm
            ddm = da[i] * b_c * t["kk"] + dat[i] * t["qk"]
            dqkb, dkkb = (dat[i] * dm).astype(BF16), dkk.astype(BF16)
            dq_ref[:, sls[i]] = _dot(dqkb, kb) + dqg[i] * egc
            dk_ref[:, sls[i]] = (_dot_tn(dqkb, qs[i].astype(BF16)) + _dot(dkkb, kb) + _dot_tn(dkkb, kb)
                                 + dbk[i] * (b_c * egc) + dkd[i] * ekd)
            dv_ref[:, sls[i]] = dbv[i] * b_c
            dbk_k = rs(dbk[i] * ks[i])
            dbeta = rs(da[i] * t["kk"] * dm) + rs(dbv[i] * vs[i]) + dbk_k * egc
            mx = ddm * dm
            ekd_sum = rs(dkd[i] * ks[i]) * ekd
            dgc = rs(mx) + dbk_k * b_c * egc + rs(dqg[i] * qs[i]) * egc - ekd_sum
            dgl = jnp.sum(rs(dnext[i] * state[i]), axis=0, keepdims=True)
            tail = jnp.sum(ekd_sum, axis=0, keepdims=True) + dgl * t["gl"]
            dg = (_dot(suffix, jnp.broadcast_to(dgc, (c, LANES)), hi=HIGH)[:, 0:1]
                  - rs(_dot_nt(suffix, mx, hi=HIGH)) + tail)
            dgb = dgb + jnp.where(lane == i, dbeta, 0.0) + jnp.where(lane == grp + i, dg, 0.0)
        dgb_ref[...] = dgb

    last = n_chunks - 1
    tok = pl.BlockSpec((c, grp * HEAD_DIM), lambda h, n: (last - n, h))
    colspec = pl.BlockSpec((None, c, grp), lambda h, n: (h, last - n, 0))
    rowspec = pl.BlockSpec((None, None, grp, c), lambda h, n: (h, last - n, 0, 0))
    return _pcall(
        body, name="gdn_bwd", grid=(nh // grp, n_chunks),
        out_shape=[jax.ShapeDtypeStruct((s_len, d), F32)] * 3
        + [jax.ShapeDtypeStruct((nh // grp, s_len, LANES), F32)],
        in_specs=[tok, tok, tok, colspec, rowspec, colspec, rowspec,
                  pl.BlockSpec((None, grp, HEAD_DIM, HEAD_DIM), lambda h, n: (last - n, h, 0, 0)),
                  pl.BlockSpec((None, grp, c, c), lambda h, n: (last - n, h, 0, 0)), tok],
        out_specs=[tok, tok, tok, pl.BlockSpec((None, c, LANES), lambda h, n: (h, last - n, 0))],
        scratch_shapes=[pltpu.VMEM((grp, HEAD_DIM, HEAD_DIM), F32)],
        compiler_params=_params(("parallel", "arbitrary")),
    )(q, k, v, g_col, g_row, b_col, b_row, states, tinvs, do)


def _shift_down(prev8, cur, k):
    if k == 0:
        return cur
    ext = jnp.concatenate([prev8, cur], axis=0)
    return pltpu.roll(ext, k, 0)[SUBLANES:, :]


def _shift_up(cur, next8, k):
    if k == 0:
        return cur
    ext = jnp.concatenate([cur, next8], axis=0)
    n = ext.shape[0]
    return pltpu.roll(ext, n - k, 0)[:cur.shape[0], :]


def _conv_pre(i, x, prev8, w):
    prev8 = jnp.where(i == 0, 0.0, prev8)
    pre = None
    for j in range(GDN_CONV):
        term = w[j:j + 1, :] * _shift_down(prev8, x, GDN_CONV - 1 - j)
        pre = term if pre is None else pre + term
    return pre, prev8


def _l2_fwd(a, mult):
    return a * (lax.rsqrt(jnp.sum(a * a, axis=1, keepdims=True) + EPS) * mult)


def _l2_bwd(a, dy, mult):
    r = lax.rsqrt(jnp.sum(a * a, axis=1, keepdims=True) + EPS)
    dy = dy * mult
    return r * dy - a * (r * r * r) * jnp.sum(a * dy, axis=1, keepdims=True)


def _conv_fwd(xb, conv_w, group, *, norm, mult, tr=256):
    d = xb.shape[1] // 3

    def fn(i, nt, tiles, prev8, next8, cv):
        pre, _ = _conv_pre(i, tiles[0], prev8[0], cv[0])
        a = _silu(pre)
        if norm:
            a = _per_head(lambda ah: _l2_fwd(ah, mult), a)
        return [a], []

    col = Col(xb, d, group)
    wg = lax.slice_in_dim(conv_w, group * d, (group + 1) * d, axis=1)
    (y,), _ = _ew(f"conv_fwd{group}", fn, tr=tr, ins=[col], halo_prev=[col], consts=[wg], outs=[(d, F32)])
    return y


def _conv_bwd(xb, conv_w, group, dy, *, norm, mult, tr=256):
    d = xb.shape[1] // 3
    col = Col(xb, d, group)
    wg = lax.slice_in_dim(conv_w, group * d, (group + 1) * d, axis=1)

    def fn_pre(i, nt, tiles, prev8, next8, cv):
        x, dyt = tiles
        pre, p8 = _conv_pre(i, x, prev8[0], cv[0])
        if norm:
            da = _per_head(lambda ah, dh: _l2_bwd(ah, dh, mult), _silu(pre), dyt)
        else:
            da = dyt
        dpre = da * _dsilu(pre)
        tap = lax.broadcasted_iota(jnp.int32, (GDN_CONV, d), 0)
        dw = jnp.zeros((GDN_CONV, d), F32)
        for j in range(GDN_CONV):
            dw = dw + jnp.where(tap == j, _colsum(dpre * _shift_down(p8, x, GDN_CONV - 1 - j)), 0.0)
        return [dpre], [dw]

    (dpre,), (dw,) = _ew(f"conv_bwd_pre{group}", fn_pre, tr=tr, ins=[col, dy], halo_prev=[col], consts=[wg],
                         outs=[(d, F32)], accs=[(GDN_CONV, d)])

    def fn_dx(i, nt, tiles, prev8, next8, cv):
        n8 = jnp.where(i == nt - 1, 0.0, next8[0])
        dx = None
        for j in range(GDN_CONV):
            term = cv[0][j:j + 1, :] * _shift_up(tiles[0], n8, GDN_CONV - 1 - j)
            dx = term if dx is None else dx + term
        return [dx], []

    (dx,), _ = _ew(f"conv_bwd_dx{group}", fn_dx, tr=tr, ins=[dpre], halo_next=[dpre], consts=[wg], outs=[(d, BF16)])
    return dx, dw


def _adamw(name, w, m, v, grads, *, tr=64):
    shape = w.shape
    w2, m2, v2 = [a.reshape(-1, shape[-1]) for a in (w, m, v)]
    n_g = len(grads)
    bc1 = 1.0 - ADAM_B1 ** ADAM_STEP
    bc2 = 1.0 - ADAM_B2 ** ADAM_STEP

    def fn(i, nt, tiles, prev8, next8, cv):
        wt, mt, vt = tiles[:3]
        g = tiles[3]
        for extra in tiles[4:]:
            g = g + extra
        mn = ADAM_B1 * mt + (1.0 - ADAM_B1) * g
        vn = ADAM_B2 * vt + (1.0 - ADAM_B2) * (g * g)
        delta = -ADAM_LR * ((mn / bc1) / (jnp.sqrt(vn / bc2) + ADAM_EPS) + ADAM_WD * wt)
        return [g, delta, mn, vn], []

    width = shape[-1]
    outs, _ = _ew(name, fn, tr=tr, ins=[w2, m2, v2] + list(grads), outs=[(width, F32)] * 4)
    assert n_g >= 1
    return tuple(o.reshape(shape) for o in outs)


def _pad_cols(a, width):
    return jnp.pad(a, ((0, 0), (0, width - a.shape[1])))


def _gdn_layouts(gbeta, nh, n_chunks):
    grp = _gdn_group(nh)
    s_len = gbeta.shape[0]

    def lay(a):
        col = a.reshape(s_len, nh // grp, grp).transpose(1, 0, 2)
        row = a.reshape(n_chunks, GDN_CHUNK, nh // grp, grp).transpose(2, 0, 3, 1)
        return col, row

    b_col, b_row = lay(gbeta[:, :nh])
    g_col, g_row = lay(gbeta[:, nh:2 * nh])
    return g_col, g_row, b_col, b_row


def kernel(x, c, w_mod, b_mod, norm1_w, w_in, q_norm_w, k_norm_w, conv_w, a_log, dt_bias, o_norm_w, p_a, p_b, w_out, norm2_w, w_gate, w_up, w_down, loss_target, m_w_mod, m_b_mod, m_norm1_w, m_w_in, m_q_norm_w, m_k_norm_w, m_conv_w, m_a_log, m_dt_bias, m_o_norm_w, m_p_a, m_p_b, m_w_out, m_norm2_w, m_w_gate, m_w_up, m_w_down, v_w_mod, v_b_mod, v_norm1_w, v_w_in, v_q_norm_w, v_k_norm_w, v_conv_w, v_a_log, v_dt_bias, v_o_norm_w, v_p_a, v_p_b, v_w_out, v_norm2_w, v_w_gate, v_w_up, v_w_down):
    s_len, d = x.shape[1], x.shape[2]
    nh = d // HEAD_DIM
    n_chunks = s_len // GDN_CHUNK
    ff = 4 * w_gate.shape[2]
    mx, my, mc = _my_pos()
    chip = 2 * mx + my
    dev = 2 * chip + mc
    x2 = x[0]
    tgt = loss_target[0]

    c_all = _allgather8("ag_c", _pad_cols(c, d).reshape(SUBLANES, d // SUBLANES)).reshape(8, d)
    wm = w_mod[0]
    mod_w = wm.shape[1]
    bm_cols = lax.dynamic_slice_in_dim(b_mod, chip * mod_w, mod_w, axis=1)

    def mod_body(c_ref, w_ref, b_ref, o_ref, ca_ref):
        ca = _silu(c_ref[...])
        ca_ref[...] = ca
        o_ref[...] = _dot(ca, w_ref[...], hi=HIGHEST) + b_ref[...]

    tn_mod = _pick(mod_w, 512)
    mod8, c_act = _pcall(
        mod_body, name="mod_fwd", grid=(mod_w // tn_mod,),
        out_shape=[jax.ShapeDtypeStruct((8, mod_w), F32), jax.ShapeDtypeStruct((8, d), F32)],
        in_specs=[pl.BlockSpec((8, d), lambda j: (0, 0)), pl.BlockSpec((d, tn_mod), lambda j: (0, j)),
                  pl.BlockSpec((1, tn_mod), lambda j: (0, j))],
        out_specs=[pl.BlockSpec((8, tn_mod), lambda j: (0, j)), pl.BlockSpec((8, d), lambda j: (0, 0))],
        compiler_params=_params(("arbitrary",)),
    )(c_all, wm, bm_cols)
    mod_all = _allgather8("ag_mod", mod8)
    mod_me = mod_all.reshape(4, 2, 8, mod_w)[:, mc, dev, :].reshape(1, 6 * d)
    shift1, scale1, gate1, shift2, scale2, gate2 = [mod_me[:, j * d:(j + 1) * d] for j in range(6)]

    shards = [w_in[0].astype(BF16), p_a[0].astype(BF16), p_b[0].astype(BF16), w_out[0].astype(BF16),
              w_gate[0].astype(BF16), w_up[0].astype(BF16), w_down[0].astype(BF16), conv_w[0]]
    gathered = [_fill_slot(g, sh, chip) for g, sh in zip(_gather4("ag_weights", shards, n_split=7), shards)]
    w_in_f = gathered[0].transpose(1, 0, 2).reshape(d, -1)
    wa = w_in_f[:, :3 * d]
    wb = w_in_f[:, 3 * d:6 * d]
    wzg = jnp.concatenate([w_in_f[:, 6 * d:7 * d], w_in_f[:, 7 * d + 2 * nh:]], axis=1)
    wba = _pad_cols(w_in_f[:, 7 * d:7 * d + 2 * nh], LANES)
    p_a_f, p_b_f, w_out_f = [g.reshape(d, d) for g in gathered[1:4]]
    w_gate_f, w_up_f = [g.transpose(1, 0, 2).reshape(d, ff) for g in gathered[4:6]]
    w_down_f = gathered[6].reshape(ff, d)
    conv_f = gathered[7].transpose(1, 0, 2).reshape(GDN_CONV, 3 * d)

    def norm_mod_fn(i, nt, tiles, prev8, next8, cv):
        w, sc, sh = cv
        return [_rms_fwd(tiles[0], w) * (1.0 + sc) + sh], []

    (u1,), _ = _ew("norm_mod1", norm_mod_fn, tr=512, ins=[x2], consts=[norm1_w, scale1, shift1], outs=[(d, BF16)])
    proj_a = _mm("proj_a", u1, wa)
    proj_b = _mm("proj_b", u1, wb)
    proj_zg = _mm("proj_zg", u1, wzg)
    proj_ba = _mm("proj_ba", u1, wba)

    def qknorm_fn(i, nt, tiles, prev8, next8, cv):
        qa, ka, va = tiles
        return [_per_head(lambda h: _rms_fwd(h, cv[0]), qa), _per_head(lambda h: _rms_fwd(h, cv[1]), ka), va], []

    (qn, kn, vb), _ = _ew("qknorm", qknorm_fn, tr=256,
                          ins=[Col(proj_a, d, 0), Col(proj_a, d, 1), Col(proj_a, d, 2)],
                          consts=[q_norm_w, k_norm_w], outs=[(d, BF16)] * 3)
    o_a = _sb_fwd(qn, kn, vb)

    lane_ids = jnp.arange(LANES)
    is_b = (lane_ids < nh)[None, :]
    is_a = ((lane_ids >= nh) & (lane_ids < 2 * nh))[None, :]
    alog128 = jnp.zeros((1, LANES), F32).at[:, nh:2 * nh].set(a_log)
    dtb128 = jnp.zeros((1, LANES), F32).at[:, nh:2 * nh].set(dt_bias)
    is_b_f, is_a_f = is_b.astype(F32), is_a.astype(F32)

    def gbeta_fn(i, nt, tiles, prev8, next8, cv):
        al, dtb, mb, ma = cv
        ba = tiles[0]
        g = -jnp.exp(al) * _softplus(ba + dtb)
        return [jnp.where(mb > 0.5, _sigmoid(ba), jnp.where(ma > 0.5, g, 0.0))], []

    (gbeta,), _ = _ew("gbeta", gbeta_fn, tr=1024, ins=[proj_ba], consts=[alog128, dtb128, is_b_f, is_a_f],
                      outs=[(LANES, F32)])
    g_col, g_row, b_col, b_row = _gdn_layouts(gbeta, nh, n_chunks)
    qscale = HEAD_DIM ** -0.5
    q_b = _conv_fwd(proj_b, conv_f, 0, norm=True, mult=qscale)
    k_b = _conv_fwd(proj_b, conv_f, 1, norm=True, mult=1.0)
    v_b = _conv_fwd(proj_b, conv_f, 2, norm=False, mult=1.0)
    o_raw, states, tinvs = _gdn_fwd(q_b, k_b, v_b, g_col, g_row, b_col, b_row)

    def gated_norm_fn(i, nt, tiles, prev8, next8, cv):
        o, z = tiles
        return [_per_head(lambda h: _rms_fwd(h, cv[0]), o) * _silu(z)], []

    (o_b,), _ = _ew("gated_norm", gated_norm_fn, tr=256, ins=[o_raw, Col(proj_zg, d, 0)], consts=[o_norm_w],
                    outs=[(d, BF16)])
    y_a = _mm("out_a", o_a, p_a_f)
    y_b = _mm("out_b", o_b, p_b_f)

    def merge_fn(i, nt, tiles, prev8, next8, cv):
        ya, yb, ga, gb = tiles
        return [_sigmoid(ga) * ya + _sigmoid(gb) * yb], []

    (merged,), _ = _ew("merge", merge_fn, tr=256, ins=[y_a, y_b, Col(proj_zg, d, 1), Col(proj_zg, d, 2)],
                       outs=[(d, BF16)])
    y_o = _mm("out_proj", merged, w_out_f)

    def resid_norm_fn(i, nt, tiles, prev8, next8, cv):
        xt, yo = tiles
        g1, w, sc, sh = cv
        h1 = xt + g1 * yo
        return [h1, _rms_fwd(h1, w) * (1.0 + sc) + sh], []

    (h1, u2), _ = _ew("resid_norm2", resid_norm_fn, tr=256, ins=[x2, y_o],
                      consts=[gate1, norm2_w, scale2, shift2], outs=[(d, F32), (d, BF16)])
    gt = _mm("ff_gate", u2, w_gate_f)
    up = _mm("ff_up", u2, w_up_f)

    def swiglu_fn(i, nt, tiles, prev8, next8, cv):
        return [_silu(tiles[0]) * tiles[1]], []

    (act,), _ = _ew("swiglu", swiglu_fn, tr=128, ins=[gt, up], outs=[(ff, BF16)])
    y_d = _mm("ff_down", act, w_down_f)

    def loss_fn(i, nt, tiles, prev8, next8, cv):
        h1t, yd, tg = tiles
        diff = h1t + cv[0] * yd - tg
        dy = diff * (1.0 / d)
        return [dy, dy * cv[0]], [_colsum(0.5 * diff * dy), _colsum(dy * yd)]

    (dy, dyd), (loss_cols, dgate2) = _ew("loss", loss_fn, tr=256, ins=[h1, y_d, tgt], consts=[gate2],
                                         outs=[(d, F32), (d, BF16)], accs=[(1, d), (1, d)])
    loss = lax.psum(jnp.sum(loss_cols), ("x", "y", "c"))

    dact = _mm("d_act", dyd, w_down_f, nt=True)
    g_w_down = _mm("g_w_down", act.T, dyd)

    def swiglu_bwd_fn(i, nt, tiles, prev8, next8, cv):
        da, g, u = tiles
        return [da * u * _dsilu(g), da * _silu(g)], []

    (dgt, dup), _ = _ew("swiglu_bwd", swiglu_bwd_fn, tr=128, ins=[dact, gt, up], outs=[(ff, BF16)] * 2)
    du2 = _mm("d_u2_up", dup, w_up_f, nt=True, add=_mm("d_u2_gate", dgt, w_gate_f, nt=True))
    u2_t = u2.T
    g_w_gate = _mm("g_w_gate", u2_t, dgt)
    g_w_up = _mm("g_w_up", u2_t, dup)

    def norm2_bwd_fn(i, nt, tiles, prev8, next8, cv):
        h1t, du, dres, yo = tiles
        w, sc, g1 = cv
        r = lax.rsqrt(jnp.mean(h1t * h1t, axis=1, keepdims=True) + EPS)
        nrm = h1t * r
        dn = du * w * (1.0 + sc)
        dh = r * (dn - nrm * jnp.mean(dn * nrm, axis=1, keepdims=True)) + dres
        return [dh, dh * g1], [_colsum(du), _colsum(du * nrm * w), _colsum(du * nrm * (1.0 + sc)), _colsum(dh * yo)]

    (dh1, dyo), (dshift2, dscale2, g_norm2, dgate1) = _ew(
        "norm2_bwd", norm2_bwd_fn, tr=256, ins=[h1, du2, dy, y_o], consts=[norm2_w, scale2, gate1],
        outs=[(d, F32), (d, BF16)], accs=[(1, d)] * 4)

    dmerged = _mm("d_merged", dyo, w_out_f, nt=True)
    g_w_out = _mm("g_w_out", merged.T, dyo)

    def merge_bwd_fn(i, nt, tiles, prev8, next8, cv):
        dm, ya, yb, ga, gb = tiles
        sa, sb = _sigmoid(ga), _sigmoid(gb)
        return [dm * sa, dm * sb, dm * ya * sa * (1.0 - sa), dm * yb * sb * (1.0 - sb)], []

    (dya, dyb, dga, dgb_gate), _ = _ew(
        "merge_bwd", merge_bwd_fn, tr=256, ins=[dmerged, y_a, y_b, Col(proj_zg, d, 1), Col(proj_zg, d, 2)],
        outs=[(d, BF16)] * 4)
    do_a = _mm("d_o_a", dya, p_a_f, nt=True, out_dtype=BF16)
    g_p_a = _mm("g_p_a", o_a.T, dya)
    do_b = _mm("d_o_b", dyb, p_b_f, nt=True)
    g_p_b = _mm("g_p_b", o_b.T, dyb)

    def gated_norm_bwd_fn(i, nt, tiles, prev8, next8, cv):
        dob, o, z = tiles
        sz = _silu(z)

        def head(oh, dh):
            return _rms_bwd(oh, cv[0], dh)

        dxo, dwn = _per_head(head, o, dob * sz)
        nrm_w = _per_head(lambda h: _rms_fwd(h, cv[0]), o)
        return [dxo, dob * nrm_w * _dsilu(z)], [_colsum(_head_sum(dwn))]

    (do_raw, dz_b), (g_o_norm,) = _ew(
        "gated_norm_bwd", gated_norm_bwd_fn, tr=256, ins=[do_b, o_raw, Col(proj_zg, d, 0)], consts=[o_norm_w],
        outs=[(d, F32), (d, BF16)], accs=[(1, HEAD_DIM)])
    dq_b, dk_b, dv_b, dgb_grp = _gdn_bwd(q_b, k_b, v_b, g_col, g_row, b_col, b_row, states, tinvs, do_raw)
    grp = _gdn_group(nh)
    dbeta = dgb_grp[:, :, :grp].transpose(1, 0, 2).reshape(s_len, nh)
    dg = dgb_grp[:, :, grp:2 * grp].transpose(1, 0, 2).reshape(s_len, nh)
    dgbeta = _pad_cols(jnp.concatenate([dbeta, dg], axis=1), LANES)

    def gbeta_bwd_fn(i, nt, tiles, prev8, next8, cv):
        al, dtb, mb, ma = cv
        ba, dgb = tiles
        beta = _sigmoid(ba)
        arg = ba + dtb
        da = dgb * (-jnp.exp(al)) * _sigmoid(arg)
        g = -jnp.exp(al) * _softplus(arg)
        dba = jnp.where(mb > 0.5, dgb * beta * (1.0 - beta), jnp.where(ma > 0.5, da, 0.0))
        return [dba], [_colsum(jnp.where(ma > 0.5, dgb * g, 0.0)), _colsum(jnp.where(ma > 0.5, da, 0.0))]

    (dba,), (g_alog128, g_dtb128) = _ew(
        "gbeta_bwd", gbeta_bwd_fn, tr=1024, ins=[proj_ba, dgbeta], consts=[alog128, dtb128, is_b_f, is_a_f],
        outs=[(LANES, BF16)], accs=[(1, LANES)] * 2)
    dxq, g_conv_q = _conv_bwd(proj_b, conv_f, 0, dq_b, norm=True, mult=qscale)
    dxk, g_conv_k = _conv_bwd(proj_b, conv_f, 1, dk_b, norm=True, mult=1.0)
    dxv, g_conv_v = _conv_bwd(proj_b, conv_f, 2, dv_b, norm=False, mult=1.0)
    g_conv = jnp.concatenate([g_conv_q, g_conv_k, g_conv_v], axis=1)

    dqn, dkn, dvb = _sb_bwd(qn, kn, vb, do_a)

    def qknorm_bwd_fn(i, nt, tiles, prev8, next8, cv):
        qa, ka, dq, dk, dv = tiles
        dxq_, dwq = _per_head(lambda h, g: _rms_bwd(h, cv[0], g), qa, dq)
        dxk_, dwk = _per_head(lambda h, g: _rms_bwd(h, cv[1], g), ka, dk)
        return [dxq_, dxk_, dv], [_colsum(_head_sum(dwq)), _colsum(_head_sum(dwk))]

    (dqa, dka, dva), (g_q_norm, g_k_norm) = _ew(
        "qknorm_bwd", qknorm_bwd_fn, tr=256, ins=[Col(proj_a, d, 0), Col(proj_a, d, 1), dqn, dkn, dvb],
        consts=[q_norm_w, k_norm_w], outs=[(d, BF16)] * 3, accs=[(1, HEAD_DIM)] * 2)

    u1_t = u1.T
    d_a = jnp.concatenate([dqa, dka, dva], axis=1)
    d_b = jnp.concatenate([dxq, dxk, dxv], axis=1)
    d_zg = jnp.concatenate([dz_b, dga, dgb_gate], axis=1)
    du1 = _mm("d_u1_a", d_a, wa, nt=True)
    du1 = _mm("d_u1_b", d_b, wb, nt=True, add=du1)
    du1 = _mm("d_u1_zg", d_zg, wzg, nt=True, add=du1)
    du1 = _mm("d_u1_ba", dba, wba, nt=True, add=du1)
    g_wa = _mm("g_w_in_a", u1_t, d_a)
    g_wb = _mm("g_w_in_b", u1_t, d_b)
    g_wzg = _mm("g_w_in_zg", u1_t, d_zg)
    g_wba = _mm("g_w_in_ba", u1_t, dba)

    def norm1_bwd_fn(i, nt, tiles, prev8, next8, cv):
        xt, du, dres = tiles
        w, sc = cv
        r = lax.rsqrt(jnp.mean(xt * xt, axis=1, keepdims=True) + EPS)
        nrm = xt * r
        dn = du * w * (1.0 + sc)
        dxt = r * (dn - nrm * jnp.mean(dn * nrm, axis=1, keepdims=True)) + dres
        return [dxt], [_colsum(du), _colsum(du * nrm * w), _colsum(du * nrm * (1.0 + sc))]

    (grad_x,), (dshift1, dscale1, g_norm1) = _ew(
        "norm1_bwd", norm1_bwd_fn, tr=256, ins=[x2, du1, dh1], consts=[norm1_w, scale1],
        outs=[(d, F32)], accs=[(1, d)] * 3)

    dmod_me = jnp.concatenate([dshift1, dscale1, dgate1, dshift2, dscale2, dgate2], axis=1)
    small = jnp.concatenate(
        [dmod_me, g_norm1, g_norm2, g_q_norm, g_k_norm, g_o_norm, g_alog128[:, nh:2 * nh], g_dtb128[:, nh:2 * nh],
         g_conv.reshape(1, -1)], axis=1)
    n_small = small.shape[1]
    pad_to = -(-n_small // (SUBLANES * LANES)) * (SUBLANES * LANES)
    small_all = _allgather8("ag_small", _pad_cols(small, pad_to).reshape(SUBLANES, pad_to // SUBLANES))
    small_all = small_all.reshape(8, pad_to)

    def sum8_fn(i, nt, tiles, prev8, next8, cv):
        return [], [_colsum(tiles[0])]

    _, (small_sum,) = _ew("sum_small", sum8_fn, tr=8, ins=[small_all], accs=[(1, pad_to)])
    offs = [0]
    for width in (6 * d, d, d, HEAD_DIM, HEAD_DIM, HEAD_DIM, nh, nh, GDN_CONV * 3 * d):
        offs.append(offs[-1] + width)
    pieces = [small_sum[:, offs[j]:offs[j + 1]] for j in range(9)]
    (gs_b_mod, gs_norm1, gs_norm2, gs_q_norm, gs_k_norm, gs_o_norm, gs_a_log, gs_dt_bias, gs_conv) = pieces
    conv_cols = 3 * d // 4
    gs_conv_mine = lax.dynamic_slice_in_dim(gs_conv.reshape(GDN_CONV, 3 * d), chip * conv_cols, conv_cols, axis=1)

    dmod_all = lax.dynamic_slice_in_dim(small_all[:, :6 * d], chip * mod_w, mod_w, axis=1)

    def wmod_grad_body(ct_ref, dm_ref, o_ref):
        o_ref[...] = _dot(ct_ref[...], dm_ref[...], hi=HIGHEST)

    g_w_mod = _pcall(
        wmod_grad_body, name="g_w_mod", grid=(mod_w // tn_mod,),
        out_shape=jax.ShapeDtypeStruct((d, mod_w), F32),
        in_specs=[pl.BlockSpec((d, 8), lambda j: (0, 0)), pl.BlockSpec((8, tn_mod), lambda j: (0, j))],
        out_specs=pl.BlockSpec((d, tn_mod), lambda j: (0, j)),
        compiler_params=_params(("arbitrary",)),
    )(c_act.T, dmod_all)

    raw = [g_wa, g_wb, g_wzg, g_wba, g_w_gate, g_w_up,
           g_p_a.reshape(4, d // 4, d), g_p_b.reshape(4, d // 4, d), g_w_out.reshape(4, d // 4, d),
           g_w_down.reshape(4, ff // 4, d)]
    raw_axes = [0] * 6 + [1] * 4
    raw_theirs = _sibling_send("swap_halves", raw, raw_axes)
    sums = []
    for t, (part, ax, other) in enumerate(zip(raw, raw_axes, raw_theirs)):
        def pair_fn(i, nt, tiles, prev8, next8, cv):
            return [tiles[0] + tiles[1]], []

        hr, width = part.shape[ax] // 2, part.shape[-1]
        mine = lax.dynamic_slice_in_dim(part, mc * hr, hr, axis=ax)
        (ch,), _ = _ew(f"pair_sum{t}", pair_fn, tr=64,
                       ins=[mine.reshape(-1, width), other.reshape(-1, width)], outs=[(width, BF16)])
        sums.append(ch.reshape(other.shape))
    s_wa, s_wb, s_wzg, s_wba, s_gate, s_up, s_pa, s_pb, s_out, s_down = sums
    s_w_in = jnp.concatenate([s_wa, s_wb, s_wzg[:, :d], s_wba[:, :2 * nh], s_wzg[:, d:]], axis=1)
    by_chip = lambda a: a.reshape(a.shape[0], 4, -1).transpose(1, 0, 2)
    chip_halves = [by_chip(s_w_in), s_pa, s_pb, s_out, by_chip(s_gate), by_chip(s_up), s_down]
    landed = [_fill_slot(land, ch, chip) for land, ch in zip(_scatter4("rs_grads", chip_halves), chip_halves)]
    g_mine = []
    for t, land in enumerate(landed):
        def sum4_fn(i, nt, tiles, prev8, next8, cv):
            f = [tl.astype(F32) for tl in tiles]
            return [(f[0] + f[1]) + (f[2] + f[3])], []

        (gh,), _ = _ew(f"chip_sum{t}", sum4_fn, tr=64, ins=[Col(land, lead=s) for s in range(4)],
                       outs=[(land.shape[-1], F32)])
        g_mine.append(gh)
    g_theirs = _sibling_send("join_grads", g_mine)
    g_full = [jnp.concatenate([jnp.where(mc == 0, a, b), jnp.where(mc == 0, b, a)], axis=0)
              for a, b in zip(g_mine, g_theirs)]

    big = {}
    names = ["w_in", "p_a", "p_b", "w_out", "w_gate", "w_up", "w_down"]
    big_w = [w_in, p_a, p_b, w_out, w_gate, w_up, w_down]
    big_m = [m_w_in, m_p_a, m_p_b, m_w_out, m_w_gate, m_w_up, m_w_down]
    big_v = [v_w_in, v_p_a, v_p_b, v_w_out, v_w_gate, v_w_up, v_w_down]
    for t, nm in enumerate(names):
        big[nm] = _adamw(f"adamw_{nm}", big_w[t], big_m[t], big_v[t], [g_full[t]])
    big["w_mod"] = _adamw("adamw_w_mod", w_mod, m_w_mod, v_w_mod, [g_w_mod])
    big["conv_w"] = _adamw("adamw_conv_w", conv_w, m_conv_w, v_conv_w, [gs_conv_mine], tr=8)
    small_names = ["b_mod", "norm1_w", "norm2_w", "q_norm_w", "k_norm_w", "o_norm_w", "a_log", "dt_bias"]
    small_w = [b_mod, norm1_w, norm2_w, q_norm_w, k_norm_w, o_norm_w, a_log, dt_bias]
    small_m = [m_b_mod, m_norm1_w, m_norm2_w, m_q_norm_w, m_k_norm_w, m_o_norm_w, m_a_log, m_dt_bias]
    small_v = [v_b_mod, v_norm1_w, v_norm2_w, v_q_norm_w, v_k_norm_w, v_o_norm_w, v_a_log, v_dt_bias]
    small_g = [gs_b_mod, gs_norm1, gs_norm2, gs_q_norm, gs_k_norm, gs_o_norm, gs_a_log, gs_dt_bias]
    rep_w = jnp.concatenate(small_w, axis=1)
    rep_m = jnp.concatenate(small_m, axis=1)
    rep_v = jnp.concatenate(small_v, axis=1)
    rep_g = jnp.concatenate(small_g, axis=1)
    rep = _adamw("adamw_small", rep_w, rep_m, rep_v, [rep_g], tr=1)
    roffs = [0]
    for a in small_w:
        roffs.append(roffs[-1] + a.shape[1])
    for j, nm in enumerate(small_names):
        big[nm] = tuple(r[:, roffs[j]:roffs[j + 1]] for r in rep)

    order = ["w_mod", "b_mod", "norm1_w", "w_in", "q_norm_w", "k_norm_w", "conv_w", "a_log", "dt_bias", "o_norm_w",
             "p_a", "p_b", "w_out", "norm2_w", "w_gate", "w_up", "w_down"]
    grads = [big[nm][0] for nm in order]
    deltas = [big[nm][1] for nm in order]
    new_m = [big[nm][2] for nm in order]
    new_v = [big[nm][3] for nm in order]
    return (loss, grad_x[None], *grads, *deltas, *new_m, *new_v)
```

```python
import jax
import jax.numpy as jnp
from jax import lax
from jax.experimental import pallas as pl
from jax.experimental.pallas import tpu as pltpu

F32 = jnp.float32
BF16 = jnp.bfloat16
HIGHEST = lax.Precision.HIGHEST
HIGH = lax.Precision.HIGH
MESH = pl.DeviceIdType.MESH

HEAD_DIM = 128
GDN_CHUNK = 64
GDN_CONV = 4
EPS = 1e-6
LANES = 128
SUBLANES = 8
VMEM_LIMIT = 56 * 1024 * 1024
MM_VMEM_BUDGET = 40 * 1024 * 1024

ADAM_LR = 0.001
ADAM_B1 = 0.9
ADAM_B2 = 0.999
ADAM_EPS = 1e-08
ADAM_WD = 0.01
ADAM_STEP = 10


def _pcall(body, **kw):
    return pl.pallas_call(body, **kw)


def _params(sem=None):
    if sem is None:
        return pltpu.CompilerParams(vmem_limit_bytes=VMEM_LIMIT)
    return pltpu.CompilerParams(dimension_semantics=sem, vmem_limit_bytes=VMEM_LIMIT)


def _pick(dim, target):
    if dim <= target:
        return dim
    best = None
    for t in range(LANES, target + 1, LANES):
        if dim % t == 0:
            best = t
    assert best is not None, (dim, target)
    return best


def _rows_tile(rows, target):
    t = min(rows, target)
    while rows % t:
        t //= 2
    assert t >= SUBLANES or t == rows, (rows, target)
    return t


def _dot(a, b, hi=None):
    return jnp.dot(a, b, preferred_element_type=F32, precision=hi)


def _dot_nt(a, b, hi=None):
    return lax.dot_general(a, b, (((1,), (1,)), ((), ())), preferred_element_type=F32, precision=hi)


def _dot_tn(a, b, hi=None):
    return lax.dot_general(a, b, (((0,), (0,)), ((), ())), preferred_element_type=F32, precision=hi)


def _sigmoid(x):
    return 1.0 / (1.0 + jnp.exp(-x))


def _softplus(x):
    return jnp.maximum(x, 0.0) + jnp.log(1.0 + jnp.exp(-jnp.abs(x)))


_HBM = pl.BlockSpec(memory_space=pltpu.HBM)


def _my_pos():
    return lax.axis_index("x"), lax.axis_index("y"), lax.axis_index("c")


def _allgather8(name, v):
    def body(v_ref, o_ref, ssem, rsem, lsem):
        x, y, c = _my_pos()
        me = 4 * x + 2 * y + c
        loc = pltpu.make_async_copy(v_ref, o_ref.at[me], lsem)
        loc.start()
        sends, recvs = [], []
        for k in range(1, 8):
            px, py, pc = (x + (k >> 2)) % 2, (y + ((k >> 1) & 1)) % 2, (c + (k & 1)) % 2
            cp = pltpu.make_async_remote_copy(
                src_ref=v_ref, dst_ref=o_ref.at[me], send_sem=ssem.at[k - 1], recv_sem=rsem.at[k - 1],
                device_id=(px, py, pc), device_id_type=MESH)
            cp.start()
            sends.append(cp)
            recvs.append(pltpu.make_async_remote_copy(
                src_ref=v_ref, dst_ref=o_ref.at[4 * px + 2 * py + pc], send_sem=ssem.at[k - 1],
                recv_sem=rsem.at[k - 1], device_id=(px, py, pc), device_id_type=MESH))
        for rc in recvs:
            rc.wait_recv()
        for cp in sends:
            cp.wait_send()
        loc.wait()

    return _pcall(
        body, name=name, out_shape=jax.ShapeDtypeStruct((8,) + v.shape, v.dtype),
        in_specs=[_HBM], out_specs=_HBM,
        scratch_shapes=[pltpu.SemaphoreType.DMA((7,)), pltpu.SemaphoreType.DMA((7,)), pltpu.SemaphoreType.DMA],
    )(v)


def _plane_peers(x, y):
    return [((x + (k >> 1)) % 2, (y + (k & 1)) % 2) for k in range(1, 4)]


class _GatherPlan:
    def __init__(self, ins, outs, sems, n_split):
        ssem, rsem, fsem, gsem = sems
        x, y, c = _my_pos()
        me = 2 * x + y
        copy = lambda src, dst, s_sem, r_sem, dev: (lambda: pltpu.make_async_remote_copy(
            src_ref=src, dst_ref=dst, send_sem=s_sem, recv_sem=r_sem, device_id=dev, device_id_type=MESH))
        self.sends, self.recvs, self.fwds, self.fwd_recvs = [], [], [], []
        for t in range(len(ins)):
            split = t < n_split
            hr = ins[t].shape[0] // 2
            for k, (px, py) in enumerate(_plane_peers(x, y)):
                peer = 2 * px + py
                sem = 3 * t + k
                if split:
                    mine = pl.ds(pl.multiple_of(c * hr, 16), hr)
                    other = pl.ds(pl.multiple_of((1 - c) * hr, 16), hr)
                    src, dst, got = ins[t].at[mine], outs[t].at[me, mine], outs[t].at[peer, mine]
                else:
                    src, dst, got = ins[t], outs[t].at[me], outs[t].at[peer]
                self.sends.append(copy(src, dst, ssem.at[sem], rsem.at[sem], (px, py, c)))
                self.recvs.append(copy(src, got, ssem.at[sem], rsem.at[sem], (px, py, c)))
                if split:
                    self.fwds.append(copy(got, got, fsem.at[sem], gsem.at[sem], (x, y, 1 - c)))
                    self.fwd_recvs.append(copy(got, outs[t].at[peer, other], fsem.at[sem], gsem.at[sem], (x, y, 1 - c)))
                else:
                    self.fwds.append(None)

    def start(self):
        for cp in self.sends:
            cp().start()

    def relay(self):
        for rc, fw in zip(self.recvs, self.fwds):
            rc().wait_recv()
            if fw is not None:
                fw().start()

    def finish(self):
        for fr in self.fwd_recvs:
            fr().wait_recv()
        for cp in self.sends + [fw for fw in self.fwds if fw is not None]:
            cp().wait_send()

    @staticmethod
    def out_shapes(shards):
        return [jax.ShapeDtypeStruct((4,) + s.shape, s.dtype) for s in shards]

    @staticmethod
    def sem_shapes(n):
        return [pltpu.SemaphoreType.DMA((3 * n,))] * 4


def _gather4(name, shards, n_split):
    n = len(shards)

    def body(*refs):
        plan = _GatherPlan(refs[:n], refs[n:2 * n], refs[2 * n:], n_split)
        plan.start()
        plan.relay()
        plan.finish()

    return _pcall(
        body, name=name, out_shape=_GatherPlan.out_shapes(shards), in_specs=[_HBM] * n, out_specs=[_HBM] * n,
        scratch_shapes=_GatherPlan.sem_shapes(n),
    )(*shards)


def _fill_slot(slots, own, slot):
    mask = (jnp.arange(4) == slot).reshape((4,) + (1,) * (slots.ndim - 1))
    return jnp.where(mask, own if own.ndim == slots.ndim else own[None], slots)


class _ScatterPlan:
    def __init__(self, ins, outs, sems):
        ssem, rsem = sems
        x, y, c = _my_pos()
        me = 2 * x + y
        copy = lambda src, dst, s_sem, r_sem, dev: (lambda: pltpu.make_async_remote_copy(
            src_ref=src, dst_ref=dst, send_sem=s_sem, recv_sem=r_sem, device_id=dev, device_id_type=MESH))
        self.sends, self.recvs = [], []
        for t in range(len(ins)):
            for k, (px, py) in enumerate(_plane_peers(x, y)):
                peer = 2 * px + py
                sem = 3 * t + k
                self.sends.append(copy(ins[t].at[peer], outs[t].at[me], ssem.at[sem], rsem.at[sem], (px, py, c)))
                self.recvs.append(copy(ins[t].at[peer], outs[t].at[peer], ssem.at[sem], rsem.at[sem], (px, py, c)))

    def start(self):
        for cp in self.sends:
            cp().start()

    def finish(self):
        for rc in self.recvs:
            rc().wait_recv()
        for cp in self.sends:
            cp().wait_send()

    @staticmethod
    def out_shapes(partials):
        return [jax.ShapeDtypeStruct(p.shape, p.dtype) for p in partials]

    @staticmethod
    def sem_shapes(n):
        return [pltpu.SemaphoreType.DMA((3 * n,))] * 2


def _scatter4(name, partials):
    n = len(partials)

    def body(*refs):
        plan = _ScatterPlan(refs[:n], refs[n:2 * n], refs[2 * n:])
        plan.start()
        plan.finish()

    return _pcall(
        body, name=name, out_shape=_ScatterPlan.out_shapes(partials), in_specs=[_HBM] * n, out_specs=[_HBM] * n,
        scratch_shapes=_ScatterPlan.sem_shapes(n),
    )(*partials)


def _sibling_send(name, arrays, axes=None):
    n = len(arrays)
    axes = [None] * n if axes is None else axes

    def body(*refs):
        ins, outs = refs[:n], refs[n:2 * n]
        ssem, rsem = refs[2 * n:]
        x, y, c = _my_pos()
        cps = []
        for t in range(n):
            src = ins[t]
            if axes[t] is not None:
                hr = ins[t].shape[axes[t]] // 2
                give = pl.ds(pl.multiple_of((1 - c) * hr, SUBLANES), hr)
                src = ins[t].at[give] if axes[t] == 0 else ins[t].at[:, give]
            cp = pltpu.make_async_remote_copy(
                src_ref=src, dst_ref=outs[t], send_sem=ssem.at[t], recv_sem=rsem.at[t],
                device_id=(x, y, 1 - c), device_id_type=MESH)
            cp.start()
            cps.append(cp)
        for cp in cps:
            cp.wait_recv()
        for cp in cps:
            cp.wait_send()

    def half(a, axis):
        shape = list(a.shape)
        if axis is not None:
            shape[axis] //= 2
        return jax.ShapeDtypeStruct(tuple(shape), a.dtype)

    return _pcall(
        body, name=name, out_shape=[half(a, ax) for a, ax in zip(arrays, axes)], in_specs=[_HBM] * n,
        out_specs=[_HBM] * n,
        scratch_shapes=[pltpu.SemaphoreType.DMA((n,)), pltpu.SemaphoreType.DMA((n,))],
    )(*arrays)


def _mm(name, a, b, *, nt=False, out_dtype=F32, add=None, tm=1024, tn=1024, tk=4096):
    m, k = a.shape
    n = b.shape[0] if nt else b.shape[1]
    assert (b.shape[1] if nt else b.shape[0]) == k
    has_add = add is not None
    tm, tn = _pick(m, tm), _pick(n, tn)
    out_bytes = jnp.dtype(out_dtype).itemsize

    def vmem_bytes(tk_):
        steps = k // tk_
        return (4 * (tm + tn) * tk_ + 2 * tm * tn * out_bytes + (8 * tm * tn if has_add else 0)
                + (4 * tm * tn if steps > 1 else 0))

    tk = _pick(k, tk)
    while vmem_bytes(tk) > MM_VMEM_BUDGET and tk > 512:
        tk = _pick(k, tk - LANES)
    nk = k // tk

    def body(*refs):
        a_ref, b_ref = refs[0], refs[1]
        c_ref = refs[2] if has_add else None
        o_ref = refs[2 + has_add]
        p = (_dot_nt if nt else _dot)(a_ref[...], b_ref[...])
        if nk == 1:
            o_ref[...] = (p + c_ref[...] if has_add else p).astype(o_ref.dtype)
            return
        acc = refs[3 + has_add]
        kk = pl.program_id(2)

        @pl.when(kk == 0)
        def _():
            acc[...] = p

        @pl.when(jnp.logical_and(kk > 0, kk < nk - 1))
        def _():
            acc[...] += p

        @pl.when(kk == nk - 1)
        def _():
            r = acc[...] + p
            if has_add:
                r = r + c_ref[...]
            o_ref[...] = r.astype(o_ref.dtype)

    a_spec = pl.BlockSpec((tm, tk), lambda j, i, kk: (i, kk))
    if nt:
        b_spec = pl.BlockSpec((tn, tk), lambda j, i, kk: (j, kk))
    else:
        b_spec = pl.BlockSpec((tk, tn), lambda j, i, kk: (kk, j))
    o_spec = pl.BlockSpec((tm, tn), lambda j, i, kk: (i, j))
    in_specs = [a_spec, b_spec] + ([o_spec] if has_add else [])
    args = (a, b) + ((add,) if has_add else ())
    return _pcall(
        body, name=name, grid=(n // tn, m // tm, nk),
        out_shape=jax.ShapeDtypeStruct((m, n), out_dtype),
        in_specs=in_specs, out_specs=o_spec,
        scratch_shapes=[pltpu.VMEM((tm, tn), F32)] if nk > 1 else [],
        compiler_params=_params(("parallel", "parallel", "arbitrary")),
    )(*args)


class Col:
    def __init__(self, arr, w=None, cb=0, lead=None):
        self.arr, self.cb, self.lead = arr, cb, lead
        self.w = arr.shape[-1] if w is None else w
        self.rows = arr.shape[-2]


def _ew(name, fn, *, tr, ins, consts=(), outs=(), accs=(), halo_prev=(), halo_next=()):
    ins = [c if isinstance(c, Col) else Col(c) for c in ins]
    halo_prev = [c if isinstance(c, Col) else Col(c) for c in halo_prev]
    halo_next = [c if isinstance(c, Col) else Col(c) for c in halo_next]
    rows = ins[0].rows
    tr = _rows_tile(rows, tr)
    nt = rows // tr
    n_in, n_hp, n_hn, n_c, n_o, n_a = len(ins), len(halo_prev), len(halo_next), len(consts), len(outs), len(accs)
    groups = tr // SUBLANES

    def spec(col, kind):
        if kind == "cur":
            shape, idx = (tr, col.w), (lambda i, cb=col.cb: (i, cb))
        elif kind == "prev":
            shape, idx = (SUBLANES, col.w), (lambda i, cb=col.cb: (jnp.maximum(i * groups - 1, 0), cb))
        else:
            shape = (SUBLANES, col.w)
            idx = (lambda i, cb=col.cb: (jnp.minimum((i + 1) * groups, rows // SUBLANES - 1), cb))
        if col.lead is None:
            return pl.BlockSpec(shape, idx)
        return pl.BlockSpec((None,) + shape, lambda i, idx=idx, lead=col.lead: (lead,) + idx(i))

    def body(*refs):
        i = pl.program_id(0)
        p = 0
        tiles = [r[...] for r in refs[p:p + n_in]]; p += n_in
        prev8 = [r[...] for r in refs[p:p + n_hp]]; p += n_hp
        next8 = [r[...] for r in refs[p:p + n_hn]]; p += n_hn
        cvals = [r[...] for r in refs[p:p + n_c]]; p += n_c
        out_refs = refs[p:p + n_o]; p += n_o
        acc_refs = refs[p:p + n_a]
        out_v, acc_v = fn(i, nt, tiles, prev8, next8, cvals)
        for r, v in zip(out_refs, out_v):
            r[...] = v.astype(r.dtype)
        if n_a:
            @pl.when(i == 0)
            def _():
                for r, v in zip(acc_refs, acc_v):
                    r[...] = v

            @pl.when(i > 0)
            def _():
                for r, v in zip(acc_refs, acc_v):
                    r[...] += v

    in_specs = ([spec(c, "cur") for c in ins] + [spec(c, "prev") for c in halo_prev]
                + [spec(c, "next") for c in halo_next]
                + [pl.BlockSpec(c.shape, lambda i, nd=c.ndim: (0,) * nd) for c in consts])
    out_specs = ([pl.BlockSpec((tr, w), lambda i: (i, 0)) for w, _ in outs]
                 + [pl.BlockSpec(s, lambda i: (0, 0)) for s in accs])
    out_shape = ([jax.ShapeDtypeStruct((rows, w), dt) for w, dt in outs]
                 + [jax.ShapeDtypeStruct(s, F32) for s in accs])
    args = [c.arr for c in ins] + [c.arr for c in halo_prev] + [c.arr for c in halo_next] + list(consts)
    res = _pcall(body, name=name, grid=(nt,), out_shape=out_shape, in_specs=in_specs, out_specs=out_specs,
                 compiler_params=_params(("arbitrary",)))(*args)
    return res[:n_o], res[n_o:]


def _colsum(v):
    return jnp.sum(v, axis=0, keepdims=True)


def _heads_of(w):
    return w // HEAD_DIM


def _per_head(fn, *arrays):
    nh = _heads_of(arrays[0].shape[1])
    res = [fn(*[a[:, h * HEAD_DIM:(h + 1) * HEAD_DIM] for a in arrays]) for h in range(nh)]
    if isinstance(res[0], tuple):
        return tuple(jnp.concatenate([r[j] for r in res], axis=1) for j in range(len(res[0])))
    return jnp.concatenate(res, axis=1)


def _head_sum(v):
    nh = _heads_of(v.shape[1])
    out = v[:, :HEAD_DIM]
    for h in range(1, nh):
        out = out + v[:, h * HEAD_DIM:(h + 1) * HEAD_DIM]
    return out


def _rms_fwd(x, w):
    r = lax.rsqrt(jnp.mean(x * x, axis=1, keepdims=True) + EPS)
    return x * r * w


def _rms_bwd(x, w, dy):
    r = lax.rsqrt(jnp.mean(x * x, axis=1, keepdims=True) + EPS)
    xh = x * r
    dxh = dy * w
    dx = r * (dxh - xh * jnp.mean(dxh * xh, axis=1, keepdims=True))
    return dx, dy * xh


def _silu(x):
    return x * _sigmoid(x)


def _dsilu(x):
    s = _sigmoid(x)
    return s * (1.0 + x * (1.0 - s))


SB_BQ = 512
SB_CUTOFF = 112.0
SB_PAIR = 2
SB_BK = 256


def _softplus_pos(z):
    return jnp.maximum(z, 0.0) + jnp.log(1.0 + jnp.exp(-jnp.abs(z)))


def _split_dot(v, tri):
    top = lax.bitcast_convert_type(lax.bitcast_convert_type(v, jnp.int32) & jnp.int32(-65536), F32)
    return _dot(top.astype(BF16), tri) + _dot((v - top).astype(BF16), tri)


def _sb_fwd(qn, kn, vb, *, bq=SB_BQ, bk=SB_BK):
    s_len, hd = qn.shape
    nh = hd // HEAD_DIM
    bk = min(bk, s_len)
    bq = min(bq, s_len)
    ndiag = bq // bk
    scale = HEAD_DIM ** -0.5

    def body(q_ref, k_ref, v_ref, o_ref):
        i = pl.program_id(1)
        krow = lax.broadcasted_iota(jnp.int32, (bk, bk), 0)
        kcol = lax.broadcasted_iota(jnp.int32, (bk, bk), 1)
        later = (krow > kcol).astype(BF16)
        row = lax.broadcasted_iota(jnp.int32, (bq, bk), 0)
        col = lax.broadcasted_iota(jnp.int32, (bq, bk), 1)
        q = q_ref[...]

        def tiles(js, carry, diags):
            run, acc = carry
            ks, vs, zs = [], [], []
            for j in js:
                off = pl.multiple_of(j * bk, bk)
                ks.append(k_ref[pl.ds(off, bk), :])
                vs.append(v_ref[pl.ds(off, bk), :])
                zs.append(_dot_nt(q, ks[-1]) * scale)
            sps, cums, masks = [], [], []
            for z, diag in zip(zs, diags):
                sp = _softplus_pos(z)
                causal = None
                if diag is not None:
                    causal = col + diag * bk < row
                    sp = jnp.where(causal, sp, 0.0)
                sps.append(sp)
                masks.append(causal)
                cums.append(_split_dot(sp, later))
            for z, sp, cum, causal, v in zip(zs, sps, cums, masks, vs):
                w = jnp.exp((z - sp) - (cum + run))
                if causal is not None:
                    w = jnp.where(causal, w, 0.0)
                acc = acc + _dot(w.astype(BF16), v)
                run = run + cum[:, 0:1] + sp[:, 0:1]
            return run, acc

        carry = (jnp.zeros((bq, 1), F32), jnp.zeros((bq, HEAD_DIM), F32))
        for dg in reversed(range(0, ndiag, SB_PAIR)):
            dgs = list(reversed(range(dg, dg + SB_PAIR)))
            carry = tiles([i * ndiag + g for g in dgs], carry, dgs)
        n_pairs = i * ndiag // SB_PAIR

        def more(st):
            return jnp.logical_and(st[0] < n_pairs, jnp.min(st[1]) < SB_CUTOFF)

        def step(st):
            t, run, acc = st
            run, acc = tiles([i * ndiag - 1 - SB_PAIR * t - u for u in range(SB_PAIR)], (run, acc), [None] * SB_PAIR)
            return t + 1, run, acc

        _, _, acc = lax.while_loop(more, step, (jnp.int32(0),) + carry)
        o_ref[...] = acc.astype(o_ref.dtype)

    qspec = pl.BlockSpec((bq, HEAD_DIM), lambda h, i: (i, h))
    kspec = pl.BlockSpec((s_len, HEAD_DIM), lambda h, i: (0, h))
    return _pcall(
        body, name="sb_fwd", grid=(nh, s_len // bq),
        out_shape=jax.ShapeDtypeStruct((s_len, hd), BF16),
        in_specs=[qspec, kspec, kspec], out_specs=qspec,
        compiler_params=_params(("parallel", "arbitrary")),
    )(qn, kn, vb)


def _sb_bwd(qn, kn, vb, do, *, bq=SB_BQ, bk=SB_BK):
    s_len, hd = qn.shape
    nh = hd // HEAD_DIM
    bk = min(bk, s_len)
    bq = min(bq, s_len)
    ndiag = bq // bk
    scale = HEAD_DIM ** -0.5

    def body(q_ref, k_ref, v_ref, do_ref, dq_ref, dk_ref, dv_ref):
        i = pl.program_id(1)

        @pl.when(i == 0)
        def _():
            dk_ref[...] = jnp.zeros_like(dk_ref)
            dv_ref[...] = jnp.zeros_like(dv_ref)

        krow = lax.broadcasted_iota(jnp.int32, (bk, bk), 0)
        kcol = lax.broadcasted_iota(jnp.int32, (bk, bk), 1)
        upto = (krow <= kcol).astype(BF16)
        before = (krow < kcol).astype(BF16)
        row = lax.broadcasted_iota(jnp.int32, (bq, bk), 0)
        col = lax.broadcasted_iota(jnp.int32, (bq, bk), 1)
        q = q_ref[...]
        do_t = do_ref[...]
        ones = jnp.ones((bk, LANES), BF16)
        n_pairs = i * ndiag // SB_PAIR

        def row_sums(js, diags):
            zs = [_dot_nt(q, k_ref[pl.ds(pl.multiple_of(j * bk, bk), bk), :]) * scale for j in js]
            tot = None
            for z, diag in zip(zs, diags):
                sp = _softplus_pos(z)
                if diag is not None:
                    sp = jnp.where(col + diag * bk < row, sp, 0.0)
                part = _split_dot(sp, ones)[:, 0:1]
                tot = part if tot is None else tot + part
            return tot

        def more(st):
            return jnp.logical_and(st[0] < n_pairs, jnp.min(st[1]) < SB_CUTOFF)

        def widen(st):
            t, run = st
            js = [i * ndiag - 1 - SB_PAIR * t - u for u in range(SB_PAIR)]
            return t + 1, run + row_sums(js, [None] * SB_PAIR)

        diag_all = list(range(ndiag))
        used, lt = lax.while_loop(more, widen, (jnp.int32(0), row_sums([i * ndiag + g for g in diag_all], diag_all)))

        def tiles(js, carry, diags):
            pre, ecar, dq = carry
            offs, ks, zs, dws = [], [], [], []
            for j in js:
                off = pl.multiple_of(j * bk, bk)
                offs.append(off)
                ks.append(k_ref[pl.ds(off, bk), :])
                zs.append(_dot_nt(q, ks[-1]) * scale)
                dws.append(_dot_nt(do_t, v_ref[pl.ds(off, bk), :]))
            sps, cums, masks = [], [], []
            for z, diag in zip(zs, diags):
                sp = _softplus_pos(z)
                causal = None
                if diag is not None:
                    causal = col + diag * bk < row
                    sp = jnp.where(causal, sp, 0.0)
                sps.append(sp)
                masks.append(causal)
                cums.append(_split_dot(sp, upto))
            es, ebs, exs, sigs = [], [], [], []
            for off, z, sp, cum, dw, causal in zip(offs, zs, sps, cums, dws, masks):
                lb = z - sp
                w = jnp.exp(lb - (lt - (pre + cum)))
                if causal is not None:
                    w = jnp.where(causal, w, 0.0)
                dv_ref[pl.ds(off, bk), :] += _dot_tn(w.astype(BF16), do_t)
                e = dw * w
                eb = e.astype(BF16)
                es.append(e)
                ebs.append(eb)
                exs.append(_dot(eb, before))
                sigs.append(jnp.exp(lb))
                pre = pre + cum[:, bk - 1:bk]
            for off, k, e, eb, exm, sig, causal in zip(offs, ks, es, ebs, exs, sigs, masks):
                ex = exm + ecar
                dz = (e - sig * (e + ex)) * scale
                if causal is not None:
                    dz = jnp.where(causal, dz, 0.0)
                dzb = dz.astype(BF16)
                dk_ref[pl.ds(off, bk), :] += _dot_tn(dzb, q)
                dq = dq + _dot(dzb, k)
                ecar = ex[:, bk - 1:bk] + eb[:, bk - 1:bk].astype(F32)
            return pre, ecar, dq

        init = (jnp.zeros((bq, 1), F32), jnp.zeros((bq, 1), F32), jnp.zeros((bq, HEAD_DIM), F32))
        carry = lax.fori_loop(
            n_pairs - used, n_pairs,
            lambda t, cr: tiles([SB_PAIR * t + u for u in range(SB_PAIR)], cr, [None] * SB_PAIR), init)
        for dg in range(0, ndiag, SB_PAIR):
            dgs = list(range(dg, dg + SB_PAIR))
            carry = tiles([i * ndiag + g for g in dgs], carry, dgs)
        dq_ref[...] = carry[2]

    qspec = pl.BlockSpec((bq, HEAD_DIM), lambda h, i: (i, h))
    kspec = pl.BlockSpec((s_len, HEAD_DIM), lambda h, i: (0, h))
    return _pcall(
        body, name="sb_bwd", grid=(nh, s_len // bq),
        out_shape=[jax.ShapeDtypeStruct((s_len, hd), F32)] * 3,
        in_specs=[qspec, kspec, kspec, qspec],
        out_specs=[qspec, kspec, kspec],
        compiler_params=_params(("parallel", "arbitrary")),
    )(qn, kn, vb, do)


GDN_GROUP = 16


def _gdn_group(nh):
    return min(GDN_GROUP, nh)


def _gdn_chunk_terms(qh, kh, vh, g_r, g_c, b_c):
    c = GDN_CHUNK
    r = lax.broadcasted_iota(jnp.int32, (c, c), 0)
    s = lax.broadcasted_iota(jnp.int32, (c, c), 1)
    tril, stril = r >= s, r > s
    gcc = jnp.sum(jnp.where(tril, g_r, 0.0), axis=1, keepdims=True)
    gcr = jnp.sum(jnp.where(r <= s, g_c, 0.0), axis=0, keepdims=True)
    dm = jnp.where(tril, jnp.exp(jnp.where(tril, gcc - gcr, 0.0)), 0.0)
    kb = kh.astype(BF16)
    kk = _dot_nt(kb, kb)
    qk = _dot_nt(qh.astype(BF16), kb)
    egc = jnp.exp(gcc)
    gcl = gcc[c - 1:c, :]
    t = dict(tril=tril, stril=stril, gcc=gcc, dm=dm, kb=kb, kk=kk, qk=qk, egc=egc,
             ekd=jnp.exp(gcl - gcc), gl=jnp.exp(gcl),
             a=jnp.where(stril, b_c * kk * dm, 0.0),
             bv=b_c * vh, bk=(b_c * egc) * kh, at=jnp.where(tril, qk * dm, 0.0))
    t["qg"] = qh * egc
    t["kd"] = kh * t["ekd"]
    return t


def _unit_lower_inverses(mats):
    c = GDN_CHUNK
    r = lax.broadcasted_iota(jnp.int32, (c, c), 0)
    s = lax.broadcasted_iota(jnp.int32, (c, c), 1)
    eye = (r == s).astype(F32)
    ps = [-a for a in mats]
    ts = [eye + p for p in ps]
    span = 2
    while span < c:
        ps = [_dot(p, p, hi=HIGH) for p in ps]
        ts = [t + _dot(t, p, hi=HIGH) for t, p in zip(ts, ps)]
        span *= 2
    return ts


def _when_step(h, n, hs, ns):
    return pl.when(jnp.logical_and(pl.program_id(0) == (hs if h < 0 else h), pl.program_id(1) == (ns if n < 0 else n)))


def _gdn_fwd(q, k, v, g_col, g_row, b_col, b_row, shards, n_split):
    s_len, d = q.shape
    nh = d // HEAD_DIM
    c = GDN_CHUNK
    n_chunks = s_len // c
    grp = _gdn_group(nh)
    n_sh = len(shards)
    last_h, last_n = nh // grp - 1, n_chunks - 1

    def body(q_ref, k_ref, v_ref, gc_ref, gr_ref, bc_ref, br_ref, *rest):
        sh_refs, rest = rest[:n_sh], rest[n_sh:]
        o_ref, ss_ref, ts_ref = rest[:3]
        got_refs, rest = rest[3:3 + n_sh], rest[3 + n_sh:]
        st, sems = rest[0], rest[1:]
        n = pl.program_id(1)

        @_when_step(0, 0, last_h, last_n)
        def _():
            _GatherPlan(sh_refs, got_refs, sems, n_split).start()

        @_when_step(-1, n_chunks // 2, last_h, last_n)
        def _():
            _GatherPlan(sh_refs, got_refs, sems, n_split).relay()

        @_when_step(-1, -1, last_h, last_n)
        def _():
            _GatherPlan(sh_refs, got_refs, sems, n_split).finish()

        @pl.when(n == 0)
        def _():
            st[...] = jnp.zeros_like(st)

        heads = range(grp)
        sls = [slice(i * HEAD_DIM, (i + 1) * HEAD_DIM) for i in heads]
        terms = [_gdn_chunk_terms(q_ref[:, sls[i]], k_ref[:, sls[i]], v_ref[:, sls[i]],
                                  gr_ref[i:i + 1, :], gc_ref[:, i:i + 1], bc_ref[:, i:i + 1]) for i in heads]
        tinvs = _unit_lower_inverses([t["a"] for t in terms])
        wvs = [_dot(tinv, t["bv"], hi=HIGH) for tinv, t in zip(tinvs, terms)]
        wks = [_dot(tinv, t["bk"], hi=HIGH) for tinv, t in zip(tinvs, terms)]
        states = [st[i] for i in heads]
        sbs = [state.astype(BF16) for state in states]
        ubs = [(wv - _dot(wk.astype(BF16), sb)).astype(BF16) for wv, wk, sb in zip(wvs, wks, sbs)]
        for i in heads:
            t = terms[i]
            o_ref[:, sls[i]] = _dot(t["qg"].astype(BF16), sbs[i]) + _dot(t["at"].astype(BF16), ubs[i])
            ss_ref[i] = states[i]
            ts_ref[i] = tinvs[i]
            st[i] = t["gl"] * states[i] + _dot_tn(t["kd"].astype(BF16), ubs[i])

    tok = pl.BlockSpec((c, grp * HEAD_DIM), lambda h, n: (n, h))
    colspec = pl.BlockSpec((None, c, grp), lambda h, n: (h, n, 0))
    rowspec = pl.BlockSpec((None, None, grp, c), lambda h, n: (h, n, 0, 0))
    res = _pcall(
        body, name="gdn_fwd", grid=(nh // grp, n_chunks),
        out_shape=[jax.ShapeDtypeStruct((s_len, d), F32),
                   jax.ShapeDtypeStruct((n_chunks, nh, HEAD_DIM, HEAD_DIM), F32),
                   jax.ShapeDtypeStruct((n_chunks, nh, c, c), F32)] + _GatherPlan.out_shapes(shards),
        in_specs=[tok, tok, tok, colspec, rowspec, colspec, rowspec] + [_HBM] * n_sh,
        out_specs=[tok, pl.BlockSpec((None, grp, HEAD_DIM, HEAD_DIM), lambda h, n: (n, h, 0, 0)),
                   pl.BlockSpec((None, grp, c, c), lambda h, n: (n, h, 0, 0))] + [_HBM] * n_sh,
        scratch_shapes=[pltpu.VMEM((grp, HEAD_DIM, HEAD_DIM), F32)] + _GatherPlan.sem_shapes(n_sh),
        compiler_params=_params(("arbitrary", "arbitrary")),
    )(q, k, v, g_col, g_row, b_col, b_row, *shards)
    return res[0], res[1], res[2], res[3:]


def _gdn_bwd(q, k, v, g_col, g_row, b_col, b_row, states, tinvs, do, partials):
    s_len, d = q.shape
    nh = d // HEAD_DIM
    c = GDN_CHUNK
    n_chunks = s_len // c
    grp = _gdn_group(nh)
    n_p = len(partials)
    last_h, last_n = nh // grp - 1, n_chunks - 1

    def body(q_ref, k_ref, v_ref, gc_ref, gr_ref, bc_ref, br_ref, ss_ref, ts_ref, do_ref, *rest):
        part_refs, rest = rest[:n_p], rest[n_p:]
        dq_ref, dk_ref, dv_ref, dgb_ref = rest[:4]
        land_refs, rest = rest[4:4 + n_p], rest[4 + n_p:]
        dst, sems = rest[0], rest[1:]
        n = pl.program_id(1)

        @_when_step(0, 0, last_h, last_n)
        def _():
            _ScatterPlan(part_refs, land_refs, sems).start()

        @_when_step(-1, -1, last_h, last_n)
        def _():
            _ScatterPlan(part_refs, land_refs, sems).finish()

        @pl.when(n == 0)
        def _():
            dst[...] = jnp.zeros_like(dst)

        r = lax.broadcasted_iota(jnp.int32, (c, c), 0)
        s = lax.broadcasted_iota(jnp.int32, (c, c), 1)
        suffix = (r <= s).astype(F32)
        lane = lax.broadcasted_iota(jnp.int32, (c, LANES), 1)
        heads = range(grp)
        sls = [slice(i * HEAD_DIM, (i + 1) * HEAD_DIM) for i in heads]
        qs = [q_ref[:, sl] for sl in sls]
        ks = [k_ref[:, sl] for sl in sls]
        vs = [v_ref[:, sl] for sl in sls]
        bcs = [bc_ref[:, i:i + 1] for i in heads]
        ts = [_gdn_chunk_terms(qs[i], ks[i], vs[i], gr_ref[i:i + 1, :], gc_ref[:, i:i + 1], bcs[i]) for i in heads]
        tinv = [ts_ref[i] for i in heads]
        state = [ss_ref[i] for i in heads]
        sb = [x.astype(BF16) for x in state]
        dnext = [dst[i] for i in heads]
        dnb = [x.astype(BF16) for x in dnext]
        dob = [do_ref[:, sl].astype(BF16) for sl in sls]
        wv = [_dot(tinv[i], ts[i]["bv"], hi=HIGH) for i in heads]
        wk = [_dot(tinv[i], ts[i]["bk"], hi=HIGH) for i in heads]
        wkb = [x.astype(BF16) for x in wk]
        ub = [(wv[i] - _dot(wkb[i], sb[i])).astype(BF16) for i in heads]
        du = [_dot_tn(ts[i]["at"].astype(BF16), dob[i]) + _dot(ts[i]["kd"].astype(BF16), dnb[i]) for i in heads]
        dub = [x.astype(BF16) for x in du]
        dat = [jnp.where(ts[i]["tril"], _dot_nt(dob[i], ub[i]), 0.0) for i in heads]
        dqg = [_dot_nt(dob[i], sb[i]) for i in heads]
        dkd = [_dot_nt(ub[i], dnb[i]) for i in heads]
        dwk = [-_dot_nt(dub[i], sb[i]) for i in heads]
        for i in heads:
            dst[i] = (ts[i]["gl"] * dnext[i] + _dot_tn(ts[i]["qg"].astype(BF16), dob[i]) - _dot_tn(wkb[i], dub[i]))
        dbv = [_dot_tn(tinv[i], du[i], hi=HIGH) for i in heads]
        dbk = [_dot_tn(tinv[i], dwk[i], hi=HIGH) for i in heads]
        dtm = [_dot_nt(du[i], ts[i]["bv"], hi=HIGH) + _dot_nt(dwk[i], ts[i]["bk"], hi=HIGH) for i in heads]
        dtt = [_dot_nt(dtm[i], tinv[i], hi=HIGH) for i in heads]
        da = [-jnp.where(ts[i]["stril"], _dot_tn(tinv[i], dtt[i], hi=HIGH), 0.0) for i in heads]
        rs = lambda m: jnp.sum(m, axis=1, keepdims=True)
        dgb = jnp.zeros((c, LANES), F32)
        for i in heads:
            t, b_c, dm, kb = ts[i], bcs[i], ts[i]["dm"], ts[i]["kb"]
            egc, ekd = t["egc"], t["ekd"]
            dkk = da[i] * b_c * dm
            ddm = da[i] * b_c * t["kk"] + dat[i] * t["qk"]
            dqkb, dkkb = (dat[i] * dm).astype(BF16), dkk.astype(BF16)
            dq_ref[:, sls[i]] = _dot(dqkb, kb) + dqg[i] * egc
            dk_ref[:, sls[i]] = (_dot_tn(dqkb, qs[i].astype(BF16)) + _dot(dkkb, kb) + _dot_tn(dkkb, kb)
                                 + dbk[i] * (b_c * egc) + dkd[i] * ekd)
            dv_ref[:, sls[i]] = dbv[i] * b_c
            dbk_k = rs(dbk[i] * ks[i])
            dbeta = rs(da[i] * t["kk"] * dm) + rs(dbv[i] * vs[i]) + dbk_k * egc
            mx = ddm * dm
            ekd_sum = rs(dkd[i] * ks[i]) * ekd
            dgc = rs(mx) + dbk_k * b_c * egc + rs(dqg[i] * qs[i]) * egc - ekd_sum
            dgl = jnp.sum(rs(dnext[i] * state[i]), axis=0, keepdims=True)
            tail = jnp.sum(ekd_sum, axis=0, keepdims=True) + dgl * t["gl"]
            dg = (_dot(suffix, jnp.broadcast_to(dgc, (c, LANES)), hi=HIGH)[:, 0:1]
                  - rs(_dot_nt(suffix, mx, hi=HIGH)) + tail)
            dgb = dgb + jnp.where(lane == i, dbeta, 0.0) + jnp.where(lane == grp + i, dg, 0.0)
        dgb_ref[...] = dgb

    last = n_chunks - 1
    tok = pl.BlockSpec((c, grp * HEAD_DIM), lambda h, n: (last - n, h))
    colspec = pl.BlockSpec((None, c, grp), lambda h, n: (h, last - n, 0))
    rowspec = pl.BlockSpec((None, None, grp, c), lambda h, n: (h, last - n, 0, 0))
    res = _pcall(
        body, name="gdn_bwd", grid=(nh // grp, n_chunks),
        out_shape=[jax.ShapeDtypeStruct((s_len, d), F32)] * 3
        + [jax.ShapeDtypeStruct((nh // grp, s_len, LANES), F32)] + _ScatterPlan.out_shapes(partials),
        in_specs=[tok, tok, tok, colspec, rowspec, colspec, rowspec,
                  pl.BlockSpec((None, grp, HEAD_DIM, HEAD_DIM), lambda h, n: (last - n, h, 0, 0)),
                  pl.BlockSpec((None, grp, c, c), lambda h, n: (last - n, h, 0, 0)), tok] + [_HBM] * n_p,
        out_specs=[tok, tok, tok, pl.BlockSpec((None, c, LANES), lambda h, n: (h, last - n, 0))] + [_HBM] * n_p,
        scratch_shapes=[pltpu.VMEM((grp, HEAD_DIM, HEAD_DIM), F32)] + _ScatterPlan.sem_shapes(n_p),
        compiler_params=_params(("arbitrary", "arbitrary")),
    )(q, k, v, g_col, g_row, b_col, b_row, states, tinvs, do, *partials)
    return res[0], res[1], res[2], res[3], res[4:]


def _shift_down(prev8, cur, k):
    if k == 0:
        return cur
    ext = jnp.concatenate([prev8, cur], axis=0)
    return pltpu.roll(ext, k, 0)[SUBLANES:, :]


def _shift_up(cur, next8, k):
    if k == 0:
        return cur
    ext = jnp.concatenate([cur, next8], axis=0)
    n = ext.shape[0]
    return pltpu.roll(ext, n - k, 0)[:cur.shape[0], :]


def _conv_pre(i, x, prev8, w):
    prev8 = jnp.where(i == 0, 0.0, prev8)
    pre = None
    for j in range(GDN_CONV):
        term = w[j:j + 1, :] * _shift_down(prev8, x, GDN_CONV - 1 - j)
        pre = term if pre is None else pre + term
    return pre, prev8


def _l2_fwd(a, mult):
    return a * (lax.rsqrt(jnp.sum(a * a, axis=1, keepdims=True) + EPS) * mult)


def _l2_bwd(a, dy, mult):
    r = lax.rsqrt(jnp.sum(a * a, axis=1, keepdims=True) + EPS)
    dy = dy * mult
    return r * dy - a * (r * r * r) * jnp.sum(a * dy, axis=1, keepdims=True)


def _conv_fwd(xb, conv_w, group, *, norm, mult, tr=256):
    d = xb.shape[1] // 3

    def fn(i, nt, tiles, prev8, next8, cv):
        pre, _ = _conv_pre(i, tiles[0], prev8[0], cv[0])
        a = _silu(pre)
        if norm:
            a = _per_head(lambda ah: _l2_fwd(ah, mult), a)
        return [a], []

    col = Col(xb, d, group)
    wg = lax.slice_in_dim(conv_w, group * d, (group + 1) * d, axis=1)
    (y,), _ = _ew(f"conv_fwd{group}", fn, tr=tr, ins=[col], halo_prev=[col], consts=[wg], outs=[(d, F32)])
    return y


def _conv_bwd(xb, conv_w, group, dy, *, norm, mult, tr=256):
    d = xb.shape[1] // 3
    col = Col(xb, d, group)
    wg = lax.slice_in_dim(conv_w, group * d, (group + 1) * d, axis=1)

    def fn_pre(i, nt, tiles, prev8, next8, cv):
        x, dyt = tiles
        pre, p8 = _conv_pre(i, x, prev8[0], cv[0])
        if norm:
            da = _per_head(lambda ah, dh: _l2_bwd(ah, dh, mult), _silu(pre), dyt)
        else:
            da = dyt
        dpre = da * _dsilu(pre)
        tap = lax.broadcasted_iota(jnp.int32, (GDN_CONV, d), 0)
        dw = jnp.zeros((GDN_CONV, d), F32)
        for j in range(GDN_CONV):
            dw = dw + jnp.where(tap == j, _colsum(dpre * _shift_down(p8, x, GDN_CONV - 1 - j)), 0.0)
        return [dpre], [dw]

    (dpre,), (dw,) = _ew(f"conv_bwd_pre{group}", fn_pre, tr=tr, ins=[col, dy], halo_prev=[col], consts=[wg],
                         outs=[(d, F32)], accs=[(GDN_CONV, d)])

    def fn_dx(i, nt, tiles, prev8, next8, cv):
        n8 = jnp.where(i == nt - 1, 0.0, next8[0])
        dx = None
        for j in range(GDN_CONV):
            term = cv[0][j:j + 1, :] * _shift_up(tiles[0], n8, GDN_CONV - 1 - j)
            dx = term if dx is None else dx + term
        return [dx], []

    (dx,), _ = _ew(f"conv_bwd_dx{group}", fn_dx, tr=tr, ins=[dpre], halo_next=[dpre], consts=[wg], outs=[(d, BF16)])
    return dx, dw


def _adamw(name, w, m, v, grads, *, tr=64):
    shape = w.shape
    w2, m2, v2 = [a.reshape(-1, shape[-1]) for a in (w, m, v)]
    n_g = len(grads)
    bc1 = 1.0 - ADAM_B1 ** ADAM_STEP
    bc2 = 1.0 - ADAM_B2 ** ADAM_STEP

    def fn(i, nt, tiles, prev8, next8, cv):
        wt, mt, vt = tiles[:3]
        g = tiles[3]
        for extra in tiles[4:]:
            g = g + extra
        mn = ADAM_B1 * mt + (1.0 - ADAM_B1) * g
        vn = ADAM_B2 * vt + (1.0 - ADAM_B2) * (g * g)
        delta = -ADAM_LR * ((mn / bc1) / (jnp.sqrt(vn / bc2) + ADAM_EPS) + ADAM_WD * wt)
        return [g, delta, mn, vn], []

    width = shape[-1]
    outs, _ = _ew(name, fn, tr=tr, ins=[w2, m2, v2] + list(grads), outs=[(width, F32)] * 4)
    assert n_g >= 1
    return tuple(o.reshape(shape) for o in outs)


def _pad_cols(a, width):
    return jnp.pad(a, ((0, 0), (0, width - a.shape[1])))


def _gdn_layouts(gbeta, nh, n_chunks):
    grp = _gdn_group(nh)
    s_len = gbeta.shape[0]

    def lay(a):
        col = a.reshape(s_len, nh // grp, grp).transpose(1, 0, 2)
        row = a.reshape(n_chunks, GDN_CHUNK, nh // grp, grp).transpose(2, 0, 3, 1)
        return col, row

    b_col, b_row = lay(gbeta[:, :nh])
    g_col, g_row = lay(gbeta[:, nh:2 * nh])
    return g_col, g_row, b_col, b_row


def kernel(x, c, w_mod, b_mod, norm1_w, w_in, q_norm_w, k_norm_w, conv_w, a_log, dt_bias, o_norm_w, p_a, p_b, w_out, norm2_w, w_gate, w_up, w_down, loss_target, m_w_mod, m_b_mod, m_norm1_w, m_w_in, m_q_norm_w, m_k_norm_w, m_conv_w, m_a_log, m_dt_bias, m_o_norm_w, m_p_a, m_p_b, m_w_out, m_norm2_w, m_w_gate, m_w_up, m_w_down, v_w_mod, v_b_mod, v_norm1_w, v_w_in, v_q_norm_w, v_k_norm_w, v_conv_w, v_a_log, v_dt_bias, v_o_norm_w, v_p_a, v_p_b, v_w_out, v_norm2_w, v_w_gate, v_w_up, v_w_down):
    s_len, d = x.shape[1], x.shape[2]
    nh = d // HEAD_DIM
    n_chunks = s_len // GDN_CHUNK
    ff = 4 * w_gate.shape[2]
    mx, my, mc = _my_pos()
    chip = 2 * mx + my
    dev = 2 * chip + mc
    x2 = x[0]
    tgt = loss_target[0]

    c_all = _allgather8("ag_c", _pad_cols(c, d).reshape(SUBLANES, d // SUBLANES)).reshape(8, d)
    wm = w_mod[0]
    mod_w = wm.shape[1]
    bm_cols = lax.dynamic_slice_in_dim(b_mod, chip * mod_w, mod_w, axis=1)

    def mod_body(c_ref, w_ref, b_ref, o_ref, ca_ref):
        ca = _silu(c_ref[...])
        ca_ref[...] = ca
        o_ref[...] = _dot(ca, w_ref[...], hi=HIGHEST) + b_ref[...]

    tn_mod = _pick(mod_w, 512)
    mod8, c_act = _pcall(
        mod_body, name="mod_fwd", grid=(mod_w // tn_mod,),
        out_shape=[jax.ShapeDtypeStruct((8, mod_w), F32), jax.ShapeDtypeStruct((8, d), F32)],
        in_specs=[pl.BlockSpec((8, d), lambda j: (0, 0)), pl.BlockSpec((d, tn_mod), lambda j: (0, j)),
                  pl.BlockSpec((1, tn_mod), lambda j: (0, j))],
        out_specs=[pl.BlockSpec((8, tn_mod), lambda j: (0, j)), pl.BlockSpec((8, d), lambda j: (0, 0))],
        compiler_params=_params(("arbitrary",)),
    )(c_all, wm, bm_cols)
    mod_all = _allgather8("ag_mod", mod8)
    mod_me = mod_all.reshape(4, 2, 8, mod_w)[:, mc, dev, :].reshape(1, 6 * d)
    shift1, scale1, gate1, shift2, scale2, gate2 = [mod_me[:, j * d:(j + 1) * d] for j in range(6)]

    first = [w_in[0].astype(BF16), conv_w[0]]
    late = [p_a[0].astype(BF16), p_b[0].astype(BF16), w_out[0].astype(BF16), w_gate[0].astype(BF16),
            w_up[0].astype(BF16), w_down[0].astype(BF16)]
    w_in_g, conv_g = [_fill_slot(g, sh, chip) for g, sh in zip(_gather4("ag_w_in", first, n_split=1), first)]
    w_in_f = w_in_g.transpose(1, 0, 2).reshape(d, -1)
    wa = w_in_f[:, :3 * d]
    wb = w_in_f[:, 3 * d:6 * d]
    wzg = jnp.concatenate([w_in_f[:, 6 * d:7 * d], w_in_f[:, 7 * d + 2 * nh:]], axis=1)
    wba = _pad_cols(w_in_f[:, 7 * d:7 * d + 2 * nh], LANES)
    conv_f = conv_g.transpose(1, 0, 2).reshape(GDN_CONV, 3 * d)

    def norm_mod_fn(i, nt, tiles, prev8, next8, cv):
        w, sc, sh = cv
        return [_rms_fwd(tiles[0], w) * (1.0 + sc) + sh], []

    (u1,), _ = _ew("norm_mod1", norm_mod_fn, tr=512, ins=[x2], consts=[norm1_w, scale1, shift1], outs=[(d, BF16)])
    proj_a = _mm("proj_a", u1, wa)
    proj_b = _mm("proj_b", u1, wb)
    proj_zg = _mm("proj_zg", u1, wzg)
    proj_ba = _mm("proj_ba", u1, wba)

    def qknorm_fn(i, nt, tiles, prev8, next8, cv):
        qa, ka, va = tiles
        return [_per_head(lambda h: _rms_fwd(h, cv[0]), qa), _per_head(lambda h: _rms_fwd(h, cv[1]), ka), va], []

    (qn, kn, vb), _ = _ew("qknorm", qknorm_fn, tr=256,
                          ins=[Col(proj_a, d, 0), Col(proj_a, d, 1), Col(proj_a, d, 2)],
                          consts=[q_norm_w, k_norm_w], outs=[(d, BF16)] * 3)
    o_a = _sb_fwd(qn, kn, vb)

    lane_ids = jnp.arange(LANES)
    is_b = (lane_ids < nh)[None, :]
    is_a = ((lane_ids >= nh) & (lane_ids < 2 * nh))[None, :]
    alog128 = jnp.zeros((1, LANES), F32).at[:, nh:2 * nh].set(a_log)
    dtb128 = jnp.zeros((1, LANES), F32).at[:, nh:2 * nh].set(dt_bias)
    is_b_f, is_a_f = is_b.astype(F32), is_a.astype(F32)

    def gbeta_fn(i, nt, tiles, prev8, next8, cv):
        al, dtb, mb, ma = cv
        ba = tiles[0]
        g = -jnp.exp(al) * _softplus(ba + dtb)
        return [jnp.where(mb > 0.5, _sigmoid(ba), jnp.where(ma > 0.5, g, 0.0))], []

    (gbeta,), _ = _ew("gbeta", gbeta_fn, tr=1024, ins=[proj_ba], consts=[alog128, dtb128, is_b_f, is_a_f],
                      outs=[(LANES, F32)])
    g_col, g_row, b_col, b_row = _gdn_layouts(gbeta, nh, n_chunks)
    qscale = HEAD_DIM ** -0.5
    q_b = _conv_fwd(proj_b, conv_f, 0, norm=True, mult=qscale)
    k_b = _conv_fwd(proj_b, conv_f, 1, norm=True, mult=1.0)
    v_b = _conv_fwd(proj_b, conv_f, 2, norm=False, mult=1.0)
    o_raw, states, tinvs, late_g = _gdn_fwd(q_b, k_b, v_b, g_col, g_row, b_col, b_row, late, n_split=len(late))
    late_g = [_fill_slot(g, sh, chip) for g, sh in zip(late_g, late)]
    p_a_f, p_b_f, w_out_f = [g.reshape(d, d) for g in late_g[0:3]]
    w_gate_f, w_up_f = [g.transpose(1, 0, 2).reshape(d, ff) for g in late_g[3:5]]
    w_down_f = late_g[5].reshape(ff, d)

    def gated_norm_fn(i, nt, tiles, prev8, next8, cv):
        o, z = tiles
        return [_per_head(lambda h: _rms_fwd(h, cv[0]), o) * _silu(z)], []

    (o_b,), _ = _ew("gated_norm", gated_norm_fn, tr=256, ins=[o_raw, Col(proj_zg, d, 0)], consts=[o_norm_w],
                    outs=[(d, BF16)])
    y_a = _mm("out_a", o_a, p_a_f)
    y_b = _mm("out_b", o_b, p_b_f)

    def merge_fn(i, nt, tiles, prev8, next8, cv):
        ya, yb, ga, gb = tiles
        return [_sigmoid(ga) * ya + _sigmoid(gb) * yb], []

    (merged,), _ = _ew("merge", merge_fn, tr=256, ins=[y_a, y_b, Col(proj_zg, d, 1), Col(proj_zg, d, 2)],
                       outs=[(d, BF16)])
    y_o = _mm("out_proj", merged, w_out_f)

    def resid_norm_fn(i, nt, tiles, prev8, next8, cv):
        xt, yo = tiles
        g1, w, sc, sh = cv
        h1 = xt + g1 * yo
        return [h1, _rms_fwd(h1, w) * (1.0 + sc) + sh], []

    (h1, u2), _ = _ew("resid_norm2", resid_norm_fn, tr=256, ins=[x2, y_o],
                      consts=[gate1, norm2_w, scale2, shift2], outs=[(d, F32), (d, BF16)])
    gt = _mm("ff_gate", u2, w_gate_f, out_dtype=BF16)
    up = _mm("ff_up", u2, w_up_f, out_dtype=BF16)

    def swiglu_fn(i, nt, tiles, prev8, next8, cv):
        return [_silu(tiles[0].astype(F32)) * tiles[1].astype(F32)], []

    (act,), _ = _ew("swiglu", swiglu_fn, tr=128, ins=[gt, up], outs=[(ff, BF16)])
    y_d = _mm("ff_down", act, w_down_f)

    def loss_fn(i, nt, tiles, prev8, next8, cv):
        h1t, yd, tg = tiles
        diff = h1t + cv[0] * yd - tg
        dy = diff * (1.0 / d)
        return [dy, dy * cv[0]], [_colsum(0.5 * diff * dy), _colsum(dy * yd)]

    (dy, dyd), (loss_cols, dgate2) = _ew("loss", loss_fn, tr=256, ins=[h1, y_d, tgt], consts=[gate2],
                                         outs=[(d, F32), (d, BF16)], accs=[(1, d), (1, d)])
    loss = lax.psum(jnp.sum(loss_cols), ("x", "y", "c"))

    dact = _mm("d_act", dyd, w_down_f, nt=True)
    g_w_down = _mm("g_w_down", act.T, dyd)

    def swiglu_bwd_fn(i, nt, tiles, prev8, next8, cv):
        da, g, u = tiles[0], tiles[1].astype(F32), tiles[2].astype(F32)
        return [da * u * _dsilu(g), da * _silu(g)], []

    (dgt, dup), _ = _ew("swiglu_bwd", swiglu_bwd_fn, tr=128, ins=[dact, gt, up], outs=[(ff, BF16)] * 2)
    du2 = _mm("d_u2_up", dup, w_up_f, nt=True, add=_mm("d_u2_gate", dgt, w_gate_f, nt=True))
    u2_t = u2.T
    g_w_gate = _mm("g_w_gate", u2_t, dgt)
    g_w_up = _mm("g_w_up", u2_t, dup)

    def norm2_bwd_fn(i, nt, tiles, prev8, next8, cv):
        h1t, du, dres, yo = tiles
        w, sc, g1 = cv
        r = lax.rsqrt(jnp.mean(h1t * h1t, axis=1, keepdims=True) + EPS)
        nrm = h1t * r
        dn = du * w * (1.0 + sc)
        dh = r * (dn - nrm * jnp.mean(dn * nrm, axis=1, keepdims=True)) + dres
        return [dh, dh * g1], [_colsum(du), _colsum(du * nrm * w), _colsum(du * nrm * (1.0 + sc)), _colsum(dh * yo)]

    (dh1, dyo), (dshift2, dscale2, g_norm2, dgate1) = _ew(
        "norm2_bwd", norm2_bwd_fn, tr=256, ins=[h1, du2, dy, y_o], consts=[norm2_w, scale2, gate1],
        outs=[(d, F32), (d, BF16)], accs=[(1, d)] * 4)

    dmerged = _mm("d_merged", dyo, w_out_f, nt=True)
    g_w_out = _mm("g_w_out", merged.T, dyo)

    def merge_bwd_fn(i, nt, tiles, prev8, next8, cv):
        dm, ya, yb, ga, gb = tiles
        sa, sb = _sigmoid(ga), _sigmoid(gb)
        return [dm * sa, dm * sb, dm * ya * sa * (1.0 - sa), dm * yb * sb * (1.0 - sb)], []

    (dya, dyb, dga, dgb_gate), _ = _ew(
        "merge_bwd", merge_bwd_fn, tr=256, ins=[dmerged, y_a, y_b, Col(proj_zg, d, 1), Col(proj_zg, d, 2)],
        outs=[(d, BF16)] * 4)
    do_a = _mm("d_o_a", dya, p_a_f, nt=True, out_dtype=BF16)
    g_p_a = _mm("g_p_a", o_a.T, dya)
    do_b = _mm("d_o_b", dyb, p_b_f, nt=True)
    g_p_b = _mm("g_p_b", o_b.T, dyb)

    def gated_norm_bwd_fn(i, nt, tiles, prev8, next8, cv):
        dob, o, z = tiles
        sz = _silu(z)

        def head(oh, dh):
            return _rms_bwd(oh, cv[0], dh)

        dxo, dwn = _per_head(head, o, dob * sz)
        nrm_w = _per_head(lambda h: _rms_fwd(h, cv[0]), o)
        return [dxo, dob * nrm_w * _dsilu(z)], [_colsum(_head_sum(dwn))]

    (do_raw, dz_b), (g_o_norm,) = _ew(
        "gated_norm_bwd", gated_norm_bwd_fn, tr=256, ins=[do_b, o_raw, Col(proj_zg, d, 0)], consts=[o_norm_w],
        outs=[(d, F32), (d, BF16)], accs=[(1, HEAD_DIM)])
    by_chip = lambda a: a.reshape(a.shape[0], 4, -1).transpose(1, 0, 2)

    def chip_sums(tag, raw, axes):
        theirs = _sibling_send(f"swap_{tag}", raw, axes)
        sums = []
        for t, (part, ax, other) in enumerate(zip(raw, axes, theirs)):
            def pair_fn(i, nt, tiles, prev8, next8, cv):
                return [tiles[0] + tiles[1]], []

            hr, width = part.shape[ax] // 2, part.shape[-1]
            mine = lax.dynamic_slice_in_dim(part, mc * hr, hr, axis=ax)
            (ch,), _ = _ew(f"pair_sum_{tag}{t}", pair_fn, tr=64,
                           ins=[mine.reshape(-1, width), other.reshape(-1, width)], outs=[(width, BF16)])
            sums.append(ch.reshape(other.shape))
        return sums

    s_gate, s_up, s_pa, s_pb, s_out, s_down = chip_sums(
        "late", [g_w_gate, g_w_up, g_p_a.reshape(4, d // 4, d), g_p_b.reshape(4, d // 4, d),
                 g_w_out.reshape(4, d // 4, d), g_w_down.reshape(4, ff // 4, d)], [0, 0, 1, 1, 1, 1])
    late_halves = [s_pa, s_pb, s_out, by_chip(s_gate), by_chip(s_up), s_down]
    dq_b, dk_b, dv_b, dgb_grp, late_landed = _gdn_bwd(q_b, k_b, v_b, g_col, g_row, b_col, b_row, states, tinvs,
                                                      do_raw, late_halves)
    grp = _gdn_group(nh)
    dbeta = dgb_grp[:, :, :grp].transpose(1, 0, 2).reshape(s_len, nh)
    dg = dgb_grp[:, :, grp:2 * grp].transpose(1, 0, 2).reshape(s_len, nh)
    dgbeta = _pad_cols(jnp.concatenate([dbeta, dg], axis=1), LANES)

    def gbeta_bwd_fn(i, nt, tiles, prev8, next8, cv):
        al, dtb, mb, ma = cv
        ba, dgb = tiles
        beta = _sigmoid(ba)
        arg = ba + dtb
        da = dgb * (-jnp.exp(al)) * _sigmoid(arg)
        g = -jnp.exp(al) * _softplus(arg)
        dba = jnp.where(mb > 0.5, dgb * beta * (1.0 - beta), jnp.where(ma > 0.5, da, 0.0))
        return [dba], [_colsum(jnp.where(ma > 0.5, dgb * g, 0.0)), _colsum(jnp.where(ma > 0.5, da, 0.0))]

    (dba,), (g_alog128, g_dtb128) = _ew(
        "gbeta_bwd", gbeta_bwd_fn, tr=1024, ins=[proj_ba, dgbeta], consts=[alog128, dtb128, is_b_f, is_a_f],
        outs=[(LANES, BF16)], accs=[(1, LANES)] * 2)
    dxq, g_conv_q = _conv_bwd(proj_b, conv_f, 0, dq_b, norm=True, mult=qscale)
    dxk, g_conv_k = _conv_bwd(proj_b, conv_f, 1, dk_b, norm=True, mult=1.0)
    dxv, g_conv_v = _conv_bwd(proj_b, conv_f, 2, dv_b, norm=False, mult=1.0)
    g_conv = jnp.concatenate([g_conv_q, g_conv_k, g_conv_v], axis=1)

    dqn, dkn, dvb = _sb_bwd(qn, kn, vb, do_a)

    def qknorm_bwd_fn(i, nt, tiles, prev8, next8, cv):
        qa, ka, dq, dk, dv = tiles
        dxq_, dwq = _per_head(lambda h, g: _rms_bwd(h, cv[0], g), qa, dq)
        dxk_, dwk = _per_head(lambda h, g: _rms_bwd(h, cv[1], g), ka, dk)
        return [dxq_, dxk_, dv], [_colsum(_head_sum(dwq)), _colsum(_head_sum(dwk))]

    (dqa, dka, dva), (g_q_norm, g_k_norm) = _ew(
        "qknorm_bwd", qknorm_bwd_fn, tr=256, ins=[Col(proj_a, d, 0), Col(proj_a, d, 1), dqn, dkn, dvb],
        consts=[q_norm_w, k_norm_w], outs=[(d, BF16)] * 3, accs=[(1, HEAD_DIM)] * 2)

    u1_t = u1.T
    d_a = jnp.concatenate([dqa, dka, dva], axis=1)
    d_b = jnp.concatenate([dxq, dxk, dxv], axis=1)
    d_zg = jnp.concatenate([dz_b, dga, dgb_gate], axis=1)
    du1 = _mm("d_u1_a", d_a, wa, nt=True)
    du1 = _mm("d_u1_b", d_b, wb, nt=True, add=du1)
    du1 = _mm("d_u1_zg", d_zg, wzg, nt=True, add=du1)
    du1 = _mm("d_u1_ba", dba, wba, nt=True, add=du1)
    g_wa = _mm("g_w_in_a", u1_t, d_a)
    g_wb = _mm("g_w_in_b", u1_t, d_b)
    g_wzg = _mm("g_w_in_zg", u1_t, d_zg)
    g_wba = _mm("g_w_in_ba", u1_t, dba)

    def norm1_bwd_fn(i, nt, tiles, prev8, next8, cv):
        xt, du, dres = tiles
        w, sc = cv
        r = lax.rsqrt(jnp.mean(xt * xt, axis=1, keepdims=True) + EPS)
        nrm = xt * r
        dn = du * w * (1.0 + sc)
        dxt = r * (dn - nrm * jnp.mean(dn * nrm, axis=1, keepdims=True)) + dres
        return [dxt], [_colsum(du), _colsum(du * nrm * w), _colsum(du * nrm * (1.0 + sc))]

    (grad_x,), (dshift1, dscale1, g_norm1) = _ew(
        "norm1_bwd", norm1_bwd_fn, tr=256, ins=[x2, du1, dh1], consts=[norm1_w, scale1],
        outs=[(d, F32)], accs=[(1, d)] * 3)

    dmod_me = jnp.concatenate([dshift1, dscale1, dgate1, dshift2, dscale2, dgate2], axis=1)
    small = jnp.concatenate(
        [dmod_me, g_norm1, g_norm2, g_q_norm, g_k_norm, g_o_norm, g_alog128[:, nh:2 * nh], g_dtb128[:, nh:2 * nh],
         g_conv.reshape(1, -1)], axis=1)
    n_small = small.shape[1]
    pad_to = -(-n_small // (SUBLANES * LANES)) * (SUBLANES * LANES)
    small_all = _allgather8("ag_small", _pad_cols(small, pad_to).reshape(SUBLANES, pad_to // SUBLANES))
    small_all = small_all.reshape(8, pad_to)

    def sum8_fn(i, nt, tiles, prev8, next8, cv):
        return [], [_colsum(tiles[0])]

    _, (small_sum,) = _ew("sum_small", sum8_fn, tr=8, ins=[small_all], accs=[(1, pad_to)])
    offs = [0]
    for width in (6 * d, d, d, HEAD_DIM, HEAD_DIM, HEAD_DIM, nh, nh, GDN_CONV * 3 * d):
        offs.append(offs[-1] + width)
    pieces = [small_sum[:, offs[j]:offs[j + 1]] for j in range(9)]
    (gs_b_mod, gs_norm1, gs_norm2, gs_q_norm, gs_k_norm, gs_o_norm, gs_a_log, gs_dt_bias, gs_conv) = pieces
    conv_cols = 3 * d // 4
    gs_conv_mine = lax.dynamic_slice_in_dim(gs_conv.reshape(GDN_CONV, 3 * d), chip * conv_cols, conv_cols, axis=1)

    dmod_all = lax.dynamic_slice_in_dim(small_all[:, :6 * d], chip * mod_w, mod_w, axis=1)

    def wmod_grad_body(ct_ref, dm_ref, o_ref):
        o_ref[...] = _dot(ct_ref[...], dm_ref[...], hi=HIGHEST)

    g_w_mod = _pcall(
        wmod_grad_body, name="g_w_mod", grid=(mod_w // tn_mod,),
        out_shape=jax.ShapeDtypeStruct((d, mod_w), F32),
        in_specs=[pl.BlockSpec((d, 8), lambda j: (0, 0)), pl.BlockSpec((8, tn_mod), lambda j: (0, j))],
        out_specs=pl.BlockSpec((d, tn_mod), lambda j: (0, j)),
        compiler_params=_params(("arbitrary",)),
    )(c_act.T, dmod_all)

    s_wa, s_wb, s_wzg, s_wba = chip_sums("w_in", [g_wa, g_wb, g_wzg, g_wba], [0, 0, 0, 0])
    s_w_in = jnp.concatenate([s_wa, s_wb, s_wzg[:, :d], s_wba[:, :2 * nh], s_wzg[:, d:]], axis=1)
    chip_halves = [by_chip(s_w_in)] + late_halves
    landed = list(_scatter4("rs_w_in", chip_halves[:1])) + list(late_landed)
    landed = [_fill_slot(land, ch, chip) for land, ch in zip(landed, chip_halves)]
    g_mine = []
    for t, land in enumerate(landed):
        def sum4_fn(i, nt, tiles, prev8, next8, cv):
            f = [tl.astype(F32) for tl in tiles]
            return [(f[0] + f[1]) + (f[2] + f[3])], []

        (gh,), _ = _ew(f"chip_sum{t}", sum4_fn, tr=64, ins=[Col(land, lead=s) for s in range(4)],
                       outs=[(land.shape[-1], F32)])
        g_mine.append(gh)
    g_theirs = _sibling_send("join_grads", g_mine)
    g_full = [jnp.concatenate([jnp.where(mc == 0, a, b), jnp.where(mc == 0, b, a)], axis=0)
              for a, b in zip(g_mine, g_theirs)]

    big = {}
    names = ["w_in", "p_a", "p_b", "w_out", "w_gate", "w_up", "w_down"]
    big_w = [w_in, p_a, p_b, w_out, w_gate, w_up, w_down]
    big_m = [m_w_in, m_p_a, m_p_b, m_w_out, m_w_gate, m_w_up, m_w_down]
    big_v = [v_w_in, v_p_a, v_p_b, v_w_out, v_w_gate, v_w_up, v_w_down]
    for t, nm in enumerate(names):
        big[nm] = _adamw(f"adamw_{nm}", big_w[t], big_m[t], big_v[t], [g_full[t]])
    big["w_mod"] = _adamw("adamw_w_mod", w_mod, m_w_mod, v_w_mod, [g_w_mod])
    big["conv_w"] = _adamw("adamw_conv_w", conv_w, m_conv_w, v_conv_w, [gs_conv_mine], tr=8)
    small_names = ["b_mod", "norm1_w", "norm2_w", "q_norm_w", "k_norm_w", "o_norm_w", "a_log", "dt_bias"]
    small_w = [b_mod, norm1_w, norm2_w, q_norm_w, k_norm_w, o_norm_w, a_log, dt_bias]
    small_m = [m_b_mod, m_norm1_w, m_norm2_w, m_q_norm_w, m_k_norm_w, m_o_norm_w, m_a_log, m_dt_bias]
    small_v = [v_b_mod, v_norm1_w, v_norm2_w, v_q_norm_w, v_k_norm_w, v_o_norm_w, v_a_log, v_dt_bias]
    small_g = [gs_b_mod, gs_norm1, gs_norm2, gs_q_norm, gs_k_norm, gs_o_norm, gs_a_log, gs_dt_bias]
    rep_w = jnp.concatenate(small_w, axis=1)
    rep_m = jnp.concatenate(small_m, axis=1)
    rep_v = jnp.concatenate(small_v, axis=1)
    rep_g = jnp.concatenate(small_g, axis=1)
    rep = _adamw("adamw_small", rep_w, rep_m, rep_v, [rep_g], tr=1)
    roffs = [0]
    for a in small_w:
        roffs.append(roffs[-1] + a.shape[1])
    for j, nm in enumerate(small_names):
        big[nm] = tuple(r[:, roffs[j]:roffs[j + 1]] for r in rep)

    order = ["w_mod", "b_mod", "norm1_w", "w_in", "q_norm_w", "k_norm_w", "conv_w", "a_log", "dt_bias", "o_norm_w",
             "p_a", "p_b", "w_out", "norm2_w", "w_gate", "w_up", "w_down"]
    grads = [big[nm][0] for nm in order]
    deltas = [big[nm][1] for nm in order]
    new_m = [big[nm][2] for nm in order]
    new_v = [big[nm][3] for nm in order]
    return (loss, grad_x[None], *grads, *deltas, *new_m, *new_v)
```

```python
import jax
import jax.numpy as jnp
from jax import lax
from jax.experimental import pallas as pl
from jax.experimental.pallas import tpu as pltpu

F32 = jnp.float32
BF16 = jnp.bfloat16
HIGHEST = lax.Precision.HIGHEST
HIGH = lax.Precision.HIGH
MESH = pl.DeviceIdType.MESH

HEAD_DIM = 128
GDN_CHUNK = 64
GDN_CONV = 4
EPS = 1e-6
LANES = 128
SUBLANES = 8
VMEM_LIMIT = 56 * 1024 * 1024
MM_VMEM_BUDGET = 40 * 1024 * 1024

ADAM_LR = 0.001
ADAM_B1 = 0.9
ADAM_B2 = 0.999
ADAM_EPS = 1e-08
ADAM_WD = 0.01
ADAM_STEP = 10


def _pcall(body, **kw):
    return pl.pallas_call(body, **kw)


def _params(sem=None):
    if sem is None:
        return pltpu.CompilerParams(vmem_limit_bytes=VMEM_LIMIT)
    return pltpu.CompilerParams(dimension_semantics=sem, vmem_limit_bytes=VMEM_LIMIT)


def _pick(dim, target):
    if dim <= target:
        return dim
    best = None
    for t in range(LANES, target + 1, LANES):
        if dim % t == 0:
            best = t
    assert best is not None, (dim, target)
    return best


def _rows_tile(rows, target):
    t = min(rows, target)
    while rows % t:
        t //= 2
    assert t >= SUBLANES or t == rows, (rows, target)
    return t


def _dot(a, b, hi=None):
    return jnp.dot(a, b, preferred_element_type=F32, precision=hi)


def _dot_nt(a, b, hi=None):
    return lax.dot_general(a, b, (((1,), (1,)), ((), ())), preferred_element_type=F32, precision=hi)


def _dot_tn(a, b, hi=None):
    return lax.dot_general(a, b, (((0,), (0,)), ((), ())), preferred_element_type=F32, precision=hi)


def _sigmoid(x):
    return 1.0 / (1.0 + jnp.exp(-x))


def _softplus(x):
    return jnp.maximum(x, 0.0) + jnp.log(1.0 + jnp.exp(-jnp.abs(x)))


_HBM = pl.BlockSpec(memory_space=pltpu.HBM)


def _my_pos():
    return lax.axis_index("x"), lax.axis_index("y"), lax.axis_index("c")


def _allgather8(name, v):
    def body(v_ref, o_ref, ssem, rsem, lsem):
        x, y, c = _my_pos()
        me = 4 * x + 2 * y + c
        loc = pltpu.make_async_copy(v_ref, o_ref.at[me], lsem)
        loc.start()
        sends, recvs = [], []
        for k in range(1, 8):
            px, py, pc = (x + (k >> 2)) % 2, (y + ((k >> 1) & 1)) % 2, (c + (k & 1)) % 2
            cp = pltpu.make_async_remote_copy(
                src_ref=v_ref, dst_ref=o_ref.at[me], send_sem=ssem.at[k - 1], recv_sem=rsem.at[k - 1],
                device_id=(px, py, pc), device_id_type=MESH)
            cp.start()
            sends.append(cp)
            recvs.append(pltpu.make_async_remote_copy(
                src_ref=v_ref, dst_ref=o_ref.at[4 * px + 2 * py + pc], send_sem=ssem.at[k - 1],
                recv_sem=rsem.at[k - 1], device_id=(px, py, pc), device_id_type=MESH))
        for rc in recvs:
            rc.wait_recv()
        for cp in sends:
            cp.wait_send()
        loc.wait()

    return _pcall(
        body, name=name, out_shape=jax.ShapeDtypeStruct((8,) + v.shape, v.dtype),
        in_specs=[_HBM], out_specs=_HBM,
        scratch_shapes=[pltpu.SemaphoreType.DMA((7,)), pltpu.SemaphoreType.DMA((7,)), pltpu.SemaphoreType.DMA],
    )(v)


def _plane_peers(x, y):
    return [((x + (k >> 1)) % 2, (y + (k & 1)) % 2) for k in range(1, 4)]


class _GatherPlan:
    def __init__(self, ins, outs, sems, n_split):
        ssem, rsem, fsem, gsem = sems
        x, y, c = _my_pos()
        me = 2 * x + y
        copy = lambda src, dst, s_sem, r_sem, dev: (lambda: pltpu.make_async_remote_copy(
            src_ref=src, dst_ref=dst, send_sem=s_sem, recv_sem=r_sem, device_id=dev, device_id_type=MESH))
        self.sends, self.recvs, self.fwds, self.fwd_recvs = [], [], [], []
        for t in range(len(ins)):
            split = t < n_split
            hr = ins[t].shape[0] // 2
            for k, (px, py) in enumerate(_plane_peers(x, y)):
                peer = 2 * px + py
                sem = 3 * t + k
                if split:
                    mine = pl.ds(pl.multiple_of(c * hr, 16), hr)
                    other = pl.ds(pl.multiple_of((1 - c) * hr, 16), hr)
                    src, dst, got = ins[t].at[mine], outs[t].at[me, mine], outs[t].at[peer, mine]
                else:
                    src, dst, got = ins[t], outs[t].at[me], outs[t].at[peer]
                self.sends.append(copy(src, dst, ssem.at[sem], rsem.at[sem], (px, py, c)))
                self.recvs.append(copy(src, got, ssem.at[sem], rsem.at[sem], (px, py, c)))
                if split:
                    self.fwds.append(copy(got, got, fsem.at[sem], gsem.at[sem], (x, y, 1 - c)))
                    self.fwd_recvs.append(copy(got, outs[t].at[peer, other], fsem.at[sem], gsem.at[sem], (x, y, 1 - c)))
                else:
                    self.fwds.append(None)

    def start(self):
        for cp in self.sends:
            cp().start()

    def relay(self):
        for rc, fw in zip(self.recvs, self.fwds):
            rc().wait_recv()
            if fw is not None:
                fw().start()

    def finish(self):
        for fr in self.fwd_recvs:
            fr().wait_recv()
        for cp in self.sends + [fw for fw in self.fwds if fw is not None]:
            cp().wait_send()

    @staticmethod
    def out_shapes(shards):
        return [jax.ShapeDtypeStruct((4,) + s.shape, s.dtype) for s in shards]

    @staticmethod
    def sem_shapes(n):
        return [pltpu.SemaphoreType.DMA((3 * n,))] * 4


def _gather4(name, shards, n_split):
    n = len(shards)

    def body(*refs):
        plan = _GatherPlan(refs[:n], refs[n:2 * n], refs[2 * n:], n_split)
        plan.start()
        plan.relay()
        plan.finish()

    return _pcall(
        body, name=name, out_shape=_GatherPlan.out_shapes(shards), in_specs=[_HBM] * n, out_specs=[_HBM] * n,
        scratch_shapes=_GatherPlan.sem_shapes(n),
    )(*shards)


def _when_step(h, n, hs, ns):
    return pl.when(jnp.logical_and(pl.program_id(0) == (hs if h < 0 else h), pl.program_id(1) == (ns if n < 0 else n)))


def _fill_slot(slots, own, slot):
    mask = (jnp.arange(4) == slot).reshape((4,) + (1,) * (slots.ndim - 1))
    return jnp.where(mask, own if own.ndim == slots.ndim else own[None], slots)


class _ScatterPlan:
    def __init__(self, ins, outs, sems):
        ssem, rsem = sems
        x, y, c = _my_pos()
        me = 2 * x + y
        copy = lambda src, dst, s_sem, r_sem, dev: (lambda: pltpu.make_async_remote_copy(
            src_ref=src, dst_ref=dst, send_sem=s_sem, recv_sem=r_sem, device_id=dev, device_id_type=MESH))
        self.sends, self.recvs = [], []
        for t in range(len(ins)):
            for k, (px, py) in enumerate(_plane_peers(x, y)):
                peer = 2 * px + py
                sem = 3 * t + k
                self.sends.append(copy(ins[t].at[peer], outs[t].at[me], ssem.at[sem], rsem.at[sem], (px, py, c)))
                self.recvs.append(copy(ins[t].at[peer], outs[t].at[peer], ssem.at[sem], rsem.at[sem], (px, py, c)))

    def start(self):
        for cp in self.sends:
            cp().start()

    def finish(self):
        for rc in self.recvs:
            rc().wait_recv()
        for cp in self.sends:
            cp().wait_send()

    @staticmethod
    def out_shapes(partials):
        return [jax.ShapeDtypeStruct(p.shape, p.dtype) for p in partials]

    @staticmethod
    def sem_shapes(n):
        return [pltpu.SemaphoreType.DMA((3 * n,))] * 2


def _scatter4(name, partials):
    n = len(partials)

    def body(*refs):
        plan = _ScatterPlan(refs[:n], refs[n:2 * n], refs[2 * n:])
        plan.start()
        plan.finish()

    return _pcall(
        body, name=name, out_shape=_ScatterPlan.out_shapes(partials), in_specs=[_HBM] * n, out_specs=[_HBM] * n,
        scratch_shapes=_ScatterPlan.sem_shapes(n),
    )(*partials)


def _sibling_send(name, arrays, axes=None):
    n = len(arrays)
    axes = [None] * n if axes is None else axes

    def body(*refs):
        ins, outs = refs[:n], refs[n:2 * n]
        ssem, rsem = refs[2 * n:]
        x, y, c = _my_pos()
        cps = []
        for t in range(n):
            src = ins[t]
            if axes[t] is not None:
                hr = ins[t].shape[axes[t]] // 2
                give = pl.ds(pl.multiple_of((1 - c) * hr, SUBLANES), hr)
                src = ins[t].at[give] if axes[t] == 0 else ins[t].at[:, give]
            cp = pltpu.make_async_remote_copy(
                src_ref=src, dst_ref=outs[t], send_sem=ssem.at[t], recv_sem=rsem.at[t],
                device_id=(x, y, 1 - c), device_id_type=MESH)
            cp.start()
            cps.append(cp)
        for cp in cps:
            cp.wait_recv()
        for cp in cps:
            cp.wait_send()

    def half(a, axis):
        shape = list(a.shape)
        if axis is not None:
            shape[axis] //= 2
        return jax.ShapeDtypeStruct(tuple(shape), a.dtype)

    return _pcall(
        body, name=name, out_shape=[half(a, ax) for a, ax in zip(arrays, axes)], in_specs=[_HBM] * n,
        out_specs=[_HBM] * n,
        scratch_shapes=[pltpu.SemaphoreType.DMA((n,)), pltpu.SemaphoreType.DMA((n,))],
    )(*arrays)


def _mm(name, a, b, *, nt=False, ta=False, out_dtype=F32, add=None, tm=1024, tn=1024, tk=4096):
    m, k = (a.shape[1], a.shape[0]) if ta else a.shape
    n = b.shape[0] if nt else b.shape[1]
    assert (b.shape[1] if nt else b.shape[0]) == k
    has_add = add is not None
    tm, tn = _pick(m, tm), _pick(n, tn)
    out_bytes = jnp.dtype(out_dtype).itemsize

    def vmem_bytes(tk_):
        steps = k // tk_
        return (4 * (tm + tn) * tk_ + 2 * tm * tn * out_bytes + (8 * tm * tn if has_add else 0)
                + (4 * tm * tn if steps > 1 else 0))

    tk = _pick(k, tk)
    while vmem_bytes(tk) > MM_VMEM_BUDGET and tk > 512:
        tk = _pick(k, tk - LANES)
    nk = k // tk

    def body(*refs):
        a_ref, b_ref = refs[0], refs[1]
        c_ref = refs[2] if has_add else None
        o_ref = refs[2 + has_add]
        p = (_dot_tn if ta else _dot_nt if nt else _dot)(a_ref[...], b_ref[...])
        if nk == 1:
            o_ref[...] = (p + c_ref[...] if has_add else p).astype(o_ref.dtype)
            return
        acc = refs[3 + has_add]
        kk = pl.program_id(2)

        @pl.when(kk == 0)
        def _():
            acc[...] = p

        @pl.when(jnp.logical_and(kk > 0, kk < nk - 1))
        def _():
            acc[...] += p

        @pl.when(kk == nk - 1)
        def _():
            r = acc[...] + p
            if has_add:
                r = r + c_ref[...]
            o_ref[...] = r.astype(o_ref.dtype)

    if ta:
        a_spec = pl.BlockSpec((tk, tm), lambda j, i, kk: (kk, i))
    else:
        a_spec = pl.BlockSpec((tm, tk), lambda j, i, kk: (i, kk))
    if nt:
        b_spec = pl.BlockSpec((tn, tk), lambda j, i, kk: (j, kk))
    else:
        b_spec = pl.BlockSpec((tk, tn), lambda j, i, kk: (kk, j))
    o_spec = pl.BlockSpec((tm, tn), lambda j, i, kk: (i, j))
    in_specs = [a_spec, b_spec] + ([o_spec] if has_add else [])
    args = (a, b) + ((add,) if has_add else ())
    return _pcall(
        body, name=name, grid=(n // tn, m // tm, nk),
        out_shape=jax.ShapeDtypeStruct((m, n), out_dtype),
        in_specs=in_specs, out_specs=o_spec,
        scratch_shapes=[pltpu.VMEM((tm, tn), F32)] if nk > 1 else [],
        compiler_params=_params(("parallel", "parallel", "arbitrary")),
    )(*args)


class Col:
    def __init__(self, arr, w=None, cb=0, lead=None):
        self.arr, self.cb, self.lead = arr, cb, lead
        self.w = arr.shape[-1] if w is None else w
        self.rows = arr.shape[-2]


def _ew(name, fn, *, tr, ins, consts=(), outs=(), accs=(), halo_prev=(), halo_next=()):
    ins = [c if isinstance(c, Col) else Col(c) for c in ins]
    halo_prev = [c if isinstance(c, Col) else Col(c) for c in halo_prev]
    halo_next = [c if isinstance(c, Col) else Col(c) for c in halo_next]
    rows = ins[0].rows
    tr = _rows_tile(rows, tr)
    nt = rows // tr
    n_in, n_hp, n_hn, n_c, n_o, n_a = len(ins), len(halo_prev), len(halo_next), len(consts), len(outs), len(accs)
    groups = tr // SUBLANES

    def spec(col, kind):
        if kind == "cur":
            shape, idx = (tr, col.w), (lambda i, cb=col.cb: (i, cb))
        elif kind == "prev":
            shape, idx = (SUBLANES, col.w), (lambda i, cb=col.cb: (jnp.maximum(i * groups - 1, 0), cb))
        else:
            shape = (SUBLANES, col.w)
            idx = (lambda i, cb=col.cb: (jnp.minimum((i + 1) * groups, rows // SUBLANES - 1), cb))
        if col.lead is None:
            return pl.BlockSpec(shape, idx)
        return pl.BlockSpec((None,) + shape, lambda i, idx=idx, lead=col.lead: (lead,) + idx(i))

    def body(*refs):
        i = pl.program_id(0)
        p = 0
        tiles = [r[...] for r in refs[p:p + n_in]]; p += n_in
        prev8 = [r[...] for r in refs[p:p + n_hp]]; p += n_hp
        next8 = [r[...] for r in refs[p:p + n_hn]]; p += n_hn
        cvals = [r[...] for r in refs[p:p + n_c]]; p += n_c
        out_refs = refs[p:p + n_o]; p += n_o
        acc_refs = refs[p:p + n_a]
        out_v, acc_v = fn(i, nt, tiles, prev8, next8, cvals)
        for r, v in zip(out_refs, out_v):
            r[...] = v.astype(r.dtype)
        if n_a:
            @pl.when(i == 0)
            def _():
                for r, v in zip(acc_refs, acc_v):
                    r[...] = v

            @pl.when(i > 0)
            def _():
                for r, v in zip(acc_refs, acc_v):
                    r[...] += v

    in_specs = ([spec(c, "cur") for c in ins] + [spec(c, "prev") for c in halo_prev]
                + [spec(c, "next") for c in halo_next]
                + [pl.BlockSpec(c.shape, lambda i, nd=c.ndim: (0,) * nd) for c in consts])
    out_specs = ([pl.BlockSpec((tr, w), lambda i: (i, 0)) for w, _ in outs]
                 + [pl.BlockSpec(s, lambda i: (0, 0)) for s in accs])
    out_shape = ([jax.ShapeDtypeStruct((rows, w), dt) for w, dt in outs]
                 + [jax.ShapeDtypeStruct(s, F32) for s in accs])
    args = [c.arr for c in ins] + [c.arr for c in halo_prev] + [c.arr for c in halo_next] + list(consts)
    res = _pcall(body, name=name, grid=(nt,), out_shape=out_shape, in_specs=in_specs, out_specs=out_specs,
                 compiler_params=_params(("arbitrary",)))(*args)
    return res[:n_o], res[n_o:]


def _colsum(v):
    return jnp.sum(v, axis=0, keepdims=True)


def _heads_of(w):
    return w // HEAD_DIM


def _per_head(fn, *arrays):
    nh = _heads_of(arrays[0].shape[1])
    res = [fn(*[a[:, h * HEAD_DIM:(h + 1) * HEAD_DIM] for a in arrays]) for h in range(nh)]
    if isinstance(res[0], tuple):
        return tuple(jnp.concatenate([r[j] for r in res], axis=1) for j in range(len(res[0])))
    return jnp.concatenate(res, axis=1)


def _head_sum(v):
    nh = _heads_of(v.shape[1])
    out = v[:, :HEAD_DIM]
    for h in range(1, nh):
        out = out + v[:, h * HEAD_DIM:(h + 1) * HEAD_DIM]
    return out


def _rms_fwd(x, w):
    r = lax.rsqrt(jnp.mean(x * x, axis=1, keepdims=True) + EPS)
    return x * r * w


def _rms_bwd(x, w, dy):
    r = lax.rsqrt(jnp.mean(x * x, axis=1, keepdims=True) + EPS)
    xh = x * r
    dxh = dy * w
    dx = r * (dxh - xh * jnp.mean(dxh * xh, axis=1, keepdims=True))
    return dx, dy * xh


def _silu(x):
    return x * _sigmoid(x)


def _dsilu(x):
    s = _sigmoid(x)
    return s * (1.0 + x * (1.0 - s))


SB_BQ = 512
SB_CUTOFF = 112.0
SB_PAIR = 2
SB_BK = 256


def _softplus_pos(z):
    return jnp.maximum(z, 0.0) + jnp.log(1.0 + jnp.exp(-jnp.abs(z)))


def _split_dot(v, tri):
    top = lax.bitcast_convert_type(lax.bitcast_convert_type(v, jnp.int32) & jnp.int32(-65536), F32)
    return _dot(top.astype(BF16), tri) + _dot((v - top).astype(BF16), tri)


def _sb_fwd(qn, kn, vb, shards, *, bq=SB_BQ, bk=SB_BK):
    s_len, hd = qn.shape
    nh = hd // HEAD_DIM
    bk = min(bk, s_len)
    bq = min(bq, s_len)
    ndiag = bq // bk
    scale = HEAD_DIM ** -0.5

    n_sh = len(shards)
    last_h, last_i = nh - 1, s_len // bq - 1

    def body(q_ref, k_ref, v_ref, *rest):
        sh_refs, o_ref = rest[:n_sh], rest[n_sh]
        got_refs, sems = rest[n_sh + 1:2 * n_sh + 1], rest[2 * n_sh + 1:]
        i = pl.program_id(1)

        if n_sh:
            @_when_step(0, 0, last_h, last_i)
            def _():
                _GatherPlan(sh_refs, got_refs, sems, n_sh).start()

            @_when_step(nh // 2, 0, last_h, last_i)
            def _():
                _GatherPlan(sh_refs, got_refs, sems, n_sh).relay()

            @_when_step(-1, -1, last_h, last_i)
            def _():
                _GatherPlan(sh_refs, got_refs, sems, n_sh).finish()

        krow = lax.broadcasted_iota(jnp.int32, (bk, bk), 0)
        kcol = lax.broadcasted_iota(jnp.int32, (bk, bk), 1)
        later = (krow > kcol).astype(BF16)
        row = lax.broadcasted_iota(jnp.int32, (bq, bk), 0)
        col = lax.broadcasted_iota(jnp.int32, (bq, bk), 1)
        q = q_ref[...]

        def tiles(js, carry, diags):
            run, acc = carry
            ks, vs, zs = [], [], []
            for j in js:
                off = pl.multiple_of(j * bk, bk)
                ks.append(k_ref[pl.ds(off, bk), :])
                vs.append(v_ref[pl.ds(off, bk), :])
                zs.append(_dot_nt(q, ks[-1]) * scale)
            sps, cums, masks = [], [], []
            for z, diag in zip(zs, diags):
                sp = _softplus_pos(z)
                causal = None
                if diag is not None:
                    causal = col + diag * bk < row
                    sp = jnp.where(causal, sp, 0.0)
                sps.append(sp)
                masks.append(causal)
                cums.append(_split_dot(sp, later))
            for z, sp, cum, causal, v in zip(zs, sps, cums, masks, vs):
                w = jnp.exp((z - sp) - (cum + run))
                if causal is not None:
                    w = jnp.where(causal, w, 0.0)
                acc = acc + _dot(w.astype(BF16), v)
                run = run + cum[:, 0:1] + sp[:, 0:1]
            return run, acc

        carry = (jnp.zeros((bq, 1), F32), jnp.zeros((bq, HEAD_DIM), F32))
        for dg in reversed(range(0, ndiag, SB_PAIR)):
            dgs = list(reversed(range(dg, dg + SB_PAIR)))
            carry = tiles([i * ndiag + g for g in dgs], carry, dgs)
        n_left = i * ndiag

        def more(st):
            return jnp.logical_and(st[0] < n_left, jnp.min(st[1]) < SB_CUTOFF)

        def step(st):
            t, run, acc = st
            run, acc = tiles([n_left - 1 - t], (run, acc), [None])
            return t + 1, run, acc

        _, _, acc = lax.while_loop(more, step, (jnp.int32(0),) + carry)
        o_ref[...] = acc.astype(o_ref.dtype)

    qspec = pl.BlockSpec((bq, HEAD_DIM), lambda h, i: (i, h))
    kspec = pl.BlockSpec((s_len, HEAD_DIM), lambda h, i: (0, h))
    return _pcall(
        body, name="sb_fwd", grid=(nh, s_len // bq),
        out_shape=[jax.ShapeDtypeStruct((s_len, hd), BF16)] + _GatherPlan.out_shapes(shards),
        in_specs=[qspec, kspec, kspec] + [_HBM] * n_sh, out_specs=[qspec] + [_HBM] * n_sh,
        scratch_shapes=_GatherPlan.sem_shapes(n_sh) if n_sh else [],
        compiler_params=_params(("arbitrary", "arbitrary")),
    )(qn, kn, vb, *shards)


def _sb_bwd(qn, kn, vb, do, *, bq=SB_BQ, bk=SB_BK):
    s_len, hd = qn.shape
    nh = hd // HEAD_DIM
    bk = min(bk, s_len)
    bq = min(bq, s_len)
    ndiag = bq // bk
    scale = HEAD_DIM ** -0.5

    def body(q_ref, k_ref, v_ref, do_ref, dq_ref, dk_ref, dv_ref):
        i = pl.program_id(1)

        @pl.when(i == 0)
        def _():
            dk_ref[...] = jnp.zeros_like(dk_ref)
            dv_ref[...] = jnp.zeros_like(dv_ref)

        krow = lax.broadcasted_iota(jnp.int32, (bk, bk), 0)
        kcol = lax.broadcasted_iota(jnp.int32, (bk, bk), 1)
        upto = (krow <= kcol).astype(BF16)
        before = (krow < kcol).astype(BF16)
        row = lax.broadcasted_iota(jnp.int32, (bq, bk), 0)
        col = lax.broadcasted_iota(jnp.int32, (bq, bk), 1)
        q = q_ref[...]
        do_t = do_ref[...]
        ones = jnp.ones((bk, LANES), BF16)
        n_left = i * ndiag

        def row_sums(js, diags):
            zs = [_dot_nt(q, k_ref[pl.ds(pl.multiple_of(j * bk, bk), bk), :]) * scale for j in js]
            tot = None
            for z, diag in zip(zs, diags):
                sp = _softplus_pos(z)
                if diag is not None:
                    sp = jnp.where(col + diag * bk < row, sp, 0.0)
                part = _split_dot(sp, ones)[:, 0:1]
                tot = part if tot is None else tot + part
            return tot

        def more(st):
            return jnp.logical_and(st[0] < n_left, jnp.min(st[1]) < SB_CUTOFF)

        def widen(st):
            t, run = st
            return t + 1, run + row_sums([n_left - 1 - t], [None])

        diag_all = list(range(ndiag))
        used, lt = lax.while_loop(more, widen, (jnp.int32(0), row_sums([i * ndiag + g for g in diag_all], diag_all)))

        def tiles(js, carry, diags):
            pre, ecar, dq = carry
            offs, ks, zs, dws = [], [], [], []
            for j in js:
                off = pl.multiple_of(j * bk, bk)
                offs.append(off)
                ks.append(k_ref[pl.ds(off, bk), :])
                zs.append(_dot_nt(q, ks[-1]) * scale)
                dws.append(_dot_nt(do_t, v_ref[pl.ds(off, bk), :]))
            sps, cums, masks = [], [], []
            for z, diag in zip(zs, diags):
                sp = _softplus_pos(z)
                causal = None
                if diag is not None:
                    causal = col + diag * bk < row
                    sp = jnp.where(causal, sp, 0.0)
                sps.append(sp)
                masks.append(causal)
                cums.append(_split_dot(sp, upto))
            es, ebs, exs, sigs = [], [], [], []
            for off, z, sp, cum, dw, causal in zip(offs, zs, sps, cums, dws, masks):
                lb = z - sp
                w = jnp.exp(lb - (lt - (pre + cum)))
                if causal is not None:
                    w = jnp.where(causal, w, 0.0)
                dv_ref[pl.ds(off, bk), :] += _dot_tn(w.astype(BF16), do_t)
                e = dw * w
                eb = e.astype(BF16)
                es.append(e)
                ebs.append(eb)
                exs.append(_dot(eb, before))
                sigs.append(jnp.exp(lb))
                pre = pre + cum[:, bk - 1:bk]
            for off, k, e, eb, exm, sig, causal in zip(offs, ks, es, ebs, exs, sigs, masks):
                ex = exm + ecar
                dz = (e - sig * (e + ex)) * scale
                if causal is not None:
                    dz = jnp.where(causal, dz, 0.0)
                dzb = dz.astype(BF16)
                dk_ref[pl.ds(off, bk), :] += _dot_tn(dzb, q)
                dq = dq + _dot(dzb, k)
                ecar = ex[:, bk - 1:bk] + eb[:, bk - 1:bk].astype(F32)
            return pre, ecar, dq

        init = (jnp.zeros((bq, 1), F32), jnp.zeros((bq, 1), F32), jnp.zeros((bq, HEAD_DIM), F32))
        carry = lax.fori_loop(n_left - used, n_left, lambda j, cr: tiles([j], cr, [None]), init)
        for dg in range(0, ndiag, SB_PAIR):
            dgs = list(range(dg, dg + SB_PAIR))
            carry = tiles([i * ndiag + g for g in dgs], carry, dgs)
        dq_ref[...] = carry[2]

    qspec = pl.BlockSpec((bq, HEAD_DIM), lambda h, i: (i, h))
    kspec = pl.BlockSpec((s_len, HEAD_DIM), lambda h, i: (0, h))
    return _pcall(
        body, name="sb_bwd", grid=(nh, s_len // bq),
        out_shape=[jax.ShapeDtypeStruct((s_len, hd), F32)] * 3,
        in_specs=[qspec, kspec, kspec, qspec],
        out_specs=[qspec, kspec, kspec],
        compiler_params=_params(("parallel", "arbitrary")),
    )(qn, kn, vb, do)


GDN_GROUP = 16


def _gdn_group(nh):
    return min(GDN_GROUP, nh)


def _gdn_chunk_terms(qh, kh, vh, g_r, g_c, b_c):
    c = GDN_CHUNK
    r = lax.broadcasted_iota(jnp.int32, (c, c), 0)
    s = lax.broadcasted_iota(jnp.int32, (c, c), 1)
    tril, stril = r >= s, r > s
    gcc = jnp.sum(jnp.where(tril, g_r, 0.0), axis=1, keepdims=True)
    gcr = jnp.sum(jnp.where(r <= s, g_c, 0.0), axis=0, keepdims=True)
    dm = jnp.where(tril, jnp.exp(jnp.where(tril, gcc - gcr, 0.0)), 0.0)
    kb = kh.astype(BF16)
    kk = _dot_nt(kb, kb)
    qk = _dot_nt(qh.astype(BF16), kb)
    egc = jnp.exp(gcc)
    gcl = gcc[c - 1:c, :]
    t = dict(tril=tril, stril=stril, gcc=gcc, dm=dm, kb=kb, kk=kk, qk=qk, egc=egc,
             ekd=jnp.exp(gcl - gcc), gl=jnp.exp(gcl),
             a=jnp.where(stril, b_c * kk * dm, 0.0),
             bv=b_c * vh, bk=(b_c * egc) * kh, at=jnp.where(tril, qk * dm, 0.0))
    t["qg"] = qh * egc
    t["kd"] = kh * t["ekd"]
    return t


def _unit_lower_inverses(mats):
    c = GDN_CHUNK
    r = lax.broadcasted_iota(jnp.int32, (c, c), 0)
    s = lax.broadcasted_iota(jnp.int32, (c, c), 1)
    eye = (r == s).astype(F32)
    ps = [-a for a in mats]
    ts = [eye + p for p in ps]
    span = 2
    while span < c:
        ps = [_dot(p, p, hi=HIGH) for p in ps]
        ts = [t + _dot(t, p, hi=HIGH) for t, p in zip(ts, ps)]
        span *= 2
    return ts


def _gdn_fwd(q, k, v, g_col, g_row, b_col, b_row, shards, n_split):
    s_len, d = q.shape
    nh = d // HEAD_DIM
    c = GDN_CHUNK
    n_chunks = s_len // c
    grp = _gdn_group(nh)
    n_sh = len(shards)
    last_h, last_n = nh // grp - 1, n_chunks - 1

    def body(q_ref, k_ref, v_ref, gc_ref, gr_ref, bc_ref, br_ref, *rest):
        sh_refs, rest = rest[:n_sh], rest[n_sh:]
        o_ref, ss_ref, ts_ref = rest[:3]
        got_refs, rest = rest[3:3 + n_sh], rest[3 + n_sh:]
        st, sems = rest[0], rest[1:]
        n = pl.program_id(1)

        @_when_step(0, 0, last_h, last_n)
        def _():
            _GatherPlan(sh_refs, got_refs, sems, n_split).start()

        @_when_step(-1, n_chunks // 2, last_h, last_n)
        def _():
            _GatherPlan(sh_refs, got_refs, sems, n_split).relay()

        @_when_step(-1, -1, last_h, last_n)
        def _():
            _GatherPlan(sh_refs, got_refs, sems, n_split).finish()

        @pl.when(n == 0)
        def _():
            st[...] = jnp.zeros_like(st)

        heads = range(grp)
        sls = [slice(i * HEAD_DIM, (i + 1) * HEAD_DIM) for i in heads]
        terms = [_gdn_chunk_terms(q_ref[:, sls[i]], k_ref[:, sls[i]], v_ref[:, sls[i]],
                                  gr_ref[i:i + 1, :], gc_ref[:, i:i + 1], bc_ref[:, i:i + 1]) for i in heads]
        tinvs = _unit_lower_inverses([t["a"] for t in terms])
        wvs = [_dot(tinv, t["bv"], hi=HIGH) for tinv, t in zip(tinvs, terms)]
        wks = [_dot(tinv, t["bk"], hi=HIGH) for tinv, t in zip(tinvs, terms)]
        states = [st[i] for i in heads]
        sbs = [state.astype(BF16) for state in states]
        ubs = [(wv - _dot(wk.astype(BF16), sb)).astype(BF16) for wv, wk, sb in zip(wvs, wks, sbs)]
        for i in heads:
            t = terms[i]
            o_ref[:, sls[i]] = _dot(t["qg"].astype(BF16), sbs[i]) + _dot(t["at"].astype(BF16), ubs[i])
            ss_ref[i] = states[i]
            ts_ref[i] = tinvs[i]
            st[i] = t["gl"] * states[i] + _dot_tn(t["kd"].astype(BF16), ubs[i])

    tok = pl.BlockSpec((c, grp * HEAD_DIM), lambda h, n: (n, h))
    colspec = pl.BlockSpec((None, c, grp), lambda h, n: (h, n, 0))
    rowspec = pl.BlockSpec((None, None, grp, c), lambda h, n: (h, n, 0, 0))
    res = _pcall(
        body, name="gdn_fwd", grid=(nh // grp, n_chunks),
        out_shape=[jax.ShapeDtypeStruct((s_len, d), F32),
                   jax.ShapeDtypeStruct((n_chunks, nh, HEAD_DIM, HEAD_DIM), F32),
                   jax.ShapeDtypeStruct((n_chunks, nh, c, c), F32)] + _GatherPlan.out_shapes(shards),
        in_specs=[tok, tok, tok, colspec, rowspec, colspec, rowspec] + [_HBM] * n_sh,
        out_specs=[tok, pl.BlockSpec((None, grp, HEAD_DIM, HEAD_DIM), lambda h, n: (n, h, 0, 0)),
                   pl.BlockSpec((None, grp, c, c), lambda h, n: (n, h, 0, 0))] + [_HBM] * n_sh,
        scratch_shapes=[pltpu.VMEM((grp, HEAD_DIM, HEAD_DIM), F32)] + _GatherPlan.sem_shapes(n_sh),
        compiler_params=_params(("arbitrary", "arbitrary")),
    )(q, k, v, g_col, g_row, b_col, b_row, *shards)
    return res[0], res[1], res[2], res[3:]


def _gdn_bwd(q, k, v, g_col, g_row, b_col, b_row, states, tinvs, do, partials):
    s_len, d = q.shape
    nh = d // HEAD_DIM
    c = GDN_CHUNK
    n_chunks = s_len // c
    grp = _gdn_group(nh)
    n_p = len(partials)
    last_h, last_n = nh // grp - 1, n_chunks - 1

    def body(q_ref, k_ref, v_ref, gc_ref, gr_ref, bc_ref, br_ref, ss_ref, ts_ref, do_ref, *rest):
        part_refs, rest = rest[:n_p], rest[n_p:]
        dq_ref, dk_ref, dv_ref, dgb_ref = rest[:4]
        land_refs, rest = rest[4:4 + n_p], rest[4 + n_p:]
        dst, sems = rest[0], rest[1:]
        n = pl.program_id(1)

        @_when_step(0, 0, last_h, last_n)
        def _():
            _ScatterPlan(part_refs, land_refs, sems).start()

        @_when_step(-1, -1, last_h, last_n)
        def _():
            _ScatterPlan(part_refs, land_refs, sems).finish()

        @pl.when(n == 0)
        def _():
            dst[...] = jnp.zeros_like(dst)

        r = lax.broadcasted_iota(jnp.int32, (c, c), 0)
        s = lax.broadcasted_iota(jnp.int32, (c, c), 1)
        suffix = (r <= s).astype(F32)
        lane = lax.broadcasted_iota(jnp.int32, (c, LANES), 1)
        heads = range(grp)
        sls = [slice(i * HEAD_DIM, (i + 1) * HEAD_DIM) for i in heads]
        qs = [q_ref[:, sl] for sl in sls]
        ks = [k_ref[:, sl] for sl in sls]
        vs = [v_ref[:, sl] for sl in sls]
        bcs = [bc_ref[:, i:i + 1] for i in heads]
        ts = [_gdn_chunk_terms(qs[i], ks[i], vs[i], gr_ref[i:i + 1, :], gc_ref[:, i:i + 1], bcs[i]) for i in heads]
        tinv = [ts_ref[i] for i in heads]
        state = [ss_ref[i] for i in heads]
        sb = [x.astype(BF16) for x in state]
        dnext = [dst[i] for i in heads]
        dnb = [x.astype(BF16) for x in dnext]
        dob = [do_ref[:, sl].astype(BF16) for sl in sls]
        wv = [_dot(tinv[i], ts[i]["bv"], hi=HIGH) for i in heads]
        wk = [_dot(tinv[i], ts[i]["bk"], hi=HIGH) for i in heads]
        wkb = [x.astype(BF16) for x in wk]
        ub = [(wv[i] - _dot(wkb[i], sb[i])).astype(BF16) for i in heads]
        du = [_dot_tn(ts[i]["at"].astype(BF16), dob[i]) + _dot(ts[i]["kd"].astype(BF16), dnb[i]) for i in heads]
        dub = [x.astype(BF16) for x in du]
        dat = [jnp.where(ts[i]["tril"], _dot_nt(dob[i], ub[i]), 0.0) for i in heads]
        dqg = [_dot_nt(dob[i], sb[i]) for i in heads]
        dkd = [_dot_nt(ub[i], dnb[i]) for i in heads]
        dwk = [-_dot_nt(dub[i], sb[i]) for i in heads]
        for i in heads:
            dst[i] = (ts[i]["gl"] * dnext[i] + _dot_tn(ts[i]["qg"].astype(BF16), dob[i]) - _dot_tn(wkb[i], dub[i]))
        dbv = [_dot_tn(tinv[i], du[i], hi=HIGH) for i in heads]
        dbk = [_dot_tn(tinv[i], dwk[i], hi=HIGH) for i in heads]
        dtm = [_dot_nt(du[i], ts[i]["bv"], hi=HIGH) + _dot_nt(dwk[i], ts[i]["bk"], hi=HIGH) for i in heads]
        dtt = [_dot_nt(dtm[i], tinv[i], hi=HIGH) for i in heads]
        da = [-jnp.where(ts[i]["stril"], _dot_tn(tinv[i], dtt[i], hi=HIGH), 0.0) for i in heads]
        rs = lambda m: jnp.sum(m, axis=1, keepdims=True)
        dgb = jnp.zeros((c, LANES), F32)
        for i in heads:
            t, b_c, dm, kb = ts[i], bcs[i], ts[i]["dm"], ts[i]["kb"]
            egc, ekd = t["egc"], t["ekd"]
            dkk = da[i] * b_c * dm
            ddm = da[i] * b_c * t["kk"] + dat[i] * t["qk"]
            dqkb, dkkb = (dat[i] * dm).astype(BF16), dkk.astype(BF16)
            dq_ref[:, sls[i]] = _dot(dqkb, kb) + dqg[i] * egc
            dk_ref[:, sls[i]] = (_dot_tn(dqkb, qs[i].astype(BF16)) + _dot(dkkb, kb) + _dot_tn(dkkb, kb)
                                 + dbk[i] * (b_c * egc) + dkd[i] * ekd)
            dv_ref[:, sls[i]] = dbv[i] * b_c
            dbk_k = rs(dbk[i] * ks[i])
            dbeta = rs(da[i] * t["kk"] * dm) + rs(dbv[i] * vs[i]) + dbk_k * egc
            mx = ddm * dm
            ekd_sum = rs(dkd[i] * ks[i]) * ekd
            dgc = rs(mx) + dbk_k * b_c * egc + rs(dqg[i] * qs[i]) * egc - ekd_sum
            dgl = jnp.sum(rs(dnext[i] * state[i]), axis=0, keepdims=True)
            tail = jnp.sum(ekd_sum, axis=0, keepdims=True) + dgl * t["gl"]
            dg = (_dot(suffix, jnp.broadcast_to(dgc, (c, LANES)), hi=HIGH)[:, 0:1]
                  - rs(_dot_nt(suffix, mx, hi=HIGH)) + tail)
            dgb = dgb + jnp.where(lane == i, dbeta, 0.0) + jnp.where(lane == grp + i, dg, 0.0)
        dgb_ref[...] = dgb

    last = n_chunks - 1
    tok = pl.BlockSpec((c, grp * HEAD_DIM), lambda h, n: (last - n, h))
    colspec = pl.BlockSpec((None, c, grp), lambda h, n: (h, last - n, 0))
    rowspec = pl.BlockSpec((None, None, grp, c), lambda h, n: (h, last - n, 0, 0))
    res = _pcall(
        body, name="gdn_bwd", grid=(nh // grp, n_chunks),
        out_shape=[jax.ShapeDtypeStruct((s_len, d), F32)] * 3
        + [jax.ShapeDtypeStruct((nh // grp, s_len, LANES), F32)] + _ScatterPlan.out_shapes(partials),
        in_specs=[tok, tok, tok, colspec, rowspec, colspec, rowspec,
                  pl.BlockSpec((None, grp, HEAD_DIM, HEAD_DIM), lambda h, n: (last - n, h, 0, 0)),
                  pl.BlockSpec((None, grp, c, c), lambda h, n: (last - n, h, 0, 0)), tok] + [_HBM] * n_p,
        out_specs=[tok, tok, tok, pl.BlockSpec((None, c, LANES), lambda h, n: (h, last - n, 0))] + [_HBM] * n_p,
        scratch_shapes=[pltpu.VMEM((grp, HEAD_DIM, HEAD_DIM), F32)] + _ScatterPlan.sem_shapes(n_p),
        compiler_params=_params(("arbitrary", "arbitrary")),
    )(q, k, v, g_col, g_row, b_col, b_row, states, tinvs, do, *partials)
    return res[0], res[1], res[2], res[3], res[4:]


def _shift_down(prev8, cur, k):
    if k == 0:
        return cur
    ext = jnp.concatenate([prev8, cur], axis=0)
    return pltpu.roll(ext, k, 0)[SUBLANES:, :]


def _shift_up(cur, next8, k):
    if k == 0:
        return cur
    ext = jnp.concatenate([cur, next8], axis=0)
    n = ext.shape[0]
    return pltpu.roll(ext, n - k, 0)[:cur.shape[0], :]


def _conv_pre(i, x, prev8, w):
    prev8 = jnp.where(i == 0, 0.0, prev8)
    pre = None
    for j in range(GDN_CONV):
        term = w[j:j + 1, :] * _shift_down(prev8, x, GDN_CONV - 1 - j)
        pre = term if pre is None else pre + term
    return pre, prev8


def _l2_fwd(a, mult):
    return a * (lax.rsqrt(jnp.sum(a * a, axis=1, keepdims=True) + EPS) * mult)


def _l2_bwd(a, dy, mult):
    r = lax.rsqrt(jnp.sum(a * a, axis=1, keepdims=True) + EPS)
    dy = dy * mult
    return r * dy - a * (r * r * r) * jnp.sum(a * dy, axis=1, keepdims=True)


def _conv_fwd(xb, conv_w, group, *, norm, mult, tr=256):
    d = xb.shape[1] // 3

    def fn(i, nt, tiles, prev8, next8, cv):
        pre, _ = _conv_pre(i, tiles[0], prev8[0], cv[0])
        a = _silu(pre)
        if norm:
            a = _per_head(lambda ah: _l2_fwd(ah, mult), a)
        return [a], []

    col = Col(xb, d, group)
    wg = lax.slice_in_dim(conv_w, group * d, (group + 1) * d, axis=1)
    (y,), _ = _ew(f"conv_fwd{group}", fn, tr=tr, ins=[col], halo_prev=[col], consts=[wg], outs=[(d, F32)])
    return y


def _conv_bwd(xb, conv_w, group, dy, *, norm, mult, tr=256):
    d = xb.shape[1] // 3
    col = Col(xb, d, group)
    wg = lax.slice_in_dim(conv_w, group * d, (group + 1) * d, axis=1)

    def fn_pre(i, nt, tiles, prev8, next8, cv):
        x, dyt = tiles
        pre, p8 = _conv_pre(i, x, prev8[0], cv[0])
        if norm:
            da = _per_head(lambda ah, dh: _l2_bwd(ah, dh, mult), _silu(pre), dyt)
        else:
            da = dyt
        dpre = da * _dsilu(pre)
        tap = lax.broadcasted_iota(jnp.int32, (GDN_CONV, d), 0)
        dw = jnp.zeros((GDN_CONV, d), F32)
        for j in range(GDN_CONV):
            dw = dw + jnp.where(tap == j, _colsum(dpre * _shift_down(p8, x, GDN_CONV - 1 - j)), 0.0)
        return [dpre], [dw]

    (dpre,), (dw,) = _ew(f"conv_bwd_pre{group}", fn_pre, tr=tr, ins=[col, dy], halo_prev=[col], consts=[wg],
                         outs=[(d, F32)], accs=[(GDN_CONV, d)])

    def fn_dx(i, nt, tiles, prev8, next8, cv):
        n8 = jnp.where(i == nt - 1, 0.0, next8[0])
        dx = None
        for j in range(GDN_CONV):
            term = cv[0][j:j + 1, :] * _shift_up(tiles[0], n8, GDN_CONV - 1 - j)
            dx = term if dx is None else dx + term
        return [dx], []

    (dx,), _ = _ew(f"conv_bwd_dx{group}", fn_dx, tr=tr, ins=[dpre], halo_next=[dpre], consts=[wg], outs=[(d, BF16)])
    return dx, dw


def _adamw(name, w, m, v, grads, *, tr=64):
    shape = w.shape
    w2, m2, v2 = [a.reshape(-1, shape[-1]) for a in (w, m, v)]
    n_g = len(grads)
    bc1 = 1.0 - ADAM_B1 ** ADAM_STEP
    bc2 = 1.0 - ADAM_B2 ** ADAM_STEP

    def fn(i, nt, tiles, prev8, next8, cv):
        wt, mt, vt = tiles[:3]
        g = tiles[3]
        for extra in tiles[4:]:
            g = g + extra
        mn = ADAM_B1 * mt + (1.0 - ADAM_B1) * g
        vn = ADAM_B2 * vt + (1.0 - ADAM_B2) * (g * g)
        delta = -ADAM_LR * ((mn / bc1) / (jnp.sqrt(vn / bc2) + ADAM_EPS) + ADAM_WD * wt)
        return [g, delta, mn, vn], []

    width = shape[-1]
    outs, _ = _ew(name, fn, tr=tr, ins=[w2, m2, v2] + list(grads), outs=[(width, F32)] * 4)
    assert n_g >= 1
    return tuple(o.reshape(shape) for o in outs)


def _pad_cols(a, width):
    return jnp.pad(a, ((0, 0), (0, width - a.shape[1])))


def _gdn_layouts(gbeta, nh, n_chunks):
    grp = _gdn_group(nh)
    s_len = gbeta.shape[0]

    def lay(a):
        col = a.reshape(s_len, nh // grp, grp).transpose(1, 0, 2)
        row = a.reshape(n_chunks, GDN_CHUNK, nh // grp, grp).transpose(2, 0, 3, 1)
        return col, row

    b_col, b_row = lay(gbeta[:, :nh])
    g_col, g_row = lay(gbeta[:, nh:2 * nh])
    return g_col, g_row, b_col, b_row


def kernel(x, c, w_mod, b_mod, norm1_w, w_in, q_norm_w, k_norm_w, conv_w, a_log, dt_bias, o_norm_w, p_a, p_b, w_out, norm2_w, w_gate, w_up, w_down, loss_target, m_w_mod, m_b_mod, m_norm1_w, m_w_in, m_q_norm_w, m_k_norm_w, m_conv_w, m_a_log, m_dt_bias, m_o_norm_w, m_p_a, m_p_b, m_w_out, m_norm2_w, m_w_gate, m_w_up, m_w_down, v_w_mod, v_b_mod, v_norm1_w, v_w_in, v_q_norm_w, v_k_norm_w, v_conv_w, v_a_log, v_dt_bias, v_o_norm_w, v_p_a, v_p_b, v_w_out, v_norm2_w, v_w_gate, v_w_up, v_w_down):
    s_len, d = x.shape[1], x.shape[2]
    nh = d // HEAD_DIM
    n_chunks = s_len // GDN_CHUNK
    ff = 4 * w_gate.shape[2]
    mx, my, mc = _my_pos()
    chip = 2 * mx + my
    dev = 2 * chip + mc
    x2 = x[0]
    tgt = loss_target[0]

    c_all = _allgather8("ag_c", _pad_cols(c, d).reshape(SUBLANES, d // SUBLANES)).reshape(8, d)
    wm = w_mod[0]
    mod_w = wm.shape[1]
    bm_cols = lax.dynamic_slice_in_dim(b_mod, chip * mod_w, mod_w, axis=1)

    def mod_body(c_ref, w_ref, b_ref, o_ref, ca_ref):
        ca = _silu(c_ref[...])
        ca_ref[...] = ca
        o_ref[...] = _dot(ca, w_ref[...], hi=HIGHEST) + b_ref[...]

    tn_mod = _pick(mod_w, 512)
    mod8, c_act = _pcall(
        mod_body, name="mod_fwd", grid=(mod_w // tn_mod,),
        out_shape=[jax.ShapeDtypeStruct((8, mod_w), F32), jax.ShapeDtypeStruct((8, d), F32)],
        in_specs=[pl.BlockSpec((8, d), lambda j: (0, 0)), pl.BlockSpec((d, tn_mod), lambda j: (0, j)),
                  pl.BlockSpec((1, tn_mod), lambda j: (0, j))],
        out_specs=[pl.BlockSpec((8, tn_mod), lambda j: (0, j)), pl.BlockSpec((8, d), lambda j: (0, 0))],
        compiler_params=_params(("arbitrary",)),
    )(c_all, wm, bm_cols)
    mod_all = _allgather8("ag_mod", mod8)
    mod_me = mod_all.reshape(4, 2, 8, mod_w)[:, mc, dev, :].reshape(1, 6 * d)
    shift1, scale1, gate1, shift2, scale2, gate2 = [mod_me[:, j * d:(j + 1) * d] for j in range(6)]

    first = [w_in[0].astype(BF16), conv_w[0]]
    late_sb = [p_a[0].astype(BF16), p_b[0].astype(BF16), w_out[0].astype(BF16)]
    late_gdn = [w_gate[0].astype(BF16), w_up[0].astype(BF16), w_down[0].astype(BF16)]
    w_in_g, conv_g = [_fill_slot(g, sh, chip) for g, sh in zip(_gather4("ag_w_in", first, n_split=1), first)]
    w_in_f = w_in_g.transpose(1, 0, 2).reshape(d, -1)
    wa = w_in_f[:, :3 * d]
    wb = w_in_f[:, 3 * d:6 * d]
    wzg = jnp.concatenate([w_in_f[:, 6 * d:7 * d], w_in_f[:, 7 * d + 2 * nh:]], axis=1)
    wba = _pad_cols(w_in_f[:, 7 * d:7 * d + 2 * nh], LANES)
    conv_f = conv_g.transpose(1, 0, 2).reshape(GDN_CONV, 3 * d)

    def norm_mod_fn(i, nt, tiles, prev8, next8, cv):
        w, sc, sh = cv
        return [_rms_fwd(tiles[0], w) * (1.0 + sc) + sh], []

    (u1,), _ = _ew("norm_mod1", norm_mod_fn, tr=512, ins=[x2], consts=[norm1_w, scale1, shift1], outs=[(d, BF16)])
    proj_a = _mm("proj_a", u1, wa, out_dtype=BF16)
    proj_b = _mm("proj_b", u1, wb)
    proj_zg = _mm("proj_zg", u1, wzg, out_dtype=BF16)
    proj_ba = _mm("proj_ba", u1, wba)

    def qknorm_fn(i, nt, tiles, prev8, next8, cv):
        qa, ka, va = [t.astype(F32) for t in tiles]
        return [_per_head(lambda h: _rms_fwd(h, cv[0]), qa), _per_head(lambda h: _rms_fwd(h, cv[1]), ka), va], []

    (qn, kn, vb), _ = _ew("qknorm", qknorm_fn, tr=256,
                          ins=[Col(proj_a, d, 0), Col(proj_a, d, 1), Col(proj_a, d, 2)],
                          consts=[q_norm_w, k_norm_w], outs=[(d, BF16)] * 3)
    sb_out = _sb_fwd(qn, kn, vb, late_sb)
    o_a = sb_out[0]
    p_a_f, p_b_f, w_out_f = [_fill_slot(g, sh, chip).reshape(d, d) for g, sh in zip(sb_out[1:], late_sb)]

    lane_ids = jnp.arange(LANES)
    is_b = (lane_ids < nh)[None, :]
    is_a = ((lane_ids >= nh) & (lane_ids < 2 * nh))[None, :]
    alog128 = jnp.zeros((1, LANES), F32).at[:, nh:2 * nh].set(a_log)
    dtb128 = jnp.zeros((1, LANES), F32).at[:, nh:2 * nh].set(dt_bias)
    is_b_f, is_a_f = is_b.astype(F32), is_a.astype(F32)

    def gbeta_fn(i, nt, tiles, prev8, next8, cv):
        al, dtb, mb, ma = cv
        ba = tiles[0]
        g = -jnp.exp(al) * _softplus(ba + dtb)
        return [jnp.where(mb > 0.5, _sigmoid(ba), jnp.where(ma > 0.5, g, 0.0))], []

    (gbeta,), _ = _ew("gbeta", gbeta_fn, tr=1024, ins=[proj_ba], consts=[alog128, dtb128, is_b_f, is_a_f],
                      outs=[(LANES, F32)])
    g_col, g_row, b_col, b_row = _gdn_layouts(gbeta, nh, n_chunks)
    qscale = HEAD_DIM ** -0.5
    q_b = _conv_fwd(proj_b, conv_f, 0, norm=True, mult=qscale)
    k_b = _conv_fwd(proj_b, conv_f, 1, norm=True, mult=1.0)
    v_b = _conv_fwd(proj_b, conv_f, 2, norm=False, mult=1.0)
    o_raw, states, tinvs, late_g = _gdn_fwd(q_b, k_b, v_b, g_col, g_row, b_col, b_row, late_gdn,
                                            n_split=len(late_gdn))
    late_g = [_fill_slot(g, sh, chip) for g, sh in zip(late_g, late_gdn)]
    w_gate_f, w_up_f = [g.transpose(1, 0, 2).reshape(d, ff) for g in late_g[0:2]]
    w_down_f = late_g[2].reshape(ff, d)

    def gated_norm_fn(i, nt, tiles, prev8, next8, cv):
        o, z = tiles[0], tiles[1].astype(F32)
        return [_per_head(lambda h: _rms_fwd(h, cv[0]), o) * _silu(z)], []

    (o_b,), _ = _ew("gated_norm", gated_norm_fn, tr=256, ins=[o_raw, Col(proj_zg, d, 0)], consts=[o_norm_w],
                    outs=[(d, BF16)])
    y_a = _mm("out_a", o_a, p_a_f)
    y_b = _mm("out_b", o_b, p_b_f)

    def merge_fn(i, nt, tiles, prev8, next8, cv):
        ya, yb, ga, gb = [t.astype(F32) for t in tiles]
        return [_sigmoid(ga) * ya + _sigmoid(gb) * yb], []

    (merged,), _ = _ew("merge", merge_fn, tr=256, ins=[y_a, y_b, Col(proj_zg, d, 1), Col(proj_zg, d, 2)],
                       outs=[(d, BF16)])
    y_o = _mm("out_proj", merged, w_out_f)

    def resid_norm_fn(i, nt, tiles, prev8, next8, cv):
        xt, yo = tiles
        g1, w, sc, sh = cv
        h1 = xt + g1 * yo
        return [h1, _rms_fwd(h1, w) * (1.0 + sc) + sh], []

    (h1, u2), _ = _ew("resid_norm2", resid_norm_fn, tr=256, ins=[x2, y_o],
                      consts=[gate1, norm2_w, scale2, shift2], outs=[(d, F32), (d, BF16)])
    gt = _mm("ff_gate", u2, w_gate_f, out_dtype=BF16)
    up = _mm("ff_up", u2, w_up_f, out_dtype=BF16)

    def swiglu_fn(i, nt, tiles, prev8, next8, cv):
        return [_silu(tiles[0].astype(F32)) * tiles[1].astype(F32)], []

    (act,), _ = _ew("swiglu", swiglu_fn, tr=128, ins=[gt, up], outs=[(ff, BF16)])
    y_d = _mm("ff_down", act, w_down_f)

    def loss_fn(i, nt, tiles, prev8, next8, cv):
        h1t, yd, tg = tiles
        diff = h1t + cv[0] * yd - tg
        dy = diff * (1.0 / d)
        return [dy, dy * cv[0]], [_colsum(0.5 * diff * dy), _colsum(dy * yd)]

    (dy, dyd), (loss_cols, dgate2) = _ew("loss", loss_fn, tr=256, ins=[h1, y_d, tgt], consts=[gate2],
                                         outs=[(d, F32), (d, BF16)], accs=[(1, d), (1, d)])
    loss = lax.psum(jnp.sum(loss_cols), ("x", "y", "c"))

    dact = _mm("d_act", dyd, w_down_f, nt=True, out_dtype=BF16)
    g_w_down = _mm("g_w_down", act, dyd, ta=True)

    def swiglu_bwd_fn(i, nt, tiles, prev8, next8, cv):
        da, g, u = [t.astype(F32) for t in tiles]
        return [da * u * _dsilu(g), da * _silu(g)], []

    (dgt, dup), _ = _ew("swiglu_bwd", swiglu_bwd_fn, tr=128, ins=[dact, gt, up], outs=[(ff, BF16)] * 2)
    du2 = _mm("d_u2_up", dup, w_up_f, nt=True, add=_mm("d_u2_gate", dgt, w_gate_f, nt=True))
    g_w_gate = _mm("g_w_gate", u2, dgt, ta=True)
    g_w_up = _mm("g_w_up", u2, dup, ta=True)

    def norm2_bwd_fn(i, nt, tiles, prev8, next8, cv):
        h1t, du, dres, yo = tiles
        w, sc, g1 = cv
        r = lax.rsqrt(jnp.mean(h1t * h1t, axis=1, keepdims=True) + EPS)
        nrm = h1t * r
        dn = du * w * (1.0 + sc)
        dh = r * (dn - nrm * jnp.mean(dn * nrm, axis=1, keepdims=True)) + dres
        return [dh, dh * g1], [_colsum(du), _colsum(du * nrm * w), _colsum(du * nrm * (1.0 + sc)), _colsum(dh * yo)]

    (dh1, dyo), (dshift2, dscale2, g_norm2, dgate1) = _ew(
        "norm2_bwd", norm2_bwd_fn, tr=256, ins=[h1, du2, dy, y_o], consts=[norm2_w, scale2, gate1],
        outs=[(d, F32), (d, BF16)], accs=[(1, d)] * 4)

    dmerged = _mm("d_merged", dyo, w_out_f, nt=True)
    g_w_out = _mm("g_w_out", merged, dyo, ta=True)

    def merge_bwd_fn(i, nt, tiles, prev8, next8, cv):
        dm, ya, yb, ga, gb = [t.astype(F32) for t in tiles]
        sa, sb = _sigmoid(ga), _sigmoid(gb)
        return [dm * sa, dm * sb, dm * ya * sa * (1.0 - sa), dm * yb * sb * (1.0 - sb)], []

    (dya, dyb, dga, dgb_gate), _ = _ew(
        "merge_bwd", merge_bwd_fn, tr=256, ins=[dmerged, y_a, y_b, Col(proj_zg, d, 1), Col(proj_zg, d, 2)],
        outs=[(d, BF16)] * 4)
    do_a = _mm("d_o_a", dya, p_a_f, nt=True, out_dtype=BF16)
    g_p_a = _mm("g_p_a", o_a, dya, ta=True)
    do_b = _mm("d_o_b", dyb, p_b_f, nt=True)
    g_p_b = _mm("g_p_b", o_b, dyb, ta=True)

    def gated_norm_bwd_fn(i, nt, tiles, prev8, next8, cv):
        dob, o, z = tiles[0], tiles[1], tiles[2].astype(F32)
        sz = _silu(z)

        def head(oh, dh):
            return _rms_bwd(oh, cv[0], dh)

        dxo, dwn = _per_head(head, o, dob * sz)
        nrm_w = _per_head(lambda h: _rms_fwd(h, cv[0]), o)
        return [dxo, dob * nrm_w * _dsilu(z)], [_colsum(_head_sum(dwn))]

    (do_raw, dz_b), (g_o_norm,) = _ew(
        "gated_norm_bwd", gated_norm_bwd_fn, tr=256, ins=[do_b, o_raw, Col(proj_zg, d, 0)], consts=[o_norm_w],
        outs=[(d, F32), (d, BF16)], accs=[(1, HEAD_DIM)])
    by_chip = lambda a: a.reshape(a.shape[0], 4, -1).transpose(1, 0, 2)

    def chip_sums(tag, raw, axes):
        theirs = _sibling_send(f"swap_{tag}", raw, axes)
        sums = []
        for t, (part, ax, other) in enumerate(zip(raw, axes, theirs)):
            def pair_fn(i, nt, tiles, prev8, next8, cv):
                return [tiles[0] + tiles[1]], []

            hr, width = part.shape[ax] // 2, part.shape[-1]
            mine = lax.dynamic_slice_in_dim(part, mc * hr, hr, axis=ax)
            (ch,), _ = _ew(f"pair_sum_{tag}{t}", pair_fn, tr=64,
                           ins=[mine.reshape(-1, width), other.reshape(-1, width)], outs=[(width, BF16)])
            sums.append(ch.reshape(other.shape))
        return sums

    s_gate, s_up, s_pa, s_pb, s_out, s_down = chip_sums(
        "late", [g_w_gate, g_w_up, g_p_a.reshape(4, d // 4, d), g_p_b.reshape(4, d // 4, d),
                 g_w_out.reshape(4, d // 4, d), g_w_down.reshape(4, ff // 4, d)], [0, 0, 1, 1, 1, 1])
    late_halves = [s_pa, s_pb, s_out, by_chip(s_gate), by_chip(s_up), s_down]
    dq_b, dk_b, dv_b, dgb_grp, late_landed = _gdn_bwd(q_b, k_b, v_b, g_col, g_row, b_col, b_row, states, tinvs,
                                                      do_raw, late_halves)
    grp = _gdn_group(nh)
    dbeta = dgb_grp[:, :, :grp].transpose(1, 0, 2).reshape(s_len, nh)
    dg = dgb_grp[:, :, grp:2 * grp].transpose(1, 0, 2).reshape(s_len, nh)
    dgbeta = _pad_cols(jnp.concatenate([dbeta, dg], axis=1), LANES)

    def gbeta_bwd_fn(i, nt, tiles, prev8, next8, cv):
        al, dtb, mb, ma = cv
        ba, dgb = tiles
        beta = _sigmoid(ba)
        arg = ba + dtb
        da = dgb * (-jnp.exp(al)) * _sigmoid(arg)
        g = -jnp.exp(al) * _softplus(arg)
        dba = jnp.where(mb > 0.5, dgb * beta * (1.0 - beta), jnp.where(ma > 0.5, da, 0.0))
        return [dba], [_colsum(jnp.where(ma > 0.5, dgb * g, 0.0)), _colsum(jnp.where(ma > 0.5, da, 0.0))]

    (dba,), (g_alog128, g_dtb128) = _ew(
        "gbeta_bwd", gbeta_bwd_fn, tr=1024, ins=[proj_ba, dgbeta], consts=[alog128, dtb128, is_b_f, is_a_f],
        outs=[(LANES, BF16)], accs=[(1, LANES)] * 2)
    dxq, g_conv_q = _conv_bwd(proj_b, conv_f, 0, dq_b, norm=True, mult=qscale)
    dxk, g_conv_k = _conv_bwd(proj_b, conv_f, 1, dk_b, norm=True, mult=1.0)
    dxv, g_conv_v = _conv_bwd(proj_b, conv_f, 2, dv_b, norm=False, mult=1.0)
    g_conv = jnp.concatenate([g_conv_q, g_conv_k, g_conv_v], axis=1)

    dqn, dkn, dvb = _sb_bwd(qn, kn, vb, do_a)

    def qknorm_bwd_fn(i, nt, tiles, prev8, next8, cv):
        qa, ka, dq, dk, dv = [t.astype(F32) for t in tiles]
        dxq_, dwq = _per_head(lambda h, g: _rms_bwd(h, cv[0], g), qa, dq)
        dxk_, dwk = _per_head(lambda h, g: _rms_bwd(h, cv[1], g), ka, dk)
        return [dxq_, dxk_, dv], [_colsum(_head_sum(dwq)), _colsum(_head_sum(dwk))]

    (dqa, dka, dva), (g_q_norm, g_k_norm) = _ew(
        "qknorm_bwd", qknorm_bwd_fn, tr=256, ins=[Col(proj_a, d, 0), Col(proj_a, d, 1), dqn, dkn, dvb],
        consts=[q_norm_w, k_norm_w], outs=[(d, BF16)] * 3, accs=[(1, HEAD_DIM)] * 2)

    d_a = jnp.concatenate([dqa, dka, dva], axis=1)
    d_b = jnp.concatenate([dxq, dxk, dxv], axis=1)
    d_zg = jnp.concatenate([dz_b, dga, dgb_gate], axis=1)
    du1 = _mm("d_u1_a", d_a, wa, nt=True)
    du1 = _mm("d_u1_b", d_b, wb, nt=True, add=du1)
    du1 = _mm("d_u1_zg", d_zg, wzg, nt=True, add=du1)
    du1 = _mm("d_u1_ba", dba, wba, nt=True, add=du1)
    g_wa = _mm("g_w_in_a", u1, d_a, ta=True)
    g_wb = _mm("g_w_in_b", u1, d_b, ta=True)
    g_wzg = _mm("g_w_in_zg", u1, d_zg, ta=True)
    g_wba = _mm("g_w_in_ba", u1, dba, ta=True)

    def norm1_bwd_fn(i, nt, tiles, prev8, next8, cv):
        xt, du, dres = tiles
        w, sc = cv
        r = lax.rsqrt(jnp.mean(xt * xt, axis=1, keepdims=True) + EPS)
        nrm = xt * r
        dn = du * w * (1.0 + sc)
        dxt = r * (dn - nrm * jnp.mean(dn * nrm, axis=1, keepdims=True)) + dres
        return [dxt], [_colsum(du), _colsum(du * nrm * w), _colsum(du * nrm * (1.0 + sc))]

    (grad_x,), (dshift1, dscale1, g_norm1) = _ew(
        "norm1_bwd", norm1_bwd_fn, tr=256, ins=[x2, du1, dh1], consts=[norm1_w, scale1],
        outs=[(d, F32)], accs=[(1, d)] * 3)

    dmod_me = jnp.concatenate([dshift1, dscale1, dgate1, dshift2, dscale2, dgate2], axis=1)
    small = jnp.concatenate(
        [dmod_me, g_norm1, g_norm2, g_q_norm, g_k_norm, g_o_norm, g_alog128[:, nh:2 * nh], g_dtb128[:, nh:2 * nh],
         g_conv.reshape(1, -1)], axis=1)
    n_small = small.shape[1]
    pad_to = -(-n_small // (SUBLANES * LANES)) * (SUBLANES * LANES)
    small_all = _allgather8("ag_small", _pad_cols(small, pad_to).reshape(SUBLANES, pad_to // SUBLANES))
    small_all = small_all.reshape(8, pad_to)

    def sum8_fn(i, nt, tiles, prev8, next8, cv):
        return [], [_colsum(tiles[0])]

    _, (small_sum,) = _ew("sum_small", sum8_fn, tr=8, ins=[small_all], accs=[(1, pad_to)])
    offs = [0]
    for width in (6 * d, d, d, HEAD_DIM, HEAD_DIM, HEAD_DIM, nh, nh, GDN_CONV * 3 * d):
        offs.append(offs[-1] + width)
    pieces = [small_sum[:, offs[j]:offs[j + 1]] for j in range(9)]
    (gs_b_mod, gs_norm1, gs_norm2, gs_q_norm, gs_k_norm, gs_o_norm, gs_a_log, gs_dt_bias, gs_conv) = pieces
    conv_cols = 3 * d // 4
    gs_conv_mine = lax.dynamic_slice_in_dim(gs_conv.reshape(GDN_CONV, 3 * d), chip * conv_cols, conv_cols, axis=1)

    dmod_all = lax.dynamic_slice_in_dim(small_all[:, :6 * d], chip * mod_w, mod_w, axis=1)

    def wmod_grad_body(ct_ref, dm_ref, o_ref):
        o_ref[...] = _dot(ct_ref[...], dm_ref[...], hi=HIGHEST)

    g_w_mod = _pcall(
        wmod_grad_body, name="g_w_mod", grid=(mod_w // tn_mod,),
        out_shape=jax.ShapeDtypeStruct((d, mod_w), F32),
        in_specs=[pl.BlockSpec((d, 8), lambda j: (0, 0)), pl.BlockSpec((8, tn_mod), lambda j: (0, j))],
        out_specs=pl.BlockSpec((d, tn_mod), lambda j: (0, j)),
        compiler_params=_params(("arbitrary",)),
    )(c_act.T, dmod_all)

    s_wa, s_wb, s_wzg, s_wba = chip_sums("w_in", [g_wa, g_wb, g_wzg, g_wba], [0, 0, 0, 0])
    s_w_in = jnp.concatenate([s_wa, s_wb, s_wzg[:, :d], s_wba[:, :2 * nh], s_wzg[:, d:]], axis=1)
    chip_halves = [by_chip(s_w_in)] + late_halves
    landed = list(_scatter4("rs_w_in", chip_halves[:1])) + list(late_landed)
    landed = [_fill_slot(land, ch, chip) for land, ch in zip(landed, chip_halves)]
    g_mine = []
    for t, land in enumerate(landed):
        def sum4_fn(i, nt, tiles, prev8, next8, cv):
            f = [tl.astype(F32) for tl in tiles]
            return [(f[0] + f[1]) + (f[2] + f[3])], []

        (gh,), _ = _ew(f"chip_sum{t}", sum4_fn, tr=64, ins=[Col(land, lead=s) for s in range(4)],
                       outs=[(land.shape[-1], F32)])
        g_mine.append(gh)
    g_theirs = _sibling_send("join_grads", g_mine)
    g_full = [jnp.concatenate([jnp.where(mc == 0, a, b), jnp.where(mc == 0, b, a)], axis=0)
              for a, b in zip(g_mine, g_theirs)]

    big = {}
    names = ["w_in", "p_a", "p_b", "w_out", "w_gate", "w_up", "w_down"]
    big_w = [w_in, p_a, p_b, w_out, w_gate, w_up, w_down]
    big_m = [m_w_in, m_p_a, m_p_b, m_w_out, m_w_gate, m_w_up, m_w_down]
    big_v = [v_w_in, v_p_a, v_p_b, v_w_out, v_w_gate, v_w_up, v_w_down]
    for t, nm in enumerate(names):
        big[nm] = _adamw(f"adamw_{nm}", big_w[t], big_m[t], big_v[t], [g_full[t]])
    big["w_mod"] = _adamw("adamw_w_mod", w_mod, m_w_mod, v_w_mod, [g_w_mod])
    big["conv_w"] = _adamw("adamw_conv_w", conv_w, m_conv_w, v_conv_w, [gs_conv_mine], tr=8)
    small_names = ["b_mod", "norm1_w", "norm2_w", "q_norm_w", "k_norm_w", "o_norm_w", "a_log", "dt_bias"]
    small_w = [b_mod, norm1_w, norm2_w, q_norm_w, k_norm_w, o_norm_w, a_log, dt_bias]
    small_m = [m_b_mod, m_norm1_w, m_norm2_w, m_q_norm_w, m_k_norm_w, m_o_norm_w, m_a_log, m_dt_bias]
    small_v = [v_b_mod, v_norm1_w, v_norm2_w, v_q_norm_w, v_k_norm_w, v_o_norm_w, v_a_log, v_dt_bias]
    small_g = [gs_b_mod, gs_norm1, gs_norm2, gs_q_norm, gs_k_norm, gs_o_norm, gs_a_log, gs_dt_bias]
    rep_w = jnp.concatenate(small_w, axis=1)
    rep_m = jnp.concatenate(small_m, axis=1)
    rep_v = jnp.concatenate(small_v, axis=1)
    rep_g = jnp.concatenate(small_g, axis=1)
    rep = _adamw("adamw_small", rep_w, rep_m, rep_v, [rep_g], tr=1)
    roffs = [0]
    for a in small_w:
        roffs.append(roffs[-1] + a.shape[1])
    for j, nm in enumerate(small_names):
        big[nm] = tuple(r[:, roffs[j]:roffs[j + 1]] for r in rep)

    order = ["w_mod", "b_mod", "norm1_w", "w_in", "q_norm_w", "k_norm_w", "conv_w", "a_log", "dt_bias", "o_norm_w",
             "p_a", "p_b", "w_out", "norm2_w", "w_gate", "w_up", "w_down"]
    grads = [big[nm][0] for nm in order]
    deltas = [big[nm][1] for nm in order]
    new_m = [big[nm][2] for nm in order]
    new_v = [big[nm][3] for nm in order]
    return (loss, grad_x[None], *grads, *deltas, *new_m, *new_v)
```

```python
import jax
import jax.numpy as jnp
from jax import lax
from jax.experimental import pallas as pl
from jax.experimental.pallas import tpu as pltpu

F32 = jnp.float32
BF16 = jnp.bfloat16
HIGHEST = lax.Precision.HIGHEST
HIGH = lax.Precision.HIGH
MESH = pl.DeviceIdType.MESH

HEAD_DIM = 128
GDN_CHUNK = 64
GDN_CONV = 4
EPS = 1e-6
LANES = 128
SUBLANES = 8
VMEM_LIMIT = 56 * 1024 * 1024
MM_VMEM_BUDGET = 40 * 1024 * 1024

ADAM_LR = 0.001
ADAM_B1 = 0.9
ADAM_B2 = 0.999
ADAM_EPS = 1e-08
ADAM_WD = 0.01
ADAM_STEP = 10


def _pcall(body, **kw):
    return pl.pallas_call(body, **kw)


def _params(sem=None):
    if sem is None:
        return pltpu.CompilerParams(vmem_limit_bytes=VMEM_LIMIT)
    return pltpu.CompilerParams(dimension_semantics=sem, vmem_limit_bytes=VMEM_LIMIT)


def _pick(dim, target):
    if dim <= target:
        return dim
    best = None
    for t in range(LANES, target + 1, LANES):
        if dim % t == 0:
            best = t
    assert best is not None, (dim, target)
    return best


def _rows_tile(rows, target):
    t = min(rows, target)
    while rows % t:
        t //= 2
    assert t >= SUBLANES or t == rows, (rows, target)
    return t


def _dot(a, b, hi=None):
    return jnp.dot(a, b, preferred_element_type=F32, precision=hi)


def _dot_nt(a, b, hi=None):
    return lax.dot_general(a, b, (((1,), (1,)), ((), ())), preferred_element_type=F32, precision=hi)


def _dot_tn(a, b, hi=None):
    return lax.dot_general(a, b, (((0,), (0,)), ((), ())), preferred_element_type=F32, precision=hi)


def _sigmoid(x):
    return 1.0 / (1.0 + jnp.exp(-x))


def _softplus(x):
    return jnp.maximum(x, 0.0) + jnp.log(1.0 + jnp.exp(-jnp.abs(x)))


_HBM = pl.BlockSpec(memory_space=pltpu.HBM)


def _my_pos():
    return lax.axis_index("x"), lax.axis_index("y"), lax.axis_index("c")


def _allgather8(name, v):
    def body(v_ref, o_ref, ssem, rsem, lsem):
        x, y, c = _my_pos()
        me = 4 * x + 2 * y + c
        loc = pltpu.make_async_copy(v_ref, o_ref.at[me], lsem)
        loc.start()
        sends, recvs = [], []
        for k in range(1, 8):
            px, py, pc = (x + (k >> 2)) % 2, (y + ((k >> 1) & 1)) % 2, (c + (k & 1)) % 2
            cp = pltpu.make_async_remote_copy(
                src_ref=v_ref, dst_ref=o_ref.at[me], send_sem=ssem.at[k - 1], recv_sem=rsem.at[k - 1],
                device_id=(px, py, pc), device_id_type=MESH)
            cp.start()
            sends.append(cp)
            recvs.append(pltpu.make_async_remote_copy(
                src_ref=v_ref, dst_ref=o_ref.at[4 * px + 2 * py + pc], send_sem=ssem.at[k - 1],
                recv_sem=rsem.at[k - 1], device_id=(px, py, pc), device_id_type=MESH))
        for rc in recvs:
            rc.wait_recv()
        for cp in sends:
            cp.wait_send()
        loc.wait()

    return _pcall(
        body, name=name, out_shape=jax.ShapeDtypeStruct((8,) + v.shape, v.dtype),
        in_specs=[_HBM], out_specs=_HBM,
        scratch_shapes=[pltpu.SemaphoreType.DMA((7,)), pltpu.SemaphoreType.DMA((7,)), pltpu.SemaphoreType.DMA],
    )(v)


def _plane_peers(x, y):
    return [((x + (k >> 1)) % 2, (y + (k & 1)) % 2) for k in range(1, 4)]


class _GatherPlan:
    def __init__(self, ins, outs, sems, n_split):
        ssem, rsem, fsem, gsem = sems
        x, y, c = _my_pos()
        me = 2 * x + y
        copy = lambda src, dst, s_sem, r_sem, dev: (lambda: pltpu.make_async_remote_copy(
            src_ref=src, dst_ref=dst, send_sem=s_sem, recv_sem=r_sem, device_id=dev, device_id_type=MESH))
        self.sends, self.recvs, self.fwds, self.fwd_recvs = [], [], [], []
        for t in range(len(ins)):
            split = t < n_split
            hr = ins[t].shape[0] // 2
            for k, (px, py) in enumerate(_plane_peers(x, y)):
                peer = 2 * px + py
                sem = 3 * t + k
                if split:
                    mine = pl.ds(pl.multiple_of(c * hr, 16), hr)
                    other = pl.ds(pl.multiple_of((1 - c) * hr, 16), hr)
                    src, dst, got = ins[t].at[mine], outs[t].at[me, mine], outs[t].at[peer, mine]
                else:
                    src, dst, got = ins[t], outs[t].at[me], outs[t].at[peer]
                self.sends.append(copy(src, dst, ssem.at[sem], rsem.at[sem], (px, py, c)))
                self.recvs.append(copy(src, got, ssem.at[sem], rsem.at[sem], (px, py, c)))
                if split:
                    self.fwds.append(copy(got, got, fsem.at[sem], gsem.at[sem], (x, y, 1 - c)))
                    self.fwd_recvs.append(copy(got, outs[t].at[peer, other], fsem.at[sem], gsem.at[sem], (x, y, 1 - c)))
                else:
                    self.fwds.append(None)

    def start(self):
        for cp in self.sends:
            cp().start()

    def relay(self):
        for rc, fw in zip(self.recvs, self.fwds):
            rc().wait_recv()
            if fw is not None:
                fw().start()

    def finish(self):
        for fr in self.fwd_recvs:
            fr().wait_recv()
        for cp in self.sends + [fw for fw in self.fwds if fw is not None]:
            cp().wait_send()

    @staticmethod
    def out_shapes(shards):
        return [jax.ShapeDtypeStruct((4,) + s.shape, s.dtype) for s in shards]

    @staticmethod
    def sem_shapes(n):
        return [pltpu.SemaphoreType.DMA((3 * n,))] * 4


def _gather4(name, shards, n_split):
    n = len(shards)

    def body(*refs):
        plan = _GatherPlan(refs[:n], refs[n:2 * n], refs[2 * n:], n_split)
        plan.start()
        plan.relay()
        plan.finish()

    return _pcall(
        body, name=name, out_shape=_GatherPlan.out_shapes(shards), in_specs=[_HBM] * n, out_specs=[_HBM] * n,
        scratch_shapes=_GatherPlan.sem_shapes(n),
    )(*shards)


def _when_step(h, n, hs, ns):
    return pl.when(jnp.logical_and(pl.program_id(0) == (hs if h < 0 else h), pl.program_id(1) == (ns if n < 0 else n)))


def _fill_slot(slots, own, slot):
    mask = (jnp.arange(4) == slot).reshape((4,) + (1,) * (slots.ndim - 1))
    return jnp.where(mask, own if own.ndim == slots.ndim else own[None], slots)


class _ScatterPlan:
    def __init__(self, ins, outs, sems):
        ssem, rsem = sems
        x, y, c = _my_pos()
        me = 2 * x + y
        copy = lambda src, dst, s_sem, r_sem, dev: (lambda: pltpu.make_async_remote_copy(
            src_ref=src, dst_ref=dst, send_sem=s_sem, recv_sem=r_sem, device_id=dev, device_id_type=MESH))
        self.sends, self.recvs = [], []
        for t in range(len(ins)):
            for k, (px, py) in enumerate(_plane_peers(x, y)):
                peer = 2 * px + py
                sem = 3 * t + k
                self.sends.append(copy(ins[t].at[peer], outs[t].at[me], ssem.at[sem], rsem.at[sem], (px, py, c)))
                self.recvs.append(copy(ins[t].at[peer], outs[t].at[peer], ssem.at[sem], rsem.at[sem], (px, py, c)))

    def start(self):
        for cp in self.sends:
            cp().start()

    def finish(self):
        for rc in self.recvs:
            rc().wait_recv()
        for cp in self.sends:
            cp().wait_send()

    @staticmethod
    def out_shapes(partials):
        return [jax.ShapeDtypeStruct(p.shape, p.dtype) for p in partials]

    @staticmethod
    def sem_shapes(n):
        return [pltpu.SemaphoreType.DMA((3 * n,))] * 2


def _sibling_send(name, arrays, axes=None):
    n = len(arrays)
    axes = [None] * n if axes is None else axes

    def body(*refs):
        ins, outs = refs[:n], refs[n:2 * n]
        ssem, rsem = refs[2 * n:]
        x, y, c = _my_pos()
        cps = []
        for t in range(n):
            src = ins[t]
            if axes[t] is not None:
                hr = ins[t].shape[axes[t]] // 2
                give = pl.ds(pl.multiple_of((1 - c) * hr, SUBLANES), hr)
                src = ins[t].at[give] if axes[t] == 0 else ins[t].at[:, give]
            cp = pltpu.make_async_remote_copy(
                src_ref=src, dst_ref=outs[t], send_sem=ssem.at[t], recv_sem=rsem.at[t],
                device_id=(x, y, 1 - c), device_id_type=MESH)
            cp.start()
            cps.append(cp)
        for cp in cps:
            cp.wait_recv()
        for cp in cps:
            cp.wait_send()

    def half(a, axis):
        shape = list(a.shape)
        if axis is not None:
            shape[axis] //= 2
        return jax.ShapeDtypeStruct(tuple(shape), a.dtype)

    return _pcall(
        body, name=name, out_shape=[half(a, ax) for a, ax in zip(arrays, axes)], in_specs=[_HBM] * n,
        out_specs=[_HBM] * n,
        scratch_shapes=[pltpu.SemaphoreType.DMA((n,)), pltpu.SemaphoreType.DMA((n,))],
    )(*arrays)


def _mm(name, a, b, *, nt=False, ta=False, out_dtype=F32, add=None, tm=1024, tn=1024, tk=4096, b_cols=None,
        scatter=()):
    m, k = (a.shape[1], a.shape[0]) if ta else a.shape
    n = b.shape[0] if nt else b.shape[1]
    assert (b.shape[1] if nt else b.shape[0]) == k
    col0 = 0
    if b_cols is not None:
        assert not nt
        col0, n = b_cols
    has_add = add is not None
    tm, tn = _pick(m, tm), _pick(n, tn)
    assert col0 % tn == 0
    n_sc = len(scatter)
    out_bytes = jnp.dtype(out_dtype).itemsize

    def vmem_bytes(tk_):
        steps = k // tk_
        return (4 * (tm + tn) * tk_ + 2 * tm * tn * out_bytes + (8 * tm * tn if has_add else 0)
                + (4 * tm * tn if steps > 1 else 0))

    tk = _pick(k, tk)
    while vmem_bytes(tk) > MM_VMEM_BUDGET and tk > 512:
        tk = _pick(k, tk - LANES)
    nk = k // tk

    grid = (n // tn, m // tm, nk)

    def body(*refs):
        a_ref, b_ref = refs[0], refs[1]
        c_ref = refs[2] if has_add else None
        part_refs = refs[2 + has_add:2 + has_add + n_sc]
        o_ref = refs[2 + has_add + n_sc]
        land_refs = refs[3 + has_add + n_sc:3 + has_add + 2 * n_sc]
        rest = refs[3 + has_add + 2 * n_sc:]
        if n_sc:
            sems = rest[-2:]
            at = lambda step: pl.when(jnp.logical_and(jnp.logical_and(
                pl.program_id(0) == step[0], pl.program_id(1) == step[1]), pl.program_id(2) == step[2]))

            @at((0, 0, 0))
            def _():
                _ScatterPlan(part_refs, land_refs, sems).start()

            @at(tuple(g - 1 for g in grid))
            def _():
                _ScatterPlan(part_refs, land_refs, sems).finish()

        p = (_dot_tn if ta else _dot_nt if nt else _dot)(a_ref[...], b_ref[...])
        if nk == 1:
            o_ref[...] = (p + c_ref[...] if has_add else p).astype(o_ref.dtype)
            return
        acc = rest[0]
        kk = pl.program_id(2)

        @pl.when(kk == 0)
        def _():
            acc[...] = p

        @pl.when(jnp.logical_and(kk > 0, kk < nk - 1))
        def _():
            acc[...] += p

        @pl.when(kk == nk - 1)
        def _():
            r = acc[...] + p
            if has_add:
                r = r + c_ref[...]
            o_ref[...] = r.astype(o_ref.dtype)

    if ta:
        a_spec = pl.BlockSpec((tk, tm), lambda j, i, kk: (kk, i))
    else:
        a_spec = pl.BlockSpec((tm, tk), lambda j, i, kk: (i, kk))
    if nt:
        b_spec = pl.BlockSpec((tn, tk), lambda j, i, kk: (j, kk))
    else:
        b_spec = pl.BlockSpec((tk, tn), lambda j, i, kk: (kk, j + col0 // tn))
    o_spec = pl.BlockSpec((tm, tn), lambda j, i, kk: (i, j))
    in_specs = [a_spec, b_spec] + ([o_spec] if has_add else []) + [_HBM] * n_sc
    args = (a, b) + ((add,) if has_add else ()) + tuple(scatter)
    res = _pcall(
        body, name=name, grid=grid,
        out_shape=[jax.ShapeDtypeStruct((m, n), out_dtype)] + _ScatterPlan.out_shapes(scatter),
        in_specs=in_specs, out_specs=[o_spec] + [_HBM] * n_sc,
        scratch_shapes=([pltpu.VMEM((tm, tn), F32)] if nk > 1 else []) + (_ScatterPlan.sem_shapes(n_sc) if n_sc else []),
        compiler_params=_params(("arbitrary",) * 3 if n_sc else ("parallel", "parallel", "arbitrary")),
    )(*args)
    return (res[0], res[1:]) if n_sc else res[0]


class Col:
    def __init__(self, arr, w=None, cb=0, lead=None):
        self.arr, self.cb, self.lead = arr, cb, lead
        self.w = arr.shape[-1] if w is None else w
        self.rows = arr.shape[-2]


def _ew(name, fn, *, tr, ins, consts=(), outs=(), accs=(), halo_prev=(), halo_next=()):
    ins = [c if isinstance(c, Col) else Col(c) for c in ins]
    halo_prev = [c if isinstance(c, Col) else Col(c) for c in halo_prev]
    halo_next = [c if isinstance(c, Col) else Col(c) for c in halo_next]
    rows = ins[0].rows
    tr = _rows_tile(rows, tr)
    nt = rows // tr
    n_in, n_hp, n_hn, n_c, n_o, n_a = len(ins), len(halo_prev), len(halo_next), len(consts), len(outs), len(accs)
    groups = tr // SUBLANES

    def spec(col, kind):
        if kind == "cur":
            shape, idx = (tr, col.w), (lambda i, cb=col.cb: (i, cb))
        elif kind == "prev":
            shape, idx = (SUBLANES, col.w), (lambda i, cb=col.cb: (jnp.maximum(i * groups - 1, 0), cb))
        else:
            shape = (SUBLANES, col.w)
            idx = (lambda i, cb=col.cb: (jnp.minimum((i + 1) * groups, rows // SUBLANES - 1), cb))
        if col.lead is None:
            return pl.BlockSpec(shape, idx)
        return pl.BlockSpec((None,) + shape, lambda i, idx=idx, lead=col.lead: (lead,) + idx(i))

    def body(*refs):
        i = pl.program_id(0)
        p = 0
        tiles = [r[...] for r in refs[p:p + n_in]]; p += n_in
        prev8 = [r[...] for r in refs[p:p + n_hp]]; p += n_hp
        next8 = [r[...] for r in refs[p:p + n_hn]]; p += n_hn
        cvals = [r[...] for r in refs[p:p + n_c]]; p += n_c
        out_refs = refs[p:p + n_o]; p += n_o
        acc_refs = refs[p:p + n_a]
        out_v, acc_v = fn(i, nt, tiles, prev8, next8, cvals)
        for r, v in zip(out_refs, out_v):
            r[...] = v.astype(r.dtype)
        if n_a:
            @pl.when(i == 0)
            def _():
                for r, v in zip(acc_refs, acc_v):
                    r[...] = v

            @pl.when(i > 0)
            def _():
                for r, v in zip(acc_refs, acc_v):
                    r[...] += v

    in_specs = ([spec(c, "cur") for c in ins] + [spec(c, "prev") for c in halo_prev]
                + [spec(c, "next") for c in halo_next]
                + [pl.BlockSpec(c.shape, lambda i, nd=c.ndim: (0,) * nd) for c in consts])
    out_specs = ([pl.BlockSpec((tr, w), lambda i: (i, 0)) for w, _ in outs]
                 + [pl.BlockSpec(s, lambda i: (0, 0)) for s in accs])
    out_shape = ([jax.ShapeDtypeStruct((rows, w), dt) for w, dt in outs]
                 + [jax.ShapeDtypeStruct(s, F32) for s in accs])
    args = [c.arr for c in ins] + [c.arr for c in halo_prev] + [c.arr for c in halo_next] + list(consts)
    res = _pcall(body, name=name, grid=(nt,), out_shape=out_shape, in_specs=in_specs, out_specs=out_specs,
                 compiler_params=_params(("arbitrary",)))(*args)
    return res[:n_o], res[n_o:]


def _colsum(v):
    return jnp.sum(v, axis=0, keepdims=True)


def _heads_of(w):
    return w // HEAD_DIM


def _per_head(fn, *arrays):
    nh = _heads_of(arrays[0].shape[1])
    res = [fn(*[a[:, h * HEAD_DIM:(h + 1) * HEAD_DIM] for a in arrays]) for h in range(nh)]
    if isinstance(res[0], tuple):
        return tuple(jnp.concatenate([r[j] for r in res], axis=1) for j in range(len(res[0])))
    return jnp.concatenate(res, axis=1)


def _head_sum(v):
    nh = _heads_of(v.shape[1])
    out = v[:, :HEAD_DIM]
    for h in range(1, nh):
        out = out + v[:, h * HEAD_DIM:(h + 1) * HEAD_DIM]
    return out


def _rms_fwd(x, w):
    r = lax.rsqrt(jnp.mean(x * x, axis=1, keepdims=True) + EPS)
    return x * r * w


def _rms_bwd(x, w, dy):
    r = lax.rsqrt(jnp.mean(x * x, axis=1, keepdims=True) + EPS)
    xh = x * r
    dxh = dy * w
    dx = r * (dxh - xh * jnp.mean(dxh * xh, axis=1, keepdims=True))
    return dx, dy * xh


def _silu(x):
    return x * _sigmoid(x)


def _dsilu(x):
    s = _sigmoid(x)
    return s * (1.0 + x * (1.0 - s))


SB_BQ = 512
SB_CUTOFF = 112.0
SB_PAIR = 2
SB_BK = 256


def _softplus_pos(z):
    return jnp.maximum(z, 0.0) + jnp.log(1.0 + jnp.exp(-jnp.abs(z)))


def _split_dot(v, tri):
    top = lax.bitcast_convert_type(lax.bitcast_convert_type(v, jnp.int32) & jnp.int32(-65536), F32)
    return _dot(top.astype(BF16), tri) + _dot((v - top).astype(BF16), tri)


def _sb_fwd(qn, kn, vb, shards, *, bq=SB_BQ, bk=SB_BK):
    s_len, hd = qn.shape
    nh = hd // HEAD_DIM
    bk = min(bk, s_len)
    bq = min(bq, s_len)
    ndiag = bq // bk
    scale = HEAD_DIM ** -0.5

    n_sh = len(shards)
    last_h, last_i = nh - 1, s_len // bq - 1

    def body(q_ref, k_ref, v_ref, *rest):
        sh_refs, o_ref = rest[:n_sh], rest[n_sh]
        got_refs, sems = rest[n_sh + 1:2 * n_sh + 1], rest[2 * n_sh + 1:]
        i = pl.program_id(1)

        if n_sh:
            @_when_step(0, 0, last_h, last_i)
            def _():
                _GatherPlan(sh_refs, got_refs, sems, n_sh).start()

            @_when_step(nh // 2, 0, last_h, last_i)
            def _():
                _GatherPlan(sh_refs, got_refs, sems, n_sh).relay()

            @_when_step(-1, -1, last_h, last_i)
            def _():
                _GatherPlan(sh_refs, got_refs, sems, n_sh).finish()

        krow = lax.broadcasted_iota(jnp.int32, (bk, bk), 0)
        kcol = lax.broadcasted_iota(jnp.int32, (bk, bk), 1)
        later = (krow > kcol).astype(BF16)
        row = lax.broadcasted_iota(jnp.int32, (bq, bk), 0)
        col = lax.broadcasted_iota(jnp.int32, (bq, bk), 1)
        q = q_ref[...]

        def tiles(js, carry, diags):
            run, acc = carry
            ks, vs, zs = [], [], []
            for j in js:
                off = pl.multiple_of(j * bk, bk)
                ks.append(k_ref[pl.ds(off, bk), :])
                vs.append(v_ref[pl.ds(off, bk), :])
                zs.append(_dot_nt(q, ks[-1]) * scale)
            sps, cums, masks = [], [], []
            for z, diag in zip(zs, diags):
                sp = _softplus_pos(z)
                causal = None
                if diag is not None:
                    causal = col + diag * bk < row
                    sp = jnp.where(causal, sp, 0.0)
                sps.append(sp)
                masks.append(causal)
                cums.append(_split_dot(sp, later))
            for z, sp, cum, causal, v in zip(zs, sps, cums, masks, vs):
                w = jnp.exp((z - sp) - (cum + run))
                if causal is not None:
                    w = jnp.where(causal, w, 0.0)
                acc = acc + _dot(w.astype(BF16), v)
                run = run + cum[:, 0:1] + sp[:, 0:1]
            return run, acc

        carry = (jnp.zeros((bq, 1), F32), jnp.zeros((bq, HEAD_DIM), F32))
        for dg in reversed(range(0, ndiag, SB_PAIR)):
            dgs = list(reversed(range(dg, dg + SB_PAIR)))
            carry = tiles([i * ndiag + g for g in dgs], carry, dgs)
        n_left = i * ndiag

        def more(st):
            return jnp.logical_and(st[0] < n_left, jnp.min(st[1]) < SB_CUTOFF)

        def step(st):
            t, run, acc = st
            run, acc = tiles([n_left - 1 - t], (run, acc), [None])
            return t + 1, run, acc

        _, _, acc = lax.while_loop(more, step, (jnp.int32(0),) + carry)
        o_ref[...] = acc.astype(o_ref.dtype)

    qspec = pl.BlockSpec((bq, HEAD_DIM), lambda h, i: (i, h))
    kspec = pl.BlockSpec((s_len, HEAD_DIM), lambda h, i: (0, h))
    return _pcall(
        body, name="sb_fwd", grid=(nh, s_len // bq),
        out_shape=[jax.ShapeDtypeStruct((s_len, hd), BF16)] + _GatherPlan.out_shapes(shards),
        in_specs=[qspec, kspec, kspec] + [_HBM] * n_sh, out_specs=[qspec] + [_HBM] * n_sh,
        scratch_shapes=_GatherPlan.sem_shapes(n_sh) if n_sh else [],
        compiler_params=_params(("arbitrary", "arbitrary")),
    )(qn, kn, vb, *shards)


def _sb_bwd(qn, kn, vb, do, *, bq=SB_BQ, bk=SB_BK):
    s_len, hd = qn.shape
    nh = hd // HEAD_DIM
    bk = min(bk, s_len)
    bq = min(bq, s_len)
    ndiag = bq // bk
    scale = HEAD_DIM ** -0.5

    def body(q_ref, k_ref, v_ref, do_ref, dq_ref, dk_ref, dv_ref):
        i = pl.program_id(1)

        @pl.when(i == 0)
        def _():
            dk_ref[...] = jnp.zeros_like(dk_ref)
            dv_ref[...] = jnp.zeros_like(dv_ref)

        krow = lax.broadcasted_iota(jnp.int32, (bk, bk), 0)
        kcol = lax.broadcasted_iota(jnp.int32, (bk, bk), 1)
        upto = (krow <= kcol).astype(BF16)
        before = (krow < kcol).astype(BF16)
        row = lax.broadcasted_iota(jnp.int32, (bq, bk), 0)
        col = lax.broadcasted_iota(jnp.int32, (bq, bk), 1)
        q = q_ref[...]
        do_t = do_ref[...]
        ones = jnp.ones((bk, LANES), BF16)
        n_left = i * ndiag

        def row_sums(js, diags):
            zs = [_dot_nt(q, k_ref[pl.ds(pl.multiple_of(j * bk, bk), bk), :]) * scale for j in js]
            tot = None
            for z, diag in zip(zs, diags):
                sp = _softplus_pos(z)
                if diag is not None:
                    sp = jnp.where(col + diag * bk < row, sp, 0.0)
                part = _split_dot(sp, ones)[:, 0:1]
                tot = part if tot is None else tot + part
            return tot

        def more(st):
            return jnp.logical_and(st[0] < n_left, jnp.min(st[1]) < SB_CUTOFF)

        def widen(st):
            t, run = st
            return t + 1, run + row_sums([n_left - 1 - t], [None])

        diag_all = list(range(ndiag))
        used, lt = lax.while_loop(more, widen, (jnp.int32(0), row_sums([i * ndiag + g for g in diag_all], diag_all)))

        def tiles(js, carry, diags):
            pre, ecar, dq = carry
            offs, ks, zs, dws = [], [], [], []
            for j in js:
                off = pl.multiple_of(j * bk, bk)
                offs.append(off)
                ks.append(k_ref[pl.ds(off, bk), :])
                zs.append(_dot_nt(q, ks[-1]) * scale)
                dws.append(_dot_nt(do_t, v_ref[pl.ds(off, bk), :]))
            sps, cums, masks = [], [], []
            for z, diag in zip(zs, diags):
                sp = _softplus_pos(z)
                causal = None
                if diag is not None:
                    causal = col + diag * bk < row
                    sp = jnp.where(causal, sp, 0.0)
                sps.append(sp)
                masks.append(causal)
                cums.append(_split_dot(sp, upto))
            es, ebs, exs, sigs = [], [], [], []
            for off, z, sp, cum, dw, causal in zip(offs, zs, sps, cums, dws, masks):
                lb = z - sp
                w = jnp.exp(lb - (lt - (pre + cum)))
                if causal is not None:
                    w = jnp.where(causal, w, 0.0)
                dv_ref[pl.ds(off, bk), :] += _dot_tn(w.astype(BF16), do_t)
                e = dw * w
                eb = e.astype(BF16)
                es.append(e)
                ebs.append(eb)
                exs.append(_dot(eb, before))
                sigs.append(jnp.exp(lb))
                pre = pre + cum[:, bk - 1:bk]
            for off, k, e, eb, exm, sig, causal in zip(offs, ks, es, ebs, exs, sigs, masks):
                ex = exm + ecar
                dz = (e - sig * (e + ex)) * scale
                if causal is not None:
                    dz = jnp.where(causal, dz, 0.0)
                dzb = dz.astype(BF16)
                dk_ref[pl.ds(off, bk), :] += _dot_tn(dzb, q)
                dq = dq + _dot(dzb, k)
                ecar = ex[:, bk - 1:bk] + eb[:, bk - 1:bk].astype(F32)
            return pre, ecar, dq

        init = (jnp.zeros((bq, 1), F32), jnp.zeros((bq, 1), F32), jnp.zeros((bq, HEAD_DIM), F32))
        carry = lax.fori_loop(n_left - used, n_left, lambda j, cr: tiles([j], cr, [None]), init)
        for dg in range(0, ndiag, SB_PAIR):
            dgs = list(range(dg, dg + SB_PAIR))
            carry = tiles([i * ndiag + g for g in dgs], carry, dgs)
        dq_ref[...] = carry[2]

    qspec = pl.BlockSpec((bq, HEAD_DIM), lambda h, i: (i, h))
    kspec = pl.BlockSpec((s_len, HEAD_DIM), lambda h, i: (0, h))
    return _pcall(
        body, name="sb_bwd", grid=(nh, s_len // bq),
        out_shape=[jax.ShapeDtypeStruct((s_len, hd), F32)] * 3,
        in_specs=[qspec, kspec, kspec, qspec],
        out_specs=[qspec, kspec, kspec],
        compiler_params=_params(("parallel", "arbitrary")),
    )(qn, kn, vb, do)


GDN_GROUP = 16


def _gdn_group(nh):
    return min(GDN_GROUP, nh)


def _gdn_chunk_terms(qh, kh, vh, g_r, g_c, b_c):
    c = GDN_CHUNK
    r = lax.broadcasted_iota(jnp.int32, (c, c), 0)
    s = lax.broadcasted_iota(jnp.int32, (c, c), 1)
    tril, stril = r >= s, r > s
    gcc = jnp.sum(jnp.where(tril, g_r, 0.0), axis=1, keepdims=True)
    gcr = jnp.sum(jnp.where(r <= s, g_c, 0.0), axis=0, keepdims=True)
    dm = jnp.where(tril, jnp.exp(jnp.where(tril, gcc - gcr, 0.0)), 0.0)
    kb = kh.astype(BF16)
    kk = _dot_nt(kb, kb)
    qk = _dot_nt(qh.astype(BF16), kb)
    egc = jnp.exp(gcc)
    gcl = gcc[c - 1:c, :]
    t = dict(tril=tril, stril=stril, gcc=gcc, dm=dm, kb=kb, kk=kk, qk=qk, egc=egc,
             ekd=jnp.exp(gcl - gcc), gl=jnp.exp(gcl),
             a=jnp.where(stril, b_c * kk * dm, 0.0),
             bv=b_c * vh, bk=(b_c * egc) * kh, at=jnp.where(tril, qk * dm, 0.0))
    t["qg"] = qh * egc
    t["kd"] = kh * t["ekd"]
    return t


def _unit_lower_inverses(mats):
    c = GDN_CHUNK
    r = lax.broadcasted_iota(jnp.int32, (c, c), 0)
    s = lax.broadcasted_iota(jnp.int32, (c, c), 1)
    eye = (r == s).astype(F32)
    ps = [-a for a in mats]
    ts = [eye + p for p in ps]
    span = 2
    while span < c:
        ps = [_dot(p, p, hi=HIGH) for p in ps]
        ts = [t + _dot(t, p, hi=HIGH) for t, p in zip(ts, ps)]
        span *= 2
    return ts


def _gdn_fwd(q, k, v, g_col, g_row, b_col, b_row, shards, n_split):
    s_len, d = q.shape
    nh = d // HEAD_DIM
    c = GDN_CHUNK
    n_chunks = s_len // c
    grp = _gdn_group(nh)
    n_sh = len(shards)
    last_h, last_n = nh // grp - 1, n_chunks - 1

    def body(q_ref, k_ref, v_ref, gc_ref, gr_ref, bc_ref, br_ref, *rest):
        sh_refs, rest = rest[:n_sh], rest[n_sh:]
        o_ref, ss_ref, ts_ref = rest[:3]
        got_refs, rest = rest[3:3 + n_sh], rest[3 + n_sh:]
        st, sems = rest[0], rest[1:]
        n = pl.program_id(1)

        @_when_step(0, 0, last_h, last_n)
        def _():
            _GatherPlan(sh_refs, got_refs, sems, n_split).start()

        @_when_step(-1, 3 * n_chunks // 4, last_h, last_n)
        def _():
            _GatherPlan(sh_refs, got_refs, sems, n_split).relay()

        @_when_step(-1, -1, last_h, last_n)
        def _():
            _GatherPlan(sh_refs, got_refs, sems, n_split).finish()

        @pl.when(n == 0)
        def _():
            st[...] = jnp.zeros_like(st)

        heads = range(grp)
        sls = [slice(i * HEAD_DIM, (i + 1) * HEAD_DIM) for i in heads]
        terms = [_gdn_chunk_terms(q_ref[:, sls[i]], k_ref[:, sls[i]], v_ref[:, sls[i]],
                                  gr_ref[i:i + 1, :], gc_ref[:, i:i + 1], bc_ref[:, i:i + 1]) for i in heads]
        tinvs = _unit_lower_inverses([t["a"] for t in terms])
        wvs = [_dot(tinv, t["bv"], hi=HIGH) for tinv, t in zip(tinvs, terms)]
        wks = [_dot(tinv, t["bk"], hi=HIGH) for tinv, t in zip(tinvs, terms)]
        states = [st[i] for i in heads]
        sbs = [state.astype(BF16) for state in states]
        ubs = [(wv - _dot(wk.astype(BF16), sb)).astype(BF16) for wv, wk, sb in zip(wvs, wks, sbs)]
        for i in heads:
            t = terms[i]
            o_ref[:, sls[i]] = _dot(t["qg"].astype(BF16), sbs[i]) + _dot(t["at"].astype(BF16), ubs[i])
            ss_ref[i] = states[i]
            ts_ref[i] = tinvs[i]
            st[i] = t["gl"] * states[i] + _dot_tn(t["kd"].astype(BF16), ubs[i])

    tok = pl.BlockSpec((c, grp * HEAD_DIM), lambda h, n: (n, h))
    colspec = pl.BlockSpec((None, c, grp), lambda h, n: (h, n, 0))
    rowspec = pl.BlockSpec((None, None, grp, c), lambda h, n: (h, n, 0, 0))
    res = _pcall(
        body, name="gdn_fwd", grid=(nh // grp, n_chunks),
        out_shape=[jax.ShapeDtypeStruct((s_len, d), F32),
                   jax.ShapeDtypeStruct((n_chunks, nh, HEAD_DIM, HEAD_DIM), F32),
                   jax.ShapeDtypeStruct((n_chunks, nh, c, c), F32)] + _GatherPlan.out_shapes(shards),
        in_specs=[tok, tok, tok, colspec, rowspec, colspec, rowspec] + [_HBM] * n_sh,
        out_specs=[tok, pl.BlockSpec((None, grp, HEAD_DIM, HEAD_DIM), lambda h, n: (n, h, 0, 0)),
                   pl.BlockSpec((None, grp, c, c), lambda h, n: (n, h, 0, 0))] + [_HBM] * n_sh,
        scratch_shapes=[pltpu.VMEM((grp, HEAD_DIM, HEAD_DIM), F32)] + _GatherPlan.sem_shapes(n_sh),
        compiler_params=_params(("arbitrary", "arbitrary")),
    )(q, k, v, g_col, g_row, b_col, b_row, *shards)
    return res[0], res[1], res[2], res[3:]


def _gdn_bwd(q, k, v, g_col, g_row, b_col, b_row, states, tinvs, do, partials):
    s_len, d = q.shape
    nh = d // HEAD_DIM
    c = GDN_CHUNK
    n_chunks = s_len // c
    grp = _gdn_group(nh)
    n_p = len(partials)
    last_h, last_n = nh // grp - 1, n_chunks - 1

    def body(q_ref, k_ref, v_ref, gc_ref, gr_ref, bc_ref, br_ref, ss_ref, ts_ref, do_ref, *rest):
        part_refs, rest = rest[:n_p], rest[n_p:]
        dq_ref, dk_ref, dv_ref, dgb_ref = rest[:4]
        land_refs, rest = rest[4:4 + n_p], rest[4 + n_p:]
        dst, sems = rest[0], rest[1:]
        n = pl.program_id(1)

        @_when_step(0, 0, last_h, last_n)
        def _():
            _ScatterPlan(part_refs, land_refs, sems).start()

        @_when_step(-1, -1, last_h, last_n)
        def _():
            _ScatterPlan(part_refs, land_refs, sems).finish()

        @pl.when(n == 0)
        def _():
            dst[...] = jnp.zeros_like(dst)

        r = lax.broadcasted_iota(jnp.int32, (c, c), 0)
        s = lax.broadcasted_iota(jnp.int32, (c, c), 1)
        suffix = (r <= s).astype(F32)
        lane = lax.broadcasted_iota(jnp.int32, (c, LANES), 1)
        heads = range(grp)
        sls = [slice(i * HEAD_DIM, (i + 1) * HEAD_DIM) for i in heads]
        qs = [q_ref[:, sl] for sl in sls]
        ks = [k_ref[:, sl] for sl in sls]
        vs = [v_ref[:, sl] for sl in sls]
        bcs = [bc_ref[:, i:i + 1] for i in heads]
        ts = [_gdn_chunk_terms(qs[i], ks[i], vs[i], gr_ref[i:i + 1, :], gc_ref[:, i:i + 1], bcs[i]) for i in heads]
        tinv = [ts_ref[i] for i in heads]
        state = [ss_ref[i] for i in heads]
        sb = [x.astype(BF16) for x in state]
        dnext = [dst[i] for i in heads]
        dnb = [x.astype(BF16) for x in dnext]
        dob = [do_ref[:, sl].astype(BF16) for sl in sls]
        wv = [_dot(tinv[i], ts[i]["bv"], hi=HIGH) for i in heads]
        wk = [_dot(tinv[i], ts[i]["bk"], hi=HIGH) for i in heads]
        wkb = [x.astype(BF16) for x in wk]
        ub = [(wv[i] - _dot(wkb[i], sb[i])).astype(BF16) for i in heads]
        du = [_dot_tn(ts[i]["at"].astype(BF16), dob[i]) + _dot(ts[i]["kd"].astype(BF16), dnb[i]) for i in heads]
        dub = [x.astype(BF16) for x in du]
        dat = [jnp.where(ts[i]["tril"], _dot_nt(dob[i], ub[i]), 0.0) for i in heads]
        dqg = [_dot_nt(dob[i], sb[i]) for i in heads]
        dkd = [_dot_nt(ub[i], dnb[i]) for i in heads]
        dwk = [-_dot_nt(dub[i], sb[i]) for i in heads]
        for i in heads:
            dst[i] = (ts[i]["gl"] * dnext[i] + _dot_tn(ts[i]["qg"].astype(BF16), dob[i]) - _dot_tn(wkb[i], dub[i]))
        dbv = [_dot_tn(tinv[i], du[i], hi=HIGH) for i in heads]
        dbk = [_dot_tn(tinv[i], dwk[i], hi=HIGH) for i in heads]
        dtm = [_dot_nt(du[i], ts[i]["bv"], hi=HIGH) + _dot_nt(dwk[i], ts[i]["bk"], hi=HIGH) for i in heads]
        dtt = [_dot_nt(dtm[i], tinv[i], hi=HIGH) for i in heads]
        da = [-jnp.where(ts[i]["stril"], _dot_tn(tinv[i], dtt[i], hi=HIGH), 0.0) for i in heads]
        rs = lambda m: jnp.sum(m, axis=1, keepdims=True)
        dgb = jnp.zeros((c, LANES), F32)
        for i in heads:
            t, b_c, dm, kb = ts[i], bcs[i], ts[i]["dm"], ts[i]["kb"]
            egc, ekd = t["egc"], t["ekd"]
            dkk = da[i] * b_c * dm
            ddm = da[i] * b_c * t["kk"] + dat[i] * t["qk"]
            dqkb, dkkb = (dat[i] * dm).astype(BF16), dkk.astype(BF16)
            dq_ref[:, sls[i]] = _dot(dqkb, kb) + dqg[i] * egc
            dk_ref[:, sls[i]] = (_dot_tn(dqkb, qs[i].astype(BF16)) + _dot(dkkb, kb) + _dot_tn(dkkb, kb)
                                 + dbk[i] * (b_c * egc) + dkd[i] * ekd)
            dv_ref[:, sls[i]] = dbv[i] * b_c
            dbk_k = rs(dbk[i] * ks[i])
            dbeta = rs(da[i] * t["kk"] * dm) + rs(dbv[i] * vs[i]) + dbk_k * egc
            mx = ddm * dm
            ekd_sum = rs(dkd[i] * ks[i]) * ekd
            dgc = rs(mx) + dbk_k * b_c * egc + rs(dqg[i] * qs[i]) * egc - ekd_sum
            dgl = jnp.sum(rs(dnext[i] * state[i]), axis=0, keepdims=True)
            tail = jnp.sum(ekd_sum, axis=0, keepdims=True) + dgl * t["gl"]
            dg = (_dot(suffix, jnp.broadcast_to(dgc, (c, LANES)), hi=HIGH)[:, 0:1]
                  - rs(_dot_nt(suffix, mx, hi=HIGH)) + tail)
            dgb = dgb + jnp.where(lane == i, dbeta, 0.0) + jnp.where(lane == grp + i, dg, 0.0)
        dgb_ref[...] = dgb

    last = n_chunks - 1
    tok = pl.BlockSpec((c, grp * HEAD_DIM), lambda h, n: (last - n, h))
    colspec = pl.BlockSpec((None, c, grp), lambda h, n: (h, last - n, 0))
    rowspec = pl.BlockSpec((None, None, grp, c), lambda h, n: (h, last - n, 0, 0))
    res = _pcall(
        body, name="gdn_bwd", grid=(nh // grp, n_chunks),
        out_shape=[jax.ShapeDtypeStruct((s_len, d), F32)] * 3
        + [jax.ShapeDtypeStruct((nh // grp, s_len, LANES), F32)] + _ScatterPlan.out_shapes(partials),
        in_specs=[tok, tok, tok, colspec, rowspec, colspec, rowspec,
                  pl.BlockSpec((None, grp, HEAD_DIM, HEAD_DIM), lambda h, n: (last - n, h, 0, 0)),
                  pl.BlockSpec((None, grp, c, c), lambda h, n: (last - n, h, 0, 0)), tok] + [_HBM] * n_p,
        out_specs=[tok, tok, tok, pl.BlockSpec((None, c, LANES), lambda h, n: (h, last - n, 0))] + [_HBM] * n_p,
        scratch_shapes=[pltpu.VMEM((grp, HEAD_DIM, HEAD_DIM), F32)] + _ScatterPlan.sem_shapes(n_p),
        compiler_params=_params(("arbitrary", "arbitrary")),
    )(q, k, v, g_col, g_row, b_col, b_row, states, tinvs, do, *partials)
    return res[0], res[1], res[2], res[3], res[4:]


def _shift_down(prev8, cur, k):
    if k == 0:
        return cur
    ext = jnp.concatenate([prev8, cur], axis=0)
    return pltpu.roll(ext, k, 0)[SUBLANES:, :]


def _shift_up(cur, next8, k):
    if k == 0:
        return cur
    ext = jnp.concatenate([cur, next8], axis=0)
    n = ext.shape[0]
    return pltpu.roll(ext, n - k, 0)[:cur.shape[0], :]


def _conv_pre(i, x, prev8, w):
    prev8 = jnp.where(i == 0, 0.0, prev8)
    pre = None
    for j in range(GDN_CONV):
        term = w[j:j + 1, :] * _shift_down(prev8, x, GDN_CONV - 1 - j)
        pre = term if pre is None else pre + term
    return pre, prev8


def _l2_fwd(a, mult):
    return a * (lax.rsqrt(jnp.sum(a * a, axis=1, keepdims=True) + EPS) * mult)


def _l2_bwd(a, dy, mult):
    r = lax.rsqrt(jnp.sum(a * a, axis=1, keepdims=True) + EPS)
    dy = dy * mult
    return r * dy - a * (r * r * r) * jnp.sum(a * dy, axis=1, keepdims=True)


def _conv_fwd(xb, conv_w, group, *, norm, mult, tr=256):
    d = xb.shape[1] // 3

    def fn(i, nt, tiles, prev8, next8, cv):
        pre, _ = _conv_pre(i, tiles[0], prev8[0], cv[0])
        a = _silu(pre)
        if norm:
            a = _per_head(lambda ah: _l2_fwd(ah, mult), a)
        return [a], []

    col = Col(xb, d, group)
    wg = lax.slice_in_dim(conv_w, group * d, (group + 1) * d, axis=1)
    (y,), _ = _ew(f"conv_fwd{group}", fn, tr=tr, ins=[col], halo_prev=[col], consts=[wg], outs=[(d, F32)])
    return y


def _conv_bwd(xb, conv_w, group, dy, *, norm, mult, tr=256):
    d = xb.shape[1] // 3
    col = Col(xb, d, group)
    wg = lax.slice_in_dim(conv_w, group * d, (group + 1) * d, axis=1)

    def fn_pre(i, nt, tiles, prev8, next8, cv):
        x, dyt = tiles
        pre, p8 = _conv_pre(i, x, prev8[0], cv[0])
        if norm:
            da = _per_head(lambda ah, dh: _l2_bwd(ah, dh, mult), _silu(pre), dyt)
        else:
            da = dyt
        dpre = da * _dsilu(pre)
        tap = lax.broadcasted_iota(jnp.int32, (GDN_CONV, d), 0)
        dw = jnp.zeros((GDN_CONV, d), F32)
        for j in range(GDN_CONV):
            dw = dw + jnp.where(tap == j, _colsum(dpre * _shift_down(p8, x, GDN_CONV - 1 - j)), 0.0)
        return [dpre], [dw]

    (dpre,), (dw,) = _ew(f"conv_bwd_pre{group}", fn_pre, tr=tr, ins=[col, dy], halo_prev=[col], consts=[wg],
                         outs=[(d, F32)], accs=[(GDN_CONV, d)])

    def fn_dx(i, nt, tiles, prev8, next8, cv):
        n8 = jnp.where(i == nt - 1, 0.0, next8[0])
        dx = None
        for j in range(GDN_CONV):
            term = cv[0][j:j + 1, :] * _shift_up(tiles[0], n8, GDN_CONV - 1 - j)
            dx = term if dx is None else dx + term
        return [dx], []

    (dx,), _ = _ew(f"conv_bwd_dx{group}", fn_dx, tr=tr, ins=[dpre], halo_next=[dpre], consts=[wg], outs=[(d, BF16)])
    return dx, dw


def _adamw(name, w, m, v, grads, *, tr=64):
    shape = w.shape
    w2, m2, v2 = [a.reshape(-1, shape[-1]) for a in (w, m, v)]
    n_g = len(grads)
    bc1 = 1.0 - ADAM_B1 ** ADAM_STEP
    bc2 = 1.0 - ADAM_B2 ** ADAM_STEP

    def fn(i, nt, tiles, prev8, next8, cv):
        wt, mt, vt = tiles[:3]
        g = tiles[3]
        for extra in tiles[4:]:
            g = g + extra
        mn = ADAM_B1 * mt + (1.0 - ADAM_B1) * g
        vn = ADAM_B2 * vt + (1.0 - ADAM_B2) * (g * g)
        delta = -ADAM_LR * ((mn / bc1) / (jnp.sqrt(vn / bc2) + ADAM_EPS) + ADAM_WD * wt)
        return [g, delta, mn, vn], []

    width = shape[-1]
    outs, _ = _ew(name, fn, tr=tr, ins=[w2, m2, v2] + list(grads), outs=[(width, F32)] * 4)
    assert n_g >= 1
    return tuple(o.reshape(shape) for o in outs)


def _pad_cols(a, width):
    return jnp.pad(a, ((0, 0), (0, width - a.shape[1])))


def _gdn_layouts(gbeta, nh, n_chunks):
    grp = _gdn_group(nh)
    s_len = gbeta.shape[0]

    def lay(a):
        col = a.reshape(s_len, nh // grp, grp).transpose(1, 0, 2)
        row = a.reshape(n_chunks, GDN_CHUNK, nh // grp, grp).transpose(2, 0, 3, 1)
        return col, row

    b_col, b_row = lay(gbeta[:, :nh])
    g_col, g_row = lay(gbeta[:, nh:2 * nh])
    return g_col, g_row, b_col, b_row


def kernel(x, c, w_mod, b_mod, norm1_w, w_in, q_norm_w, k_norm_w, conv_w, a_log, dt_bias, o_norm_w, p_a, p_b, w_out, norm2_w, w_gate, w_up, w_down, loss_target, m_w_mod, m_b_mod, m_norm1_w, m_w_in, m_q_norm_w, m_k_norm_w, m_conv_w, m_a_log, m_dt_bias, m_o_norm_w, m_p_a, m_p_b, m_w_out, m_norm2_w, m_w_gate, m_w_up, m_w_down, v_w_mod, v_b_mod, v_norm1_w, v_w_in, v_q_norm_w, v_k_norm_w, v_conv_w, v_a_log, v_dt_bias, v_o_norm_w, v_p_a, v_p_b, v_w_out, v_norm2_w, v_w_gate, v_w_up, v_w_down):
    s_len, d = x.shape[1], x.shape[2]
    nh = d // HEAD_DIM
    n_chunks = s_len // GDN_CHUNK
    ff = 4 * w_gate.shape[2]
    mx, my, mc = _my_pos()
    chip = 2 * mx + my
    dev = 2 * chip + mc
    x2 = x[0]
    tgt = loss_target[0]

    c_all = _allgather8("ag_c", _pad_cols(c, d).reshape(SUBLANES, d // SUBLANES)).reshape(8, d)
    wm = w_mod[0]
    mod_w = wm.shape[1]
    bm_cols = lax.dynamic_slice_in_dim(b_mod, chip * mod_w, mod_w, axis=1)

    def mod_body(c_ref, w_ref, b_ref, o_ref, ca_ref):
        ca = _silu(c_ref[...])
        ca_ref[...] = ca
        o_ref[...] = _dot(ca, w_ref[...], hi=HIGHEST) + b_ref[...]

    tn_mod = _pick(mod_w, 512)
    mod8, c_act = _pcall(
        mod_body, name="mod_fwd", grid=(mod_w // tn_mod,),
        out_shape=[jax.ShapeDtypeStruct((8, mod_w), F32), jax.ShapeDtypeStruct((8, d), F32)],
        in_specs=[pl.BlockSpec((8, d), lambda j: (0, 0)), pl.BlockSpec((d, tn_mod), lambda j: (0, j)),
                  pl.BlockSpec((1, tn_mod), lambda j: (0, j))],
        out_specs=[pl.BlockSpec((8, tn_mod), lambda j: (0, j)), pl.BlockSpec((8, d), lambda j: (0, 0))],
        compiler_params=_params(("arbitrary",)),
    )(c_all, wm, bm_cols)
    mod_all = _allgather8("ag_mod", mod8)
    mod_me = mod_all.reshape(4, 2, 8, mod_w)[:, mc, dev, :].reshape(1, 6 * d)
    shift1, scale1, gate1, shift2, scale2, gate2 = [mod_me[:, j * d:(j + 1) * d] for j in range(6)]

    first = [w_in[0].astype(BF16), conv_w[0]]
    late_sb = [p_a[0].astype(BF16), p_b[0].astype(BF16), w_out[0].astype(BF16)]
    late_gdn = [w_gate[0].astype(BF16), w_up[0].astype(BF16), w_down[0].astype(BF16)]
    w_in_g, conv_g = [_fill_slot(g, sh, chip) for g, sh in zip(_gather4("ag_w_in", first, n_split=1), first)]
    w_in_f = w_in_g.transpose(1, 0, 2).reshape(d, -1)
    wa = w_in_f[:, :3 * d]
    wb = w_in_f[:, 3 * d:6 * d]
    wzg = jnp.concatenate([w_in_f[:, 6 * d:7 * d], w_in_f[:, 7 * d + 2 * nh:]], axis=1)
    wba = _pad_cols(w_in_f[:, 7 * d:7 * d + 2 * nh], LANES)
    conv_f = conv_g.transpose(1, 0, 2).reshape(GDN_CONV, 3 * d)

    def norm_mod_fn(i, nt, tiles, prev8, next8, cv):
        w, sc, sh = cv
        return [_rms_fwd(tiles[0], w) * (1.0 + sc) + sh], []

    (u1,), _ = _ew("norm_mod1", norm_mod_fn, tr=512, ins=[x2], consts=[norm1_w, scale1, shift1], outs=[(d, BF16)])
    proj_a = _mm("proj_a", u1, wa, out_dtype=BF16)
    proj_b = _mm("proj_b", u1, wb)
    proj_zg = _mm("proj_zg", u1, wzg, out_dtype=BF16)
    proj_ba = _mm("proj_ba", u1, wba)

    def qknorm_fn(i, nt, tiles, prev8, next8, cv):
        qa, ka, va = [t.astype(F32) for t in tiles]
        return [_per_head(lambda h: _rms_fwd(h, cv[0]), qa), _per_head(lambda h: _rms_fwd(h, cv[1]), ka), va], []

    (qn, kn, vb), _ = _ew("qknorm", qknorm_fn, tr=256,
                          ins=[Col(proj_a, d, 0), Col(proj_a, d, 1), Col(proj_a, d, 2)],
                          consts=[q_norm_w, k_norm_w], outs=[(d, BF16)] * 3)
    sb_out = _sb_fwd(qn, kn, vb, late_sb)
    o_a = sb_out[0]
    p_a_f, p_b_f, w_out_f = [_fill_slot(g, sh, chip).reshape(d, d) for g, sh in zip(sb_out[1:], late_sb)]

    lane_ids = jnp.arange(LANES)
    is_b = (lane_ids < nh)[None, :]
    is_a = ((lane_ids >= nh) & (lane_ids < 2 * nh))[None, :]
    alog128 = jnp.zeros((1, LANES), F32).at[:, nh:2 * nh].set(a_log)
    dtb128 = jnp.zeros((1, LANES), F32).at[:, nh:2 * nh].set(dt_bias)
    is_b_f, is_a_f = is_b.astype(F32), is_a.astype(F32)

    def gbeta_fn(i, nt, tiles, prev8, next8, cv):
        al, dtb, mb, ma = cv
        ba = tiles[0]
        g = -jnp.exp(al) * _softplus(ba + dtb)
        return [jnp.where(mb > 0.5, _sigmoid(ba), jnp.where(ma > 0.5, g, 0.0))], []

    (gbeta,), _ = _ew("gbeta", gbeta_fn, tr=1024, ins=[proj_ba], consts=[alog128, dtb128, is_b_f, is_a_f],
                      outs=[(LANES, F32)])
    g_col, g_row, b_col, b_row = _gdn_layouts(gbeta, nh, n_chunks)
    qscale = HEAD_DIM ** -0.5
    q_b = _conv_fwd(proj_b, conv_f, 0, norm=True, mult=qscale)
    k_b = _conv_fwd(proj_b, conv_f, 1, norm=True, mult=1.0)
    v_b = _conv_fwd(proj_b, conv_f, 2, norm=False, mult=1.0)
    o_raw, states, tinvs, late_g = _gdn_fwd(q_b, k_b, v_b, g_col, g_row, b_col, b_row, late_gdn,
                                            n_split=len(late_gdn))
    late_g = [_fill_slot(g, sh, chip) for g, sh in zip(late_g, late_gdn)]
    w_gate_f, w_up_f = [g.transpose(1, 0, 2).reshape(d, ff) for g in late_g[0:2]]
    w_down_f = late_g[2].reshape(ff, d)

    def gated_norm_fn(i, nt, tiles, prev8, next8, cv):
        o, z = tiles[0], tiles[1].astype(F32)
        return [_per_head(lambda h: _rms_fwd(h, cv[0]), o) * _silu(z)], []

    (o_b,), _ = _ew("gated_norm", gated_norm_fn, tr=256, ins=[o_raw, Col(proj_zg, d, 0)], consts=[o_norm_w],
                    outs=[(d, BF16)])
    y_a = _mm("out_a", o_a, p_a_f)
    y_b = _mm("out_b", o_b, p_b_f)

    def merge_fn(i, nt, tiles, prev8, next8, cv):
        ya, yb, ga, gb = [t.astype(F32) for t in tiles]
        return [_sigmoid(ga) * ya + _sigmoid(gb) * yb], []

    (merged,), _ = _ew("merge", merge_fn, tr=256, ins=[y_a, y_b, Col(proj_zg, d, 1), Col(proj_zg, d, 2)],
                       outs=[(d, BF16)])
    y_o = _mm("out_proj", merged, w_out_f)

    def resid_norm_fn(i, nt, tiles, prev8, next8, cv):
        xt, yo = tiles
        g1, w, sc, sh = cv
        h1 = xt + g1 * yo
        return [h1, _rms_fwd(h1, w) * (1.0 + sc) + sh], []

    (h1, u2), _ = _ew("resid_norm2", resid_norm_fn, tr=256, ins=[x2, y_o],
                      consts=[gate1, norm2_w, scale2, shift2], outs=[(d, F32), (d, BF16)])
    gt = _mm("ff_gate", u2, w_gate_f, out_dtype=BF16)
    up = _mm("ff_up", u2, w_up_f, out_dtype=BF16)

    def swiglu_fn(i, nt, tiles, prev8, next8, cv):
        return [_silu(tiles[0].astype(F32)) * tiles[1].astype(F32)], []

    (act,), _ = _ew("swiglu", swiglu_fn, tr=128, ins=[gt, up], outs=[(ff, BF16)])
    y_d = _mm("ff_down", act, w_down_f)

    def loss_fn(i, nt, tiles, prev8, next8, cv):
        h1t, yd, tg = tiles
        diff = h1t + cv[0] * yd - tg
        dy = diff * (1.0 / d)
        return [dy, dy * cv[0]], [_colsum(0.5 * diff * dy), _colsum(dy * yd)]

    (dy, dyd), (loss_cols, dgate2) = _ew("loss", loss_fn, tr=256, ins=[h1, y_d, tgt], consts=[gate2],
                                         outs=[(d, F32), (d, BF16)], accs=[(1, d), (1, d)])
    loss = lax.psum(jnp.sum(loss_cols), ("x", "y", "c"))

    dact = _mm("d_act", dyd, w_down_f, nt=True, out_dtype=BF16)
    g_w_down = _mm("g_w_down", act, dyd, ta=True)

    def swiglu_bwd_fn(i, nt, tiles, prev8, next8, cv):
        da, g, u = [t.astype(F32) for t in tiles]
        return [da * u * _dsilu(g), da * _silu(g)], []

    (dgt, dup), _ = _ew("swiglu_bwd", swiglu_bwd_fn, tr=128, ins=[dact, gt, up], outs=[(ff, BF16)] * 2)
    du2 = _mm("d_u2_up", dup, w_up_f, nt=True, add=_mm("d_u2_gate", dgt, w_gate_f, nt=True))
    g_w_gate = _mm("g_w_gate", u2, dgt, ta=True)
    g_w_up = _mm("g_w_up", u2, dup, ta=True)

    def norm2_bwd_fn(i, nt, tiles, prev8, next8, cv):
        h1t, du, dres, yo = tiles
        w, sc, g1 = cv
        r = lax.rsqrt(jnp.mean(h1t * h1t, axis=1, keepdims=True) + EPS)
        nrm = h1t * r
        dn = du * w * (1.0 + sc)
        dh = r * (dn - nrm * jnp.mean(dn * nrm, axis=1, keepdims=True)) + dres
        return [dh, dh * g1], [_colsum(du), _colsum(du * nrm * w), _colsum(du * nrm * (1.0 + sc)), _colsum(dh * yo)]

    (dh1, dyo), (dshift2, dscale2, g_norm2, dgate1) = _ew(
        "norm2_bwd", norm2_bwd_fn, tr=256, ins=[h1, du2, dy, y_o], consts=[norm2_w, scale2, gate1],
        outs=[(d, F32), (d, BF16)], accs=[(1, d)] * 4)

    dmerged = _mm("d_merged", dyo, w_out_f, nt=True)
    g_w_out = _mm("g_w_out", merged, dyo, ta=True)

    def merge_bwd_fn(i, nt, tiles, prev8, next8, cv):
        dm, ya, yb, ga, gb = [t.astype(F32) for t in tiles]
        sa, sb = _sigmoid(ga), _sigmoid(gb)
        return [dm * sa, dm * sb, dm * ya * sa * (1.0 - sa), dm * yb * sb * (1.0 - sb)], []

    (dya, dyb, dga, dgb_gate), _ = _ew(
        "merge_bwd", merge_bwd_fn, tr=256, ins=[dmerged, y_a, y_b, Col(proj_zg, d, 1), Col(proj_zg, d, 2)],
        outs=[(d, BF16)] * 4)
    do_a = _mm("d_o_a", dya, p_a_f, nt=True, out_dtype=BF16)
    g_p_a = _mm("g_p_a", o_a, dya, ta=True)
    do_b = _mm("d_o_b", dyb, p_b_f, nt=True)
    g_p_b = _mm("g_p_b", o_b, dyb, ta=True)

    def gated_norm_bwd_fn(i, nt, tiles, prev8, next8, cv):
        dob, o, z = tiles[0], tiles[1], tiles[2].astype(F32)
        sz = _silu(z)

        def head(oh, dh):
            return _rms_bwd(oh, cv[0], dh)

        dxo, dwn = _per_head(head, o, dob * sz)
        nrm_w = _per_head(lambda h: _rms_fwd(h, cv[0]), o)
        return [dxo, dob * nrm_w * _dsilu(z)], [_colsum(_head_sum(dwn))]

    (do_raw, dz_b), (g_o_norm,) = _ew(
        "gated_norm_bwd", gated_norm_bwd_fn, tr=256, ins=[do_b, o_raw, Col(proj_zg, d, 0)], consts=[o_norm_w],
        outs=[(d, F32), (d, BF16)], accs=[(1, HEAD_DIM)])
    by_chip = lambda a: a.reshape(a.shape[0], 4, -1).transpose(1, 0, 2)

    def chip_sums(tag, raw, axes):
        theirs = _sibling_send(f"swap_{tag}", raw, axes)
        sums = []
        for t, (part, ax, other) in enumerate(zip(raw, axes, theirs)):
            def pair_fn(i, nt, tiles, prev8, next8, cv):
                return [tiles[0] + tiles[1]], []

            hr, width = part.shape[ax] // 2, part.shape[-1]
            mine = lax.dynamic_slice_in_dim(part, mc * hr, hr, axis=ax)
            (ch,), _ = _ew(f"pair_sum_{tag}{t}", pair_fn, tr=64,
                           ins=[mine.reshape(-1, width), other.reshape(-1, width)], outs=[(width, BF16)])
            sums.append(ch.reshape(other.shape))
        return sums

    s_gate, s_up, s_pa, s_pb, s_out, s_down = chip_sums(
        "late", [g_w_gate, g_w_up, g_p_a.reshape(4, d // 4, d), g_p_b.reshape(4, d // 4, d),
                 g_w_out.reshape(4, d // 4, d), g_w_down.reshape(4, ff // 4, d)], [0, 0, 1, 1, 1, 1])
    late_halves = [s_pa, s_pb, s_out, by_chip(s_gate), by_chip(s_up), s_down]
    dq_b, dk_b, dv_b, dgb_grp, late_landed = _gdn_bwd(q_b, k_b, v_b, g_col, g_row, b_col, b_row, states, tinvs,
                                                      do_raw, late_halves)
    grp = _gdn_group(nh)
    dbeta = dgb_grp[:, :, :grp].transpose(1, 0, 2).reshape(s_len, nh)
    dg = dgb_grp[:, :, grp:2 * grp].transpose(1, 0, 2).reshape(s_len, nh)
    dgbeta = _pad_cols(jnp.concatenate([dbeta, dg], axis=1), LANES)

    def gbeta_bwd_fn(i, nt, tiles, prev8, next8, cv):
        al, dtb, mb, ma = cv
        ba, dgb = tiles
        beta = _sigmoid(ba)
        arg = ba + dtb
        da = dgb * (-jnp.exp(al)) * _sigmoid(arg)
        g = -jnp.exp(al) * _softplus(arg)
        dba = jnp.where(mb > 0.5, dgb * beta * (1.0 - beta), jnp.where(ma > 0.5, da, 0.0))
        return [dba], [_colsum(jnp.where(ma > 0.5, dgb * g, 0.0)), _colsum(jnp.where(ma > 0.5, da, 0.0))]

    (dba,), (g_alog128, g_dtb128) = _ew(
        "gbeta_bwd", gbeta_bwd_fn, tr=1024, ins=[proj_ba, dgbeta], consts=[alog128, dtb128, is_b_f, is_a_f],
        outs=[(LANES, BF16)], accs=[(1, LANES)] * 2)
    dxq, g_conv_q = _conv_bwd(proj_b, conv_f, 0, dq_b, norm=True, mult=qscale)
    dxk, g_conv_k = _conv_bwd(proj_b, conv_f, 1, dk_b, norm=True, mult=1.0)
    dxv, g_conv_v = _conv_bwd(proj_b, conv_f, 2, dv_b, norm=False, mult=1.0)
    g_conv = jnp.concatenate([g_conv_q, g_conv_k, g_conv_v], axis=1)

    dqn, dkn, dvb = _sb_bwd(qn, kn, vb, do_a)

    def qknorm_bwd_fn(i, nt, tiles, prev8, next8, cv):
        qa, ka, dq, dk, dv = [t.astype(F32) for t in tiles]
        dxq_, dwq = _per_head(lambda h, g: _rms_bwd(h, cv[0], g), qa, dq)
        dxk_, dwk = _per_head(lambda h, g: _rms_bwd(h, cv[1], g), ka, dk)
        return [dxq_, dxk_, dv], [_colsum(_head_sum(dwq)), _colsum(_head_sum(dwk))]

    (dqa, dka, dva), (g_q_norm, g_k_norm) = _ew(
        "qknorm_bwd", qknorm_bwd_fn, tr=256, ins=[Col(proj_a, d, 0), Col(proj_a, d, 1), dqn, dkn, dvb],
        consts=[q_norm_w, k_norm_w], outs=[(d, BF16)] * 3, accs=[(1, HEAD_DIM)] * 2)

    d_all = jnp.concatenate([dqa, dka, dva, dxq, dxk, dxv, dz_b, dga, dgb_gate, dba], axis=1)
    w_all = jnp.concatenate([wa, wb, wzg, wba], axis=1)
    g_wa = _mm("g_w_in_a", u1, d_all, ta=True, b_cols=(0, 3 * d))
    g_wb = _mm("g_w_in_b", u1, d_all, ta=True, b_cols=(3 * d, 3 * d))
    g_wzg = _mm("g_w_in_zg", u1, d_all, ta=True, b_cols=(6 * d, 3 * d))
    g_wba = _mm("g_w_in_ba", u1, d_all, ta=True, b_cols=(9 * d, LANES))
    s_wa, s_wb, s_wzg, s_wba = chip_sums("w_in", [g_wa, g_wb, g_wzg, g_wba], [0, 0, 0, 0])
    s_w_in = jnp.concatenate([s_wa, s_wb, s_wzg[:, :d], s_wba[:, :2 * nh], s_wzg[:, d:]], axis=1)
    w_in_halves = [by_chip(s_w_in)]
    du1, w_in_landed = _mm("d_u1", d_all, w_all, nt=True, tm=512, scatter=w_in_halves)


    def norm1_bwd_fn(i, nt, tiles, prev8, next8, cv):
        xt, du, dres = tiles
        w, sc = cv
        r = lax.rsqrt(jnp.mean(xt * xt, axis=1, keepdims=True) + EPS)
        nrm = xt * r
        dn = du * w * (1.0 + sc)
        dxt = r * (dn - nrm * jnp.mean(dn * nrm, axis=1, keepdims=True)) + dres
        return [dxt], [_colsum(du), _colsum(du * nrm * w), _colsum(du * nrm * (1.0 + sc))]

    (grad_x,), (dshift1, dscale1, g_norm1) = _ew(
        "norm1_bwd", norm1_bwd_fn, tr=256, ins=[x2, du1, dh1], consts=[norm1_w, scale1],
        outs=[(d, F32)], accs=[(1, d)] * 3)

    dmod_me = jnp.concatenate([dshift1, dscale1, dgate1, dshift2, dscale2, dgate2], axis=1)
    small = jnp.concatenate(
        [dmod_me, g_norm1, g_norm2, g_q_norm, g_k_norm, g_o_norm, g_alog128[:, nh:2 * nh], g_dtb128[:, nh:2 * nh],
         g_conv.reshape(1, -1)], axis=1)
    n_small = small.shape[1]
    pad_to = -(-n_small // (SUBLANES * LANES)) * (SUBLANES * LANES)
    small_all = _allgather8("ag_small", _pad_cols(small, pad_to).reshape(SUBLANES, pad_to // SUBLANES))
    small_all = small_all.reshape(8, pad_to)

    def sum8_fn(i, nt, tiles, prev8, next8, cv):
        return [], [_colsum(tiles[0])]

    _, (small_sum,) = _ew("sum_small", sum8_fn, tr=8, ins=[small_all], accs=[(1, pad_to)])
    offs = [0]
    for width in (6 * d, d, d, HEAD_DIM, HEAD_DIM, HEAD_DIM, nh, nh, GDN_CONV * 3 * d):
        offs.append(offs[-1] + width)
    pieces = [small_sum[:, offs[j]:offs[j + 1]] for j in range(9)]
    (gs_b_mod, gs_norm1, gs_norm2, gs_q_norm, gs_k_norm, gs_o_norm, gs_a_log, gs_dt_bias, gs_conv) = pieces
    conv_cols = 3 * d // 4
    gs_conv_mine = lax.dynamic_slice_in_dim(gs_conv.reshape(GDN_CONV, 3 * d), chip * conv_cols, conv_cols, axis=1)

    dmod_all = lax.dynamic_slice_in_dim(small_all[:, :6 * d], chip * mod_w, mod_w, axis=1)

    def wmod_grad_body(ct_ref, dm_ref, o_ref):
        o_ref[...] = _dot(ct_ref[...], dm_ref[...], hi=HIGHEST)

    g_w_mod = _pcall(
        wmod_grad_body, name="g_w_mod", grid=(mod_w // tn_mod,),
        out_shape=jax.ShapeDtypeStruct((d, mod_w), F32),
        in_specs=[pl.BlockSpec((d, 8), lambda j: (0, 0)), pl.BlockSpec((8, tn_mod), lambda j: (0, j))],
        out_specs=pl.BlockSpec((d, tn_mod), lambda j: (0, j)),
        compiler_params=_params(("arbitrary",)),
    )(c_act.T, dmod_all)

    chip_halves = w_in_halves + late_halves
    landed = list(w_in_landed) + list(late_landed)
    landed = [_fill_slot(land, ch, chip) for land, ch in zip(landed, chip_halves)]
    g_mine = []
    for t, land in enumerate(landed):
        def sum4_fn(i, nt, tiles, prev8, next8, cv):
            f = [tl.astype(F32) for tl in tiles]
            return [(f[0] + f[1]) + (f[2] + f[3])], []

        (gh,), _ = _ew(f"chip_sum{t}", sum4_fn, tr=64, ins=[Col(land, lead=s) for s in range(4)],
                       outs=[(land.shape[-1], F32)])
        g_mine.append(gh)
    g_theirs = _sibling_send("join_grads", g_mine)
    g_full = [jnp.concatenate([jnp.where(mc == 0, a, b), jnp.where(mc == 0, b, a)], axis=0)
              for a, b in zip(g_mine, g_theirs)]

    big = {}
    names = ["w_in", "p_a", "p_b", "w_out", "w_gate", "w_up", "w_down"]
    big_w = [w_in, p_a, p_b, w_out, w_gate, w_up, w_down]
    big_m = [m_w_in, m_p_a, m_p_b, m_w_out, m_w_gate, m_w_up, m_w_down]
    big_v = [v_w_in, v_p_a, v_p_b, v_w_out, v_w_gate, v_w_up, v_w_down]
    for t, nm in enumerate(names):
        big[nm] = _adamw(f"adamw_{nm}", big_w[t], big_m[t], big_v[t], [g_full[t]])
    big["w_mod"] = _adamw("adamw_w_mod", w_mod, m_w_mod, v_w_mod, [g_w_mod])
    big["conv_w"] = _adamw("adamw_conv_w", conv_w, m_conv_w, v_conv_w, [gs_conv_mine], tr=8)
    small_names = ["b_mod", "norm1_w", "norm2_w", "q_norm_w", "k_norm_w", "o_norm_w", "a_log", "dt_bias"]
    small_w = [b_mod, norm1_w, norm2_w, q_norm_w, k_norm_w, o_norm_w, a_log, dt_bias]
    small_m = [m_b_mod, m_norm1_w, m_norm2_w, m_q_norm_w, m_k_norm_w, m_o_norm_w, m_a_log, m_dt_bias]
    small_v = [v_b_mod, v_norm1_w, v_norm2_w, v_q_norm_w, v_k_norm_w, v_o_norm_w, v_a_log, v_dt_bias]
    small_g = [gs_b_mod, gs_norm1, gs_norm2, gs_q_norm, gs_k_norm, gs_o_norm, gs_a_log, gs_dt_bias]
    rep_w = jnp.concatenate(small_w, axis=1)
    rep_m = jnp.concatenate(small_m, axis=1)
    rep_v = jnp.concatenate(small_v, axis=1)
    rep_g = jnp.concatenate(small_g, axis=1)
    rep = _adamw("adamw_small", rep_w, rep_m, rep_v, [rep_g], tr=1)
    roffs = [0]
    for a in small_w:
        roffs.append(roffs[-1] + a.shape[1])
    for j, nm in enumerate(small_names):
        big[nm] = tuple(r[:, roffs[j]:roffs[j + 1]] for r in rep)

    order = ["w_mod", "b_mod", "norm1_w", "w_in", "q_norm_w", "k_norm_w", "conv_w", "a_log", "dt_bias", "o_norm_w",
             "p_a", "p_b", "w_out", "norm2_w", "w_gate", "w_up", "w_down"]
    grads = [big[nm][0] for nm in order]
    deltas = [big[nm][1] for nm in order]
    new_m = [big[nm][2] for nm in order]
    new_v = [big[nm][3] for nm in order]
    return (loss, grad_x[None], *grads, *deltas, *new_m, *new_v)
```

```python
import jax
import jax.numpy as jnp
from jax import lax
from jax.experimental import pallas as pl
from jax.experimental.pallas import tpu as pltpu

F32 = jnp.float32
BF16 = jnp.bfloat16
HIGHEST = lax.Precision.HIGHEST
HIGH = lax.Precision.HIGH
MESH = pl.DeviceIdType.MESH

HEAD_DIM = 128
GDN_CHUNK = 64
GDN_CONV = 4
EPS = 1e-6
LANES = 128
SUBLANES = 8
VMEM_LIMIT = 56 * 1024 * 1024
MM_VMEM_BUDGET = 40 * 1024 * 1024

ADAM_LR = 0.001
ADAM_B1 = 0.9
ADAM_B2 = 0.999
ADAM_EPS = 1e-08
ADAM_WD = 0.01
ADAM_STEP = 10


def _pcall(body, **kw):
    return pl.pallas_call(body, **kw)


def _params(sem=None):
    if sem is None:
        return pltpu.CompilerParams(vmem_limit_bytes=VMEM_LIMIT)
    return pltpu.CompilerParams(dimension_semantics=sem, vmem_limit_bytes=VMEM_LIMIT)


def _pick(dim, target):
    if dim <= target:
        return dim
    best = None
    for t in range(LANES, target + 1, LANES):
        if dim % t == 0:
            best = t
    assert best is not None, (dim, target)
    return best


def _rows_tile(rows, target):
    t = min(rows, target)
    while rows % t:
        t //= 2
    assert t >= SUBLANES or t == rows, (rows, target)
    return t


def _dot(a, b, hi=None):
    return jnp.dot(a, b, preferred_element_type=F32, precision=hi)


def _dot_nt(a, b, hi=None):
    return lax.dot_general(a, b, (((1,), (1,)), ((), ())), preferred_element_type=F32, precision=hi)


def _dot_tn(a, b, hi=None):
    return lax.dot_general(a, b, (((0,), (0,)), ((), ())), preferred_element_type=F32, precision=hi)


def _sigmoid(x):
    return 1.0 / (1.0 + jnp.exp(-x))


def _softplus(x):
    return jnp.maximum(x, 0.0) + jnp.log(1.0 + jnp.exp(-jnp.abs(x)))


_HBM = pl.BlockSpec(memory_space=pltpu.HBM)


def _my_pos():
    return lax.axis_index("x"), lax.axis_index("y"), lax.axis_index("c")


def _allgather8(name, v):
    def body(v_ref, o_ref, ssem, rsem, lsem):
        x, y, c = _my_pos()
        me = 4 * x + 2 * y + c
        loc = pltpu.make_async_copy(v_ref, o_ref.at[me], lsem)
        loc.start()
        sends, recvs = [], []
        for k in range(1, 8):
            px, py, pc = (x + (k >> 2)) % 2, (y + ((k >> 1) & 1)) % 2, (c + (k & 1)) % 2
            cp = pltpu.make_async_remote_copy(
                src_ref=v_ref, dst_ref=o_ref.at[me], send_sem=ssem.at[k - 1], recv_sem=rsem.at[k - 1],
                device_id=(px, py, pc), device_id_type=MESH)
            cp.start()
            sends.append(cp)
            recvs.append(pltpu.make_async_remote_copy(
                src_ref=v_ref, dst_ref=o_ref.at[4 * px + 2 * py + pc], send_sem=ssem.at[k - 1],
                recv_sem=rsem.at[k - 1], device_id=(px, py, pc), device_id_type=MESH))
        for rc in recvs:
            rc.wait_recv()
        for cp in sends:
            cp.wait_send()
        loc.wait()

    return _pcall(
        body, name=name, out_shape=jax.ShapeDtypeStruct((8,) + v.shape, v.dtype),
        in_specs=[_HBM], out_specs=_HBM,
        scratch_shapes=[pltpu.SemaphoreType.DMA((7,)), pltpu.SemaphoreType.DMA((7,)), pltpu.SemaphoreType.DMA],
    )(v)


def _plane_peers(x, y):
    return [((x + (k >> 1)) % 2, (y + (k & 1)) % 2) for k in range(1, 4)]


class _GatherPlan:
    def __init__(self, ins, outs, sems, n_split):
        ssem, rsem, fsem, gsem = sems
        x, y, c = _my_pos()
        me = 2 * x + y
        copy = lambda src, dst, s_sem, r_sem, dev: (lambda: pltpu.make_async_remote_copy(
            src_ref=src, dst_ref=dst, send_sem=s_sem, recv_sem=r_sem, device_id=dev, device_id_type=MESH))
        self.sends, self.recvs, self.fwds, self.fwd_recvs = [], [], [], []
        for t in range(len(ins)):
            split = t < n_split
            hr = ins[t].shape[0] // 2
            for k, (px, py) in enumerate(_plane_peers(x, y)):
                peer = 2 * px + py
                sem = 3 * t + k
                if split:
                    mine = pl.ds(pl.multiple_of(c * hr, 16), hr)
                    other = pl.ds(pl.multiple_of((1 - c) * hr, 16), hr)
                    src, dst, got = ins[t].at[mine], outs[t].at[me, mine], outs[t].at[peer, mine]
                else:
                    src, dst, got = ins[t], outs[t].at[me], outs[t].at[peer]
                self.sends.append(copy(src, dst, ssem.at[sem], rsem.at[sem], (px, py, c)))
                self.recvs.append(copy(src, got, ssem.at[sem], rsem.at[sem], (px, py, c)))
                if split:
                    self.fwds.append(copy(got, got, fsem.at[sem], gsem.at[sem], (x, y, 1 - c)))
                    self.fwd_recvs.append(copy(got, outs[t].at[peer, other], fsem.at[sem], gsem.at[sem], (x, y, 1 - c)))
                else:
                    self.fwds.append(None)

    def start(self):
        for cp in self.sends:
            cp().start()

    def relay(self):
        for rc, fw in zip(self.recvs, self.fwds):
            rc().wait_recv()
            if fw is not None:
                fw().start()

    def finish(self):
        for fr in self.fwd_recvs:
            fr().wait_recv()
        for cp in self.sends + [fw for fw in self.fwds if fw is not None]:
            cp().wait_send()

    @staticmethod
    def out_shapes(shards):
        return [jax.ShapeDtypeStruct((4,) + s.shape, s.dtype) for s in shards]

    @staticmethod
    def sem_shapes(n):
        return [pltpu.SemaphoreType.DMA((3 * n,))] * 4


def _gather4(name, shards, n_split):
    n = len(shards)

    def body(*refs):
        plan = _GatherPlan(refs[:n], refs[n:2 * n], refs[2 * n:], n_split)
        plan.start()
        plan.relay()
        plan.finish()

    return _pcall(
        body, name=name, out_shape=_GatherPlan.out_shapes(shards), in_specs=[_HBM] * n, out_specs=[_HBM] * n,
        scratch_shapes=_GatherPlan.sem_shapes(n),
    )(*shards)


def _when_step(h, n, hs, ns):
    return pl.when(jnp.logical_and(pl.program_id(0) == (hs if h < 0 else h), pl.program_id(1) == (ns if n < 0 else n)))


def _fill_slot(slots, own, slot):
    mask = (jnp.arange(4) == slot).reshape((4,) + (1,) * (slots.ndim - 1))
    return jnp.where(mask, own if own.ndim == slots.ndim else own[None], slots)


class _ScatterPlan:
    def __init__(self, ins, outs, sems):
        ssem, rsem = sems
        x, y, c = _my_pos()
        me = 2 * x + y
        copy = lambda src, dst, s_sem, r_sem, dev: (lambda: pltpu.make_async_remote_copy(
            src_ref=src, dst_ref=dst, send_sem=s_sem, recv_sem=r_sem, device_id=dev, device_id_type=MESH))
        self.sends, self.recvs = [], []
        for t in range(len(ins)):
            for k, (px, py) in enumerate(_plane_peers(x, y)):
                peer = 2 * px + py
                sem = 3 * t + k
                self.sends.append(copy(ins[t].at[peer], outs[t].at[me], ssem.at[sem], rsem.at[sem], (px, py, c)))
                self.recvs.append(copy(ins[t].at[peer], outs[t].at[peer], ssem.at[sem], rsem.at[sem], (px, py, c)))

    def start(self):
        for cp in self.sends:
            cp().start()

    def finish(self):
        for rc in self.recvs:
            rc().wait_recv()
        for cp in self.sends:
            cp().wait_send()

    @staticmethod
    def out_shapes(partials):
        return [jax.ShapeDtypeStruct(p.shape, p.dtype) for p in partials]

    @staticmethod
    def sem_shapes(n):
        return [pltpu.SemaphoreType.DMA((3 * n,))] * 2


def _sibling_send(name, arrays, axes=None):
    n = len(arrays)
    axes = [None] * n if axes is None else axes

    def body(*refs):
        ins, outs = refs[:n], refs[n:2 * n]
        ssem, rsem = refs[2 * n:]
        x, y, c = _my_pos()
        cps = []
        for t in range(n):
            src = ins[t]
            if axes[t] is not None:
                hr = ins[t].shape[axes[t]] // 2
                give = pl.ds(pl.multiple_of((1 - c) * hr, SUBLANES), hr)
                src = ins[t].at[give] if axes[t] == 0 else ins[t].at[:, give]
            cp = pltpu.make_async_remote_copy(
                src_ref=src, dst_ref=outs[t], send_sem=ssem.at[t], recv_sem=rsem.at[t],
                device_id=(x, y, 1 - c), device_id_type=MESH)
            cp.start()
            cps.append(cp)
        for cp in cps:
            cp.wait_recv()
        for cp in cps:
            cp.wait_send()

    def half(a, axis):
        shape = list(a.shape)
        if axis is not None:
            shape[axis] //= 2
        return jax.ShapeDtypeStruct(tuple(shape), a.dtype)

    return _pcall(
        body, name=name, out_shape=[half(a, ax) for a, ax in zip(arrays, axes)], in_specs=[_HBM] * n,
        out_specs=[_HBM] * n,
        scratch_shapes=[pltpu.SemaphoreType.DMA((n,)), pltpu.SemaphoreType.DMA((n,))],
    )(*arrays)


def _mm(name, a, b, *, nt=False, ta=False, out_dtype=F32, add=None, tm=1024, tn=1024, tk=4096, b_cols=None,
        scatter=()):
    m, k = (a.shape[1], a.shape[0]) if ta else a.shape
    n = b.shape[0] if nt else b.shape[1]
    assert (b.shape[1] if nt else b.shape[0]) == k
    col0 = 0
    if b_cols is not None:
        assert not nt
        col0, n = b_cols
    has_add = add is not None
    tm, tn = _pick(m, tm), _pick(n, tn)
    assert col0 % tn == 0
    n_sc = len(scatter)
    out_bytes = jnp.dtype(out_dtype).itemsize

    def vmem_bytes(tk_):
        steps = k // tk_
        return (4 * (tm + tn) * tk_ + 2 * tm * tn * out_bytes + (8 * tm * tn if has_add else 0)
                + (4 * tm * tn if steps > 1 else 0))

    tk = _pick(k, tk)
    while vmem_bytes(tk) > MM_VMEM_BUDGET and tk > 512:
        tk = _pick(k, tk - LANES)
    nk = k // tk

    grid = (n // tn, m // tm, nk)

    def body(*refs):
        a_ref, b_ref = refs[0], refs[1]
        c_ref = refs[2] if has_add else None
        part_refs = refs[2 + has_add:2 + has_add + n_sc]
        o_ref = refs[2 + has_add + n_sc]
        land_refs = refs[3 + has_add + n_sc:3 + has_add + 2 * n_sc]
        rest = refs[3 + has_add + 2 * n_sc:]
        if n_sc:
            sems = rest[-2:]
            at = lambda step: pl.when(jnp.logical_and(jnp.logical_and(
                pl.program_id(0) == step[0], pl.program_id(1) == step[1]), pl.program_id(2) == step[2]))

            @at((0, 0, 0))
            def _():
                _ScatterPlan(part_refs, land_refs, sems).start()

            @at(tuple(g - 1 for g in grid))
            def _():
                _ScatterPlan(part_refs, land_refs, sems).finish()

        p = (_dot_tn if ta else _dot_nt if nt else _dot)(a_ref[...], b_ref[...])
        if nk == 1:
            o_ref[...] = (p + c_ref[...] if has_add else p).astype(o_ref.dtype)
            return
        acc = rest[0]
        kk = pl.program_id(2)

        @pl.when(kk == 0)
        def _():
            acc[...] = p

        @pl.when(jnp.logical_and(kk > 0, kk < nk - 1))
        def _():
            acc[...] += p

        @pl.when(kk == nk - 1)
        def _():
            r = acc[...] + p
            if has_add:
                r = r + c_ref[...]
            o_ref[...] = r.astype(o_ref.dtype)

    if ta:
        a_spec = pl.BlockSpec((tk, tm), lambda j, i, kk: (kk, i))
    else:
        a_spec = pl.BlockSpec((tm, tk), lambda j, i, kk: (i, kk))
    if nt:
        b_spec = pl.BlockSpec((tn, tk), lambda j, i, kk: (j, kk))
    else:
        b_spec = pl.BlockSpec((tk, tn), lambda j, i, kk: (kk, j + col0 // tn))
    o_spec = pl.BlockSpec((tm, tn), lambda j, i, kk: (i, j))
    in_specs = [a_spec, b_spec] + ([o_spec] if has_add else []) + [_HBM] * n_sc
    args = (a, b) + ((add,) if has_add else ()) + tuple(scatter)
    res = _pcall(
        body, name=name, grid=grid,
        out_shape=[jax.ShapeDtypeStruct((m, n), out_dtype)] + _ScatterPlan.out_shapes(scatter),
        in_specs=in_specs, out_specs=[o_spec] + [_HBM] * n_sc,
        scratch_shapes=([pltpu.VMEM((tm, tn), F32)] if nk > 1 else []) + (_ScatterPlan.sem_shapes(n_sc) if n_sc else []),
        compiler_params=_params(("arbitrary",) * 3 if n_sc else ("parallel", "parallel", "arbitrary")),
    )(*args)
    return (res[0], res[1:]) if n_sc else res[0]


class Col:
    def __init__(self, arr, w=None, cb=0, lead=None):
        self.arr, self.cb, self.lead = arr, cb, lead
        self.w = arr.shape[-1] if w is None else w
        self.rows = arr.shape[-2]


def _ew(name, fn, *, tr, ins, consts=(), outs=(), accs=(), halo_prev=(), halo_next=()):
    ins = [c if isinstance(c, Col) else Col(c) for c in ins]
    halo_prev = [c if isinstance(c, Col) else Col(c) for c in halo_prev]
    halo_next = [c if isinstance(c, Col) else Col(c) for c in halo_next]
    rows = ins[0].rows
    tr = _rows_tile(rows, tr)
    nt = rows // tr
    n_in, n_hp, n_hn, n_c, n_o, n_a = len(ins), len(halo_prev), len(halo_next), len(consts), len(outs), len(accs)
    groups = tr // SUBLANES

    def spec(col, kind):
        if kind == "cur":
            shape, idx = (tr, col.w), (lambda i, cb=col.cb: (i, cb))
        elif kind == "prev":
            shape, idx = (SUBLANES, col.w), (lambda i, cb=col.cb: (jnp.maximum(i * groups - 1, 0), cb))
        else:
            shape = (SUBLANES, col.w)
            idx = (lambda i, cb=col.cb: (jnp.minimum((i + 1) * groups, rows // SUBLANES - 1), cb))
        if col.lead is None:
            return pl.BlockSpec(shape, idx)
        return pl.BlockSpec((None,) + shape, lambda i, idx=idx, lead=col.lead: (lead,) + idx(i))

    def body(*refs):
        i = pl.program_id(0)
        p = 0
        tiles = [r[...] for r in refs[p:p + n_in]]; p += n_in
        prev8 = [r[...] for r in refs[p:p + n_hp]]; p += n_hp
        next8 = [r[...] for r in refs[p:p + n_hn]]; p += n_hn
        cvals = [r[...] for r in refs[p:p + n_c]]; p += n_c
        out_refs = refs[p:p + n_o]; p += n_o
        acc_refs = refs[p:p + n_a]
        out_v, acc_v = fn(i, nt, tiles, prev8, next8, cvals)
        for r, v in zip(out_refs, out_v):
            r[...] = v.astype(r.dtype)
        if n_a:
            @pl.when(i == 0)
            def _():
                for r, v in zip(acc_refs, acc_v):
                    r[...] = v

            @pl.when(i > 0)
            def _():
                for r, v in zip(acc_refs, acc_v):
                    r[...] += v

    in_specs = ([spec(c, "cur") for c in ins] + [spec(c, "prev") for c in halo_prev]
                + [spec(c, "next") for c in halo_next]
                + [pl.BlockSpec(c.shape, lambda i, nd=c.ndim: (0,) * nd) for c in consts])
    out_specs = ([pl.BlockSpec((tr, w), lambda i: (i, 0)) for w, _ in outs]
                 + [pl.BlockSpec(s, lambda i: (0, 0)) for s in accs])
    out_shape = ([jax.ShapeDtypeStruct((rows, w), dt) for w, dt in outs]
                 + [jax.ShapeDtypeStruct(s, F32) for s in accs])
    args = [c.arr for c in ins] + [c.arr for c in halo_prev] + [c.arr for c in halo_next] + list(consts)
    res = _pcall(body, name=name, grid=(nt,), out_shape=out_shape, in_specs=in_specs, out_specs=out_specs,
                 compiler_params=_params(("arbitrary",)))(*args)
    return res[:n_o], res[n_o:]


def _colsum(v):
    return jnp.sum(v, axis=0, keepdims=True)


def _heads_of(w):
    return w // HEAD_DIM


def _per_head(fn, *arrays):
    nh = _heads_of(arrays[0].shape[1])
    res = [fn(*[a[:, h * HEAD_DIM:(h + 1) * HEAD_DIM] for a in arrays]) for h in range(nh)]
    if isinstance(res[0], tuple):
        return tuple(jnp.concatenate([r[j] for r in res], axis=1) for j in range(len(res[0])))
    return jnp.concatenate(res, axis=1)


def _head_sum(v):
    nh = _heads_of(v.shape[1])
    out = v[:, :HEAD_DIM]
    for h in range(1, nh):
        out = out + v[:, h * HEAD_DIM:(h + 1) * HEAD_DIM]
    return out


def _rms_fwd(x, w):
    r = lax.rsqrt(jnp.mean(x * x, axis=1, keepdims=True) + EPS)
    return x * r * w


def _rms_bwd(x, w, dy):
    r = lax.rsqrt(jnp.mean(x * x, axis=1, keepdims=True) + EPS)
    xh = x * r
    dxh = dy * w
    dx = r * (dxh - xh * jnp.mean(dxh * xh, axis=1, keepdims=True))
    return dx, dy * xh


def _silu(x):
    return x * _sigmoid(x)


def _dsilu(x):
    s = _sigmoid(x)
    return s * (1.0 + x * (1.0 - s))


SB_BQ = 512
SB_CUTOFF = 112.0
SB_PAIR = 2
SB_BK = 256


def _softplus_pos(z):
    return jnp.maximum(z, 0.0) + jnp.log(1.0 + jnp.exp(-jnp.abs(z)))


def _split_dot(v, tri):
    top = lax.bitcast_convert_type(lax.bitcast_convert_type(v, jnp.int32) & jnp.int32(-65536), F32)
    return _dot(top.astype(BF16), tri) + _dot((v - top).astype(BF16), tri)


def _sb_fwd(qn, kn, vb, shards, *, bq=SB_BQ, bk=SB_BK):
    s_len, hd = qn.shape
    nh = hd // HEAD_DIM
    bk = min(bk, s_len)
    bq = min(bq, s_len)
    ndiag = bq // bk
    scale = HEAD_DIM ** -0.5

    n_sh = len(shards)
    last_h, last_i = nh - 1, s_len // bq - 1

    def body(q_ref, k_ref, v_ref, *rest):
        sh_refs, o_ref, ld_ref = rest[:n_sh], rest[n_sh], rest[n_sh + 1]
        got_refs, sems = rest[n_sh + 2:2 * n_sh + 2], rest[2 * n_sh + 2:]
        i = pl.program_id(1)

        if n_sh:
            @_when_step(0, 0, last_h, last_i)
            def _():
                _GatherPlan(sh_refs, got_refs, sems, n_sh).start()

            @_when_step(nh // 2, 0, last_h, last_i)
            def _():
                _GatherPlan(sh_refs, got_refs, sems, n_sh).relay()

            @_when_step(-1, -1, last_h, last_i)
            def _():
                _GatherPlan(sh_refs, got_refs, sems, n_sh).finish()

        krow = lax.broadcasted_iota(jnp.int32, (bk, bk), 0)
        kcol = lax.broadcasted_iota(jnp.int32, (bk, bk), 1)
        later = (krow > kcol).astype(BF16)
        row = lax.broadcasted_iota(jnp.int32, (bq, bk), 0)
        col = lax.broadcasted_iota(jnp.int32, (bq, bk), 1)
        q = q_ref[...]

        def tiles(js, carry, diags):
            run, acc = carry
            ks, vs, zs = [], [], []
            for j in js:
                off = pl.multiple_of(j * bk, bk)
                ks.append(k_ref[pl.ds(off, bk), :])
                vs.append(v_ref[pl.ds(off, bk), :])
                zs.append(_dot_nt(q, ks[-1]) * scale)
            sps, cums, masks = [], [], []
            for z, diag in zip(zs, diags):
                sp = _softplus_pos(z)
                causal = None
                if diag is not None:
                    causal = col + diag * bk < row
                    sp = jnp.where(causal, sp, 0.0)
                sps.append(sp)
                masks.append(causal)
                cums.append(_split_dot(sp, later))
            for z, sp, cum, causal, v in zip(zs, sps, cums, masks, vs):
                w = jnp.exp((z - sp) - (cum + run))
                if causal is not None:
                    w = jnp.where(causal, w, 0.0)
                acc = acc + _dot(w.astype(BF16), v)
                run = run + cum[:, 0:1] + sp[:, 0:1]
            return run, acc

        carry = (jnp.zeros((bq, 1), F32), jnp.zeros((bq, HEAD_DIM), F32))
        for dg in reversed(range(0, ndiag, SB_PAIR)):
            dgs = list(reversed(range(dg, dg + SB_PAIR)))
            carry = tiles([i * ndiag + g for g in dgs], carry, dgs)
        ld_ref[...] = carry[0]
        n_left = i * ndiag

        def more(st):
            return jnp.logical_and(st[0] < n_left, jnp.min(st[1]) < SB_CUTOFF)

        def step(st):
            t, run, acc = st
            run, acc = tiles([n_left - 1 - t], (run, acc), [None])
            return t + 1, run, acc

        _, _, acc = lax.while_loop(more, step, (jnp.int32(0),) + carry)
        o_ref[...] = acc.astype(o_ref.dtype)

    qspec = pl.BlockSpec((bq, HEAD_DIM), lambda h, i: (i, h))
    kspec = pl.BlockSpec((s_len, HEAD_DIM), lambda h, i: (0, h))
    return _pcall(
        body, name="sb_fwd", grid=(nh, s_len // bq),
        out_shape=[jax.ShapeDtypeStruct((s_len, hd), BF16), jax.ShapeDtypeStruct((nh, s_len, 1), F32)]
        + _GatherPlan.out_shapes(shards),
        in_specs=[qspec, kspec, kspec] + [_HBM] * n_sh,
        out_specs=[qspec, pl.BlockSpec((None, bq, 1), lambda h, i: (h, i, 0))] + [_HBM] * n_sh,
        scratch_shapes=_GatherPlan.sem_shapes(n_sh) if n_sh else [],
        compiler_params=_params(("arbitrary", "arbitrary")),
    )(qn, kn, vb, *shards)


def _sb_bwd(qn, kn, vb, do, lt_diag, *, bq=SB_BQ, bk=SB_BK):
    s_len, hd = qn.shape
    nh = hd // HEAD_DIM
    bk = min(bk, s_len)
    bq = min(bq, s_len)
    ndiag = bq // bk
    scale = HEAD_DIM ** -0.5

    def body(q_ref, k_ref, v_ref, do_ref, ld_ref, dq_ref, dk_ref, dv_ref):
        i = pl.program_id(1)

        @pl.when(i == 0)
        def _():
            dk_ref[...] = jnp.zeros_like(dk_ref)
            dv_ref[...] = jnp.zeros_like(dv_ref)

        krow = lax.broadcasted_iota(jnp.int32, (bk, bk), 0)
        kcol = lax.broadcasted_iota(jnp.int32, (bk, bk), 1)
        upto = (krow <= kcol).astype(BF16)
        before = (krow < kcol).astype(BF16)
        row = lax.broadcasted_iota(jnp.int32, (bq, bk), 0)
        col = lax.broadcasted_iota(jnp.int32, (bq, bk), 1)
        q = q_ref[...]
        do_t = do_ref[...]
        ones = jnp.ones((bk, LANES), BF16)
        n_left = i * ndiag

        def row_sums(js, diags):
            zs = [_dot_nt(q, k_ref[pl.ds(pl.multiple_of(j * bk, bk), bk), :]) * scale for j in js]
            tot = None
            for z, diag in zip(zs, diags):
                assert diag is None
                part = _split_dot(_softplus_pos(z), ones)[:, 0:1]
                tot = part if tot is None else tot + part
            return tot

        def more(st):
            return jnp.logical_and(st[0] < n_left, jnp.min(st[1]) < SB_CUTOFF)

        def widen(st):
            t, run = st
            return t + 1, run + row_sums([n_left - 1 - t], [None])

        used, lt = lax.while_loop(more, widen, (jnp.int32(0), ld_ref[...]))

        def tiles(js, carry, diags):
            pre, ecar, dq = carry
            offs, ks, zs, dws = [], [], [], []
            for j in js:
                off = pl.multiple_of(j * bk, bk)
                offs.append(off)
                ks.append(k_ref[pl.ds(off, bk), :])
                zs.append(_dot_nt(q, ks[-1]) * scale)
                dws.append(_dot_nt(do_t, v_ref[pl.ds(off, bk), :]))
            sps, cums, masks = [], [], []
            for z, diag in zip(zs, diags):
                sp = _softplus_pos(z)
                causal = None
                if diag is not None:
                    causal = col + diag * bk < row
                    sp = jnp.where(causal, sp, 0.0)
                sps.append(sp)
                masks.append(causal)
                cums.append(_split_dot(sp, upto))
            es, ebs, exs, sigs = [], [], [], []
            for off, z, sp, cum, dw, causal in zip(offs, zs, sps, cums, dws, masks):
                lb = z - sp
                w = jnp.exp(lb - (lt - (pre + cum)))
                if causal is not None:
                    w = jnp.where(causal, w, 0.0)
                dv_ref[pl.ds(off, bk), :] += _dot_tn(w.astype(BF16), do_t)
                e = dw * w
                eb = e.astype(BF16)
                es.append(e)
                ebs.append(eb)
                exs.append(_dot(eb, before))
                sigs.append(jnp.exp(lb))
                pre = pre + cum[:, bk - 1:bk]
            for off, k, e, eb, exm, sig, causal in zip(offs, ks, es, ebs, exs, sigs, masks):
                ex = exm + ecar
                dz = (e - sig * (e + ex)) * scale
                if causal is not None:
                    dz = jnp.where(causal, dz, 0.0)
                dzb = dz.astype(BF16)
                dk_ref[pl.ds(off, bk), :] += _dot_tn(dzb, q)
                dq = dq + _dot(dzb, k)
                ecar = ex[:, bk - 1:bk] + eb[:, bk - 1:bk].astype(F32)
            return pre, ecar, dq

        init = (jnp.zeros((bq, 1), F32), jnp.zeros((bq, 1), F32), jnp.zeros((bq, HEAD_DIM), F32))
        carry = lax.fori_loop(n_left - used, n_left, lambda j, cr: tiles([j], cr, [None]), init)
        for dg in range(0, ndiag, SB_PAIR):
            dgs = list(range(dg, dg + SB_PAIR))
            carry = tiles([i * ndiag + g for g in dgs], carry, dgs)
        dq_ref[...] = carry[2]

    qspec = pl.BlockSpec((bq, HEAD_DIM), lambda h, i: (i, h))
    kspec = pl.BlockSpec((s_len, HEAD_DIM), lambda h, i: (0, h))
    return _pcall(
        body, name="sb_bwd", grid=(nh, s_len // bq),
        out_shape=[jax.ShapeDtypeStruct((s_len, hd), F32)] * 3,
        in_specs=[qspec, kspec, kspec, qspec, pl.BlockSpec((None, bq, 1), lambda h, i: (h, i, 0))],
        out_specs=[qspec, kspec, kspec],
        compiler_params=_params(("parallel", "arbitrary")),
    )(qn, kn, vb, do, lt_diag)


GDN_GROUP = 16


def _gdn_group(nh):
    return min(GDN_GROUP, nh)


def _gdn_chunk_terms(qh, kh, vh, g_r, g_c, b_c):
    c = GDN_CHUNK
    r = lax.broadcasted_iota(jnp.int32, (c, c), 0)
    s = lax.broadcasted_iota(jnp.int32, (c, c), 1)
    tril, stril = r >= s, r > s
    gcc = jnp.sum(jnp.where(tril, g_r, 0.0), axis=1, keepdims=True)
    gcr = jnp.sum(jnp.where(r <= s, g_c, 0.0), axis=0, keepdims=True)
    dm = jnp.where(tril, jnp.exp(jnp.where(tril, gcc - gcr, 0.0)), 0.0)
    kb = kh.astype(BF16)
    kk = _dot_nt(kb, kb)
    qk = _dot_nt(qh.astype(BF16), kb)
    egc = jnp.exp(gcc)
    gcl = gcc[c - 1:c, :]
    t = dict(tril=tril, stril=stril, gcc=gcc, dm=dm, kb=kb, kk=kk, qk=qk, egc=egc,
             ekd=jnp.exp(gcl - gcc), gl=jnp.exp(gcl),
             a=jnp.where(stril, b_c * kk * dm, 0.0),
             bv=b_c * vh, bk=(b_c * egc) * kh, at=jnp.where(tril, qk * dm, 0.0))
    t["qg"] = qh * egc
    t["kd"] = kh * t["ekd"]
    return t


def _unit_lower_inverses(mats):
    c = GDN_CHUNK
    r = lax.broadcasted_iota(jnp.int32, (c, c), 0)
    s = lax.broadcasted_iota(jnp.int32, (c, c), 1)
    eye = (r == s).astype(F32)
    ps = [-a for a in mats]
    ts = [eye + p for p in ps]
    span = 2
    while span < c:
        ps = [_dot(p, p, hi=HIGH) for p in ps]
        ts = [t + _dot(t, p, hi=HIGH) for t, p in zip(ts, ps)]
        span *= 2
    return ts


def _gdn_fwd(q, k, v, g_col, g_row, b_col, b_row, shards, n_split):
    s_len, d = q.shape
    nh = d // HEAD_DIM
    c = GDN_CHUNK
    n_chunks = s_len // c
    grp = _gdn_group(nh)
    n_sh = len(shards)
    last_h, last_n = nh // grp - 1, n_chunks - 1

    def body(q_ref, k_ref, v_ref, gc_ref, gr_ref, bc_ref, br_ref, *rest):
        sh_refs, rest = rest[:n_sh], rest[n_sh:]
        o_ref, ss_ref, ts_ref = rest[:3]
        got_refs, rest = rest[3:3 + n_sh], rest[3 + n_sh:]
        st, sems = rest[0], rest[1:]
        n = pl.program_id(1)

        @_when_step(0, 0, last_h, last_n)
        def _():
            _GatherPlan(sh_refs, got_refs, sems, n_split).start()

        @_when_step(-1, 3 * n_chunks // 4, last_h, last_n)
        def _():
            _GatherPlan(sh_refs, got_refs, sems, n_split).relay()

        @_when_step(-1, -1, last_h, last_n)
        def _():
            _GatherPlan(sh_refs, got_refs, sems, n_split).finish()

        @pl.when(n == 0)
        def _():
            st[...] = jnp.zeros_like(st)

        heads = range(grp)
        sls = [slice(i * HEAD_DIM, (i + 1) * HEAD_DIM) for i in heads]
        terms = [_gdn_chunk_terms(q_ref[:, sls[i]], k_ref[:, sls[i]], v_ref[:, sls[i]],
                                  gr_ref[i:i + 1, :], gc_ref[:, i:i + 1], bc_ref[:, i:i + 1]) for i in heads]
        tinvs = _unit_lower_inverses([t["a"] for t in terms])
        wvs = [_dot(tinv, t["bv"], hi=HIGH) for tinv, t in zip(tinvs, terms)]
        wks = [_dot(tinv, t["bk"], hi=HIGH) for tinv, t in zip(tinvs, terms)]
        states = [st[i] for i in heads]
        sbs = [state.astype(BF16) for state in states]
        ubs = [(wv - _dot(wk.astype(BF16), sb)).astype(BF16) for wv, wk, sb in zip(wvs, wks, sbs)]
        for i in heads:
            t = terms[i]
            o_ref[:, sls[i]] = _dot(t["qg"].astype(BF16), sbs[i]) + _dot(t["at"].astype(BF16), ubs[i])
            ss_ref[i] = states[i]
            ts_ref[i] = tinvs[i]
            st[i] = t["gl"] * states[i] + _dot_tn(t["kd"].astype(BF16), ubs[i])

    tok = pl.BlockSpec((c, grp * HEAD_DIM), lambda h, n: (n, h))
    colspec = pl.BlockSpec((None, c, grp), lambda h, n: (h, n, 0))
    rowspec = pl.BlockSpec((None, None, grp, c), lambda h, n: (h, n, 0, 0))
    res = _pcall(
        body, name="gdn_fwd", grid=(nh // grp, n_chunks),
        out_shape=[jax.ShapeDtypeStruct((s_len, d), F32),
                   jax.ShapeDtypeStruct((n_chunks, nh, HEAD_DIM, HEAD_DIM), F32),
                   jax.ShapeDtypeStruct((n_chunks, nh, c, c), F32)] + _GatherPlan.out_shapes(shards),
        in_specs=[tok, tok, tok, colspec, rowspec, colspec, rowspec] + [_HBM] * n_sh,
        out_specs=[tok, pl.BlockSpec((None, grp, HEAD_DIM, HEAD_DIM), lambda h, n: (n, h, 0, 0)),
                   pl.BlockSpec((None, grp, c, c), lambda h, n: (n, h, 0, 0))] + [_HBM] * n_sh,
        scratch_shapes=[pltpu.VMEM((grp, HEAD_DIM, HEAD_DIM), F32)] + _GatherPlan.sem_shapes(n_sh),
        compiler_params=_params(("arbitrary", "arbitrary")),
    )(q, k, v, g_col, g_row, b_col, b_row, *shards)
    return res[0], res[1], res[2], res[3:]


def _gdn_bwd(q, k, v, g_col, g_row, b_col, b_row, states, tinvs, do, partials):
    s_len, d = q.shape
    nh = d // HEAD_DIM
    c = GDN_CHUNK
    n_chunks = s_len // c
    grp = _gdn_group(nh)
    n_p = len(partials)
    last_h, last_n = nh // grp - 1, n_chunks - 1

    def body(q_ref, k_ref, v_ref, gc_ref, gr_ref, bc_ref, br_ref, ss_ref, ts_ref, do_ref, *rest):
        part_refs, rest = rest[:n_p], rest[n_p:]
        dq_ref, dk_ref, dv_ref, dgb_ref = rest[:4]
        land_refs, rest = rest[4:4 + n_p], rest[4 + n_p:]
        dst, sems = rest[0], rest[1:]
        n = pl.program_id(1)

        @_when_step(0, 0, last_h, last_n)
        def _():
            _ScatterPlan(part_refs, land_refs, sems).start()

        @_when_step(-1, -1, last_h, last_n)
        def _():
            _ScatterPlan(part_refs, land_refs, sems).finish()

        @pl.when(n == 0)
        def _():
            dst[...] = jnp.zeros_like(dst)

        r = lax.broadcasted_iota(jnp.int32, (c, c), 0)
        s = lax.broadcasted_iota(jnp.int32, (c, c), 1)
        suffix = (r <= s).astype(F32)
        lane = lax.broadcasted_iota(jnp.int32, (c, LANES), 1)
        heads = range(grp)
        sls = [slice(i * HEAD_DIM, (i + 1) * HEAD_DIM) for i in heads]
        qs = [q_ref[:, sl] for sl in sls]
        ks = [k_ref[:, sl] for sl in sls]
        vs = [v_ref[:, sl] for sl in sls]
        bcs = [bc_ref[:, i:i + 1] for i in heads]
        ts = [_gdn_chunk_terms(qs[i], ks[i], vs[i], gr_ref[i:i + 1, :], gc_ref[:, i:i + 1], bcs[i]) for i in heads]
        tinv = [ts_ref[i] for i in heads]
        state = [ss_ref[i] for i in heads]
        sb = [x.astype(BF16) for x in state]
        dnext = [dst[i] for i in heads]
        dnb = [x.astype(BF16) for x in dnext]
        dob = [do_ref[:, sl].astype(BF16) for sl in sls]
        wv = [_dot(tinv[i], ts[i]["bv"], hi=HIGH) for i in heads]
        wk = [_dot(tinv[i], ts[i]["bk"], hi=HIGH) for i in heads]
        wkb = [x.astype(BF16) for x in wk]
        ub = [(wv[i] - _dot(wkb[i], sb[i])).astype(BF16) for i in heads]
        du = [_dot_tn(ts[i]["at"].astype(BF16), dob[i]) + _dot(ts[i]["kd"].astype(BF16), dnb[i]) for i in heads]
        dub = [x.astype(BF16) for x in du]
        dat = [jnp.where(ts[i]["tril"], _dot_nt(dob[i], ub[i]), 0.0) for i in heads]
        dqg = [_dot_nt(dob[i], sb[i]) for i in heads]
        dkd = [_dot_nt(ub[i], dnb[i]) for i in heads]
        dwk = [-_dot_nt(dub[i], sb[i]) for i in heads]
        for i in heads:
            dst[i] = (ts[i]["gl"] * dnext[i] + _dot_tn(ts[i]["qg"].astype(BF16), dob[i]) - _dot_tn(wkb[i], dub[i]))
        dbv = [_dot_tn(tinv[i], du[i], hi=HIGH) for i in heads]
        dbk = [_dot_tn(tinv[i], dwk[i], hi=HIGH) for i in heads]
        dtm = [_dot_nt(du[i], ts[i]["bv"], hi=HIGH) + _dot_nt(dwk[i], ts[i]["bk"], hi=HIGH) for i in heads]
        dtt = [_dot_nt(dtm[i], tinv[i], hi=HIGH) for i in heads]
        da = [-jnp.where(ts[i]["stril"], _dot_tn(tinv[i], dtt[i], hi=HIGH), 0.0) for i in heads]
        rs = lambda m: jnp.sum(m, axis=1, keepdims=True)
        dgb = jnp.zeros((c, LANES), F32)
        for i in heads:
            t, b_c, dm, kb = ts[i], bcs[i], ts[i]["dm"], ts[i]["kb"]
            egc, ekd = t["egc"], t["ekd"]
            dkk = da[i] * b_c * dm
            ddm = da[i] * b_c * t["kk"] + dat[i] * t["qk"]
            dqkb, dkkb = (dat[i] * dm).astype(BF16), dkk.astype(BF16)
            dq_ref[:, sls[i]] = _dot(dqkb, kb) + dqg[i] * egc
            dk_ref[:, sls[i]] = (_dot_tn(dqkb, qs[i].astype(BF16)) + _dot(dkkb, kb) + _dot_tn(dkkb, kb)
                                 + dbk[i] * (b_c * egc) + dkd[i] * ekd)
            dv_ref[:, sls[i]] = dbv[i] * b_c
            dbk_k = rs(dbk[i] * ks[i])
            dbeta = rs(da[i] * t["kk"] * dm) + rs(dbv[i] * vs[i]) + dbk_k * egc
            mx = ddm * dm
            ekd_sum = rs(dkd[i] * ks[i]) * ekd
            dgc = rs(mx) + dbk_k * b_c * egc + rs(dqg[i] * qs[i]) * egc - ekd_sum
            dgl = jnp.sum(rs(dnext[i] * state[i]), axis=0, keepdims=True)
            tail = jnp.sum(ekd_sum, axis=0, keepdims=True) + dgl * t["gl"]
            dg = (_dot(suffix, jnp.broadcast_to(dgc, (c, LANES)), hi=HIGH)[:, 0:1]
                  - rs(_dot_nt(suffix, mx, hi=HIGH)) + tail)
            dgb = dgb + jnp.where(lane == i, dbeta, 0.0) + jnp.where(lane == grp + i, dg, 0.0)
        dgb_ref[...] = dgb

    last = n_chunks - 1
    tok = pl.BlockSpec((c, grp * HEAD_DIM), lambda h, n: (last - n, h))
    colspec = pl.BlockSpec((None, c, grp), lambda h, n: (h, last - n, 0))
    rowspec = pl.BlockSpec((None, None, grp, c), lambda h, n: (h, last - n, 0, 0))
    res = _pcall(
        body, name="gdn_bwd", grid=(nh // grp, n_chunks),
        out_shape=[jax.ShapeDtypeStruct((s_len, d), F32)] * 3
        + [jax.ShapeDtypeStruct((nh // grp, s_len, LANES), F32)] + _ScatterPlan.out_shapes(partials),
        in_specs=[tok, tok, tok, colspec, rowspec, colspec, rowspec,
                  pl.BlockSpec((None, grp, HEAD_DIM, HEAD_DIM), lambda h, n: (last - n, h, 0, 0)),
                  pl.BlockSpec((None, grp, c, c), lambda h, n: (last - n, h, 0, 0)), tok] + [_HBM] * n_p,
        out_specs=[tok, tok, tok, pl.BlockSpec((None, c, LANES), lambda h, n: (h, last - n, 0))] + [_HBM] * n_p,
        scratch_shapes=[pltpu.VMEM((grp, HEAD_DIM, HEAD_DIM), F32)] + _ScatterPlan.sem_shapes(n_p),
        compiler_params=_params(("arbitrary", "arbitrary")),
    )(q, k, v, g_col, g_row, b_col, b_row, states, tinvs, do, *partials)
    return res[0], res[1], res[2], res[3], res[4:]


def _shift_down(prev8, cur, k):
    if k == 0:
        return cur
    ext = jnp.concatenate([prev8, cur], axis=0)
    return pltpu.roll(ext, k, 0)[SUBLANES:, :]


def _shift_up(cur, next8, k):
    if k == 0:
        return cur
    ext = jnp.concatenate([cur, next8], axis=0)
    n = ext.shape[0]
    return pltpu.roll(ext, n - k, 0)[:cur.shape[0], :]


def _conv_pre(i, x, prev8, w):
    prev8 = jnp.where(i == 0, 0.0, prev8)
    pre = None
    for j in range(GDN_CONV):
        term = w[j:j + 1, :] * _shift_down(prev8, x, GDN_CONV - 1 - j)
        pre = term if pre is None else pre + term
    return pre, prev8


def _l2_fwd(a, mult):
    return a * (lax.rsqrt(jnp.sum(a * a, axis=1, keepdims=True) + EPS) * mult)


def _l2_bwd(a, dy, mult):
    r = lax.rsqrt(jnp.sum(a * a, axis=1, keepdims=True) + EPS)
    dy = dy * mult
    return r * dy - a * (r * r * r) * jnp.sum(a * dy, axis=1, keepdims=True)


def _conv_fwd(xb, conv_w, group, *, norm, mult, tr=256):
    d = xb.shape[1] // 3

    def fn(i, nt, tiles, prev8, next8, cv):
        pre, _ = _conv_pre(i, tiles[0], prev8[0], cv[0])
        a = _silu(pre)
        if norm:
            a = _per_head(lambda ah: _l2_fwd(ah, mult), a)
        return [a], []

    col = Col(xb, d, group)
    wg = lax.slice_in_dim(conv_w, group * d, (group + 1) * d, axis=1)
    (y,), _ = _ew(f"conv_fwd{group}", fn, tr=tr, ins=[col], halo_prev=[col], consts=[wg], outs=[(d, F32)])
    return y


def _conv_bwd(xb, conv_w, group, dy, *, norm, mult, tr=256):
    d = xb.shape[1] // 3
    col = Col(xb, d, group)
    wg = lax.slice_in_dim(conv_w, group * d, (group + 1) * d, axis=1)

    def fn_pre(i, nt, tiles, prev8, next8, cv):
        x, dyt = tiles
        pre, p8 = _conv_pre(i, x, prev8[0], cv[0])
        if norm:
            da = _per_head(lambda ah, dh: _l2_bwd(ah, dh, mult), _silu(pre), dyt)
        else:
            da = dyt
        dpre = da * _dsilu(pre)
        tap = lax.broadcasted_iota(jnp.int32, (GDN_CONV, d), 0)
        dw = jnp.zeros((GDN_CONV, d), F32)
        for j in range(GDN_CONV):
            dw = dw + jnp.where(tap == j, _colsum(dpre * _shift_down(p8, x, GDN_CONV - 1 - j)), 0.0)
        return [dpre], [dw]

    (dpre,), (dw,) = _ew(f"conv_bwd_pre{group}", fn_pre, tr=tr, ins=[col, dy], halo_prev=[col], consts=[wg],
                         outs=[(d, F32)], accs=[(GDN_CONV, d)])

    def fn_dx(i, nt, tiles, prev8, next8, cv):
        n8 = jnp.where(i == nt - 1, 0.0, next8[0])
        dx = None
        for j in range(GDN_CONV):
            term = cv[0][j:j + 1, :] * _shift_up(tiles[0], n8, GDN_CONV - 1 - j)
            dx = term if dx is None else dx + term
        return [dx], []

    (dx,), _ = _ew(f"conv_bwd_dx{group}", fn_dx, tr=tr, ins=[dpre], halo_next=[dpre], consts=[wg], outs=[(d, BF16)])
    return dx, dw


def _adamw(name, w, m, v, grads, *, tr=64):
    shape = w.shape
    w2, m2, v2 = [a.reshape(-1, shape[-1]) for a in (w, m, v)]
    n_g = len(grads)
    bc1 = 1.0 - ADAM_B1 ** ADAM_STEP
    bc2 = 1.0 - ADAM_B2 ** ADAM_STEP

    def fn(i, nt, tiles, prev8, next8, cv):
        wt, mt, vt = tiles[:3]
        g = tiles[3]
        for extra in tiles[4:]:
            g = g + extra
        mn = ADAM_B1 * mt + (1.0 - ADAM_B1) * g
        vn = ADAM_B2 * vt + (1.0 - ADAM_B2) * (g * g)
        delta = -ADAM_LR * ((mn / bc1) / (jnp.sqrt(vn / bc2) + ADAM_EPS) + ADAM_WD * wt)
        return [g, delta, mn, vn], []

    width = shape[-1]
    outs, _ = _ew(name, fn, tr=tr, ins=[w2, m2, v2] + list(grads), outs=[(width, F32)] * 4)
    assert n_g >= 1
    return tuple(o.reshape(shape) for o in outs)


def _pad_cols(a, width):
    return jnp.pad(a, ((0, 0), (0, width - a.shape[1])))


def _gdn_layouts(gbeta, nh, n_chunks):
    grp = _gdn_group(nh)
    s_len = gbeta.shape[0]

    def lay(a):
        col = a.reshape(s_len, nh // grp, grp).transpose(1, 0, 2)
        row = a.reshape(n_chunks, GDN_CHUNK, nh // grp, grp).transpose(2, 0, 3, 1)
        return col, row

    b_col, b_row = lay(gbeta[:, :nh])
    g_col, g_row = lay(gbeta[:, nh:2 * nh])
    return g_col, g_row, b_col, b_row


def kernel(x, c, w_mod, b_mod, norm1_w, w_in, q_norm_w, k_norm_w, conv_w, a_log, dt_bias, o_norm_w, p_a, p_b, w_out, norm2_w, w_gate, w_up, w_down, loss_target, m_w_mod, m_b_mod, m_norm1_w, m_w_in, m_q_norm_w, m_k_norm_w, m_conv_w, m_a_log, m_dt_bias, m_o_norm_w, m_p_a, m_p_b, m_w_out, m_norm2_w, m_w_gate, m_w_up, m_w_down, v_w_mod, v_b_mod, v_norm1_w, v_w_in, v_q_norm_w, v_k_norm_w, v_conv_w, v_a_log, v_dt_bias, v_o_norm_w, v_p_a, v_p_b, v_w_out, v_norm2_w, v_w_gate, v_w_up, v_w_down):
    s_len, d = x.shape[1], x.shape[2]
    nh = d // HEAD_DIM
    n_chunks = s_len // GDN_CHUNK
    ff = 4 * w_gate.shape[2]
    mx, my, mc = _my_pos()
    chip = 2 * mx + my
    dev = 2 * chip + mc
    x2 = x[0]
    tgt = loss_target[0]

    c_all = _allgather8("ag_c", _pad_cols(c, d).reshape(SUBLANES, d // SUBLANES)).reshape(8, d)
    wm = w_mod[0]
    mod_w = wm.shape[1]
    bm_cols = lax.dynamic_slice_in_dim(b_mod, chip * mod_w, mod_w, axis=1)

    def mod_body(c_ref, w_ref, b_ref, o_ref, ca_ref):
        ca = _silu(c_ref[...])
        ca_ref[...] = ca
        o_ref[...] = _dot(ca, w_ref[...], hi=HIGHEST) + b_ref[...]

    tn_mod = _pick(mod_w, 512)
    mod8, c_act = _pcall(
        mod_body, name="mod_fwd", grid=(mod_w // tn_mod,),
        out_shape=[jax.ShapeDtypeStruct((8, mod_w), F32), jax.ShapeDtypeStruct((8, d), F32)],
        in_specs=[pl.BlockSpec((8, d), lambda j: (0, 0)), pl.BlockSpec((d, tn_mod), lambda j: (0, j)),
                  pl.BlockSpec((1, tn_mod), lambda j: (0, j))],
        out_specs=[pl.BlockSpec((8, tn_mod), lambda j: (0, j)), pl.BlockSpec((8, d), lambda j: (0, 0))],
        compiler_params=_params(("arbitrary",)),
    )(c_all, wm, bm_cols)
    mod_all = _allgather8("ag_mod", mod8)
    mod_me = mod_all.reshape(4, 2, 8, mod_w)[:, mc, dev, :].reshape(1, 6 * d)
    shift1, scale1, gate1, shift2, scale2, gate2 = [mod_me[:, j * d:(j + 1) * d] for j in range(6)]

    first = [w_in[0].astype(BF16), conv_w[0]]
    late_sb = [p_a[0].astype(BF16), p_b[0].astype(BF16), w_out[0].astype(BF16)]
    late_gdn = [w_gate[0].astype(BF16), w_up[0].astype(BF16), w_down[0].astype(BF16)]
    w_in_g, conv_g = [_fill_slot(g, sh, chip) for g, sh in zip(_gather4("ag_w_in", first, n_split=1), first)]
    w_in_f = w_in_g.transpose(1, 0, 2).reshape(d, -1)
    wa = w_in_f[:, :3 * d]
    wb = w_in_f[:, 3 * d:6 * d]
    wzg = jnp.concatenate([w_in_f[:, 6 * d:7 * d], w_in_f[:, 7 * d + 2 * nh:]], axis=1)
    wba = _pad_cols(w_in_f[:, 7 * d:7 * d + 2 * nh], LANES)
    conv_f = conv_g.transpose(1, 0, 2).reshape(GDN_CONV, 3 * d)

    def norm_mod_fn(i, nt, tiles, prev8, next8, cv):
        w, sc, sh = cv
        return [_rms_fwd(tiles[0], w) * (1.0 + sc) + sh], []

    (u1,), _ = _ew("norm_mod1", norm_mod_fn, tr=512, ins=[x2], consts=[norm1_w, scale1, shift1], outs=[(d, BF16)])
    proj_a = _mm("proj_a", u1, wa, out_dtype=BF16)
    proj_b = _mm("proj_b", u1, wb)
    proj_zg = _mm("proj_zg", u1, wzg, out_dtype=BF16)
    proj_ba = _mm("proj_ba", u1, wba)

    def qknorm_fn(i, nt, tiles, prev8, next8, cv):
        qa, ka, va = [t.astype(F32) for t in tiles]
        return [_per_head(lambda h: _rms_fwd(h, cv[0]), qa), _per_head(lambda h: _rms_fwd(h, cv[1]), ka), va], []

    (qn, kn, vb), _ = _ew("qknorm", qknorm_fn, tr=256,
                          ins=[Col(proj_a, d, 0), Col(proj_a, d, 1), Col(proj_a, d, 2)],
                          consts=[q_norm_w, k_norm_w], outs=[(d, BF16)] * 3)
    sb_out = _sb_fwd(qn, kn, vb, late_sb)
    o_a, lt_diag = sb_out[0], sb_out[1]
    p_a_f, p_b_f, w_out_f = [_fill_slot(g, sh, chip).reshape(d, d) for g, sh in zip(sb_out[2:], late_sb)]

    lane_ids = jnp.arange(LANES)
    is_b = (lane_ids < nh)[None, :]
    is_a = ((lane_ids >= nh) & (lane_ids < 2 * nh))[None, :]
    alog128 = jnp.zeros((1, LANES), F32).at[:, nh:2 * nh].set(a_log)
    dtb128 = jnp.zeros((1, LANES), F32).at[:, nh:2 * nh].set(dt_bias)
    is_b_f, is_a_f = is_b.astype(F32), is_a.astype(F32)

    def gbeta_fn(i, nt, tiles, prev8, next8, cv):
        al, dtb, mb, ma = cv
        ba = tiles[0]
        g = -jnp.exp(al) * _softplus(ba + dtb)
        return [jnp.where(mb > 0.5, _sigmoid(ba), jnp.where(ma > 0.5, g, 0.0))], []

    (gbeta,), _ = _ew("gbeta", gbeta_fn, tr=1024, ins=[proj_ba], consts=[alog128, dtb128, is_b_f, is_a_f],
                      outs=[(LANES, F32)])
    g_col, g_row, b_col, b_row = _gdn_layouts(gbeta, nh, n_chunks)
    qscale = HEAD_DIM ** -0.5
    q_b = _conv_fwd(proj_b, conv_f, 0, norm=True, mult=qscale)
    k_b = _conv_fwd(proj_b, conv_f, 1, norm=True, mult=1.0)
    v_b = _conv_fwd(proj_b, conv_f, 2, norm=False, mult=1.0)
    o_raw, states, tinvs, late_g = _gdn_fwd(q_b, k_b, v_b, g_col, g_row, b_col, b_row, late_gdn,
                                            n_split=len(late_gdn))
    late_g = [_fill_slot(g, sh, chip) for g, sh in zip(late_g, late_gdn)]
    w_gate_f, w_up_f = [g.transpose(1, 0, 2).reshape(d, ff) for g in late_g[0:2]]
    w_down_f = late_g[2].reshape(ff, d)

    def gated_norm_fn(i, nt, tiles, prev8, next8, cv):
        o, z = tiles[0], tiles[1].astype(F32)
        return [_per_head(lambda h: _rms_fwd(h, cv[0]), o) * _silu(z)], []

    (o_b,), _ = _ew("gated_norm", gated_norm_fn, tr=256, ins=[o_raw, Col(proj_zg, d, 0)], consts=[o_norm_w],
                    outs=[(d, BF16)])
    y_a = _mm("out_a", o_a, p_a_f)
    y_b = _mm("out_b", o_b, p_b_f)

    def merge_fn(i, nt, tiles, prev8, next8, cv):
        ya, yb, ga, gb = [t.astype(F32) for t in tiles]
        return [_sigmoid(ga) * ya + _sigmoid(gb) * yb], []

    (merged,), _ = _ew("merge", merge_fn, tr=256, ins=[y_a, y_b, Col(proj_zg, d, 1), Col(proj_zg, d, 2)],
                       outs=[(d, BF16)])
    y_o = _mm("out_proj", merged, w_out_f)

    def resid_norm_fn(i, nt, tiles, prev8, next8, cv):
        xt, yo = tiles
        g1, w, sc, sh = cv
        h1 = xt + g1 * yo
        return [h1, _rms_fwd(h1, w) * (1.0 + sc) + sh], []

    (h1, u2), _ = _ew("resid_norm2", resid_norm_fn, tr=256, ins=[x2, y_o],
                      consts=[gate1, norm2_w, scale2, shift2], outs=[(d, F32), (d, BF16)])
    gt = _mm("ff_gate", u2, w_gate_f, out_dtype=BF16)
    up = _mm("ff_up", u2, w_up_f, out_dtype=BF16)

    def swiglu_fn(i, nt, tiles, prev8, next8, cv):
        return [_silu(tiles[0].astype(F32)) * tiles[1].astype(F32)], []

    (act,), _ = _ew("swiglu", swiglu_fn, tr=128, ins=[gt, up], outs=[(ff, BF16)])
    y_d = _mm("ff_down", act, w_down_f)

    def loss_fn(i, nt, tiles, prev8, next8, cv):
        h1t, yd, tg = tiles
        diff = h1t + cv[0] * yd - tg
        dy = diff * (1.0 / d)
        return [dy, dy * cv[0]], [_colsum(0.5 * diff * dy), _colsum(dy * yd)]

    (dy, dyd), (loss_cols, dgate2) = _ew("loss", loss_fn, tr=256, ins=[h1, y_d, tgt], consts=[gate2],
                                         outs=[(d, F32), (d, BF16)], accs=[(1, d), (1, d)])
    loss = lax.psum(jnp.sum(loss_cols), ("x", "y", "c"))

    dact = _mm("d_act", dyd, w_down_f, nt=True, out_dtype=BF16)
    g_w_down = _mm("g_w_down", act, dyd, ta=True)

    def swiglu_bwd_fn(i, nt, tiles, prev8, next8, cv):
        da, g, u = [t.astype(F32) for t in tiles]
        return [jnp.concatenate([da * u * _dsilu(g), da * _silu(g)], axis=1)], []

    (d_gu,), _ = _ew("swiglu_bwd", swiglu_bwd_fn, tr=128, ins=[dact, gt, up], outs=[(2 * ff, BF16)])
    du2 = _mm("d_u2", d_gu, jnp.concatenate([w_gate_f, w_up_f], axis=1), nt=True)
    g_w_gate = _mm("g_w_gate", u2, d_gu, ta=True, b_cols=(0, ff))
    g_w_up = _mm("g_w_up", u2, d_gu, ta=True, b_cols=(ff, ff))

    def norm2_bwd_fn(i, nt, tiles, prev8, next8, cv):
        h1t, du, dres, yo = tiles
        w, sc, g1 = cv
        r = lax.rsqrt(jnp.mean(h1t * h1t, axis=1, keepdims=True) + EPS)
        nrm = h1t * r
        dn = du * w * (1.0 + sc)
        dh = r * (dn - nrm * jnp.mean(dn * nrm, axis=1, keepdims=True)) + dres
        return [dh, dh * g1], [_colsum(du), _colsum(du * nrm * w), _colsum(du * nrm * (1.0 + sc)), _colsum(dh * yo)]

    (dh1, dyo), (dshift2, dscale2, g_norm2, dgate1) = _ew(
        "norm2_bwd", norm2_bwd_fn, tr=256, ins=[h1, du2, dy, y_o], consts=[norm2_w, scale2, gate1],
        outs=[(d, F32), (d, BF16)], accs=[(1, d)] * 4)

    dmerged = _mm("d_merged", dyo, w_out_f, nt=True)
    g_w_out = _mm("g_w_out", merged, dyo, ta=True)

    def merge_bwd_fn(i, nt, tiles, prev8, next8, cv):
        dm, ya, yb, ga, gb = [t.astype(F32) for t in tiles]
        sa, sb = _sigmoid(ga), _sigmoid(gb)
        return [dm * sa, dm * sb, dm * ya * sa * (1.0 - sa), dm * yb * sb * (1.0 - sb)], []

    (dya, dyb, dga, dgb_gate), _ = _ew(
        "merge_bwd", merge_bwd_fn, tr=256, ins=[dmerged, y_a, y_b, Col(proj_zg, d, 1), Col(proj_zg, d, 2)],
        outs=[(d, BF16)] * 4)
    do_a = _mm("d_o_a", dya, p_a_f, nt=True, out_dtype=BF16)
    g_p_a = _mm("g_p_a", o_a, dya, ta=True)
    do_b = _mm("d_o_b", dyb, p_b_f, nt=True)
    g_p_b = _mm("g_p_b", o_b, dyb, ta=True)

    def gated_norm_bwd_fn(i, nt, tiles, prev8, next8, cv):
        dob, o, z = tiles[0], tiles[1], tiles[2].astype(F32)
        sz = _silu(z)

        def head(oh, dh):
            return _rms_bwd(oh, cv[0], dh)

        dxo, dwn = _per_head(head, o, dob * sz)
        nrm_w = _per_head(lambda h: _rms_fwd(h, cv[0]), o)
        return [dxo, dob * nrm_w * _dsilu(z)], [_colsum(_head_sum(dwn))]

    (do_raw, dz_b), (g_o_norm,) = _ew(
        "gated_norm_bwd", gated_norm_bwd_fn, tr=256, ins=[do_b, o_raw, Col(proj_zg, d, 0)], consts=[o_norm_w],
        outs=[(d, F32), (d, BF16)], accs=[(1, HEAD_DIM)])
    by_chip = lambda a: a.reshape(a.shape[0], 4, -1).transpose(1, 0, 2)

    def chip_sums(tag, raw, axes):
        theirs = _sibling_send(f"swap_{tag}", raw, axes)
        sums = []
        for t, (part, ax, other) in enumerate(zip(raw, axes, theirs)):
            def pair_fn(i, nt, tiles, prev8, next8, cv):
                return [tiles[0] + tiles[1]], []

            hr, width = part.shape[ax] // 2, part.shape[-1]
            mine = lax.dynamic_slice_in_dim(part, mc * hr, hr, axis=ax)
            (ch,), _ = _ew(f"pair_sum_{tag}{t}", pair_fn, tr=64,
                           ins=[mine.reshape(-1, width), other.reshape(-1, width)], outs=[(width, BF16)])
            sums.append(ch.reshape(other.shape))
        return sums

    s_gate, s_up, s_pa, s_pb, s_out, s_down = chip_sums(
        "late", [g_w_gate, g_w_up, g_p_a.reshape(4, d // 4, d), g_p_b.reshape(4, d // 4, d),
                 g_w_out.reshape(4, d // 4, d), g_w_down.reshape(4, ff // 4, d)], [0, 0, 1, 1, 1, 1])
    late_halves = [s_pa, s_pb, s_out, by_chip(s_gate), by_chip(s_up), s_down]
    dq_b, dk_b, dv_b, dgb_grp, late_landed = _gdn_bwd(q_b, k_b, v_b, g_col, g_row, b_col, b_row, states, tinvs,
                                                      do_raw, late_halves)
    grp = _gdn_group(nh)
    dbeta = dgb_grp[:, :, :grp].transpose(1, 0, 2).reshape(s_len, nh)
    dg = dgb_grp[:, :, grp:2 * grp].transpose(1, 0, 2).reshape(s_len, nh)
    dgbeta = _pad_cols(jnp.concatenate([dbeta, dg], axis=1), LANES)

    def gbeta_bwd_fn(i, nt, tiles, prev8, next8, cv):
        al, dtb, mb, ma = cv
        ba, dgb = tiles
        beta = _sigmoid(ba)
        arg = ba + dtb
        da = dgb * (-jnp.exp(al)) * _sigmoid(arg)
        g = -jnp.exp(al) * _softplus(arg)
        dba = jnp.where(mb > 0.5, dgb * beta * (1.0 - beta), jnp.where(ma > 0.5, da, 0.0))
        return [dba], [_colsum(jnp.where(ma > 0.5, dgb * g, 0.0)), _colsum(jnp.where(ma > 0.5, da, 0.0))]

    (dba,), (g_alog128, g_dtb128) = _ew(
        "gbeta_bwd", gbeta_bwd_fn, tr=1024, ins=[proj_ba, dgbeta], consts=[alog128, dtb128, is_b_f, is_a_f],
        outs=[(LANES, BF16)], accs=[(1, LANES)] * 2)
    dxq, g_conv_q = _conv_bwd(proj_b, conv_f, 0, dq_b, norm=True, mult=qscale)
    dxk, g_conv_k = _conv_bwd(proj_b, conv_f, 1, dk_b, norm=True, mult=1.0)
    dxv, g_conv_v = _conv_bwd(proj_b, conv_f, 2, dv_b, norm=False, mult=1.0)
    g_conv = jnp.concatenate([g_conv_q, g_conv_k, g_conv_v], axis=1)

    dqn, dkn, dvb = _sb_bwd(qn, kn, vb, do_a, lt_diag)

    def qknorm_bwd_fn(i, nt, tiles, prev8, next8, cv):
        qa, ka, dq, dk, dv = [t.astype(F32) for t in tiles]
        dxq_, dwq = _per_head(lambda h, g: _rms_bwd(h, cv[0], g), qa, dq)
        dxk_, dwk = _per_head(lambda h, g: _rms_bwd(h, cv[1], g), ka, dk)
        return [dxq_, dxk_, dv], [_colsum(_head_sum(dwq)), _colsum(_head_sum(dwk))]

    (dqa, dka, dva), (g_q_norm, g_k_norm) = _ew(
        "qknorm_bwd", qknorm_bwd_fn, tr=256, ins=[Col(proj_a, d, 0), Col(proj_a, d, 1), dqn, dkn, dvb],
        consts=[q_norm_w, k_norm_w], outs=[(d, BF16)] * 3, accs=[(1, HEAD_DIM)] * 2)

    d_all = jnp.concatenate([dqa, dka, dva, dxq, dxk, dxv, dz_b, dga, dgb_gate, dba], axis=1)
    w_all = jnp.concatenate([wa, wb, wzg, wba], axis=1)
    g_wa = _mm("g_w_in_a", u1, d_all, ta=True, b_cols=(0, 3 * d))
    g_wb = _mm("g_w_in_b", u1, d_all, ta=True, b_cols=(3 * d, 3 * d))
    g_wzg = _mm("g_w_in_zg", u1, d_all, ta=True, b_cols=(6 * d, 3 * d))
    g_wba = _mm("g_w_in_ba", u1, d_all, ta=True, b_cols=(9 * d, LANES))
    s_wa, s_wb, s_wzg, s_wba = chip_sums("w_in", [g_wa, g_wb, g_wzg, g_wba], [0, 0, 0, 0])
    s_w_in = jnp.concatenate([s_wa, s_wb, s_wzg[:, :d], s_wba[:, :2 * nh], s_wzg[:, d:]], axis=1)
    w_in_halves = [by_chip(s_w_in)]
    du1, w_in_landed = _mm("d_u1", d_all, w_all, nt=True, tm=512, scatter=w_in_halves)


    def norm1_bwd_fn(i, nt, tiles, prev8, next8, cv):
        xt, du, dres = tiles
        w, sc = cv
        r = lax.rsqrt(jnp.mean(xt * xt, axis=1, keepdims=True) + EPS)
        nrm = xt * r
        dn = du * w * (1.0 + sc)
        dxt = r * (dn - nrm * jnp.mean(dn * nrm, axis=1, keepdims=True)) + dres
        return [dxt], [_colsum(du), _colsum(du * nrm * w), _colsum(du * nrm * (1.0 + sc))]

    (grad_x,), (dshift1, dscale1, g_norm1) = _ew(
        "norm1_bwd", norm1_bwd_fn, tr=256, ins=[x2, du1, dh1], consts=[norm1_w, scale1],
        outs=[(d, F32)], accs=[(1, d)] * 3)

    dmod_me = jnp.concatenate([dshift1, dscale1, dgate1, dshift2, dscale2, dgate2], axis=1)
    small = jnp.concatenate(
        [dmod_me, g_norm1, g_norm2, g_q_norm, g_k_norm, g_o_norm, g_alog128[:, nh:2 * nh], g_dtb128[:, nh:2 * nh],
         g_conv.reshape(1, -1)], axis=1)
    n_small = small.shape[1]
    pad_to = -(-n_small // (SUBLANES * LANES)) * (SUBLANES * LANES)
    small_all = _allgather8("ag_small", _pad_cols(small, pad_to).reshape(SUBLANES, pad_to // SUBLANES))
    small_all = small_all.reshape(8, pad_to)

    def sum8_fn(i, nt, tiles, prev8, next8, cv):
        return [], [_colsum(tiles[0])]

    _, (small_sum,) = _ew("sum_small", sum8_fn, tr=8, ins=[small_all], accs=[(1, pad_to)])
    offs = [0]
    for width in (6 * d, d, d, HEAD_DIM, HEAD_DIM, HEAD_DIM, nh, nh, GDN_CONV * 3 * d):
        offs.append(offs[-1] + width)
    pieces = [small_sum[:, offs[j]:offs[j + 1]] for j in range(9)]
    (gs_b_mod, gs_norm1, gs_norm2, gs_q_norm, gs_k_norm, gs_o_norm, gs_a_log, gs_dt_bias, gs_conv) = pieces
    conv_cols = 3 * d // 4
    gs_conv_mine = lax.dynamic_slice_in_dim(gs_conv.reshape(GDN_CONV, 3 * d), chip * conv_cols, conv_cols, axis=1)

    dmod_all = lax.dynamic_slice_in_dim(small_all[:, :6 * d], chip * mod_w, mod_w, axis=1)

    def wmod_grad_body(ct_ref, dm_ref, o_ref):
        o_ref[...] = _dot(ct_ref[...], dm_ref[...], hi=HIGHEST)

    g_w_mod = _pcall(
        wmod_grad_body, name="g_w_mod", grid=(mod_w // tn_mod,),
        out_shape=jax.ShapeDtypeStruct((d, mod_w), F32),
        in_specs=[pl.BlockSpec((d, 8), lambda j: (0, 0)), pl.BlockSpec((8, tn_mod), lambda j: (0, j))],
        out_specs=pl.BlockSpec((d, tn_mod), lambda j: (0, j)),
        compiler_params=_params(("arbitrary",)),
    )(c_act.T, dmod_all)

    chip_halves = w_in_halves + late_halves
    landed = list(w_in_landed) + list(late_landed)
    landed = [_fill_slot(land, ch, chip) for land, ch in zip(landed, chip_halves)]
    g_mine = []
    for t, land in enumerate(landed):
        def sum4_fn(i, nt, tiles, prev8, next8, cv):
            f = [tl.astype(F32) for tl in tiles]
            return [(f[0] + f[1]) + (f[2] + f[3])], []

        (gh,), _ = _ew(f"chip_sum{t}", sum4_fn, tr=64, ins=[Col(land, lead=s) for s in range(4)],
                       outs=[(land.shape[-1], F32)])
        g_mine.append(gh)
    g_theirs = _sibling_send("join_grads", g_mine)
    g_full = [jnp.concatenate([jnp.where(mc == 0, a, b), jnp.where(mc == 0, b, a)], axis=0)
              for a, b in zip(g_mine, g_theirs)]

    big = {}
    names = ["w_in", "p_a", "p_b", "w_out", "w_gate", "w_up", "w_down"]
    big_w = [w_in, p_a, p_b, w_out, w_gate, w_up, w_down]
    big_m = [m_w_in, m_p_a, m_p_b, m_w_out, m_w_gate, m_w_up, m_w_down]
    big_v = [v_w_in, v_p_a, v_p_b, v_w_out, v_w_gate, v_w_up, v_w_down]
    for t, nm in enumerate(names):
        big[nm] = _adamw(f"adamw_{nm}", big_w[t], big_m[t], big_v[t], [g_full[t]])
    big["w_mod"] = _adamw("adamw_w_mod", w_mod, m_w_mod, v_w_mod, [g_w_mod])
    big["conv_w"] = _adamw("adamw_conv_w", conv_w, m_conv_w, v_conv_w, [gs_conv_mine], tr=8)
    small_names = ["b_mod", "norm1_w", "norm2_w", "q_norm_w", "k_norm_w", "o_norm_w", "a_log", "dt_bias"]
    small_w = [b_mod, norm1_w, norm2_w, q_norm_w, k_norm_w, o_norm_w, a_log, dt_bias]
    small_m = [m_b_mod, m_norm1_w, m_norm2_w, m_q_norm_w, m_k_norm_w, m_o_norm_w, m_a_log, m_dt_bias]
    small_v = [v_b_mod, v_norm1_w, v_norm2_w, v_q_norm_w, v_k_norm_w, v_o_norm_w, v_a_log, v_dt_bias]
    small_g = [gs_b_mod, gs_norm1, gs_norm2, gs_q_norm, gs_k_norm, gs_o_norm, gs_a_log, gs_dt_bias]
    rep_w = jnp.concatenate(small_w, axis=1)
    rep_m = jnp.concatenate(small_m, axis=1)
    rep_v = jnp.concatenate(small_v, axis=1)
    rep_g = jnp.concatenate(small_g, axis=1)
    rep = _adamw("adamw_small", rep_w, rep_m, rep_v, [rep_g], tr=1)
    roffs = [0]
    for a in small_w:
        roffs.append(roffs[-1] + a.shape[1])
    for j, nm in enumerate(small_names):
        big[nm] = tuple(r[:, roffs[j]:roffs[j + 1]] for r in rep)

    order = ["w_mod", "b_mod", "norm1_w", "w_in", "q_norm_w", "k_norm_w", "conv_w", "a_log", "dt_bias", "o_norm_w",
             "p_a", "p_b", "w_out", "norm2_w", "w_gate", "w_up", "w_down"]
    grads = [big[nm][0] for nm in order]
    deltas = [big[nm][1] for nm in order]
    new_m = [big[nm][2] for nm in order]
    new_v = [big[nm][3] for nm in order]
    return (loss, grad_x[None], *grads, *deltas, *new_m, *new_v)
```

```python
import jax
import jax.numpy as jnp
from jax import lax
from jax.experimental import pallas as pl
from jax.experimental.pallas import tpu as pltpu

F32 = jnp.float32
BF16 = jnp.bfloat16
HIGHEST = lax.Precision.HIGHEST
HIGH = lax.Precision.HIGH
MESH = pl.DeviceIdType.MESH

HEAD_DIM = 128
GDN_CHUNK = 64
GDN_CONV = 4
EPS = 1e-6
LANES = 128
SUBLANES = 8
VMEM_LIMIT = 56 * 1024 * 1024
MM_VMEM_BUDGET = 40 * 1024 * 1024

ADAM_LR = 0.001
ADAM_B1 = 0.9
ADAM_B2 = 0.999
ADAM_EPS = 1e-08
ADAM_WD = 0.01
ADAM_STEP = 10


def _pcall(body, **kw):
    return pl.pallas_call(body, **kw)


def _params(sem=None):
    if sem is None:
        return pltpu.CompilerParams(vmem_limit_bytes=VMEM_LIMIT)
    return pltpu.CompilerParams(dimension_semantics=sem, vmem_limit_bytes=VMEM_LIMIT)


def _pick(dim, target):
    if dim <= target:
        return dim
    best = None
    for t in range(LANES, target + 1, LANES):
        if dim % t == 0:
            best = t
    assert best is not None, (dim, target)
    return best


def _rows_tile(rows, target):
    t = min(rows, target)
    while rows % t:
        t //= 2
    assert t >= SUBLANES or t == rows, (rows, target)
    return t


def _dot(a, b, hi=None):
    return jnp.dot(a, b, preferred_element_type=F32, precision=hi)


def _dot_nt(a, b, hi=None):
    return lax.dot_general(a, b, (((1,), (1,)), ((), ())), preferred_element_type=F32, precision=hi)


def _dot_tn(a, b, hi=None):
    return lax.dot_general(a, b, (((0,), (0,)), ((), ())), preferred_element_type=F32, precision=hi)


def _sigmoid(x):
    return 1.0 / (1.0 + jnp.exp(-x))


def _softplus(x):
    return jnp.maximum(x, 0.0) + jnp.log(1.0 + jnp.exp(-jnp.abs(x)))


_HBM = pl.BlockSpec(memory_space=pltpu.HBM)


def _my_pos():
    return lax.axis_index("x"), lax.axis_index("y"), lax.axis_index("c")


def _allgather8(name, v):
    def body(v_ref, o_ref, ssem, rsem, lsem):
        x, y, c = _my_pos()
        me = 4 * x + 2 * y + c
        loc = pltpu.make_async_copy(v_ref, o_ref.at[me], lsem)
        loc.start()
        sends, recvs = [], []
        for k in range(1, 8):
            px, py, pc = (x + (k >> 2)) % 2, (y + ((k >> 1) & 1)) % 2, (c + (k & 1)) % 2
            cp = pltpu.make_async_remote_copy(
                src_ref=v_ref, dst_ref=o_ref.at[me], send_sem=ssem.at[k - 1], recv_sem=rsem.at[k - 1],
                device_id=(px, py, pc), device_id_type=MESH)
            cp.start()
            sends.append(cp)
            recvs.append(pltpu.make_async_remote_copy(
                src_ref=v_ref, dst_ref=o_ref.at[4 * px + 2 * py + pc], send_sem=ssem.at[k - 1],
                recv_sem=rsem.at[k - 1], device_id=(px, py, pc), device_id_type=MESH))
        for rc in recvs:
            rc.wait_recv()
        for cp in sends:
            cp.wait_send()
        loc.wait()

    return _pcall(
        body, name=name, out_shape=jax.ShapeDtypeStruct((8,) + v.shape, v.dtype),
        in_specs=[_HBM], out_specs=_HBM,
        scratch_shapes=[pltpu.SemaphoreType.DMA((7,)), pltpu.SemaphoreType.DMA((7,)), pltpu.SemaphoreType.DMA],
    )(v)


def _plane_peers(x, y):
    return [((x + (k >> 1)) % 2, (y + (k & 1)) % 2) for k in range(1, 4)]


class _GatherPlan:
    def __init__(self, ins, outs, sems, n_split):
        ssem, rsem, fsem, gsem = sems
        x, y, c = _my_pos()
        me = 2 * x + y
        copy = lambda src, dst, s_sem, r_sem, dev: (lambda: pltpu.make_async_remote_copy(
            src_ref=src, dst_ref=dst, send_sem=s_sem, recv_sem=r_sem, device_id=dev, device_id_type=MESH))
        self.sends, self.recvs, self.fwds, self.fwd_recvs = [], [], [], []
        for t in range(len(ins)):
            split = t < n_split
            hr = ins[t].shape[0] // 2
            for k, (px, py) in enumerate(_plane_peers(x, y)):
                peer = 2 * px + py
                sem = 3 * t + k
                if split:
                    mine = pl.ds(pl.multiple_of(c * hr, 16), hr)
                    other = pl.ds(pl.multiple_of((1 - c) * hr, 16), hr)
                    src, dst, got = ins[t].at[mine], outs[t].at[me, mine], outs[t].at[peer, mine]
                else:
                    src, dst, got = ins[t], outs[t].at[me], outs[t].at[peer]
                self.sends.append(copy(src, dst, ssem.at[sem], rsem.at[sem], (px, py, c)))
                self.recvs.append(copy(src, got, ssem.at[sem], rsem.at[sem], (px, py, c)))
                if split:
                    self.fwds.append(copy(got, got, fsem.at[sem], gsem.at[sem], (x, y, 1 - c)))
                    self.fwd_recvs.append(copy(got, outs[t].at[peer, other], fsem.at[sem], gsem.at[sem], (x, y, 1 - c)))
                else:
                    self.fwds.append(None)

    def start(self):
        for cp in self.sends:
            cp().start()

    def relay(self):
        for rc, fw in zip(self.recvs, self.fwds):
            rc().wait_recv()
            if fw is not None:
                fw().start()

    def finish(self):
        for fr in self.fwd_recvs:
            fr().wait_recv()
        for cp in self.sends + [fw for fw in self.fwds if fw is not None]:
            cp().wait_send()

    @staticmethod
    def out_shapes(shards):
        return [jax.ShapeDtypeStruct((4,) + s.shape, s.dtype) for s in shards]

    @staticmethod
    def sem_shapes(n):
        return [pltpu.SemaphoreType.DMA((3 * n,))] * 4


def _gather4(name, shards, n_split):
    n = len(shards)

    def body(*refs):
        plan = _GatherPlan(refs[:n], refs[n:2 * n], refs[2 * n:], n_split)
        plan.start()
        plan.relay()
        plan.finish()

    return _pcall(
        body, name=name, out_shape=_GatherPlan.out_shapes(shards), in_specs=[_HBM] * n, out_specs=[_HBM] * n,
        scratch_shapes=_GatherPlan.sem_shapes(n),
    )(*shards)


def _when_step(h, n, hs, ns):
    return pl.when(jnp.logical_and(pl.program_id(0) == (hs if h < 0 else h), pl.program_id(1) == (ns if n < 0 else n)))


def _fill_slot(slots, own, slot):
    mask = (jnp.arange(4) == slot).reshape((4,) + (1,) * (slots.ndim - 1))
    return jnp.where(mask, own if own.ndim == slots.ndim else own[None], slots)


class _ScatterPlan:
    def __init__(self, ins, outs, sems):
        ssem, rsem = sems
        x, y, c = _my_pos()
        me = 2 * x + y
        copy = lambda src, dst, s_sem, r_sem, dev: (lambda: pltpu.make_async_remote_copy(
            src_ref=src, dst_ref=dst, send_sem=s_sem, recv_sem=r_sem, device_id=dev, device_id_type=MESH))
        self.sends, self.recvs = [], []
        for t in range(len(ins)):
            for k, (px, py) in enumerate(_plane_peers(x, y)):
                peer = 2 * px + py
                sem = 3 * t + k
                self.sends.append(copy(ins[t].at[peer], outs[t].at[me], ssem.at[sem], rsem.at[sem], (px, py, c)))
                self.recvs.append(copy(ins[t].at[peer], outs[t].at[peer], ssem.at[sem], rsem.at[sem], (px, py, c)))

    def start(self):
        for cp in self.sends:
            cp().start()

    def finish(self):
        for rc in self.recvs:
            rc().wait_recv()
        for cp in self.sends:
            cp().wait_send()

    @staticmethod
    def out_shapes(partials):
        return [jax.ShapeDtypeStruct(p.shape, p.dtype) for p in partials]

    @staticmethod
    def sem_shapes(n):
        return [pltpu.SemaphoreType.DMA((3 * n,))] * 2


def _sibling_send(name, arrays, axes=None):
    n = len(arrays)
    axes = [None] * n if axes is None else axes

    def body(*refs):
        ins, outs = refs[:n], refs[n:2 * n]
        ssem, rsem = refs[2 * n:]
        x, y, c = _my_pos()
        cps = []
        for t in range(n):
            src = ins[t]
            if axes[t] is not None:
                hr = ins[t].shape[axes[t]] // 2
                give = pl.ds(pl.multiple_of((1 - c) * hr, SUBLANES), hr)
                src = ins[t].at[give] if axes[t] == 0 else ins[t].at[:, give]
            cp = pltpu.make_async_remote_copy(
                src_ref=src, dst_ref=outs[t], send_sem=ssem.at[t], recv_sem=rsem.at[t],
                device_id=(x, y, 1 - c), device_id_type=MESH)
            cp.start()
            cps.append(cp)
        for cp in cps:
            cp.wait_recv()
        for cp in cps:
            cp.wait_send()

    def half(a, axis):
        shape = list(a.shape)
        if axis is not None:
            shape[axis] //= 2
        return jax.ShapeDtypeStruct(tuple(shape), a.dtype)

    return _pcall(
        body, name=name, out_shape=[half(a, ax) for a, ax in zip(arrays, axes)], in_specs=[_HBM] * n,
        out_specs=[_HBM] * n,
        scratch_shapes=[pltpu.SemaphoreType.DMA((n,)), pltpu.SemaphoreType.DMA((n,))],
    )(*arrays)


def _mm(name, a, b, *, nt=False, ta=False, out_dtype=F32, add=None, tm=1024, tn=1024, tk=4096, b_cols=None,
        scatter=()):
    m, k = (a.shape[1], a.shape[0]) if ta else a.shape
    n = b.shape[0] if nt else b.shape[1]
    assert (b.shape[1] if nt else b.shape[0]) == k
    col0 = 0
    if b_cols is not None:
        assert not nt
        col0, n = b_cols
    has_add = add is not None
    tm, tn = _pick(m, tm), _pick(n, tn)
    assert col0 % tn == 0
    n_sc = len(scatter)
    out_bytes = jnp.dtype(out_dtype).itemsize

    def vmem_bytes(tk_):
        steps = k // tk_
        return (4 * (tm + tn) * tk_ + 2 * tm * tn * out_bytes + (8 * tm * tn if has_add else 0)
                + (4 * tm * tn if steps > 1 else 0))

    tk = _pick(k, tk)
    while vmem_bytes(tk) > MM_VMEM_BUDGET and tk > 512:
        tk = _pick(k, tk - LANES)
    nk = k // tk

    grid = (n // tn, m // tm, nk)

    def body(*refs):
        a_ref, b_ref = refs[0], refs[1]
        c_ref = refs[2] if has_add else None
        part_refs = refs[2 + has_add:2 + has_add + n_sc]
        o_ref = refs[2 + has_add + n_sc]
        land_refs = refs[3 + has_add + n_sc:3 + has_add + 2 * n_sc]
        rest = refs[3 + has_add + 2 * n_sc:]
        if n_sc:
            sems = rest[-2:]
            at = lambda step: pl.when(jnp.logical_and(jnp.logical_and(
                pl.program_id(0) == step[0], pl.program_id(1) == step[1]), pl.program_id(2) == step[2]))

            @at((0, 0, 0))
            def _():
                _ScatterPlan(part_refs, land_refs, sems).start()

            @at(tuple(g - 1 for g in grid))
            def _():
                _ScatterPlan(part_refs, land_refs, sems).finish()

        p = (_dot_tn if ta else _dot_nt if nt else _dot)(a_ref[...], b_ref[...])
        if nk == 1:
            o_ref[...] = (p + c_ref[...] if has_add else p).astype(o_ref.dtype)
            return
        acc = rest[0]
        kk = pl.program_id(2)

        @pl.when(kk == 0)
        def _():
            acc[...] = p

        @pl.when(jnp.logical_and(kk > 0, kk < nk - 1))
        def _():
            acc[...] += p

        @pl.when(kk == nk - 1)
        def _():
            r = acc[...] + p
            if has_add:
                r = r + c_ref[...]
            o_ref[...] = r.astype(o_ref.dtype)

    if ta:
        a_spec = pl.BlockSpec((tk, tm), lambda j, i, kk: (kk, i))
    else:
        a_spec = pl.BlockSpec((tm, tk), lambda j, i, kk: (i, kk))
    if nt:
        b_spec = pl.BlockSpec((tn, tk), lambda j, i, kk: (j, kk))
    else:
        b_spec = pl.BlockSpec((tk, tn), lambda j, i, kk: (kk, j + col0 // tn))
    o_spec = pl.BlockSpec((tm, tn), lambda j, i, kk: (i, j))
    in_specs = [a_spec, b_spec] + ([o_spec] if has_add else []) + [_HBM] * n_sc
    args = (a, b) + ((add,) if has_add else ()) + tuple(scatter)
    res = _pcall(
        body, name=name, grid=grid,
        out_shape=[jax.ShapeDtypeStruct((m, n), out_dtype)] + _ScatterPlan.out_shapes(scatter),
        in_specs=in_specs, out_specs=[o_spec] + [_HBM] * n_sc,
        scratch_shapes=([pltpu.VMEM((tm, tn), F32)] if nk > 1 else []) + (_ScatterPlan.sem_shapes(n_sc) if n_sc else []),
        compiler_params=_params(("arbitrary",) * 3 if n_sc else ("parallel", "parallel", "arbitrary")),
    )(*args)
    return (res[0], res[1:]) if n_sc else res[0]


class Col:
    def __init__(self, arr, w=None, cb=0, lead=None):
        self.arr, self.cb, self.lead = arr, cb, lead
        self.w = arr.shape[-1] if w is None else w
        self.rows = arr.shape[-2]


def _ew(name, fn, *, tr, ins, consts=(), outs=(), accs=(), halo_prev=(), halo_next=()):
    ins = [c if isinstance(c, Col) else Col(c) for c in ins]
    halo_prev = [c if isinstance(c, Col) else Col(c) for c in halo_prev]
    halo_next = [c if isinstance(c, Col) else Col(c) for c in halo_next]
    rows = ins[0].rows
    tr = _rows_tile(rows, tr)
    nt = rows // tr
    n_in, n_hp, n_hn, n_c, n_o, n_a = len(ins), len(halo_prev), len(halo_next), len(consts), len(outs), len(accs)
    groups = tr // SUBLANES

    def spec(col, kind):
        if kind == "cur":
            shape, idx = (tr, col.w), (lambda i, cb=col.cb: (i, cb))
        elif kind == "prev":
            shape, idx = (SUBLANES, col.w), (lambda i, cb=col.cb: (jnp.maximum(i * groups - 1, 0), cb))
        else:
            shape = (SUBLANES, col.w)
            idx = (lambda i, cb=col.cb: (jnp.minimum((i + 1) * groups, rows // SUBLANES - 1), cb))
        if col.lead is None:
            return pl.BlockSpec(shape, idx)
        return pl.BlockSpec((None,) + shape, lambda i, idx=idx, lead=col.lead: (lead,) + idx(i))

    def body(*refs):
        i = pl.program_id(0)
        p = 0
        tiles = [r[...] for r in refs[p:p + n_in]]; p += n_in
        prev8 = [r[...] for r in refs[p:p + n_hp]]; p += n_hp
        next8 = [r[...] for r in refs[p:p + n_hn]]; p += n_hn
        cvals = [r[...] for r in refs[p:p + n_c]]; p += n_c
        out_refs = refs[p:p + n_o]; p += n_o
        acc_refs = refs[p:p + n_a]
        out_v, acc_v = fn(i, nt, tiles, prev8, next8, cvals)
        for r, v in zip(out_refs, out_v):
            r[...] = v.astype(r.dtype)
        if n_a:
            @pl.when(i == 0)
            def _():
                for r, v in zip(acc_refs, acc_v):
                    r[...] = v

            @pl.when(i > 0)
            def _():
                for r, v in zip(acc_refs, acc_v):
                    r[...] += v

    in_specs = ([spec(c, "cur") for c in ins] + [spec(c, "prev") for c in halo_prev]
                + [spec(c, "next") for c in halo_next]
                + [pl.BlockSpec(c.shape, lambda i, nd=c.ndim: (0,) * nd) for c in consts])
    out_specs = ([pl.BlockSpec((tr, w), lambda i: (i, 0)) for w, _ in outs]
                 + [pl.BlockSpec(s, lambda i: (0, 0)) for s in accs])
    out_shape = ([jax.ShapeDtypeStruct((rows, w), dt) for w, dt in outs]
                 + [jax.ShapeDtypeStruct(s, F32) for s in accs])
    args = [c.arr for c in ins] + [c.arr for c in halo_prev] + [c.arr for c in halo_next] + list(consts)
    res = _pcall(body, name=name, grid=(nt,), out_shape=out_shape, in_specs=in_specs, out_specs=out_specs,
                 compiler_params=_params(("arbitrary",)))(*args)
    return res[:n_o], res[n_o:]


def _colsum(v):
    return jnp.sum(v, axis=0, keepdims=True)


def _heads_of(w):
    return w // HEAD_DIM


def _per_head(fn, *arrays):
    nh = _heads_of(arrays[0].shape[1])
    res = [fn(*[a[:, h * HEAD_DIM:(h + 1) * HEAD_DIM] for a in arrays]) for h in range(nh)]
    if isinstance(res[0], tuple):
        return tuple(jnp.concatenate([r[j] for r in res], axis=1) for j in range(len(res[0])))
    return jnp.concatenate(res, axis=1)


def _head_sum(v):
    nh = _heads_of(v.shape[1])
    out = v[:, :HEAD_DIM]
    for h in range(1, nh):
        out = out + v[:, h * HEAD_DIM:(h + 1) * HEAD_DIM]
    return out


def _rms_fwd(x, w):
    r = lax.rsqrt(jnp.mean(x * x, axis=1, keepdims=True) + EPS)
    return x * r * w


def _rms_bwd(x, w, dy):
    r = lax.rsqrt(jnp.mean(x * x, axis=1, keepdims=True) + EPS)
    xh = x * r
    dxh = dy * w
    dx = r * (dxh - xh * jnp.mean(dxh * xh, axis=1, keepdims=True))
    return dx, dy * xh


def _silu(x):
    return x * _sigmoid(x)


def _dsilu(x):
    s = _sigmoid(x)
    return s * (1.0 + x * (1.0 - s))


SB_BQ = 512
SB_CUTOFF = 112.0
SB_PAIR = 2
SB_BK = 256


def _softplus_pos(z):
    return jnp.maximum(z, 0.0) + jnp.log(1.0 + jnp.exp(-jnp.abs(z)))


def _split_dot(v, tri):
    top = lax.bitcast_convert_type(lax.bitcast_convert_type(v, jnp.int32) & jnp.int32(-65536), F32)
    return _dot(top.astype(BF16), tri) + _dot((v - top).astype(BF16), tri)


def _sb_fwd(qn, kn, vb, shards, *, bq=SB_BQ, bk=SB_BK):
    s_len, hd = qn.shape
    nh = hd // HEAD_DIM
    bk = min(bk, s_len)
    bq = min(bq, s_len)
    ndiag = bq // bk
    scale = HEAD_DIM ** -0.5

    n_sh = len(shards)
    last_h, last_i = nh - 1, s_len // bq - 1

    def body(q_ref, k_ref, v_ref, *rest):
        sh_refs, o_ref, ld_ref = rest[:n_sh], rest[n_sh], rest[n_sh + 1]
        got_refs, sems = rest[n_sh + 2:2 * n_sh + 2], rest[2 * n_sh + 2:]
        i = pl.program_id(1)

        if n_sh:
            @_when_step(0, 0, last_h, last_i)
            def _():
                _GatherPlan(sh_refs, got_refs, sems, n_sh).start()

            @_when_step(nh // 2, 0, last_h, last_i)
            def _():
                _GatherPlan(sh_refs, got_refs, sems, n_sh).relay()

            @_when_step(-1, -1, last_h, last_i)
            def _():
                _GatherPlan(sh_refs, got_refs, sems, n_sh).finish()

        krow = lax.broadcasted_iota(jnp.int32, (bk, bk), 0)
        kcol = lax.broadcasted_iota(jnp.int32, (bk, bk), 1)
        later = (krow > kcol).astype(BF16)
        row = lax.broadcasted_iota(jnp.int32, (bq, bk), 0)
        col = lax.broadcasted_iota(jnp.int32, (bq, bk), 1)
        q = q_ref[...]

        def tiles(js, carry, diags):
            run, acc = carry
            ks, vs, zs = [], [], []
            for j in js:
                off = pl.multiple_of(j * bk, bk)
                ks.append(k_ref[pl.ds(off, bk), :])
                vs.append(v_ref[pl.ds(off, bk), :])
                zs.append(_dot_nt(q, ks[-1]) * scale)
            sps, cums, masks = [], [], []
            for z, diag in zip(zs, diags):
                sp = _softplus_pos(z)
                causal = None
                if diag is not None:
                    causal = col + diag * bk < row
                    sp = jnp.where(causal, sp, 0.0)
                sps.append(sp)
                masks.append(causal)
                cums.append(_split_dot(sp, later))
            for z, sp, cum, causal, v in zip(zs, sps, cums, masks, vs):
                w = jnp.exp((z - sp) - (cum + run))
                if causal is not None:
                    w = jnp.where(causal, w, 0.0)
                acc = acc + _dot(w.astype(BF16), v)
                run = run + cum[:, 0:1] + sp[:, 0:1]
            return run, acc

        carry = (jnp.zeros((bq, 1), F32), jnp.zeros((bq, HEAD_DIM), F32))
        for dg in reversed(range(0, ndiag, SB_PAIR)):
            dgs = list(reversed(range(dg, dg + SB_PAIR)))
            carry = tiles([i * ndiag + g for g in dgs], carry, dgs)
        ld_ref[...] = carry[0]
        n_left = i * ndiag

        def more(st):
            return jnp.logical_and(st[0] < n_left, jnp.min(st[1]) < SB_CUTOFF)

        def step(st):
            t, run, acc = st
            run, acc = tiles([n_left - 1 - t], (run, acc), [None])
            return t + 1, run, acc

        _, _, acc = lax.while_loop(more, step, (jnp.int32(0),) + carry)
        o_ref[...] = acc.astype(o_ref.dtype)

    qspec = pl.BlockSpec((bq, HEAD_DIM), lambda h, i: (i, h))
    kspec = pl.BlockSpec((s_len, HEAD_DIM), lambda h, i: (0, h))
    return _pcall(
        body, name="sb_fwd", grid=(nh, s_len // bq),
        out_shape=[jax.ShapeDtypeStruct((s_len, hd), BF16), jax.ShapeDtypeStruct((nh, s_len, 1), F32)]
        + _GatherPlan.out_shapes(shards),
        in_specs=[qspec, kspec, kspec] + [_HBM] * n_sh,
        out_specs=[qspec, pl.BlockSpec((None, bq, 1), lambda h, i: (h, i, 0))] + [_HBM] * n_sh,
        scratch_shapes=_GatherPlan.sem_shapes(n_sh) if n_sh else [],
        compiler_params=_params(("arbitrary", "arbitrary")),
    )(qn, kn, vb, *shards)


def _sb_bwd(qn, kn, vb, do, lt_diag, *, bq=SB_BQ, bk=SB_BK):
    s_len, hd = qn.shape
    nh = hd // HEAD_DIM
    bk = min(bk, s_len)
    bq = min(bq, s_len)
    ndiag = bq // bk
    scale = HEAD_DIM ** -0.5

    def body(q_ref, k_ref, v_ref, do_ref, ld_ref, dq_ref, dk_ref, dv_ref):
        i = pl.program_id(1)

        @pl.when(i == 0)
        def _():
            dk_ref[...] = jnp.zeros_like(dk_ref)
            dv_ref[...] = jnp.zeros_like(dv_ref)

        krow = lax.broadcasted_iota(jnp.int32, (bk, bk), 0)
        kcol = lax.broadcasted_iota(jnp.int32, (bk, bk), 1)
        upto = (krow <= kcol).astype(BF16)
        before = (krow < kcol).astype(BF16)
        row = lax.broadcasted_iota(jnp.int32, (bq, bk), 0)
        col = lax.broadcasted_iota(jnp.int32, (bq, bk), 1)
        q = q_ref[...]
        do_t = do_ref[...]
        ones = jnp.ones((bk, LANES), BF16)
        n_left = i * ndiag

        def row_sums(js, diags):
            zs = [_dot_nt(q, k_ref[pl.ds(pl.multiple_of(j * bk, bk), bk), :]) * scale for j in js]
            tot = None
            for z, diag in zip(zs, diags):
                assert diag is None
                part = _split_dot(_softplus_pos(z), ones)[:, 0:1]
                tot = part if tot is None else tot + part
            return tot

        def more(st):
            return jnp.logical_and(st[0] < n_left, jnp.min(st[1]) < SB_CUTOFF)

        def widen(st):
            t, run = st
            return t + 1, run + row_sums([n_left - 1 - t], [None])

        used, lt = lax.while_loop(more, widen, (jnp.int32(0), ld_ref[...]))

        def tiles(js, carry, diags):
            pre, ecar, dq = carry
            offs, ks, zs, dws = [], [], [], []
            for j in js:
                off = pl.multiple_of(j * bk, bk)
                offs.append(off)
                ks.append(k_ref[pl.ds(off, bk), :])
                zs.append(_dot_nt(q, ks[-1]) * scale)
                dws.append(_dot_nt(do_t, v_ref[pl.ds(off, bk), :]))
            sps, cums, masks = [], [], []
            for z, diag in zip(zs, diags):
                sp = _softplus_pos(z)
                causal = None
                if diag is not None:
                    causal = col + diag * bk < row
                    sp = jnp.where(causal, sp, 0.0)
                sps.append(sp)
                masks.append(causal)
                cums.append(_split_dot(sp, upto))
            es, ebs, exs, sigs = [], [], [], []
            for off, z, sp, cum, dw, causal in zip(offs, zs, sps, cums, dws, masks):
                lb = z - sp
                w = jnp.exp(lb - (lt - (pre + cum)))
                if causal is not None:
                    w = jnp.where(causal, w, 0.0)
                dv_ref[pl.ds(off, bk), :] += _dot_tn(w.astype(BF16), do_t)
                e = dw * w
                eb = e.astype(BF16)
                es.append(e)
                ebs.append(eb)
                exs.append(_dot(eb, before))
                sigs.append(jnp.exp(lb))
                pre = pre + cum[:, bk - 1:bk]
            for off, k, e, eb, exm, sig, causal in zip(offs, ks, es, ebs, exs, sigs, masks):
                ex = exm + ecar
                dz = (e - sig * (e + ex)) * scale
                if causal is not None:
                    dz = jnp.where(causal, dz, 0.0)
                dzb = dz.astype(BF16)
                dk_ref[pl.ds(off, bk), :] += _dot_tn(dzb, q)
                dq = dq + _dot(dzb, k)
                ecar = ex[:, bk - 1:bk] + eb[:, bk - 1:bk].astype(F32)
            return pre, ecar, dq

        init = (jnp.zeros((bq, 1), F32), jnp.zeros((bq, 1), F32), jnp.zeros((bq, HEAD_DIM), F32))
        carry = lax.fori_loop(n_left - used, n_left, lambda j, cr: tiles([j], cr, [None]), init)
        for dg in range(0, ndiag, SB_PAIR):
            dgs = list(range(dg, dg + SB_PAIR))
            carry = tiles([i * ndiag + g for g in dgs], carry, dgs)
        dq_ref[...] = carry[2]

    qspec = pl.BlockSpec((bq, HEAD_DIM), lambda h, i: (i, h))
    kspec = pl.BlockSpec((s_len, HEAD_DIM), lambda h, i: (0, h))
    return _pcall(
        body, name="sb_bwd", grid=(nh, s_len // bq),
        out_shape=[jax.ShapeDtypeStruct((s_len, hd), F32)] * 3,
        in_specs=[qspec, kspec, kspec, qspec, pl.BlockSpec((None, bq, 1), lambda h, i: (h, i, 0))],
        out_specs=[qspec, kspec, kspec],
        compiler_params=_params(("parallel", "arbitrary")),
    )(qn, kn, vb, do, lt_diag)


GDN_GROUP = 16


def _gdn_group(nh):
    return min(GDN_GROUP, nh)


def _gdn_chunk_terms(qh, kh, vh, g_r, g_c, b_c):
    c = GDN_CHUNK
    r = lax.broadcasted_iota(jnp.int32, (c, c), 0)
    s = lax.broadcasted_iota(jnp.int32, (c, c), 1)
    tril, stril = r >= s, r > s
    gcc = jnp.sum(jnp.where(tril, g_r, 0.0), axis=1, keepdims=True)
    gcr = jnp.sum(jnp.where(r <= s, g_c, 0.0), axis=0, keepdims=True)
    dm = jnp.where(tril, jnp.exp(jnp.where(tril, gcc - gcr, 0.0)), 0.0)
    kb = kh.astype(BF16)
    kk = _dot_nt(kb, kb)
    qk = _dot_nt(qh.astype(BF16), kb)
    egc = jnp.exp(gcc)
    gcl = gcc[c - 1:c, :]
    t = dict(tril=tril, stril=stril, gcc=gcc, dm=dm, kb=kb, kk=kk, qk=qk, egc=egc,
             ekd=jnp.exp(gcl - gcc), gl=jnp.exp(gcl),
             a=jnp.where(stril, b_c * kk * dm, 0.0),
             bv=b_c * vh, bk=(b_c * egc) * kh, at=jnp.where(tril, qk * dm, 0.0))
    t["qg"] = qh * egc
    t["kd"] = kh * t["ekd"]
    return t


def _unit_lower_inverses(mats):
    c = GDN_CHUNK
    r = lax.broadcasted_iota(jnp.int32, (c, c), 0)
    s = lax.broadcasted_iota(jnp.int32, (c, c), 1)
    eye = (r == s).astype(F32)
    ps = [-a for a in mats]
    ts = [eye + p for p in ps]
    span = 2
    while span < c:
        ps = [_dot(p, p, hi=HIGH) for p in ps]
        ts = [t + _dot(t, p, hi=HIGH) for t, p in zip(ts, ps)]
        span *= 2
    return ts


def _gdn_fwd(q, k, v, g_col, g_row, b_col, b_row, shards, n_split):
    s_len, d = q.shape
    nh = d // HEAD_DIM
    c = GDN_CHUNK
    n_chunks = s_len // c
    grp = _gdn_group(nh)
    n_sh = len(shards)
    last_h, last_n = nh // grp - 1, n_chunks - 1

    def body(q_ref, k_ref, v_ref, gc_ref, gr_ref, bc_ref, br_ref, *rest):
        sh_refs, rest = rest[:n_sh], rest[n_sh:]
        o_ref, ss_ref, ts_ref = rest[:3]
        got_refs, rest = rest[3:3 + n_sh], rest[3 + n_sh:]
        st, sems = rest[0], rest[1:]
        n = pl.program_id(1)

        @_when_step(0, 0, last_h, last_n)
        def _():
            _GatherPlan(sh_refs, got_refs, sems, n_split).start()

        @_when_step(-1, 3 * n_chunks // 4, last_h, last_n)
        def _():
            _GatherPlan(sh_refs, got_refs, sems, n_split).relay()

        @_when_step(-1, -1, last_h, last_n)
        def _():
            _GatherPlan(sh_refs, got_refs, sems, n_split).finish()

        @pl.when(n == 0)
        def _():
            st[...] = jnp.zeros_like(st)

        heads = range(grp)
        sls = [slice(i * HEAD_DIM, (i + 1) * HEAD_DIM) for i in heads]
        terms = [_gdn_chunk_terms(q_ref[:, sls[i]], k_ref[:, sls[i]], v_ref[:, sls[i]],
                                  gr_ref[i:i + 1, :], gc_ref[:, i:i + 1], bc_ref[:, i:i + 1]) for i in heads]
        tinvs = _unit_lower_inverses([t["a"] for t in terms])
        wvs = [_dot(tinv, t["bv"], hi=HIGH) for tinv, t in zip(tinvs, terms)]
        wks = [_dot(tinv, t["bk"], hi=HIGH) for tinv, t in zip(tinvs, terms)]
        states = [st[i] for i in heads]
        sbs = [state.astype(BF16) for state in states]
        ubs = [(wv - _dot(wk.astype(BF16), sb)).astype(BF16) for wv, wk, sb in zip(wvs, wks, sbs)]
        for i in heads:
            t = terms[i]
            o_ref[:, sls[i]] = _dot(t["qg"].astype(BF16), sbs[i]) + _dot(t["at"].astype(BF16), ubs[i])
            ss_ref[i] = states[i]
            ts_ref[i] = tinvs[i]
            st[i] = t["gl"] * states[i] + _dot_tn(t["kd"].astype(BF16), ubs[i])

    tok = pl.BlockSpec((c, grp * HEAD_DIM), lambda h, n: (n, h))
    colspec = pl.BlockSpec((None, c, grp), lambda h, n: (h, n, 0))
    rowspec = pl.BlockSpec((None, None, grp, c), lambda h, n: (h, n, 0, 0))
    res = _pcall(
        body, name="gdn_fwd", grid=(nh // grp, n_chunks),
        out_shape=[jax.ShapeDtypeStruct((s_len, d), F32),
                   jax.ShapeDtypeStruct((n_chunks, nh, HEAD_DIM, HEAD_DIM), F32),
                   jax.ShapeDtypeStruct((n_chunks, nh, c, c), F32)] + _GatherPlan.out_shapes(shards),
        in_specs=[tok, tok, tok, colspec, rowspec, colspec, rowspec] + [_HBM] * n_sh,
        out_specs=[tok, pl.BlockSpec((None, grp, HEAD_DIM, HEAD_DIM), lambda h, n: (n, h, 0, 0)),
                   pl.BlockSpec((None, grp, c, c), lambda h, n: (n, h, 0, 0))] + [_HBM] * n_sh,
        scratch_shapes=[pltpu.VMEM((grp, HEAD_DIM, HEAD_DIM), F32)] + _GatherPlan.sem_shapes(n_sh),
        compiler_params=_params(("arbitrary", "arbitrary")),
    )(q, k, v, g_col, g_row, b_col, b_row, *shards)
    return res[0], res[1], res[2], res[3:]


def _gdn_bwd(q, k, v, g_col, g_row, b_col, b_row, states, tinvs, do, partials):
    s_len, d = q.shape
    nh = d // HEAD_DIM
    c = GDN_CHUNK
    n_chunks = s_len // c
    grp = _gdn_group(nh)
    n_p = len(partials)
    last_h, last_n = nh // grp - 1, n_chunks - 1

    def body(q_ref, k_ref, v_ref, gc_ref, gr_ref, bc_ref, br_ref, ss_ref, ts_ref, do_ref, *rest):
        part_refs, rest = rest[:n_p], rest[n_p:]
        dq_ref, dk_ref, dv_ref, dgb_ref = rest[:4]
        land_refs, rest = rest[4:4 + n_p], rest[4 + n_p:]
        dst, sems = rest[0], rest[1:]
        n = pl.program_id(1)

        @_when_step(0, 0, last_h, last_n)
        def _():
            _ScatterPlan(part_refs, land_refs, sems).start()

        @_when_step(-1, -1, last_h, last_n)
        def _():
            _ScatterPlan(part_refs, land_refs, sems).finish()

        @pl.when(n == 0)
        def _():
            dst[...] = jnp.zeros_like(dst)

        r = lax.broadcasted_iota(jnp.int32, (c, c), 0)
        s = lax.broadcasted_iota(jnp.int32, (c, c), 1)
        suffix = (r <= s).astype(F32)
        lane = lax.broadcasted_iota(jnp.int32, (c, LANES), 1)
        heads = range(grp)
        sls = [slice(i * HEAD_DIM, (i + 1) * HEAD_DIM) for i in heads]
        qs = [q_ref[:, sl] for sl in sls]
        ks = [k_ref[:, sl] for sl in sls]
        vs = [v_ref[:, sl] for sl in sls]
        bcs = [bc_ref[:, i:i + 1] for i in heads]
        ts = [_gdn_chunk_terms(qs[i], ks[i], vs[i], gr_ref[i:i + 1, :], gc_ref[:, i:i + 1], bcs[i]) for i in heads]
        tinv = [ts_ref[i] for i in heads]
        state = [ss_ref[i] for i in heads]
        sb = [x.astype(BF16) for x in state]
        dnext = [dst[i] for i in heads]
        dnb = [x.astype(BF16) for x in dnext]
        dob = [do_ref[:, sl].astype(BF16) for sl in sls]
        wv = [_dot(tinv[i], ts[i]["bv"], hi=HIGH) for i in heads]
        wk = [_dot(tinv[i], ts[i]["bk"], hi=HIGH) for i in heads]
        wkb = [x.astype(BF16) for x in wk]
        ub = [(wv[i] - _dot(wkb[i], sb[i])).astype(BF16) for i in heads]
        du = [_dot_tn(ts[i]["at"].astype(BF16), dob[i]) + _dot(ts[i]["kd"].astype(BF16), dnb[i]) for i in heads]
        dub = [x.astype(BF16) for x in du]
        dat = [jnp.where(ts[i]["tril"], _dot_nt(dob[i], ub[i]), 0.0) for i in heads]
        dqg = [_dot_nt(dob[i], sb[i]) for i in heads]
        dkd = [_dot_nt(ub[i], dnb[i]) for i in heads]
        dwk = [-_dot_nt(dub[i], sb[i]) for i in heads]
        for i in heads:
            dst[i] = (ts[i]["gl"] * dnext[i] + _dot_tn(ts[i]["qg"].astype(BF16), dob[i]) - _dot_tn(wkb[i], dub[i]))
        dbv = [_dot_tn(tinv[i], du[i], hi=HIGH) for i in heads]
        dbk = [_dot_tn(tinv[i], dwk[i], hi=HIGH) for i in heads]
        dtm = [_dot_nt(du[i], ts[i]["bv"], hi=HIGH) + _dot_nt(dwk[i], ts[i]["bk"], hi=HIGH) for i in heads]
        dtt = [_dot_nt(dtm[i], tinv[i], hi=HIGH) for i in heads]
        da = [-jnp.where(ts[i]["stril"], _dot_tn(tinv[i], dtt[i], hi=HIGH), 0.0) for i in heads]
        rs = lambda m: jnp.sum(m, axis=1, keepdims=True)
        dgb = jnp.zeros((c, LANES), F32)
        for i in heads:
            t, b_c, dm, kb = ts[i], bcs[i], ts[i]["dm"], ts[i]["kb"]
            egc, ekd = t["egc"], t["ekd"]
            dkk = da[i] * b_c * dm
            ddm = da[i] * b_c * t["kk"] + dat[i] * t["qk"]
            dqkb, dkkb = (dat[i] * dm).astype(BF16), dkk.astype(BF16)
            dq_ref[:, sls[i]] = _dot(dqkb, kb) + dqg[i] * egc
            dk_ref[:, sls[i]] = (_dot_tn(dqkb, qs[i].astype(BF16)) + _dot(dkkb, kb) + _dot_tn(dkkb, kb)
                                 + dbk[i] * (b_c * egc) + dkd[i] * ekd)
            dv_ref[:, sls[i]] = dbv[i] * b_c
            dbk_k = rs(dbk[i] * ks[i])
            dbeta = rs(da[i] * t["kk"] * dm) + rs(dbv[i] * vs[i]) + dbk_k * egc
            mx = ddm * dm
            ekd_sum = rs(dkd[i] * ks[i]) * ekd
            dgc = rs(mx) + dbk_k * b_c * egc + rs(dqg[i] * qs[i]) * egc - ekd_sum
            dgl = jnp.sum(rs(dnext[i] * state[i]), axis=0, keepdims=True)
            tail = jnp.sum(ekd_sum, axis=0, keepdims=True) + dgl * t["gl"]
            dg = (_dot(suffix, jnp.broadcast_to(dgc, (c, LANES)), hi=HIGH)[:, 0:1]
                  - rs(_dot_nt(suffix, mx, hi=HIGH)) + tail)
            dgb = dgb + jnp.where(lane == i, dbeta, 0.0) + jnp.where(lane == grp + i, dg, 0.0)
        dgb_ref[...] = dgb

    last = n_chunks - 1
    tok = pl.BlockSpec((c, grp * HEAD_DIM), lambda h, n: (last - n, h))
    colspec = pl.BlockSpec((None, c, grp), lambda h, n: (h, last - n, 0))
    rowspec = pl.BlockSpec((None, None, grp, c), lambda h, n: (h, last - n, 0, 0))
    res = _pcall(
        body, name="gdn_bwd", grid=(nh // grp, n_chunks),
        out_shape=[jax.ShapeDtypeStruct((s_len, d), F32)] * 3
        + [jax.ShapeDtypeStruct((nh // grp, s_len, LANES), F32)] + _ScatterPlan.out_shapes(partials),
        in_specs=[tok, tok, tok, colspec, rowspec, colspec, rowspec,
                  pl.BlockSpec((None, grp, HEAD_DIM, HEAD_DIM), lambda h, n: (last - n, h, 0, 0)),
                  pl.BlockSpec((None, grp, c, c), lambda h, n: (last - n, h, 0, 0)), tok] + [_HBM] * n_p,
        out_specs=[tok, tok, tok, pl.BlockSpec((None, c, LANES), lambda h, n: (h, last - n, 0))] + [_HBM] * n_p,
        scratch_shapes=[pltpu.VMEM((grp, HEAD_DIM, HEAD_DIM), F32)] + _ScatterPlan.sem_shapes(n_p),
        compiler_params=_params(("arbitrary", "arbitrary")),
    )(q, k, v, g_col, g_row, b_col, b_row, states, tinvs, do, *partials)
    return res[0], res[1], res[2], res[3], res[4:]


def _shift_down(prev8, cur, k):
    if k == 0:
        return cur
    ext = jnp.concatenate([prev8, cur], axis=0)
    return pltpu.roll(ext, k, 0)[SUBLANES:, :]


def _shift_up(cur, next8, k):
    if k == 0:
        return cur
    ext = jnp.concatenate([cur, next8], axis=0)
    n = ext.shape[0]
    return pltpu.roll(ext, n - k, 0)[:cur.shape[0], :]


def _conv_pre(i, x, prev8, w):
    prev8 = jnp.where(i == 0, 0.0, prev8)
    pre = None
    for j in range(GDN_CONV):
        term = w[j:j + 1, :] * _shift_down(prev8, x, GDN_CONV - 1 - j)
        pre = term if pre is None else pre + term
    return pre, prev8


def _l2_fwd(a, mult):
    return a * (lax.rsqrt(jnp.sum(a * a, axis=1, keepdims=True) + EPS) * mult)


def _l2_bwd(a, dy, mult):
    r = lax.rsqrt(jnp.sum(a * a, axis=1, keepdims=True) + EPS)
    dy = dy * mult
    return r * dy - a * (r * r * r) * jnp.sum(a * dy, axis=1, keepdims=True)


def _conv_fwd(xb, conv_w, group, *, norm, mult, tr=256):
    d = xb.shape[1] // 3

    def fn(i, nt, tiles, prev8, next8, cv):
        pre, _ = _conv_pre(i, tiles[0], prev8[0], cv[0])
        a = _silu(pre)
        if norm:
            a = _per_head(lambda ah: _l2_fwd(ah, mult), a)
        return [a], []

    col = Col(xb, d, group)
    wg = lax.slice_in_dim(conv_w, group * d, (group + 1) * d, axis=1)
    (y,), _ = _ew(f"conv_fwd{group}", fn, tr=tr, ins=[col], halo_prev=[col], consts=[wg], outs=[(d, F32)])
    return y


def _conv_bwd(xb, conv_w, group, dy, *, norm, mult, tr=256):
    d = xb.shape[1] // 3
    col = Col(xb, d, group)
    wg = lax.slice_in_dim(conv_w, group * d, (group + 1) * d, axis=1)

    def fn_pre(i, nt, tiles, prev8, next8, cv):
        x, dyt = tiles
        pre, p8 = _conv_pre(i, x, prev8[0], cv[0])
        if norm:
            da = _per_head(lambda ah, dh: _l2_bwd(ah, dh, mult), _silu(pre), dyt)
        else:
            da = dyt
        dpre = da * _dsilu(pre)
        tap = lax.broadcasted_iota(jnp.int32, (GDN_CONV, d), 0)
        dw = jnp.zeros((GDN_CONV, d), F32)
        for j in range(GDN_CONV):
            dw = dw + jnp.where(tap == j, _colsum(dpre * _shift_down(p8, x, GDN_CONV - 1 - j)), 0.0)
        return [dpre], [dw]

    (dpre,), (dw,) = _ew(f"conv_bwd_pre{group}", fn_pre, tr=tr, ins=[col, dy], halo_prev=[col], consts=[wg],
                         outs=[(d, F32)], accs=[(GDN_CONV, d)])

    def fn_dx(i, nt, tiles, prev8, next8, cv):
        n8 = jnp.where(i == nt - 1, 0.0, next8[0])
        dx = None
        for j in range(GDN_CONV):
            term = cv[0][j:j + 1, :] * _shift_up(tiles[0], n8, GDN_CONV - 1 - j)
            dx = term if dx is None else dx + term
        return [dx], []

    (dx,), _ = _ew(f"conv_bwd_dx{group}", fn_dx, tr=tr, ins=[dpre], halo_next=[dpre], consts=[wg], outs=[(d, BF16)])
    return dx, dw


def _adamw(name, w, m, v, grads, *, tr=64):
    shape = w.shape
    w2, m2, v2 = [a.reshape(-1, shape[-1]) for a in (w, m, v)]
    n_g = len(grads)
    bc1 = 1.0 - ADAM_B1 ** ADAM_STEP
    bc2 = 1.0 - ADAM_B2 ** ADAM_STEP

    def fn(i, nt, tiles, prev8, next8, cv):
        wt, mt, vt, g = tiles
        mn = ADAM_B1 * mt + (1.0 - ADAM_B1) * g
        vn = ADAM_B2 * vt + (1.0 - ADAM_B2) * (g * g)
        delta = -ADAM_LR * ((mn / bc1) / (jnp.sqrt(vn / bc2) + ADAM_EPS) + ADAM_WD * wt)
        return [delta, mn, vn], []

    width = shape[-1]
    assert n_g == 1
    outs, _ = _ew(name, fn, tr=tr, ins=[w2, m2, v2] + list(grads), outs=[(width, F32)] * 3)
    return (grads[0].reshape(shape),) + tuple(o.reshape(shape) for o in outs)


def _pad_cols(a, width):
    return jnp.pad(a, ((0, 0), (0, width - a.shape[1])))


def _gdn_layouts(gbeta, nh, n_chunks):
    grp = _gdn_group(nh)
    s_len = gbeta.shape[0]

    def lay(a):
        col = a.reshape(s_len, nh // grp, grp).transpose(1, 0, 2)
        row = a.reshape(n_chunks, GDN_CHUNK, nh // grp, grp).transpose(2, 0, 3, 1)
        return col, row

    b_col, b_row = lay(gbeta[:, :nh])
    g_col, g_row = lay(gbeta[:, nh:2 * nh])
    return g_col, g_row, b_col, b_row


def kernel(x, c, w_mod, b_mod, norm1_w, w_in, q_norm_w, k_norm_w, conv_w, a_log, dt_bias, o_norm_w, p_a, p_b, w_out, norm2_w, w_gate, w_up, w_down, loss_target, m_w_mod, m_b_mod, m_norm1_w, m_w_in, m_q_norm_w, m_k_norm_w, m_conv_w, m_a_log, m_dt_bias, m_o_norm_w, m_p_a, m_p_b, m_w_out, m_norm2_w, m_w_gate, m_w_up, m_w_down, v_w_mod, v_b_mod, v_norm1_w, v_w_in, v_q_norm_w, v_k_norm_w, v_conv_w, v_a_log, v_dt_bias, v_o_norm_w, v_p_a, v_p_b, v_w_out, v_norm2_w, v_w_gate, v_w_up, v_w_down):
    s_len, d = x.shape[1], x.shape[2]
    nh = d // HEAD_DIM
    n_chunks = s_len // GDN_CHUNK
    ff = 4 * w_gate.shape[2]
    mx, my, mc = _my_pos()
    chip = 2 * mx + my
    dev = 2 * chip + mc
    x2 = x[0]
    tgt = loss_target[0]

    c_all = _allgather8("ag_c", _pad_cols(c, d).reshape(SUBLANES, d // SUBLANES)).reshape(8, d)
    wm = w_mod[0]
    mod_w = wm.shape[1]
    bm_cols = lax.dynamic_slice_in_dim(b_mod, chip * mod_w, mod_w, axis=1)

    def mod_body(c_ref, w_ref, b_ref, o_ref, ca_ref):
        ca = _silu(c_ref[...])
        ca_ref[...] = ca
        o_ref[...] = _dot(ca, w_ref[...], hi=HIGHEST) + b_ref[...]

    tn_mod = _pick(mod_w, 512)
    mod8, c_act = _pcall(
        mod_body, name="mod_fwd", grid=(mod_w // tn_mod,),
        out_shape=[jax.ShapeDtypeStruct((8, mod_w), F32), jax.ShapeDtypeStruct((8, d), F32)],
        in_specs=[pl.BlockSpec((8, d), lambda j: (0, 0)), pl.BlockSpec((d, tn_mod), lambda j: (0, j)),
                  pl.BlockSpec((1, tn_mod), lambda j: (0, j))],
        out_specs=[pl.BlockSpec((8, tn_mod), lambda j: (0, j)), pl.BlockSpec((8, d), lambda j: (0, 0))],
        compiler_params=_params(("arbitrary",)),
    )(c_all, wm, bm_cols)
    mod_all = _allgather8("ag_mod", mod8)
    mod_me = mod_all.reshape(4, 2, 8, mod_w)[:, mc, dev, :].reshape(1, 6 * d)
    shift1, scale1, gate1, shift2, scale2, gate2 = [mod_me[:, j * d:(j + 1) * d] for j in range(6)]

    first = [w_in[0].astype(BF16), conv_w[0]]
    late_sb = [p_a[0].astype(BF16), p_b[0].astype(BF16), w_out[0].astype(BF16)]
    late_gdn = [w_gate[0].astype(BF16), w_up[0].astype(BF16), w_down[0].astype(BF16)]
    w_in_g, conv_g = [_fill_slot(g, sh, chip) for g, sh in zip(_gather4("ag_w_in", first, n_split=1), first)]
    w_in_f = w_in_g.transpose(1, 0, 2).reshape(d, -1)
    w_all = jnp.concatenate([w_in_f[:, :7 * d], w_in_f[:, 7 * d + 2 * nh:],
                             _pad_cols(w_in_f[:, 7 * d:7 * d + 2 * nh], LANES)], axis=1)
    conv_f = conv_g.transpose(1, 0, 2).reshape(GDN_CONV, 3 * d)

    def norm_mod_fn(i, nt, tiles, prev8, next8, cv):
        w, sc, sh = cv
        return [_rms_fwd(tiles[0], w) * (1.0 + sc) + sh], []

    (u1,), _ = _ew("norm_mod1", norm_mod_fn, tr=512, ins=[x2], consts=[norm1_w, scale1, shift1], outs=[(d, BF16)])
    proj_a = _mm("proj_a", u1, w_all, b_cols=(0, 3 * d), out_dtype=BF16)
    proj_b = _mm("proj_b", u1, w_all, b_cols=(3 * d, 3 * d))
    proj_zg = _mm("proj_zg", u1, w_all, b_cols=(6 * d, 3 * d), out_dtype=BF16)
    proj_ba = _mm("proj_ba", u1, w_all, b_cols=(9 * d, LANES))

    def qknorm_fn(i, nt, tiles, prev8, next8, cv):
        qa, ka, va = [t.astype(F32) for t in tiles]
        return [_per_head(lambda h: _rms_fwd(h, cv[0]), qa), _per_head(lambda h: _rms_fwd(h, cv[1]), ka), va], []

    (qn, kn, vb), _ = _ew("qknorm", qknorm_fn, tr=256,
                          ins=[Col(proj_a, d, 0), Col(proj_a, d, 1), Col(proj_a, d, 2)],
                          consts=[q_norm_w, k_norm_w], outs=[(d, BF16)] * 3)
    sb_out = _sb_fwd(qn, kn, vb, late_sb)
    o_a, lt_diag = sb_out[0], sb_out[1]
    p_a_f, p_b_f, w_out_f = [_fill_slot(g, sh, chip).reshape(d, d) for g, sh in zip(sb_out[2:], late_sb)]

    lane_ids = jnp.arange(LANES)
    is_b = (lane_ids < nh)[None, :]
    is_a = ((lane_ids >= nh) & (lane_ids < 2 * nh))[None, :]
    alog128 = jnp.zeros((1, LANES), F32).at[:, nh:2 * nh].set(a_log)
    dtb128 = jnp.zeros((1, LANES), F32).at[:, nh:2 * nh].set(dt_bias)
    is_b_f, is_a_f = is_b.astype(F32), is_a.astype(F32)

    def gbeta_fn(i, nt, tiles, prev8, next8, cv):
        al, dtb, mb, ma = cv
        ba = tiles[0]
        g = -jnp.exp(al) * _softplus(ba + dtb)
        return [jnp.where(mb > 0.5, _sigmoid(ba), jnp.where(ma > 0.5, g, 0.0))], []

    (gbeta,), _ = _ew("gbeta", gbeta_fn, tr=1024, ins=[proj_ba], consts=[alog128, dtb128, is_b_f, is_a_f],
                      outs=[(LANES, F32)])
    g_col, g_row, b_col, b_row = _gdn_layouts(gbeta, nh, n_chunks)
    qscale = HEAD_DIM ** -0.5
    q_b = _conv_fwd(proj_b, conv_f, 0, norm=True, mult=qscale)
    k_b = _conv_fwd(proj_b, conv_f, 1, norm=True, mult=1.0)
    v_b = _conv_fwd(proj_b, conv_f, 2, norm=False, mult=1.0)
    o_raw, states, tinvs, late_g = _gdn_fwd(q_b, k_b, v_b, g_col, g_row, b_col, b_row, late_gdn,
                                            n_split=len(late_gdn))
    late_g = [_fill_slot(g, sh, chip) for g, sh in zip(late_g, late_gdn)]
    w_gate_f, w_up_f = [g.transpose(1, 0, 2).reshape(d, ff) for g in late_g[0:2]]
    w_down_f = late_g[2].reshape(ff, d)

    def gated_norm_fn(i, nt, tiles, prev8, next8, cv):
        o, z = tiles[0], tiles[1].astype(F32)
        return [_per_head(lambda h: _rms_fwd(h, cv[0]), o) * _silu(z)], []

    (o_b,), _ = _ew("gated_norm", gated_norm_fn, tr=256, ins=[o_raw, Col(proj_zg, d, 0)], consts=[o_norm_w],
                    outs=[(d, BF16)])
    y_a = _mm("out_a", o_a, p_a_f)
    y_b = _mm("out_b", o_b, p_b_f)

    def merge_fn(i, nt, tiles, prev8, next8, cv):
        ya, yb, ga, gb = [t.astype(F32) for t in tiles]
        return [_sigmoid(ga) * ya + _sigmoid(gb) * yb], []

    (merged,), _ = _ew("merge", merge_fn, tr=256, ins=[y_a, y_b, Col(proj_zg, d, 1), Col(proj_zg, d, 2)],
                       outs=[(d, BF16)])
    y_o = _mm("out_proj", merged, w_out_f)

    def resid_norm_fn(i, nt, tiles, prev8, next8, cv):
        xt, yo = tiles
        g1, w, sc, sh = cv
        h1 = xt + g1 * yo
        return [h1, _rms_fwd(h1, w) * (1.0 + sc) + sh], []

    (h1, u2), _ = _ew("resid_norm2", resid_norm_fn, tr=256, ins=[x2, y_o],
                      consts=[gate1, norm2_w, scale2, shift2], outs=[(d, F32), (d, BF16)])
    w_gu = jnp.concatenate([w_gate_f, w_up_f], axis=1)
    gu = _mm("ff_gate_up", u2, w_gu, out_dtype=BF16)
    gt, up = Col(gu, ff, 0), Col(gu, ff, 1)

    def swiglu_fn(i, nt, tiles, prev8, next8, cv):
        return [_silu(tiles[0].astype(F32)) * tiles[1].astype(F32)], []

    (act,), _ = _ew("swiglu", swiglu_fn, tr=128, ins=[gt, up], outs=[(ff, BF16)])
    y_d = _mm("ff_down", act, w_down_f)

    def loss_fn(i, nt, tiles, prev8, next8, cv):
        h1t, yd, tg = tiles
        diff = h1t + cv[0] * yd - tg
        dy = diff * (1.0 / d)
        return [dy, dy * cv[0]], [_colsum(0.5 * diff * dy), _colsum(dy * yd)]

    (dy, dyd), (loss_cols, dgate2) = _ew("loss", loss_fn, tr=256, ins=[h1, y_d, tgt], consts=[gate2],
                                         outs=[(d, F32), (d, BF16)], accs=[(1, d), (1, d)])
    loss = lax.psum(jnp.sum(loss_cols), ("x", "y", "c"))

    dact = _mm("d_act", dyd, w_down_f, nt=True, out_dtype=BF16)
    g_w_down = _mm("g_w_down", act, dyd, ta=True)

    def swiglu_bwd_fn(i, nt, tiles, prev8, next8, cv):
        da, g, u = [t.astype(F32) for t in tiles]
        return [jnp.concatenate([da * u * _dsilu(g), da * _silu(g)], axis=1)], []

    (d_gu,), _ = _ew("swiglu_bwd", swiglu_bwd_fn, tr=128, ins=[dact, gt, up], outs=[(2 * ff, BF16)])
    du2 = _mm("d_u2", d_gu, w_gu, nt=True)
    g_w_gate = _mm("g_w_gate", u2, d_gu, ta=True, b_cols=(0, ff))
    g_w_up = _mm("g_w_up", u2, d_gu, ta=True, b_cols=(ff, ff))

    def norm2_bwd_fn(i, nt, tiles, prev8, next8, cv):
        h1t, du, dres, yo = tiles
        w, sc, g1 = cv
        r = lax.rsqrt(jnp.mean(h1t * h1t, axis=1, keepdims=True) + EPS)
        nrm = h1t * r
        dn = du * w * (1.0 + sc)
        dh = r * (dn - nrm * jnp.mean(dn * nrm, axis=1, keepdims=True)) + dres
        return [dh, dh * g1], [_colsum(du), _colsum(du * nrm * w), _colsum(du * nrm * (1.0 + sc)), _colsum(dh * yo)]

    (dh1, dyo), (dshift2, dscale2, g_norm2, dgate1) = _ew(
        "norm2_bwd", norm2_bwd_fn, tr=256, ins=[h1, du2, dy, y_o], consts=[norm2_w, scale2, gate1],
        outs=[(d, F32), (d, BF16)], accs=[(1, d)] * 4)

    dmerged = _mm("d_merged", dyo, w_out_f, nt=True)
    g_w_out = _mm("g_w_out", merged, dyo, ta=True)

    def merge_bwd_fn(i, nt, tiles, prev8, next8, cv):
        dm, ya, yb, ga, gb = [t.astype(F32) for t in tiles]
        sa, sb = _sigmoid(ga), _sigmoid(gb)
        return [dm * sa, dm * sb, dm * ya * sa * (1.0 - sa), dm * yb * sb * (1.0 - sb)], []

    (dya, dyb, dga, dgb_gate), _ = _ew(
        "merge_bwd", merge_bwd_fn, tr=256, ins=[dmerged, y_a, y_b, Col(proj_zg, d, 1), Col(proj_zg, d, 2)],
        outs=[(d, BF16)] * 4)
    do_a = _mm("d_o_a", dya, p_a_f, nt=True, out_dtype=BF16)
    g_p_a = _mm("g_p_a", o_a, dya, ta=True)
    do_b = _mm("d_o_b", dyb, p_b_f, nt=True)
    g_p_b = _mm("g_p_b", o_b, dyb, ta=True)

    def gated_norm_bwd_fn(i, nt, tiles, prev8, next8, cv):
        dob, o, z = tiles[0], tiles[1], tiles[2].astype(F32)
        sz = _silu(z)

        def head(oh, dh):
            return _rms_bwd(oh, cv[0], dh)

        dxo, dwn = _per_head(head, o, dob * sz)
        nrm_w = _per_head(lambda h: _rms_fwd(h, cv[0]), o)
        return [dxo, dob * nrm_w * _dsilu(z)], [_colsum(_head_sum(dwn))]

    (do_raw, dz_b), (g_o_norm,) = _ew(
        "gated_norm_bwd", gated_norm_bwd_fn, tr=256, ins=[do_b, o_raw, Col(proj_zg, d, 0)], consts=[o_norm_w],
        outs=[(d, F32), (d, BF16)], accs=[(1, HEAD_DIM)])
    by_chip = lambda a: a.reshape(a.shape[0], 4, -1).transpose(1, 0, 2)

    def chip_sums(tag, raw, axes):
        theirs = _sibling_send(f"swap_{tag}", raw, axes)
        sums = []
        for t, (part, ax, other) in enumerate(zip(raw, axes, theirs)):
            def pair_fn(i, nt, tiles, prev8, next8, cv):
                return [tiles[0] + tiles[1]], []

            hr, width = part.shape[ax] // 2, part.shape[-1]
            mine = lax.dynamic_slice_in_dim(part, mc * hr, hr, axis=ax)
            (ch,), _ = _ew(f"pair_sum_{tag}{t}", pair_fn, tr=64,
                           ins=[mine.reshape(-1, width), other.reshape(-1, width)], outs=[(width, BF16)])
            sums.append(ch.reshape(other.shape))
        return sums

    s_gate, s_up, s_pa, s_pb, s_out, s_down = chip_sums(
        "late", [g_w_gate, g_w_up, g_p_a.reshape(4, d // 4, d), g_p_b.reshape(4, d // 4, d),
                 g_w_out.reshape(4, d // 4, d), g_w_down.reshape(4, ff // 4, d)], [0, 0, 1, 1, 1, 1])
    late_halves = [s_pa, s_pb, s_out, by_chip(s_gate), by_chip(s_up), s_down]
    dq_b, dk_b, dv_b, dgb_grp, late_landed = _gdn_bwd(q_b, k_b, v_b, g_col, g_row, b_col, b_row, states, tinvs,
                                                      do_raw, late_halves)
    grp = _gdn_group(nh)
    dbeta = dgb_grp[:, :, :grp].transpose(1, 0, 2).reshape(s_len, nh)
    dg = dgb_grp[:, :, grp:2 * grp].transpose(1, 0, 2).reshape(s_len, nh)
    dgbeta = _pad_cols(jnp.concatenate([dbeta, dg], axis=1), LANES)

    def gbeta_bwd_fn(i, nt, tiles, prev8, next8, cv):
        al, dtb, mb, ma = cv
        ba, dgb = tiles
        beta = _sigmoid(ba)
        arg = ba + dtb
        da = dgb * (-jnp.exp(al)) * _sigmoid(arg)
        g = -jnp.exp(al) * _softplus(arg)
        dba = jnp.where(mb > 0.5, dgb * beta * (1.0 - beta), jnp.where(ma > 0.5, da, 0.0))
        return [dba], [_colsum(jnp.where(ma > 0.5, dgb * g, 0.0)), _colsum(jnp.where(ma > 0.5, da, 0.0))]

    (dba,), (g_alog128, g_dtb128) = _ew(
        "gbeta_bwd", gbeta_bwd_fn, tr=1024, ins=[proj_ba, dgbeta], consts=[alog128, dtb128, is_b_f, is_a_f],
        outs=[(LANES, BF16)], accs=[(1, LANES)] * 2)
    dxq, g_conv_q = _conv_bwd(proj_b, conv_f, 0, dq_b, norm=True, mult=qscale)
    dxk, g_conv_k = _conv_bwd(proj_b, conv_f, 1, dk_b, norm=True, mult=1.0)
    dxv, g_conv_v = _conv_bwd(proj_b, conv_f, 2, dv_b, norm=False, mult=1.0)
    g_conv = jnp.concatenate([g_conv_q, g_conv_k, g_conv_v], axis=1)

    dqn, dkn, dvb = _sb_bwd(qn, kn, vb, do_a, lt_diag)

    def qknorm_bwd_fn(i, nt, tiles, prev8, next8, cv):
        qa, ka, dq, dk, dv = [t.astype(F32) for t in tiles]
        dxq_, dwq = _per_head(lambda h, g: _rms_bwd(h, cv[0], g), qa, dq)
        dxk_, dwk = _per_head(lambda h, g: _rms_bwd(h, cv[1], g), ka, dk)
        return [dxq_, dxk_, dv], [_colsum(_head_sum(dwq)), _colsum(_head_sum(dwk))]

    (dqa, dka, dva), (g_q_norm, g_k_norm) = _ew(
        "qknorm_bwd", qknorm_bwd_fn, tr=256, ins=[Col(proj_a, d, 0), Col(proj_a, d, 1), dqn, dkn, dvb],
        consts=[q_norm_w, k_norm_w], outs=[(d, BF16)] * 3, accs=[(1, HEAD_DIM)] * 2)

    d_all = jnp.concatenate([dqa, dka, dva, dxq, dxk, dxv, dz_b, dga, dgb_gate, dba], axis=1)
    g_wa = _mm("g_w_in_a", u1, d_all, ta=True, b_cols=(0, 3 * d))
    g_wb = _mm("g_w_in_b", u1, d_all, ta=True, b_cols=(3 * d, 3 * d))
    g_wzg = _mm("g_w_in_zg", u1, d_all, ta=True, b_cols=(6 * d, 3 * d))
    g_wba = _mm("g_w_in_ba", u1, d_all, ta=True, b_cols=(9 * d, LANES))
    s_wa, s_wb, s_wzg, s_wba = chip_sums("w_in", [g_wa, g_wb, g_wzg, g_wba], [0, 0, 0, 0])
    s_w_in = jnp.concatenate([s_wa, s_wb, s_wzg[:, :d], s_wba[:, :2 * nh], s_wzg[:, d:]], axis=1)
    w_in_halves = [by_chip(s_w_in)]
    du1, w_in_landed = _mm("d_u1", d_all, w_all, nt=True, tm=512, scatter=w_in_halves)


    def norm1_bwd_fn(i, nt, tiles, prev8, next8, cv):
        xt, du, dres = tiles
        w, sc = cv
        r = lax.rsqrt(jnp.mean(xt * xt, axis=1, keepdims=True) + EPS)
        nrm = xt * r
        dn = du * w * (1.0 + sc)
        dxt = r * (dn - nrm * jnp.mean(dn * nrm, axis=1, keepdims=True)) + dres
        return [dxt], [_colsum(du), _colsum(du * nrm * w), _colsum(du * nrm * (1.0 + sc))]

    (grad_x,), (dshift1, dscale1, g_norm1) = _ew(
        "norm1_bwd", norm1_bwd_fn, tr=256, ins=[x2, du1, dh1], consts=[norm1_w, scale1],
        outs=[(d, F32)], accs=[(1, d)] * 3)

    dmod_me = jnp.concatenate([dshift1, dscale1, dgate1, dshift2, dscale2, dgate2], axis=1)
    small = jnp.concatenate(
        [dmod_me, g_norm1, g_norm2, g_q_norm, g_k_norm, g_o_norm, g_alog128[:, nh:2 * nh], g_dtb128[:, nh:2 * nh],
         g_conv.reshape(1, -1)], axis=1)
    n_small = small.shape[1]
    pad_to = -(-n_small // (SUBLANES * LANES)) * (SUBLANES * LANES)
    small_all = _allgather8("ag_small", _pad_cols(small, pad_to).reshape(SUBLANES, pad_to // SUBLANES))
    small_all = small_all.reshape(8, pad_to)

    def sum8_fn(i, nt, tiles, prev8, next8, cv):
        return [], [_colsum(tiles[0])]

    _, (small_sum,) = _ew("sum_small", sum8_fn, tr=8, ins=[small_all], accs=[(1, pad_to)])
    offs = [0]
    for width in (6 * d, d, d, HEAD_DIM, HEAD_DIM, HEAD_DIM, nh, nh, GDN_CONV * 3 * d):
        offs.append(offs[-1] + width)
    pieces = [small_sum[:, offs[j]:offs[j + 1]] for j in range(9)]
    (gs_b_mod, gs_norm1, gs_norm2, gs_q_norm, gs_k_norm, gs_o_norm, gs_a_log, gs_dt_bias, gs_conv) = pieces
    conv_cols = 3 * d // 4
    gs_conv_mine = lax.dynamic_slice_in_dim(gs_conv.reshape(GDN_CONV, 3 * d), chip * conv_cols, conv_cols, axis=1)

    dmod_all = lax.dynamic_slice_in_dim(small_all[:, :6 * d], chip * mod_w, mod_w, axis=1)

    def wmod_grad_body(ct_ref, dm_ref, o_ref):
        o_ref[...] = _dot(ct_ref[...], dm_ref[...], hi=HIGHEST)

    g_w_mod = _pcall(
        wmod_grad_body, name="g_w_mod", grid=(mod_w // tn_mod,),
        out_shape=jax.ShapeDtypeStruct((d, mod_w), F32),
        in_specs=[pl.BlockSpec((d, 8), lambda j: (0, 0)), pl.BlockSpec((8, tn_mod), lambda j: (0, j))],
        out_specs=pl.BlockSpec((d, tn_mod), lambda j: (0, j)),
        compiler_params=_params(("arbitrary",)),
    )(c_act.T, dmod_all)

    chip_halves = w_in_halves + late_halves
    landed = list(w_in_landed) + list(late_landed)
    landed = [_fill_slot(land, ch, chip) for land, ch in zip(landed, chip_halves)]
    g_mine = []
    for t, land in enumerate(landed):
        def sum4_fn(i, nt, tiles, prev8, next8, cv):
            f = [tl.astype(F32) for tl in tiles]
            return [(f[0] + f[1]) + (f[2] + f[3])], []

        (gh,), _ = _ew(f"chip_sum{t}", sum4_fn, tr=64, ins=[Col(land, lead=s) for s in range(4)],
                       outs=[(land.shape[-1], F32)])
        g_mine.append(gh)
    g_theirs = _sibling_send("join_grads", g_mine)
    g_full = [jnp.concatenate([jnp.where(mc == 0, a, b), jnp.where(mc == 0, b, a)], axis=0)
              for a, b in zip(g_mine, g_theirs)]

    big = {}
    names = ["w_in", "p_a", "p_b", "w_out", "w_gate", "w_up", "w_down"]
    big_w = [w_in, p_a, p_b, w_out, w_gate, w_up, w_down]
    big_m = [m_w_in, m_p_a, m_p_b, m_w_out, m_w_gate, m_w_up, m_w_down]
    big_v = [v_w_in, v_p_a, v_p_b, v_w_out, v_w_gate, v_w_up, v_w_down]
    for t, nm in enumerate(names):
        big[nm] = _adamw(f"adamw_{nm}", big_w[t], big_m[t], big_v[t], [g_full[t]])
    big["w_mod"] = _adamw("adamw_w_mod", w_mod, m_w_mod, v_w_mod, [g_w_mod])
    big["conv_w"] = _adamw("adamw_conv_w", conv_w, m_conv_w, v_conv_w, [gs_conv_mine], tr=8)
    small_names = ["b_mod", "norm1_w", "norm2_w", "q_norm_w", "k_norm_w", "o_norm_w", "a_log", "dt_bias"]
    small_w = [b_mod, norm1_w, norm2_w, q_norm_w, k_norm_w, o_norm_w, a_log, dt_bias]
    small_m = [m_b_mod, m_norm1_w, m_norm2_w, m_q_norm_w, m_k_norm_w, m_o_norm_w, m_a_log, m_dt_bias]
    small_v = [v_b_mod, v_norm1_w, v_norm2_w, v_q_norm_w, v_k_norm_w, v_o_norm_w, v_a_log, v_dt_bias]
    small_g = [gs_b_mod, gs_norm1, gs_norm2, gs_q_norm, gs_k_norm, gs_o_norm, gs_a_log, gs_dt_bias]
    rep_w = jnp.concatenate(small_w, axis=1)
    rep_m = jnp.concatenate(small_m, axis=1)
    rep_v = jnp.concatenate(small_v, axis=1)
    rep_g = jnp.concatenate(small_g, axis=1)
    rep = _adamw("adamw_small", rep_w, rep_m, rep_v, [rep_g], tr=1)
    roffs = [0]
    for a in small_w:
        roffs.append(roffs[-1] + a.shape[1])
    for j, nm in enumerate(small_names):
        big[nm] = tuple(r[:, roffs[j]:roffs[j + 1]] for r in rep)

    order = ["w_mod", "b_mod", "norm1_w", "w_in", "q_norm_w", "k_norm_w", "conv_w", "a_log", "dt_bias", "o_norm_w",
             "p_a", "p_b", "w_out", "norm2_w", "w_gate", "w_up", "w_down"]
    grads = [big[nm][0] for nm in order]
    deltas = [big[nm][1] for nm in order]
    new_m = [big[nm][2] for nm in order]
    new_v = [big[nm][3] for nm in order]
    return (loss, grad_x[None], *grads, *deltas, *new_m, *new_v)
```

```python
import jax
import jax.numpy as jnp
from jax import lax
from jax.experimental import pallas as pl
from jax.experimental.pallas import tpu as pltpu

F32 = jnp.float32
BF16 = jnp.bfloat16
HIGHEST = lax.Precision.HIGHEST
HIGH = lax.Precision.HIGH
MESH = pl.DeviceIdType.MESH

HEAD_DIM = 128
GDN_CHUNK = 64
GDN_CONV = 4
EPS = 1e-6
LANES = 128
SUBLANES = 8
VMEM_LIMIT = 56 * 1024 * 1024
MM_VMEM_BUDGET = 40 * 1024 * 1024

ADAM_LR = 0.001
ADAM_B1 = 0.9
ADAM_B2 = 0.999
ADAM_EPS = 1e-08
ADAM_WD = 0.01
ADAM_STEP = 10


def _pcall(body, **kw):
    return pl.pallas_call(body, **kw)


def _params(sem=None):
    if sem is None:
        return pltpu.CompilerParams(vmem_limit_bytes=VMEM_LIMIT)
    return pltpu.CompilerParams(dimension_semantics=sem, vmem_limit_bytes=VMEM_LIMIT)


def _pick(dim, target):
    if dim <= target:
        return dim
    best = None
    for t in range(LANES, target + 1, LANES):
        if dim % t == 0:
            best = t
    assert best is not None, (dim, target)
    return best


def _rows_tile(rows, target):
    t = min(rows, target)
    while rows % t:
        t //= 2
    assert t >= SUBLANES or t == rows, (rows, target)
    return t


def _dot(a, b, hi=None):
    return jnp.dot(a, b, preferred_element_type=F32, precision=hi)


def _dot_nt(a, b, hi=None):
    return lax.dot_general(a, b, (((1,), (1,)), ((), ())), preferred_element_type=F32, precision=hi)


def _dot_tn(a, b, hi=None):
    return lax.dot_general(a, b, (((0,), (0,)), ((), ())), preferred_element_type=F32, precision=hi)


def _sigmoid(x):
    return 1.0 / (1.0 + jnp.exp(-x))


def _softplus(x):
    return jnp.maximum(x, 0.0) + jnp.log(1.0 + jnp.exp(-jnp.abs(x)))


_HBM = pl.BlockSpec(memory_space=pltpu.HBM)


def _my_pos():
    return lax.axis_index("x"), lax.axis_index("y"), lax.axis_index("c")


def _allgather8(name, v):
    def body(v_ref, o_ref, ssem, rsem, lsem):
        x, y, c = _my_pos()
        me = 4 * x + 2 * y + c
        loc = pltpu.make_async_copy(v_ref, o_ref.at[me], lsem)
        loc.start()
        sends, recvs = [], []
        for k in range(1, 8):
            px, py, pc = (x + (k >> 2)) % 2, (y + ((k >> 1) & 1)) % 2, (c + (k & 1)) % 2
            cp = pltpu.make_async_remote_copy(
                src_ref=v_ref, dst_ref=o_ref.at[me], send_sem=ssem.at[k - 1], recv_sem=rsem.at[k - 1],
                device_id=(px, py, pc), device_id_type=MESH)
            cp.start()
            sends.append(cp)
            recvs.append(pltpu.make_async_remote_copy(
                src_ref=v_ref, dst_ref=o_ref.at[4 * px + 2 * py + pc], send_sem=ssem.at[k - 1],
                recv_sem=rsem.at[k - 1], device_id=(px, py, pc), device_id_type=MESH))
        for rc in recvs:
            rc.wait_recv()
        for cp in sends:
            cp.wait_send()
        loc.wait()

    return _pcall(
        body, name=name, out_shape=jax.ShapeDtypeStruct((8,) + v.shape, v.dtype),
        in_specs=[_HBM], out_specs=_HBM,
        scratch_shapes=[pltpu.SemaphoreType.DMA((7,)), pltpu.SemaphoreType.DMA((7,)), pltpu.SemaphoreType.DMA],
    )(v)


def _plane_peers(x, y):
    return [((x + (k >> 1)) % 2, (y + (k & 1)) % 2) for k in range(1, 4)]


class _GatherPlan:
    def __init__(self, ins, outs, sems, n_split):
        ssem, rsem, fsem, gsem = sems
        x, y, c = _my_pos()
        me = 2 * x + y
        copy = lambda src, dst, s_sem, r_sem, dev: (lambda: pltpu.make_async_remote_copy(
            src_ref=src, dst_ref=dst, send_sem=s_sem, recv_sem=r_sem, device_id=dev, device_id_type=MESH))
        self.sends, self.recvs, self.fwds, self.fwd_recvs = [], [], [], []
        for t in range(len(ins)):
            split = t < n_split
            hr = ins[t].shape[0] // 2
            for k, (px, py) in enumerate(_plane_peers(x, y)):
                peer = 2 * px + py
                sem = 3 * t + k
                if split:
                    mine = pl.ds(pl.multiple_of(c * hr, 16), hr)
                    other = pl.ds(pl.multiple_of((1 - c) * hr, 16), hr)
                    src, dst, got = ins[t].at[mine], outs[t].at[me, mine], outs[t].at[peer, mine]
                else:
                    src, dst, got = ins[t], outs[t].at[me], outs[t].at[peer]
                self.sends.append(copy(src, dst, ssem.at[sem], rsem.at[sem], (px, py, c)))
                self.recvs.append(copy(src, got, ssem.at[sem], rsem.at[sem], (px, py, c)))
                if split:
                    self.fwds.append(copy(got, got, fsem.at[sem], gsem.at[sem], (x, y, 1 - c)))
                    self.fwd_recvs.append(copy(got, outs[t].at[peer, other], fsem.at[sem], gsem.at[sem], (x, y, 1 - c)))
                else:
                    self.fwds.append(None)

    def start(self):
        for cp in self.sends:
            cp().start()

    def relay(self):
        for rc, fw in zip(self.recvs, self.fwds):
            rc().wait_recv()
            if fw is not None:
                fw().start()

    def finish(self):
        for fr in self.fwd_recvs:
            fr().wait_recv()
        for cp in self.sends + [fw for fw in self.fwds if fw is not None]:
            cp().wait_send()

    @staticmethod
    def out_shapes(shards):
        return [jax.ShapeDtypeStruct((4,) + s.shape, s.dtype) for s in shards]

    @staticmethod
    def sem_shapes(n):
        return [pltpu.SemaphoreType.DMA((3 * n,))] * 4


def _gather4(name, shards, n_split):
    n = len(shards)

    def body(*refs):
        plan = _GatherPlan(refs[:n], refs[n:2 * n], refs[2 * n:], n_split)
        plan.start()
        plan.relay()
        plan.finish()

    return _pcall(
        body, name=name, out_shape=_GatherPlan.out_shapes(shards), in_specs=[_HBM] * n, out_specs=[_HBM] * n,
        scratch_shapes=_GatherPlan.sem_shapes(n),
    )(*shards)


def _when_step(h, n, hs, ns):
    return pl.when(jnp.logical_and(pl.program_id(0) == (hs if h < 0 else h), pl.program_id(1) == (ns if n < 0 else n)))


def _fill_slot(slots, own, slot):
    mask = (jnp.arange(4) == slot).reshape((4,) + (1,) * (slots.ndim - 1))
    return jnp.where(mask, own if own.ndim == slots.ndim else own[None], slots)


class _ScatterPlan:
    def __init__(self, ins, outs, sems):
        ssem, rsem = sems
        x, y, c = _my_pos()
        me = 2 * x + y
        copy = lambda src, dst, s_sem, r_sem, dev: (lambda: pltpu.make_async_remote_copy(
            src_ref=src, dst_ref=dst, send_sem=s_sem, recv_sem=r_sem, device_id=dev, device_id_type=MESH))
        self.sends, self.recvs = [], []
        for t in range(len(ins)):
            for k, (px, py) in enumerate(_plane_peers(x, y)):
                peer = 2 * px + py
                sem = 3 * t + k
                self.sends.append(copy(ins[t].at[peer], outs[t].at[me], ssem.at[sem], rsem.at[sem], (px, py, c)))
                self.recvs.append(copy(ins[t].at[peer], outs[t].at[peer], ssem.at[sem], rsem.at[sem], (px, py, c)))

    def start(self):
        for cp in self.sends:
            cp().start()

    def finish(self):
        for rc in self.recvs:
            rc().wait_recv()
        for cp in self.sends:
            cp().wait_send()

    @staticmethod
    def out_shapes(partials):
        return [jax.ShapeDtypeStruct(p.shape, p.dtype) for p in partials]

    @staticmethod
    def sem_shapes(n):
        return [pltpu.SemaphoreType.DMA((3 * n,))] * 2


def _sibling_send(name, arrays, axes=None):
    n = len(arrays)
    axes = [None] * n if axes is None else axes

    def body(*refs):
        ins, outs = refs[:n], refs[n:2 * n]
        ssem, rsem = refs[2 * n:]
        x, y, c = _my_pos()
        cps = []
        for t in range(n):
            src = ins[t]
            if axes[t] is not None:
                hr = ins[t].shape[axes[t]] // 2
                give = pl.ds(pl.multiple_of((1 - c) * hr, SUBLANES), hr)
                src = ins[t].at[give] if axes[t] == 0 else ins[t].at[:, give]
            cp = pltpu.make_async_remote_copy(
                src_ref=src, dst_ref=outs[t], send_sem=ssem.at[t], recv_sem=rsem.at[t],
                device_id=(x, y, 1 - c), device_id_type=MESH)
            cp.start()
            cps.append(cp)
        for cp in cps:
            cp.wait_recv()
        for cp in cps:
            cp.wait_send()

    def half(a, axis):
        shape = list(a.shape)
        if axis is not None:
            shape[axis] //= 2
        return jax.ShapeDtypeStruct(tuple(shape), a.dtype)

    return _pcall(
        body, name=name, out_shape=[half(a, ax) for a, ax in zip(arrays, axes)], in_specs=[_HBM] * n,
        out_specs=[_HBM] * n,
        scratch_shapes=[pltpu.SemaphoreType.DMA((n,)), pltpu.SemaphoreType.DMA((n,))],
    )(*arrays)


def _mm(name, a, b, *, nt=False, ta=False, out_dtype=F32, add=None, tm=1024, tn=1024, tk=4096, b_cols=None,
        scatter=()):
    m, k = (a.shape[1], a.shape[0]) if ta else a.shape
    n = b.shape[0] if nt else b.shape[1]
    assert (b.shape[1] if nt else b.shape[0]) == k
    col0 = 0
    if b_cols is not None:
        assert not nt
        col0, n = b_cols
    has_add = add is not None
    tm, tn = _pick(m, tm), _pick(n, tn)
    assert col0 % tn == 0
    n_sc = len(scatter)
    out_bytes = jnp.dtype(out_dtype).itemsize

    def vmem_bytes(tk_):
        steps = k // tk_
        return (4 * (tm + tn) * tk_ + 2 * tm * tn * out_bytes + (8 * tm * tn if has_add else 0)
                + (4 * tm * tn if steps > 1 else 0))

    tk = _pick(k, tk)
    while vmem_bytes(tk) > MM_VMEM_BUDGET and tk > 512:
        tk = _pick(k, tk - LANES)
    nk = k // tk

    grid = (n // tn, m // tm, nk)

    def body(*refs):
        a_ref, b_ref = refs[0], refs[1]
        c_ref = refs[2] if has_add else None
        part_refs = refs[2 + has_add:2 + has_add + n_sc]
        o_ref = refs[2 + has_add + n_sc]
        land_refs = refs[3 + has_add + n_sc:3 + has_add + 2 * n_sc]
        rest = refs[3 + has_add + 2 * n_sc:]
        if n_sc:
            sems = rest[-2:]
            at = lambda step: pl.when(jnp.logical_and(jnp.logical_and(
                pl.program_id(0) == step[0], pl.program_id(1) == step[1]), pl.program_id(2) == step[2]))

            @at((0, 0, 0))
            def _():
                _ScatterPlan(part_refs, land_refs, sems).start()

            @at(tuple(g - 1 for g in grid))
            def _():
                _ScatterPlan(part_refs, land_refs, sems).finish()

        p = (_dot_tn if ta else _dot_nt if nt else _dot)(a_ref[...], b_ref[...])
        if nk == 1:
            o_ref[...] = (p + c_ref[...] if has_add else p).astype(o_ref.dtype)
            return
        acc = rest[0]
        kk = pl.program_id(2)

        @pl.when(kk == 0)
        def _():
            acc[...] = p

        @pl.when(jnp.logical_and(kk > 0, kk < nk - 1))
        def _():
            acc[...] += p

        @pl.when(kk == nk - 1)
        def _():
            r = acc[...] + p
            if has_add:
                r = r + c_ref[...]
            o_ref[...] = r.astype(o_ref.dtype)

    if ta:
        a_spec = pl.BlockSpec((tk, tm), lambda j, i, kk: (kk, i))
    else:
        a_spec = pl.BlockSpec((tm, tk), lambda j, i, kk: (i, kk))
    if nt:
        b_spec = pl.BlockSpec((tn, tk), lambda j, i, kk: (j, kk))
    else:
        b_spec = pl.BlockSpec((tk, tn), lambda j, i, kk: (kk, j + col0 // tn))
    o_spec = pl.BlockSpec((tm, tn), lambda j, i, kk: (i, j))
    in_specs = [a_spec, b_spec] + ([o_spec] if has_add else []) + [_HBM] * n_sc
    args = (a, b) + ((add,) if has_add else ()) + tuple(scatter)
    res = _pcall(
        body, name=name, grid=grid,
        out_shape=[jax.ShapeDtypeStruct((m, n), out_dtype)] + _ScatterPlan.out_shapes(scatter),
        in_specs=in_specs, out_specs=[o_spec] + [_HBM] * n_sc,
        scratch_shapes=([pltpu.VMEM((tm, tn), F32)] if nk > 1 else []) + (_ScatterPlan.sem_shapes(n_sc) if n_sc else []),
        compiler_params=_params(("arbitrary",) * 3 if n_sc else ("parallel", "parallel", "arbitrary")),
    )(*args)
    return (res[0], res[1:]) if n_sc else res[0]


class Col:
    def __init__(self, arr, w=None, cb=0, lead=None):
        self.arr, self.cb, self.lead = arr, cb, lead
        self.w = arr.shape[-1] if w is None else w
        self.rows = arr.shape[-2]


def _ew(name, fn, *, tr, ins, consts=(), outs=(), accs=(), halo_prev=(), halo_next=()):
    ins = [c if isinstance(c, Col) else Col(c) for c in ins]
    halo_prev = [c if isinstance(c, Col) else Col(c) for c in halo_prev]
    halo_next = [c if isinstance(c, Col) else Col(c) for c in halo_next]
    rows = ins[0].rows
    tr = _rows_tile(rows, tr)
    nt = rows // tr
    n_in, n_hp, n_hn, n_c, n_o, n_a = len(ins), len(halo_prev), len(halo_next), len(consts), len(outs), len(accs)
    groups = tr // SUBLANES

    def spec(col, kind):
        if kind == "cur":
            shape, idx = (tr, col.w), (lambda i, cb=col.cb: (i, cb))
        elif kind == "prev":
            shape, idx = (SUBLANES, col.w), (lambda i, cb=col.cb: (jnp.maximum(i * groups - 1, 0), cb))
        else:
            shape = (SUBLANES, col.w)
            idx = (lambda i, cb=col.cb: (jnp.minimum((i + 1) * groups, rows // SUBLANES - 1), cb))
        if col.lead is None:
            return pl.BlockSpec(shape, idx)
        return pl.BlockSpec((None,) + shape, lambda i, idx=idx, lead=col.lead: (lead,) + idx(i))

    def body(*refs):
        i = pl.program_id(0)
        p = 0
        tiles = [r[...] for r in refs[p:p + n_in]]; p += n_in
        prev8 = [r[...] for r in refs[p:p + n_hp]]; p += n_hp
        next8 = [r[...] for r in refs[p:p + n_hn]]; p += n_hn
        cvals = [r[...] for r in refs[p:p + n_c]]; p += n_c
        out_refs = refs[p:p + n_o]; p += n_o
        acc_refs = refs[p:p + n_a]
        out_v, acc_v = fn(i, nt, tiles, prev8, next8, cvals)
        for r, v in zip(out_refs, out_v):
            r[...] = v.astype(r.dtype)
        if n_a:
            @pl.when(i == 0)
            def _():
                for r, v in zip(acc_refs, acc_v):
                    r[...] = v

            @pl.when(i > 0)
            def _():
                for r, v in zip(acc_refs, acc_v):
                    r[...] += v

    in_specs = ([spec(c, "cur") for c in ins] + [spec(c, "prev") for c in halo_prev]
                + [spec(c, "next") for c in halo_next]
                + [pl.BlockSpec(c.shape, lambda i, nd=c.ndim: (0,) * nd) for c in consts])
    out_specs = ([pl.BlockSpec((tr, w), lambda i: (i, 0)) for w, _ in outs]
                 + [pl.BlockSpec(s, lambda i: (0, 0)) for s in accs])
    out_shape = ([jax.ShapeDtypeStruct((rows, w), dt) for w, dt in outs]
                 + [jax.ShapeDtypeStruct(s, F32) for s in accs])
    args = [c.arr for c in ins] + [c.arr for c in halo_prev] + [c.arr for c in halo_next] + list(consts)
    res = _pcall(body, name=name, grid=(nt,), out_shape=out_shape, in_specs=in_specs, out_specs=out_specs,
                 compiler_params=_params(("arbitrary",)))(*args)
    return res[:n_o], res[n_o:]


def _colsum(v):
    return jnp.sum(v, axis=0, keepdims=True)


def _heads_of(w):
    return w // HEAD_DIM


def _per_head(fn, *arrays):
    nh = _heads_of(arrays[0].shape[1])
    res = [fn(*[a[:, h * HEAD_DIM:(h + 1) * HEAD_DIM] for a in arrays]) for h in range(nh)]
    if isinstance(res[0], tuple):
        return tuple(jnp.concatenate([r[j] for r in res], axis=1) for j in range(len(res[0])))
    return jnp.concatenate(res, axis=1)


def _head_sum(v):
    nh = _heads_of(v.shape[1])
    out = v[:, :HEAD_DIM]
    for h in range(1, nh):
        out = out + v[:, h * HEAD_DIM:(h + 1) * HEAD_DIM]
    return out


def _rms_fwd(x, w):
    r = lax.rsqrt(jnp.mean(x * x, axis=1, keepdims=True) + EPS)
    return x * r * w


def _rms_bwd(x, w, dy):
    r = lax.rsqrt(jnp.mean(x * x, axis=1, keepdims=True) + EPS)
    xh = x * r
    dxh = dy * w
    dx = r * (dxh - xh * jnp.mean(dxh * xh, axis=1, keepdims=True))
    return dx, dy * xh


def _silu(x):
    return x * _sigmoid(x)


def _dsilu(x):
    s = _sigmoid(x)
    return s * (1.0 + x * (1.0 - s))


SB_BQ = 512
SB_CUTOFF = 112.0
SB_PAIR = 2
SB_BK = 256


def _softplus_pos(z):
    return jnp.maximum(z, 0.0) + jnp.log(1.0 + jnp.exp(-jnp.abs(z)))


def _split_dot(v, tri):
    top = lax.bitcast_convert_type(lax.bitcast_convert_type(v, jnp.int32) & jnp.int32(-65536), F32)
    return _dot(top.astype(BF16), tri) + _dot((v - top).astype(BF16), tri)


def _sb_fwd(qn, kn, vb, shards, *, bq=SB_BQ, bk=SB_BK):
    s_len, hd = qn.shape
    nh = hd // HEAD_DIM
    bk = min(bk, s_len)
    bq = min(bq, s_len)
    ndiag = bq // bk
    scale = HEAD_DIM ** -0.5

    n_sh = len(shards)
    last_h, last_i = nh - 1, s_len // bq - 1

    def body(q_ref, k_ref, v_ref, *rest):
        sh_refs, o_ref, ld_ref = rest[:n_sh], rest[n_sh], rest[n_sh + 1]
        got_refs, sems = rest[n_sh + 2:2 * n_sh + 2], rest[2 * n_sh + 2:]
        i = pl.program_id(1)

        if n_sh:
            @_when_step(0, 0, last_h, last_i)
            def _():
                _GatherPlan(sh_refs, got_refs, sems, n_sh).start()

            @_when_step(nh // 2, 0, last_h, last_i)
            def _():
                _GatherPlan(sh_refs, got_refs, sems, n_sh).relay()

            @_when_step(-1, -1, last_h, last_i)
            def _():
                _GatherPlan(sh_refs, got_refs, sems, n_sh).finish()

        krow = lax.broadcasted_iota(jnp.int32, (bk, bk), 0)
        kcol = lax.broadcasted_iota(jnp.int32, (bk, bk), 1)
        later = (krow > kcol).astype(BF16)
        row = lax.broadcasted_iota(jnp.int32, (bq, bk), 0)
        col = lax.broadcasted_iota(jnp.int32, (bq, bk), 1)
        q = q_ref[...]

        def tiles(js, carry, diags):
            run, acc = carry
            ks, vs, zs = [], [], []
            for j in js:
                off = pl.multiple_of(j * bk, bk)
                ks.append(k_ref[pl.ds(off, bk), :])
                vs.append(v_ref[pl.ds(off, bk), :])
                zs.append(_dot_nt(q, ks[-1]) * scale)
            sps, cums, masks = [], [], []
            for z, diag in zip(zs, diags):
                sp = _softplus_pos(z)
                causal = None
                if diag is not None:
                    causal = col + diag * bk < row
                    sp = jnp.where(causal, sp, 0.0)
                sps.append(sp)
                masks.append(causal)
                cums.append(_split_dot(sp, later))
            for z, sp, cum, causal, v in zip(zs, sps, cums, masks, vs):
                w = jnp.exp((z - sp) - (cum + run))
                if causal is not None:
                    w = jnp.where(causal, w, 0.0)
                acc = acc + _dot(w.astype(BF16), v)
                run = run + cum[:, 0:1] + sp[:, 0:1]
            return run, acc

        carry = (jnp.zeros((bq, 1), F32), jnp.zeros((bq, HEAD_DIM), F32))
        for dg in reversed(range(0, ndiag, SB_PAIR)):
            dgs = list(reversed(range(dg, dg + SB_PAIR)))
            carry = tiles([i * ndiag + g for g in dgs], carry, dgs)
        ld_ref[...] = carry[0]
        n_left = i * ndiag

        def more(st):
            return jnp.logical_and(st[0] < n_left, jnp.min(st[1]) < SB_CUTOFF)

        def step(st):
            t, run, acc = st
            run, acc = tiles([n_left - 1 - t], (run, acc), [None])
            return t + 1, run, acc

        _, _, acc = lax.while_loop(more, step, (jnp.int32(0),) + carry)
        o_ref[...] = acc.astype(o_ref.dtype)

    qspec = pl.BlockSpec((bq, HEAD_DIM), lambda h, i: (i, h))
    kspec = pl.BlockSpec((s_len, HEAD_DIM), lambda h, i: (0, h))
    return _pcall(
        body, name="sb_fwd", grid=(nh, s_len // bq),
        out_shape=[jax.ShapeDtypeStruct((s_len, hd), BF16), jax.ShapeDtypeStruct((nh, s_len, 1), F32)]
        + _GatherPlan.out_shapes(shards),
        in_specs=[qspec, kspec, kspec] + [_HBM] * n_sh,
        out_specs=[qspec, pl.BlockSpec((None, bq, 1), lambda h, i: (h, i, 0))] + [_HBM] * n_sh,
        scratch_shapes=_GatherPlan.sem_shapes(n_sh) if n_sh else [],
        compiler_params=_params(("arbitrary", "arbitrary")),
    )(qn, kn, vb, *shards)


def _sb_bwd(qn, kn, vb, do, lt_diag, *, bq=SB_BQ, bk=SB_BK):
    s_len, hd = qn.shape
    nh = hd // HEAD_DIM
    bk = min(bk, s_len)
    bq = min(bq, s_len)
    ndiag = bq // bk
    scale = HEAD_DIM ** -0.5

    def body(q_ref, k_ref, v_ref, do_ref, ld_ref, dq_ref, dk_ref, dv_ref):
        i = pl.program_id(1)

        @pl.when(i == 0)
        def _():
            dk_ref[...] = jnp.zeros_like(dk_ref)
            dv_ref[...] = jnp.zeros_like(dv_ref)

        krow = lax.broadcasted_iota(jnp.int32, (bk, bk), 0)
        kcol = lax.broadcasted_iota(jnp.int32, (bk, bk), 1)
        upto = (krow <= kcol).astype(BF16)
        before = (krow < kcol).astype(BF16)
        row = lax.broadcasted_iota(jnp.int32, (bq, bk), 0)
        col = lax.broadcasted_iota(jnp.int32, (bq, bk), 1)
        q = q_ref[...]
        do_t = do_ref[...]
        ones = jnp.ones((bk, LANES), BF16)
        n_left = i * ndiag

        def row_sums(js, diags):
            zs = [_dot_nt(q, k_ref[pl.ds(pl.multiple_of(j * bk, bk), bk), :]) * scale for j in js]
            tot = None
            for z, diag in zip(zs, diags):
                assert diag is None
                part = _split_dot(_softplus_pos(z), ones)[:, 0:1]
                tot = part if tot is None else tot + part
            return tot

        def more(st):
            return jnp.logical_and(st[0] < n_left, jnp.min(st[1]) < SB_CUTOFF)

        def widen(st):
            t, run = st
            return t + 1, run + row_sums([n_left - 1 - t], [None])

        used, lt = lax.while_loop(more, widen, (jnp.int32(0), ld_ref[...]))

        def tiles(js, carry, diags):
            pre, ecar, dq = carry
            offs, ks, zs, dws = [], [], [], []
            for j in js:
                off = pl.multiple_of(j * bk, bk)
                offs.append(off)
                ks.append(k_ref[pl.ds(off, bk), :])
                zs.append(_dot_nt(q, ks[-1]) * scale)
                dws.append(_dot_nt(do_t, v_ref[pl.ds(off, bk), :]))
            sps, cums, masks = [], [], []
            for z, diag in zip(zs, diags):
                sp = _softplus_pos(z)
                causal = None
                if diag is not None:
                    causal = col + diag * bk < row
                    sp = jnp.where(causal, sp, 0.0)
                sps.append(sp)
                masks.append(causal)
                cums.append(_split_dot(sp, upto))
            es, ebs, exs, sigs = [], [], [], []
            for off, z, sp, cum, dw, causal in zip(offs, zs, sps, cums, dws, masks):
                lb = z - sp
                w = jnp.exp(lb - (lt - (pre + cum)))
                if causal is not None:
                    w = jnp.where(causal, w, 0.0)
                dv_ref[pl.ds(off, bk), :] += _dot_tn(w.astype(BF16), do_t)
                e = dw * w
                eb = e.astype(BF16)
                es.append(e)
                ebs.append(eb)
                exs.append(_dot(eb, before))
                sigs.append(jnp.exp(lb))
                pre = pre + cum[:, bk - 1:bk]
            for off, k, e, eb, exm, sig, causal in zip(offs, ks, es, ebs, exs, sigs, masks):
                ex = exm + ecar
                dz = (e - sig * (e + ex)) * scale
                if causal is not None:
                    dz = jnp.where(causal, dz, 0.0)
                dzb = dz.astype(BF16)
                dk_ref[pl.ds(off, bk), :] += _dot_tn(dzb, q)
                dq = dq + _dot(dzb, k)
                ecar = ex[:, bk - 1:bk] + eb[:, bk - 1:bk].astype(F32)
            return pre, ecar, dq

        init = (jnp.zeros((bq, 1), F32), jnp.zeros((bq, 1), F32), jnp.zeros((bq, HEAD_DIM), F32))
        carry = lax.fori_loop(n_left - used, n_left, lambda j, cr: tiles([j], cr, [None]), init)
        for dg in range(0, ndiag, SB_PAIR):
            dgs = list(range(dg, dg + SB_PAIR))
            carry = tiles([i * ndiag + g for g in dgs], carry, dgs)
        dq_ref[...] = carry[2]

    qspec = pl.BlockSpec((bq, HEAD_DIM), lambda h, i: (i, h))
    kspec = pl.BlockSpec((s_len, HEAD_DIM), lambda h, i: (0, h))
    return _pcall(
        body, name="sb_bwd", grid=(nh, s_len // bq),
        out_shape=[jax.ShapeDtypeStruct((s_len, hd), F32)] * 3,
        in_specs=[qspec, kspec, kspec, qspec, pl.BlockSpec((None, bq, 1), lambda h, i: (h, i, 0))],
        out_specs=[qspec, kspec, kspec],
        compiler_params=_params(("parallel", "arbitrary")),
    )(qn, kn, vb, do, lt_diag)


GDN_GROUP = 16


def _gdn_group(nh):
    return min(GDN_GROUP, nh)


def _gdn_chunk_terms(qh, kh, vh, g_r, g_c, b_c):
    c = GDN_CHUNK
    r = lax.broadcasted_iota(jnp.int32, (c, c), 0)
    s = lax.broadcasted_iota(jnp.int32, (c, c), 1)
    tril, stril = r >= s, r > s
    gcc = jnp.sum(jnp.where(tril, g_r, 0.0), axis=1, keepdims=True)
    gcr = jnp.sum(jnp.where(r <= s, g_c, 0.0), axis=0, keepdims=True)
    dm = jnp.where(tril, jnp.exp(jnp.where(tril, gcc - gcr, 0.0)), 0.0)
    kb = kh.astype(BF16)
    kk = _dot_nt(kb, kb)
    qk = _dot_nt(qh.astype(BF16), kb)
    egc = jnp.exp(gcc)
    gcl = gcc[c - 1:c, :]
    t = dict(tril=tril, stril=stril, gcc=gcc, dm=dm, kb=kb, kk=kk, qk=qk, egc=egc,
             ekd=jnp.exp(gcl - gcc), gl=jnp.exp(gcl),
             a=jnp.where(stril, b_c * kk * dm, 0.0),
             bv=b_c * vh, bk=(b_c * egc) * kh, at=jnp.where(tril, qk * dm, 0.0))
    t["qg"] = qh * egc
    t["kd"] = kh * t["ekd"]
    return t


def _unit_lower_inverses(mats):
    c = GDN_CHUNK
    r = lax.broadcasted_iota(jnp.int32, (c, c), 0)
    s = lax.broadcasted_iota(jnp.int32, (c, c), 1)
    eye = (r == s).astype(F32)
    ps = [-a for a in mats]
    ts = [eye + p for p in ps]
    span = 2
    while span < c:
        ps = [_dot(p, p, hi=HIGH) for p in ps]
        ts = [t + _dot(t, p, hi=HIGH) for t, p in zip(ts, ps)]
        span *= 2
    return ts


def _gdn_fwd(q, k, v, g_col, g_row, b_col, b_row, shards, n_split):
    s_len, d = q.shape
    nh = d // HEAD_DIM
    c = GDN_CHUNK
    n_chunks = s_len // c
    grp = _gdn_group(nh)
    n_sh = len(shards)
    last_h, last_n = nh // grp - 1, n_chunks - 1

    def body(q_ref, k_ref, v_ref, gc_ref, gr_ref, bc_ref, br_ref, *rest):
        sh_refs, rest = rest[:n_sh], rest[n_sh:]
        o_ref, ss_ref, ts_ref = rest[:3]
        got_refs, rest = rest[3:3 + n_sh], rest[3 + n_sh:]
        st, sems = rest[0], rest[1:]
        n = pl.program_id(1)

        @_when_step(0, 0, last_h, last_n)
        def _():
            _GatherPlan(sh_refs, got_refs, sems, n_split).start()

        @_when_step(-1, 3 * n_chunks // 4, last_h, last_n)
        def _():
            _GatherPlan(sh_refs, got_refs, sems, n_split).relay()

        @_when_step(-1, -1, last_h, last_n)
        def _():
            _GatherPlan(sh_refs, got_refs, sems, n_split).finish()

        @pl.when(n == 0)
        def _():
            st[...] = jnp.zeros_like(st)

        heads = range(grp)
        sls = [slice(i * HEAD_DIM, (i + 1) * HEAD_DIM) for i in heads]
        terms = [_gdn_chunk_terms(q_ref[:, sls[i]], k_ref[:, sls[i]], v_ref[:, sls[i]],
                                  gr_ref[i:i + 1, :], gc_ref[:, i:i + 1], bc_ref[:, i:i + 1]) for i in heads]
        tinvs = _unit_lower_inverses([t["a"] for t in terms])
        wvs = [_dot(tinv, t["bv"], hi=HIGH) for tinv, t in zip(tinvs, terms)]
        wks = [_dot(tinv, t["bk"], hi=HIGH) for tinv, t in zip(tinvs, terms)]
        states = [st[i] for i in heads]
        sbs = [state.astype(BF16) for state in states]
        ubs = [(wv - _dot(wk.astype(BF16), sb)).astype(BF16) for wv, wk, sb in zip(wvs, wks, sbs)]
        for i in heads:
            t = terms[i]
            o_ref[:, sls[i]] = _dot(t["qg"].astype(BF16), sbs[i]) + _dot(t["at"].astype(BF16), ubs[i])
            ss_ref[i] = states[i]
            ts_ref[i] = tinvs[i]
            st[i] = t["gl"] * states[i] + _dot_tn(t["kd"].astype(BF16), ubs[i])

    tok = pl.BlockSpec((c, grp * HEAD_DIM), lambda h, n: (n, h))
    colspec = pl.BlockSpec((None, c, grp), lambda h, n: (h, n, 0))
    rowspec = pl.BlockSpec((None, None, grp, c), lambda h, n: (h, n, 0, 0))
    res = _pcall(
        body, name="gdn_fwd", grid=(nh // grp, n_chunks),
        out_shape=[jax.ShapeDtypeStruct((s_len, d), F32),
                   jax.ShapeDtypeStruct((n_chunks, nh, HEAD_DIM, HEAD_DIM), F32),
                   jax.ShapeDtypeStruct((n_chunks, nh, c, c), F32)] + _GatherPlan.out_shapes(shards),
        in_specs=[tok, tok, tok, colspec, rowspec, colspec, rowspec] + [_HBM] * n_sh,
        out_specs=[tok, pl.BlockSpec((None, grp, HEAD_DIM, HEAD_DIM), lambda h, n: (n, h, 0, 0)),
                   pl.BlockSpec((None, grp, c, c), lambda h, n: (n, h, 0, 0))] + [_HBM] * n_sh,
        scratch_shapes=[pltpu.VMEM((grp, HEAD_DIM, HEAD_DIM), F32)] + _GatherPlan.sem_shapes(n_sh),
        compiler_params=_params(("arbitrary", "arbitrary")),
    )(q, k, v, g_col, g_row, b_col, b_row, *shards)
    return res[0], res[1], res[2], res[3:]


def _gdn_bwd(q, k, v, g_col, g_row, b_col, b_row, states, tinvs, do, partials):
    s_len, d = q.shape
    nh = d // HEAD_DIM
    c = GDN_CHUNK
    n_chunks = s_len // c
    grp = _gdn_group(nh)
    n_p = len(partials)
    last_h, last_n = nh // grp - 1, n_chunks - 1

    def body(q_ref, k_ref, v_ref, gc_ref, gr_ref, bc_ref, br_ref, ss_ref, ts_ref, do_ref, *rest):
        part_refs, rest = rest[:n_p], rest[n_p:]
        dq_ref, dk_ref, dv_ref, dgb_ref = rest[:4]
        land_refs, rest = rest[4:4 + n_p], rest[4 + n_p:]
        dst, sems = rest[0], rest[1:]
        n = pl.program_id(1)

        @_when_step(0, 0, last_h, last_n)
        def _():
            _ScatterPlan(part_refs, land_refs, sems).start()

        @_when_step(-1, -1, last_h, last_n)
        def _():
            _ScatterPlan(part_refs, land_refs, sems).finish()

        @pl.when(n == 0)
        def _():
            dst[...] = jnp.zeros_like(dst)

        r = lax.broadcasted_iota(jnp.int32, (c, c), 0)
        s = lax.broadcasted_iota(jnp.int32, (c, c), 1)
        suffix = (r <= s).astype(F32)
        lane = lax.broadcasted_iota(jnp.int32, (c, LANES), 1)
        heads = range(grp)
        sls = [slice(i * HEAD_DIM, (i + 1) * HEAD_DIM) for i in heads]
        qs = [q_ref[:, sl] for sl in sls]
        ks = [k_ref[:, sl] for sl in sls]
        vs = [v_ref[:, sl] for sl in sls]
        bcs = [bc_ref[:, i:i + 1] for i in heads]
        ts = [_gdn_chunk_terms(qs[i], ks[i], vs[i], gr_ref[i:i + 1, :], gc_ref[:, i:i + 1], bcs[i]) for i in heads]
        tinv = [ts_ref[i] for i in heads]
        state = [ss_ref[i] for i in heads]
        sb = [x.astype(BF16) for x in state]
        dnext = [dst[i] for i in heads]
        dnb = [x.astype(BF16) for x in dnext]
        dob = [do_ref[:, sl].astype(BF16) for sl in sls]
        wv = [_dot(tinv[i], ts[i]["bv"], hi=HIGH) for i in heads]
        wk = [_dot(tinv[i], ts[i]["bk"], hi=HIGH) for i in heads]
        wkb = [x.astype(BF16) for x in wk]
        ub = [(wv[i] - _dot(wkb[i], sb[i])).astype(BF16) for i in heads]
        du = [_dot_tn(ts[i]["at"].astype(BF16), dob[i]) + _dot(ts[i]["kd"].astype(BF16), dnb[i]) for i in heads]
        dub = [x.astype(BF16) for x in du]
        dat = [jnp.where(ts[i]["tril"], _dot_nt(dob[i], ub[i]), 0.0) for i in heads]
        dqg = [_dot_nt(dob[i], sb[i]) for i in heads]
        dkd = [_dot_nt(ub[i], dnb[i]) for i in heads]
        dwk = [-_dot_nt(dub[i], sb[i]) for i in heads]
        for i in heads:
            dst[i] = (ts[i]["gl"] * dnext[i] + _dot_tn(ts[i]["qg"].astype(BF16), dob[i]) - _dot_tn(wkb[i], dub[i]))
        dbv = [_dot_tn(tinv[i], du[i], hi=HIGH) for i in heads]
        dbk = [_dot_tn(tinv[i], dwk[i], hi=HIGH) for i in heads]
        dtm = [_dot_nt(du[i], ts[i]["bv"], hi=HIGH) + _dot_nt(dwk[i], ts[i]["bk"], hi=HIGH) for i in heads]
        dtt = [_dot_nt(dtm[i], tinv[i], hi=HIGH) for i in heads]
        da = [-jnp.where(ts[i]["stril"], _dot_tn(tinv[i], dtt[i], hi=HIGH), 0.0) for i in heads]
        rs = lambda m: jnp.sum(m, axis=1, keepdims=True)
        dgb = jnp.zeros((c, LANES), F32)
        for i in heads:
            t, b_c, dm, kb = ts[i], bcs[i], ts[i]["dm"], ts[i]["kb"]
            egc, ekd = t["egc"], t["ekd"]
            dkk = da[i] * b_c * dm
            ddm = da[i] * b_c * t["kk"] + dat[i] * t["qk"]
            dqkb, dkkb = (dat[i] * dm).astype(BF16), dkk.astype(BF16)
            dq_ref[:, sls[i]] = _dot(dqkb, kb) + dqg[i] * egc
            dk_ref[:, sls[i]] = (_dot_tn(dqkb, qs[i].astype(BF16)) + _dot(dkkb, kb) + _dot_tn(dkkb, kb)
                                 + dbk[i] * (b_c * egc) + dkd[i] * ekd)
            dv_ref[:, sls[i]] = dbv[i] * b_c
            dbk_k = rs(dbk[i] * ks[i])
            dbeta = rs(da[i] * t["kk"] * dm) + rs(dbv[i] * vs[i]) + dbk_k * egc
            mx = ddm * dm
            ekd_sum = rs(dkd[i] * ks[i]) * ekd
            dgc = rs(mx) + dbk_k * b_c * egc + rs(dqg[i] * qs[i]) * egc - ekd_sum
            dgl = jnp.sum(rs(dnext[i] * state[i]), axis=0, keepdims=True)
            tail = jnp.sum(ekd_sum, axis=0, keepdims=True) + dgl * t["gl"]
            dg = (_dot(suffix, jnp.broadcast_to(dgc, (c, LANES)), hi=HIGH)[:, 0:1]
                  - rs(_dot_nt(suffix, mx, hi=HIGH)) + tail)
            dgb = dgb + jnp.where(lane == i, dbeta, 0.0) + jnp.where(lane == grp + i, dg, 0.0)
        dgb_ref[...] = dgb

    last = n_chunks - 1
    tok = pl.BlockSpec((c, grp * HEAD_DIM), lambda h, n: (last - n, h))
    colspec = pl.BlockSpec((None, c, grp), lambda h, n: (h, last - n, 0))
    rowspec = pl.BlockSpec((None, None, grp, c), lambda h, n: (h, last - n, 0, 0))
    res = _pcall(
        body, name="gdn_bwd", grid=(nh // grp, n_chunks),
        out_shape=[jax.ShapeDtypeStruct((s_len, d), F32)] * 3
        + [jax.ShapeDtypeStruct((nh // grp, s_len, LANES), F32)] + _ScatterPlan.out_shapes(partials),
        in_specs=[tok, tok, tok, colspec, rowspec, colspec, rowspec,
                  pl.BlockSpec((None, grp, HEAD_DIM, HEAD_DIM), lambda h, n: (last - n, h, 0, 0)),
                  pl.BlockSpec((None, grp, c, c), lambda h, n: (last - n, h, 0, 0)), tok] + [_HBM] * n_p,
        out_specs=[tok, tok, tok, pl.BlockSpec((None, c, LANES), lambda h, n: (h, last - n, 0))] + [_HBM] * n_p,
        scratch_shapes=[pltpu.VMEM((grp, HEAD_DIM, HEAD_DIM), F32)] + _ScatterPlan.sem_shapes(n_p),
        compiler_params=_params(("arbitrary", "arbitrary")),
    )(q, k, v, g_col, g_row, b_col, b_row, states, tinvs, do, *partials)
    return res[0], res[1], res[2], res[3], res[4:]


def _shift_down(prev8, cur, k):
    if k == 0:
        return cur
    ext = jnp.concatenate([prev8, cur], axis=0)
    return pltpu.roll(ext, k, 0)[SUBLANES:, :]


def _shift_up(cur, next8, k):
    if k == 0:
        return cur
    ext = jnp.concatenate([cur, next8], axis=0)
    n = ext.shape[0]
    return pltpu.roll(ext, n - k, 0)[:cur.shape[0], :]


def _conv_pre(i, x, prev8, w):
    prev8 = jnp.where(i == 0, 0.0, prev8)
    pre = None
    for j in range(GDN_CONV):
        term = w[j:j + 1, :] * _shift_down(prev8, x, GDN_CONV - 1 - j)
        pre = term if pre is None else pre + term
    return pre, prev8


def _l2_fwd(a, mult):
    return a * (lax.rsqrt(jnp.sum(a * a, axis=1, keepdims=True) + EPS) * mult)


def _l2_bwd(a, dy, mult):
    r = lax.rsqrt(jnp.sum(a * a, axis=1, keepdims=True) + EPS)
    dy = dy * mult
    return r * dy - a * (r * r * r) * jnp.sum(a * dy, axis=1, keepdims=True)


def _conv_fwd(xb, conv_w, group, *, norm, mult, tr=256):
    d = xb.shape[1] // 3

    def fn(i, nt, tiles, prev8, next8, cv):
        pre, _ = _conv_pre(i, tiles[0], prev8[0], cv[0])
        a = _silu(pre)
        if norm:
            a = _per_head(lambda ah: _l2_fwd(ah, mult), a)
        return [a], []

    col = Col(xb, d, group)
    wg = lax.slice_in_dim(conv_w, group * d, (group + 1) * d, axis=1)
    (y,), _ = _ew(f"conv_fwd{group}", fn, tr=tr, ins=[col], halo_prev=[col], consts=[wg], outs=[(d, F32)])
    return y


def _conv_bwd(xb, conv_w, group, dy, *, norm, mult, tr=256):
    d = xb.shape[1] // 3
    col = Col(xb, d, group)
    wg = lax.slice_in_dim(conv_w, group * d, (group + 1) * d, axis=1)

    def fn_pre(i, nt, tiles, prev8, next8, cv):
        x, dyt = tiles
        pre, p8 = _conv_pre(i, x, prev8[0], cv[0])
        if norm:
            da = _per_head(lambda ah, dh: _l2_bwd(ah, dh, mult), _silu(pre), dyt)
        else:
            da = dyt
        dpre = da * _dsilu(pre)
        tap = lax.broadcasted_iota(jnp.int32, (GDN_CONV, d), 0)
        dw = jnp.zeros((GDN_CONV, d), F32)
        for j in range(GDN_CONV):
            dw = dw + jnp.where(tap == j, _colsum(dpre * _shift_down(p8, x, GDN_CONV - 1 - j)), 0.0)
        return [dpre], [dw]

    (dpre,), (dw,) = _ew(f"conv_bwd_pre{group}", fn_pre, tr=tr, ins=[col, dy], halo_prev=[col], consts=[wg],
                         outs=[(d, F32)], accs=[(GDN_CONV, d)])

    def fn_dx(i, nt, tiles, prev8, next8, cv):
        n8 = jnp.where(i == nt - 1, 0.0, next8[0])
        dx = None
        for j in range(GDN_CONV):
            term = cv[0][j:j + 1, :] * _shift_up(tiles[0], n8, GDN_CONV - 1 - j)
            dx = term if dx is None else dx + term
        return [dx], []

    (dx,), _ = _ew(f"conv_bwd_dx{group}", fn_dx, tr=tr, ins=[dpre], halo_next=[dpre], consts=[wg], outs=[(d, BF16)])
    return dx, dw


def _adamw(name, w, m, v, grads, *, tr=64):
    shape = w.shape
    w2, m2, v2 = [a.reshape(-1, shape[-1]) for a in (w, m, v)]
    n_g = len(grads)
    bc1 = 1.0 - ADAM_B1 ** ADAM_STEP
    bc2 = 1.0 - ADAM_B2 ** ADAM_STEP

    def fn(i, nt, tiles, prev8, next8, cv):
        wt, mt, vt, g = tiles
        mn = ADAM_B1 * mt + (1.0 - ADAM_B1) * g
        vn = ADAM_B2 * vt + (1.0 - ADAM_B2) * (g * g)
        delta = -ADAM_LR * ((mn / bc1) / (jnp.sqrt(vn / bc2) + ADAM_EPS) + ADAM_WD * wt)
        return [delta, mn, vn], []

    width = shape[-1]
    assert n_g == 1
    outs, _ = _ew(name, fn, tr=tr, ins=[w2, m2, v2] + list(grads), outs=[(width, F32)] * 3)
    return (grads[0].reshape(shape),) + tuple(o.reshape(shape) for o in outs)


def _pad_cols(a, width):
    return jnp.pad(a, ((0, 0), (0, width - a.shape[1])))


def _gdn_layouts(gbeta, nh, n_chunks):
    grp = _gdn_group(nh)
    s_len = gbeta.shape[0]

    def lay(a):
        col = a.reshape(s_len, nh // grp, grp).transpose(1, 0, 2)
        row = a.reshape(n_chunks, GDN_CHUNK, nh // grp, grp).transpose(2, 0, 3, 1)
        return col, row

    b_col, b_row = lay(gbeta[:, :nh])
    g_col, g_row = lay(gbeta[:, nh:2 * nh])
    return g_col, g_row, b_col, b_row


def kernel(x, c, w_mod, b_mod, norm1_w, w_in, q_norm_w, k_norm_w, conv_w, a_log, dt_bias, o_norm_w, p_a, p_b, w_out, norm2_w, w_gate, w_up, w_down, loss_target, m_w_mod, m_b_mod, m_norm1_w, m_w_in, m_q_norm_w, m_k_norm_w, m_conv_w, m_a_log, m_dt_bias, m_o_norm_w, m_p_a, m_p_b, m_w_out, m_norm2_w, m_w_gate, m_w_up, m_w_down, v_w_mod, v_b_mod, v_norm1_w, v_w_in, v_q_norm_w, v_k_norm_w, v_conv_w, v_a_log, v_dt_bias, v_o_norm_w, v_p_a, v_p_b, v_w_out, v_norm2_w, v_w_gate, v_w_up, v_w_down):
    s_len, d = x.shape[1], x.shape[2]
    nh = d // HEAD_DIM
    n_chunks = s_len // GDN_CHUNK
    ff = 4 * w_gate.shape[2]
    mx, my, mc = _my_pos()
    chip = 2 * mx + my
    dev = 2 * chip + mc
    x2 = x[0]
    tgt = loss_target[0]

    c_all = _allgather8("ag_c", _pad_cols(c, d).reshape(SUBLANES, d // SUBLANES)).reshape(8, d)
    wm = w_mod[0]
    mod_w = wm.shape[1]
    bm_cols = lax.dynamic_slice_in_dim(b_mod, chip * mod_w, mod_w, axis=1)

    def mod_body(c_ref, w_ref, b_ref, o_ref, ca_ref):
        ca = _silu(c_ref[...])
        ca_ref[...] = ca
        o_ref[...] = _dot(ca, w_ref[...], hi=HIGHEST) + b_ref[...]

    tn_mod = _pick(mod_w, 512)
    mod8, c_act = _pcall(
        mod_body, name="mod_fwd", grid=(mod_w // tn_mod,),
        out_shape=[jax.ShapeDtypeStruct((8, mod_w), F32), jax.ShapeDtypeStruct((8, d), F32)],
        in_specs=[pl.BlockSpec((8, d), lambda j: (0, 0)), pl.BlockSpec((d, tn_mod), lambda j: (0, j)),
                  pl.BlockSpec((1, tn_mod), lambda j: (0, j))],
        out_specs=[pl.BlockSpec((8, tn_mod), lambda j: (0, j)), pl.BlockSpec((8, d), lambda j: (0, 0))],
        compiler_params=_params(("arbitrary",)),
    )(c_all, wm, bm_cols)
    mod_all = _allgather8("ag_mod", mod8)
    mod_me = mod_all.reshape(4, 2, 8, mod_w)[:, mc, dev, :].reshape(1, 6 * d)
    shift1, scale1, gate1, shift2, scale2, gate2 = [mod_me[:, j * d:(j + 1) * d] for j in range(6)]

    first = [w_in[0].astype(BF16), conv_w[0]]
    late_sb = [p_a[0].astype(BF16), p_b[0].astype(BF16), w_out[0].astype(BF16)]
    late_gdn = [w_gate[0].astype(BF16), w_up[0].astype(BF16), w_down[0].astype(BF16)]
    w_in_g, conv_g = [_fill_slot(g, sh, chip) for g, sh in zip(_gather4("ag_w_in", first, n_split=1), first)]
    w_in_f = w_in_g.transpose(1, 0, 2).reshape(d, -1)
    w_all = jnp.concatenate([w_in_f[:, :7 * d], w_in_f[:, 7 * d + 2 * nh:],
                             _pad_cols(w_in_f[:, 7 * d:7 * d + 2 * nh], LANES)], axis=1)
    conv_f = conv_g.transpose(1, 0, 2).reshape(GDN_CONV, 3 * d)

    def norm_mod_fn(i, nt, tiles, prev8, next8, cv):
        w, sc, sh = cv
        return [_rms_fwd(tiles[0], w) * (1.0 + sc) + sh], []

    (u1,), _ = _ew("norm_mod1", norm_mod_fn, tr=512, ins=[x2], consts=[norm1_w, scale1, shift1], outs=[(d, BF16)])
    proj_a = _mm("proj_a", u1, w_all, b_cols=(0, 3 * d), out_dtype=BF16)
    proj_b = _mm("proj_b", u1, w_all, b_cols=(3 * d, 3 * d))
    proj_zg = _mm("proj_zg", u1, w_all, b_cols=(6 * d, 3 * d), out_dtype=BF16)
    proj_ba = _mm("proj_ba", u1, w_all, b_cols=(9 * d, LANES))

    def qknorm_fn(i, nt, tiles, prev8, next8, cv):
        qa, ka, va = [t.astype(F32) for t in tiles]
        return [_per_head(lambda h: _rms_fwd(h, cv[0]), qa), _per_head(lambda h: _rms_fwd(h, cv[1]), ka), va], []

    (qn, kn, vb), _ = _ew("qknorm", qknorm_fn, tr=256,
                          ins=[Col(proj_a, d, 0), Col(proj_a, d, 1), Col(proj_a, d, 2)],
                          consts=[q_norm_w, k_norm_w], outs=[(d, BF16)] * 3)
    sb_out = _sb_fwd(qn, kn, vb, late_sb)
    o_a, lt_diag = sb_out[0], sb_out[1]
    p_a_f, p_b_f, w_out_f = [_fill_slot(g, sh, chip).reshape(d, d) for g, sh in zip(sb_out[2:], late_sb)]

    lane_ids = jnp.arange(LANES)
    is_b = (lane_ids < nh)[None, :]
    is_a = ((lane_ids >= nh) & (lane_ids < 2 * nh))[None, :]
    alog128 = jnp.zeros((1, LANES), F32).at[:, nh:2 * nh].set(a_log)
    dtb128 = jnp.zeros((1, LANES), F32).at[:, nh:2 * nh].set(dt_bias)
    is_b_f, is_a_f = is_b.astype(F32), is_a.astype(F32)

    def gbeta_fn(i, nt, tiles, prev8, next8, cv):
        al, dtb, mb, ma = cv
        ba = tiles[0]
        g = -jnp.exp(al) * _softplus(ba + dtb)
        return [jnp.where(mb > 0.5, _sigmoid(ba), jnp.where(ma > 0.5, g, 0.0))], []

    (gbeta,), _ = _ew("gbeta", gbeta_fn, tr=1024, ins=[proj_ba], consts=[alog128, dtb128, is_b_f, is_a_f],
                      outs=[(LANES, F32)])
    g_col, g_row, b_col, b_row = _gdn_layouts(gbeta, nh, n_chunks)
    qscale = HEAD_DIM ** -0.5
    q_b = _conv_fwd(proj_b, conv_f, 0, norm=True, mult=qscale)
    k_b = _conv_fwd(proj_b, conv_f, 1, norm=True, mult=1.0)
    v_b = _conv_fwd(proj_b, conv_f, 2, norm=False, mult=1.0)
    o_raw, states, tinvs, late_g = _gdn_fwd(q_b, k_b, v_b, g_col, g_row, b_col, b_row, late_gdn,
                                            n_split=len(late_gdn))
    late_g = [_fill_slot(g, sh, chip) for g, sh in zip(late_g, late_gdn)]
    w_gate_f, w_up_f = [g.transpose(1, 0, 2).reshape(d, ff) for g in late_g[0:2]]
    w_down_f = late_g[2].reshape(ff, d)

    def gated_norm_fn(i, nt, tiles, prev8, next8, cv):
        o, z = tiles[0], tiles[1].astype(F32)
        return [_per_head(lambda h: _rms_fwd(h, cv[0]), o) * _silu(z)], []

    (o_b,), _ = _ew("gated_norm", gated_norm_fn, tr=256, ins=[o_raw, Col(proj_zg, d, 0)], consts=[o_norm_w],
                    outs=[(d, BF16)])
    y_a = _mm("out_a", o_a, p_a_f, out_dtype=BF16)
    y_b = _mm("out_b", o_b, p_b_f, out_dtype=BF16)

    def merge_fn(i, nt, tiles, prev8, next8, cv):
        ya, yb, ga, gb = [t.astype(F32) for t in tiles]
        return [_sigmoid(ga) * ya + _sigmoid(gb) * yb], []

    (merged,), _ = _ew("merge", merge_fn, tr=256, ins=[y_a, y_b, Col(proj_zg, d, 1), Col(proj_zg, d, 2)],
                       outs=[(d, BF16)])
    y_o = _mm("out_proj", merged, w_out_f, out_dtype=BF16)

    def resid_norm_fn(i, nt, tiles, prev8, next8, cv):
        xt, yo = tiles
        g1, w, sc, sh = cv
        h1 = xt + g1 * yo
        return [h1, _rms_fwd(h1, w) * (1.0 + sc) + sh], []

    (h1, u2), _ = _ew("resid_norm2", resid_norm_fn, tr=256, ins=[x2, y_o],
                      consts=[gate1, norm2_w, scale2, shift2], outs=[(d, F32), (d, BF16)])
    w_gu = jnp.concatenate([w_gate_f, w_up_f], axis=1)
    gu = _mm("ff_gate_up", u2, w_gu, out_dtype=BF16)
    gt, up = Col(gu, ff, 0), Col(gu, ff, 1)

    def swiglu_fn(i, nt, tiles, prev8, next8, cv):
        return [_silu(tiles[0].astype(F32)) * tiles[1].astype(F32)], []

    (act,), _ = _ew("swiglu", swiglu_fn, tr=128, ins=[gt, up], outs=[(ff, BF16)])
    y_d = _mm("ff_down", act, w_down_f, out_dtype=BF16)

    def loss_fn(i, nt, tiles, prev8, next8, cv):
        h1t, yd, tg = tiles
        diff = h1t + cv[0] * yd - tg
        dy = diff * (1.0 / d)
        return [dy, dy * cv[0]], [_colsum(0.5 * diff * dy), _colsum(dy * yd)]

    (dy, dyd), (loss_cols, dgate2) = _ew("loss", loss_fn, tr=256, ins=[h1, y_d, tgt], consts=[gate2],
                                         outs=[(d, F32), (d, BF16)], accs=[(1, d), (1, d)])
    loss = lax.psum(jnp.sum(loss_cols), ("x", "y", "c"))

    dact = _mm("d_act", dyd, w_down_f, nt=True, out_dtype=BF16)
    g_w_down = _mm("g_w_down", act, dyd, ta=True)

    def swiglu_bwd_fn(i, nt, tiles, prev8, next8, cv):
        da, g, u = [t.astype(F32) for t in tiles]
        return [jnp.concatenate([da * u * _dsilu(g), da * _silu(g)], axis=1)], []

    (d_gu,), _ = _ew("swiglu_bwd", swiglu_bwd_fn, tr=128, ins=[dact, gt, up], outs=[(2 * ff, BF16)])
    du2 = _mm("d_u2", d_gu, w_gu, nt=True)
    g_w_gate = _mm("g_w_gate", u2, d_gu, ta=True, b_cols=(0, ff))
    g_w_up = _mm("g_w_up", u2, d_gu, ta=True, b_cols=(ff, ff))

    def norm2_bwd_fn(i, nt, tiles, prev8, next8, cv):
        h1t, du, dres, yo = tiles
        w, sc, g1 = cv
        r = lax.rsqrt(jnp.mean(h1t * h1t, axis=1, keepdims=True) + EPS)
        nrm = h1t * r
        dn = du * w * (1.0 + sc)
        dh = r * (dn - nrm * jnp.mean(dn * nrm, axis=1, keepdims=True)) + dres
        return [dh, dh * g1], [_colsum(du), _colsum(du * nrm * w), _colsum(du * nrm * (1.0 + sc)), _colsum(dh * yo)]

    (dh1, dyo), (dshift2, dscale2, g_norm2, dgate1) = _ew(
        "norm2_bwd", norm2_bwd_fn, tr=256, ins=[h1, du2, dy, y_o], consts=[norm2_w, scale2, gate1],
        outs=[(d, F32), (d, BF16)], accs=[(1, d)] * 4)

    dmerged = _mm("d_merged", dyo, w_out_f, nt=True, out_dtype=BF16)
    g_w_out = _mm("g_w_out", merged, dyo, ta=True)

    def merge_bwd_fn(i, nt, tiles, prev8, next8, cv):
        dm, ya, yb, ga, gb = [t.astype(F32) for t in tiles]
        sa, sb = _sigmoid(ga), _sigmoid(gb)
        return [dm * sa, dm * sb, dm * ya * sa * (1.0 - sa), dm * yb * sb * (1.0 - sb)], []

    (dya, dyb, dga, dgb_gate), _ = _ew(
        "merge_bwd", merge_bwd_fn, tr=256, ins=[dmerged, y_a, y_b, Col(proj_zg, d, 1), Col(proj_zg, d, 2)],
        outs=[(d, BF16)] * 4)
    do_a = _mm("d_o_a", dya, p_a_f, nt=True, out_dtype=BF16)
    g_p_a = _mm("g_p_a", o_a, dya, ta=True)
    do_b = _mm("d_o_b", dyb, p_b_f, nt=True, out_dtype=BF16)
    g_p_b = _mm("g_p_b", o_b, dyb, ta=True)

    def gated_norm_bwd_fn(i, nt, tiles, prev8, next8, cv):
        dob, o, z = tiles[0], tiles[1], tiles[2].astype(F32)
        sz = _silu(z)

        def head(oh, dh):
            return _rms_bwd(oh, cv[0], dh)

        dxo, dwn = _per_head(head, o, dob * sz)
        nrm_w = _per_head(lambda h: _rms_fwd(h, cv[0]), o)
        return [dxo, dob * nrm_w * _dsilu(z)], [_colsum(_head_sum(dwn))]

    (do_raw, dz_b), (g_o_norm,) = _ew(
        "gated_norm_bwd", gated_norm_bwd_fn, tr=256, ins=[do_b, o_raw, Col(proj_zg, d, 0)], consts=[o_norm_w],
        outs=[(d, F32), (d, BF16)], accs=[(1, HEAD_DIM)])
    by_chip = lambda a: a.reshape(a.shape[0], 4, -1).transpose(1, 0, 2)

    def chip_sums(tag, raw, axes):
        theirs = _sibling_send(f"swap_{tag}", raw, axes)
        sums = []
        for t, (part, ax, other) in enumerate(zip(raw, axes, theirs)):
            def pair_fn(i, nt, tiles, prev8, next8, cv):
                return [tiles[0] + tiles[1]], []

            hr, width = part.shape[ax] // 2, part.shape[-1]
            mine = lax.dynamic_slice_in_dim(part, mc * hr, hr, axis=ax)
            (ch,), _ = _ew(f"pair_sum_{tag}{t}", pair_fn, tr=64,
                           ins=[mine.reshape(-1, width), other.reshape(-1, width)], outs=[(width, BF16)])
            sums.append(ch.reshape(other.shape))
        return sums

    s_gate, s_up, s_pa, s_pb, s_out, s_down = chip_sums(
        "late", [g_w_gate, g_w_up, g_p_a.reshape(4, d // 4, d), g_p_b.reshape(4, d // 4, d),
                 g_w_out.reshape(4, d // 4, d), g_w_down.reshape(4, ff // 4, d)], [0, 0, 1, 1, 1, 1])
    late_halves = [s_pa, s_pb, s_out, by_chip(s_gate), by_chip(s_up), s_down]
    dq_b, dk_b, dv_b, dgb_grp, late_landed = _gdn_bwd(q_b, k_b, v_b, g_col, g_row, b_col, b_row, states, tinvs,
                                                      do_raw, late_halves)
    grp = _gdn_group(nh)
    dbeta = dgb_grp[:, :, :grp].transpose(1, 0, 2).reshape(s_len, nh)
    dg = dgb_grp[:, :, grp:2 * grp].transpose(1, 0, 2).reshape(s_len, nh)
    dgbeta = _pad_cols(jnp.concatenate([dbeta, dg], axis=1), LANES)

    def gbeta_bwd_fn(i, nt, tiles, prev8, next8, cv):
        al, dtb, mb, ma = cv
        ba, dgb = tiles
        beta = _sigmoid(ba)
        arg = ba + dtb
        da = dgb * (-jnp.exp(al)) * _sigmoid(arg)
        g = -jnp.exp(al) * _softplus(arg)
        dba = jnp.where(mb > 0.5, dgb * beta * (1.0 - beta), jnp.where(ma > 0.5, da, 0.0))
        return [dba], [_colsum(jnp.where(ma > 0.5, dgb * g, 0.0)), _colsum(jnp.where(ma > 0.5, da, 0.0))]

    (dba,), (g_alog128, g_dtb128) = _ew(
        "gbeta_bwd", gbeta_bwd_fn, tr=1024, ins=[proj_ba, dgbeta], consts=[alog128, dtb128, is_b_f, is_a_f],
        outs=[(LANES, BF16)], accs=[(1, LANES)] * 2)
    dxq, g_conv_q = _conv_bwd(proj_b, conv_f, 0, dq_b, norm=True, mult=qscale)
    dxk, g_conv_k = _conv_bwd(proj_b, conv_f, 1, dk_b, norm=True, mult=1.0)
    dxv, g_conv_v = _conv_bwd(proj_b, conv_f, 2, dv_b, norm=False, mult=1.0)
    g_conv = jnp.concatenate([g_conv_q, g_conv_k, g_conv_v], axis=1)

    dqn, dkn, dvb = _sb_bwd(qn, kn, vb, do_a, lt_diag)

    def qknorm_bwd_fn(i, nt, tiles, prev8, next8, cv):
        qa, ka, dq, dk, dv = [t.astype(F32) for t in tiles]
        dxq_, dwq = _per_head(lambda h, g: _rms_bwd(h, cv[0], g), qa, dq)
        dxk_, dwk = _per_head(lambda h, g: _rms_bwd(h, cv[1], g), ka, dk)
        return [dxq_, dxk_, dv], [_colsum(_head_sum(dwq)), _colsum(_head_sum(dwk))]

    (dqa, dka, dva), (g_q_norm, g_k_norm) = _ew(
        "qknorm_bwd", qknorm_bwd_fn, tr=256, ins=[Col(proj_a, d, 0), Col(proj_a, d, 1), dqn, dkn, dvb],
        consts=[q_norm_w, k_norm_w], outs=[(d, BF16)] * 3, accs=[(1, HEAD_DIM)] * 2)

    d_all = jnp.concatenate([dqa, dka, dva, dxq, dxk, dxv, dz_b, dga, dgb_gate, dba], axis=1)
    g_wa = _mm("g_w_in_a", u1, d_all, ta=True, b_cols=(0, 3 * d))
    g_wb = _mm("g_w_in_b", u1, d_all, ta=True, b_cols=(3 * d, 3 * d))
    g_wzg = _mm("g_w_in_zg", u1, d_all, ta=True, b_cols=(6 * d, 3 * d))
    g_wba = _mm("g_w_in_ba", u1, d_all, ta=True, b_cols=(9 * d, LANES))
    s_wa, s_wb, s_wzg, s_wba = chip_sums("w_in", [g_wa, g_wb, g_wzg, g_wba], [0, 0, 0, 0])
    s_w_in = jnp.concatenate([s_wa, s_wb, s_wzg[:, :d], s_wba[:, :2 * nh], s_wzg[:, d:]], axis=1)
    w_in_halves = [by_chip(s_w_in)]
    du1, w_in_landed = _mm("d_u1", d_all, w_all, nt=True, tm=512, scatter=w_in_halves)


    def norm1_bwd_fn(i, nt, tiles, prev8, next8, cv):
        xt, du, dres = tiles
        w, sc = cv
        r = lax.rsqrt(jnp.mean(xt * xt, axis=1, keepdims=True) + EPS)
        nrm = xt * r
        dn = du * w * (1.0 + sc)
        dxt = r * (dn - nrm * jnp.mean(dn * nrm, axis=1, keepdims=True)) + dres
        return [dxt], [_colsum(du), _colsum(du * nrm * w), _colsum(du * nrm * (1.0 + sc))]

    (grad_x,), (dshift1, dscale1, g_norm1) = _ew(
        "norm1_bwd", norm1_bwd_fn, tr=256, ins=[x2, du1, dh1], consts=[norm1_w, scale1],
        outs=[(d, F32)], accs=[(1, d)] * 3)

    dmod_me = jnp.concatenate([dshift1, dscale1, dgate1, dshift2, dscale2, dgate2], axis=1)
    small = jnp.concatenate(
        [dmod_me, g_norm1, g_norm2, g_q_norm, g_k_norm, g_o_norm, g_alog128[:, nh:2 * nh], g_dtb128[:, nh:2 * nh],
         g_conv.reshape(1, -1)], axis=1)
    n_small = small.shape[1]
    pad_to = -(-n_small // (SUBLANES * LANES)) * (SUBLANES * LANES)
    small_all = _allgather8("ag_small", _pad_cols(small, pad_to).reshape(SUBLANES, pad_to // SUBLANES))
    small_all = small_all.reshape(8, pad_to)

    def sum8_fn(i, nt, tiles, prev8, next8, cv):
        return [], [_colsum(tiles[0])]

    _, (small_sum,) = _ew("sum_small", sum8_fn, tr=8, ins=[small_all], accs=[(1, pad_to)])
    offs = [0]
    for width in (6 * d, d, d, HEAD_DIM, HEAD_DIM, HEAD_DIM, nh, nh, GDN_CONV * 3 * d):
        offs.append(offs[-1] + width)
    pieces = [small_sum[:, offs[j]:offs[j + 1]] for j in range(9)]
    (gs_b_mod, gs_norm1, gs_norm2, gs_q_norm, gs_k_norm, gs_o_norm, gs_a_log, gs_dt_bias, gs_conv) = pieces
    conv_cols = 3 * d // 4
    gs_conv_mine = lax.dynamic_slice_in_dim(gs_conv.reshape(GDN_CONV, 3 * d), chip * conv_cols, conv_cols, axis=1)

    dmod_all = lax.dynamic_slice_in_dim(small_all[:, :6 * d], chip * mod_w, mod_w, axis=1)

    def wmod_grad_body(ct_ref, dm_ref, o_ref):
        o_ref[...] = _dot(ct_ref[...], dm_ref[...], hi=HIGHEST)

    g_w_mod = _pcall(
        wmod_grad_body, name="g_w_mod", grid=(mod_w // tn_mod,),
        out_shape=jax.ShapeDtypeStruct((d, mod_w), F32),
        in_specs=[pl.BlockSpec((d, 8), lambda j: (0, 0)), pl.BlockSpec((8, tn_mod), lambda j: (0, j))],
        out_specs=pl.BlockSpec((d, tn_mod), lambda j: (0, j)),
        compiler_params=_params(("arbitrary",)),
    )(c_act.T, dmod_all)

    chip_halves = w_in_halves + late_halves
    landed = list(w_in_landed) + list(late_landed)
    landed = [_fill_slot(land, ch, chip) for land, ch in zip(landed, chip_halves)]
    g_mine = []
    for t, land in enumerate(landed):
        def sum4_fn(i, nt, tiles, prev8, next8, cv):
            f = [tl.astype(F32) for tl in tiles]
            return [(f[0] + f[1]) + (f[2] + f[3])], []

        (gh,), _ = _ew(f"chip_sum{t}", sum4_fn, tr=64, ins=[Col(land, lead=s) for s in range(4)],
                       outs=[(land.shape[-1], F32)])
        g_mine.append(gh)
    g_theirs = _sibling_send("join_grads", g_mine)
    g_full = [jnp.concatenate([jnp.where(mc == 0, a, b), jnp.where(mc == 0, b, a)], axis=0)
              for a, b in zip(g_mine, g_theirs)]

    big = {}
    names = ["w_in", "p_a", "p_b", "w_out", "w_gate", "w_up", "w_down"]
    big_w = [w_in, p_a, p_b, w_out, w_gate, w_up, w_down]
    big_m = [m_w_in, m_p_a, m_p_b, m_w_out, m_w_gate, m_w_up, m_w_down]
    big_v = [v_w_in, v_p_a, v_p_b, v_w_out, v_w_gate, v_w_up, v_w_down]
    for t, nm in enumerate(names):
        big[nm] = _adamw(f"adamw_{nm}", big_w[t], big_m[t], big_v[t], [g_full[t]])
    big["w_mod"] = _adamw("adamw_w_mod", w_mod, m_w_mod, v_w_mod, [g_w_mod])
    big["conv_w"] = _adamw("adamw_conv_w", conv_w, m_conv_w, v_conv_w, [gs_conv_mine], tr=8)
    small_names = ["b_mod", "norm1_w", "norm2_w", "q_norm_w", "k_norm_w", "o_norm_w", "a_log", "dt_bias"]
    small_w = [b_mod, norm1_w, norm2_w, q_norm_w, k_norm_w, o_norm_w, a_log, dt_bias]
    small_m = [m_b_mod, m_norm1_w, m_norm2_w, m_q_norm_w, m_k_norm_w, m_o_norm_w, m_a_log, m_dt_bias]
    small_v = [v_b_mod, v_norm1_w, v_norm2_w, v_q_norm_w, v_k_norm_w, v_o_norm_w, v_a_log, v_dt_bias]
    small_g = [gs_b_mod, gs_norm1, gs_norm2, gs_q_norm, gs_k_norm, gs_o_norm, gs_a_log, gs_dt_bias]
    rep_w = jnp.concatenate(small_w, axis=1)
    rep_m = jnp.concatenate(small_m, axis=1)
    rep_v = jnp.concatenate(small_v, axis=1)
    rep_g = jnp.concatenate(small_g, axis=1)
    rep = _adamw("adamw_small", rep_w, rep_m, rep_v, [rep_g], tr=1)
    roffs = [0]
    for a in small_w:
        roffs.append(roffs[-1] + a.shape[1])
    for j, nm in enumerate(small_names):
        big[nm] = tuple(r[:, roffs[j]:roffs[j + 1]] for r in rep)

    order = ["w_mod", "b_mod", "norm1_w", "w_in", "q_norm_w", "k_norm_w", "conv_w", "a_log", "dt_bias", "o_norm_w",
             "p_a", "p_b", "w_out", "norm2_w", "w_gate", "w_up", "w_down"]
    grads = [big[nm][0] for nm in order]
    deltas = [big[nm][1] for nm in order]
    new_m = [big[nm][2] for nm in order]
    new_v = [big[nm][3] for nm in order]
    return (loss, grad_x[None], *grads, *deltas, *new_m, *new_v)
```
